```python
import jax, jax.numpy as jnp
from jax import lax
import numpy as np

D_MODEL = 2048
BATCH = 16
SEQ = 256
DEPTH = 4
DEC_BATCH = 4
DEC_SEQ = 1024
PAST_LEN = 256

GRID_W = 64
N_EVEN = (DEPTH + 1) // 2
N_ODD = DEPTH // 2
EPS = 1e-6
CHUNK = 64
H_A = 16
HD_A = 64
D_A = H_A * HD_A
LORA_W = 64
LORA_A = 64
LORA_G = 128
A_IN = 3 * D_A + 2 * LORA_W + 2 * LORA_A + LORA_G
H_B = 4
DK_B = 128
DV_B = 256
QK_B = H_B * DK_B
V_B = H_B * DV_B
LORA_GK = 16
GATE_NORM = 16.0
B_IN = 2 * QK_B + V_B + 2 * LORA_GK + V_B
EV_IN = A_IN + B_IN
H_C = 8
DK_C = 128
DV_C = 256
QK_C = H_C * DK_C
V_C = H_C * DV_C
OD_IN = 2 * QK_C + 2 * V_C + 4 * H_C
D_FF = 5632
N_EXPERTS = 8
TOP_K = 2

kernel_name = 'hybrid_rwkv7_gla_mlstm_diffusion_step'


def _split(x, sizes):
    idx = [int(i) for i in np.cumsum(sizes[:-1])]
    return jnp.split(x, idx, axis=-1)


def _rmsnorm(x, g):
    xf = x.astype(jnp.float32)
    return (xf * lax.rsqrt(jnp.mean(xf * xf, -1, keepdims=True) + EPS) * g).astype(x.dtype)


def _head_rms(y, g):
    return y * lax.rsqrt(jnp.mean(y * y, -1, keepdims=True) + EPS) * g


def _head_layernorm(y, g, b):
    mu = jnp.mean(y, -1, keepdims=True)
    yc = y - mu
    return yc * lax.rsqrt(jnp.mean(yc * yc, -1, keepdims=True) + EPS) * g + b


def _token_shift(p, rows):
    B, T, C = p.shape
    if rows is None:
        half = C // 2
        prev = jnp.pad(p[:, :-1, :half], ((0, 0), (1, 0), (0, 0)))
        nxt = jnp.pad(p[:, 1:, half:], ((0, 0), (0, 1), (0, 0)))
        return jnp.concatenate([prev, nxt], -1)
    g = p.reshape(B, rows, GRID_W, C)
    q = C // 4
    left = jnp.pad(g[:, :, :-1, :q], ((0, 0), (0, 0), (1, 0), (0, 0)))
    right = jnp.pad(g[:, :, 1:, q:2 * q], ((0, 0), (0, 0), (0, 1), (0, 0)))
    up = jnp.pad(g[:, :-1, :, 2 * q:3 * q], ((0, 0), (1, 0), (0, 0), (0, 0)))
    down = jnp.pad(g[:, 1:, :, 3 * q:], ((0, 0), (0, 1), (0, 0), (0, 0)))
    return jnp.concatenate([left, right, up, down], -1).reshape(B, T, C)


def _rwkv7_step(S, inp):
    r, w, k, v, kk, a = inp
    sa = jnp.einsum('bhvk,bhk->bhv', S, kk)
    S = S * w[:, :, None, :] - sa[..., None] * (kk * a)[:, :, None, :] + v[..., None] * k[:, :, None, :]
    return S, jnp.einsum('bhvk,bhk->bhv', S, r)


def _rwkv7(pa, s0, w0, w2, a0, a2, g2, k_k, k_a, r_k, ln_g, ln_b):
    B, T, _ = pa.shape
    r, k, v, wl, al, gl = _split(pa, [D_A, D_A, D_A, 2 * LORA_W, 2 * LORA_A, LORA_G])
    wl = wl.reshape(B, T, 2, LORA_W)
    al = al.reshape(B, T, 2, LORA_A)
    w = -jax.nn.softplus(-(w0 + jnp.einsum('btdr,drc->btdc', jnp.tanh(wl), w2))) - 0.5
    decay = jnp.exp(-jnp.exp(w))
    a = jax.nn.sigmoid(a0 + jnp.einsum('btdr,drc->btdc', al, a2))
    g = jax.nn.sigmoid(gl) @ g2
    heads = lambda t: t.reshape(t.shape[:-1] + (H_A, HD_A))
    kk = heads(k * k_k)
    kk = kk / jnp.maximum(jnp.sqrt(jnp.sum(kk * kk, -1, keepdims=True)), 1e-12)
    k_dir = heads(k[:, :, None, :] * (1.0 + (a - 1.0) * k_a))
    rh, vh = heads(r), heads(v)
    decay_h, a_h = heads(decay), heads(a)

    def run(d, rev):
        xs = (rh, decay_h[:, :, d], k_dir[:, :, d], vh, kk, a_h[:, :, d])
        xs = tuple(jnp.moveaxis(t, 1, 0) for t in xs)
        S, y = lax.scan(_rwkv7_step, s0[:, d], xs, reverse=rev)
        return jnp.moveaxis(y, 0, 1), S

    y_f, s_f = run(0, False)
    y_b, s_b = run(1, True)
    bonus = jnp.sum(jnp.sum(rh[:, :, None] * k_dir * r_k.reshape(H_A, HD_A), -1, keepdims=True), 2) * vh
    y = _head_layernorm(y_f + y_b, ln_g.reshape(H_A, HD_A), ln_b.reshape(H_A, HD_A)) + bonus
    return y.reshape(B, T, D_A) * g, jnp.stack([s_f, s_b], 1)


def _gla_chunked(q, k, v, log_a, s0):
    B, H, T, DK = q.shape
    DV = v.shape[-1]
    n = T // CHUNK
    qc = (q * DK ** -0.5).reshape(B, H, n, CHUNK, DK)
    kc = k.reshape(B, H, n, CHUNK, DK)
    vc = v.reshape(B, H, n, CHUNK, DV)
    b = jnp.cumsum(log_a.reshape(B, H, n, CHUNK, DK), axis=3)
    b_last = b[:, :, :, -1:, :]
    q_in = qc * jnp.exp(b)
    k_in = kc * jnp.exp(-b)
    k_end = kc * jnp.exp(b_last - b)
    mask = jnp.tril(jnp.ones((CHUNK, CHUNK), bool))
    att = jnp.where(mask, jnp.einsum('bhnld,bhnsd->bhnls', q_in, k_in), 0.0)
    o_intra = jnp.einsum('bhnls,bhnsv->bhnlv', att, vc)

    def step(S, inp):
        qt, kt, vt, dl = inp
        o = jnp.einsum('bhld,bhdv->bhlv', qt, S)
        S = S * jnp.exp(dl)[..., None] + jnp.einsum('bhld,bhlv->bhdv', kt, vt)
        return S, o

    xs = (jnp.moveaxis(q_in, 2, 0), jnp.moveaxis(k_end, 2, 0), jnp.moveaxis(vc, 2, 0),
          jnp.moveaxis(b_last[:, :, :, 0, :], 2, 0))
    S, o_inter = lax.scan(step, s0, xs)
    o = o_intra + jnp.moveaxis(o_inter, 0, 2)
    return o.reshape(B, H, T, DV), S


def _gla(pb, s0, w2, bias, g):
    B, T, _ = pb.shape
    q, k, v, gl, og = _split(pb, [QK_B, QK_B, V_B, 2 * LORA_GK, V_B])
    la = jax.nn.log_sigmoid(jnp.einsum('btdr,drc->btdc', gl.reshape(B, T, 2, LORA_GK), w2) + bias) / GATE_NORM
    hd = lambda t, d: t.reshape(B, T, H_B, d).transpose(0, 2, 1, 3)
    qh, kh, vh = hd(q, DK_B), hd(k, DK_B), hd(v, DV_B)
    fl = lambda t: jnp.flip(t, axis=2)
    o_f, s_f = _gla_chunked(qh, kh, vh, hd(la[:, :, 0], DK_B), s0[:, 0])
    o_b, s_b = _gla_chunked(fl(qh), fl(kh), fl(vh), fl(hd(la[:, :, 1], DK_B)), s0[:, 1])
    o = (o_f + fl(o_b)).transpose(0, 2, 1, 3)
    o = _head_rms(o, g.reshape(H_B, DV_B)).reshape(B, T, V_B) * jax.nn.silu(og)
    return o, jnp.stack([s_f, s_b], 1)


def _even_mixer(h, s_rwkv, s_gla, rows, w_in, mu, rw_w0, rw_w2, rw_a0, rw_a2, rw_g2, rw_kk, rw_ka,
                rw_rk, rw_ln_g, rw_ln_b, gla_w2, gla_b, gla_g, w_out):
    p = (h @ w_in).astype(jnp.float32)
    pa, pb = p[..., :A_IN], p[..., A_IN:]
    pa = pa + mu * (_token_shift(pa, rows) - pa)
    ya, sa = _rwkv7(pa, s_rwkv.astype(jnp.float32), rw_w0, rw_w2, rw_a0, rw_a2, rw_g2, rw_kk, rw_ka,
                    rw_rk, rw_ln_g, rw_ln_b)
    yb, sb = _gla(pb, s_gla.astype(jnp.float32), gla_w2, gla_b, gla_g)
    y = jnp.concatenate([ya, yb], -1).astype(h.dtype) @ w_out
    return y, (sa, sb)


def _mlstm_chunked(q, k, v, log_i, log_f, c0, n0, m0):
    B, H, T, DK = q.shape
    DV = v.shape[-1]
    n = T // CHUNK
    ch = lambda t: jnp.moveaxis(t.reshape((B, H, n, CHUNK) + t.shape[3:]), 2, 0)
    xs = (ch(q * DK ** -0.5), ch(k), ch(v), ch(log_i), ch(log_f))
    mask = jnp.tril(jnp.ones((CHUNK, CHUNK), bool))

    def step(carry, inp):
        C, nv, m = carry
        qt, kt, vt, it, ft = inp
        F = jnp.cumsum(ft, axis=-1)
        logD = jnp.where(mask, F[..., :, None] - F[..., None, :] + it[..., None, :], -jnp.inf)
        log_inter = F + m[..., None]
        m_t = jnp.maximum(log_inter, jnp.max(logD, axis=-1))
        Dm = jnp.exp(logD - m_t[..., None])
        w_inter = jnp.exp(log_inter - m_t)
        s = jnp.einsum('bhld,bhsd->bhls', qt, kt) * Dm
        num = w_inter[..., None] * jnp.einsum('bhld,bhdv->bhlv', qt, C) + jnp.einsum('bhls,bhsv->bhlv', s, vt)
        den = w_inter * jnp.einsum('bhld,bhd->bhl', qt, nv) + jnp.sum(s, -1)
        h = num / jnp.maximum(jnp.abs(den), jnp.exp(-m_t))[..., None]
        m_new = m_t[..., -1]
        w_s = jnp.exp(F[..., -1:] - F + it - m_new[..., None])
        dec = jnp.exp(F[..., -1] + m - m_new)
        C = dec[..., None, None] * C + jnp.einsum('bhs,bhsd,bhsv->bhdv', w_s, kt, vt)
        nv = dec[..., None] * nv + jnp.einsum('bhs,bhsd->bhd', w_s, kt)
        return (C, nv, m_new), h

    (C, nv, m), h = lax.scan(step, (c0, n0, m0), xs)
    return jnp.moveaxis(h, 0, 2).reshape(B, H, T, DV), C, nv, m


def _odd_mixer(h, s_c, s_n, s_m, w_in, b_i, b_f, g, w_out):
    B, T, _ = h.shape
    p = (h @ w_in).astype(jnp.float32)
    q, k, v, o, ig, fg = _split(p, [QK_C, QK_C, V_C, V_C, 2 * H_C, 2 * H_C])
    ig = ig.reshape(B, T, 2, H_C) + b_i
    lf = jax.nn.log_sigmoid(fg.reshape(B, T, 2, H_C) + b_f)
    hd = lambda t, d: t.reshape(B, T, H_C, d).transpose(0, 2, 1, 3)
    gt = lambda t, d: t[:, :, d].transpose(0, 2, 1)
    fl = lambda t: jnp.flip(t, axis=2)
    qh, kh, vh = hd(q, DK_C), hd(k, DK_C), hd(v, DV_C)
    s_c, s_n, s_m = s_c.astype(jnp.float32), s_n.astype(jnp.float32), s_m.astype(jnp.float32)
    h_f, cf, nf, mf = _mlstm_chunked(qh, kh, vh, gt(ig, 0), gt(lf, 0), s_c[:, 0], s_n[:, 0], s_m[:, 0])
    h_b, cb, nb, mb = _mlstm_chunked(fl(qh), fl(kh), fl(vh), fl(gt(ig, 1)), fl(gt(lf, 1)),
                                     s_c[:, 1], s_n[:, 1], s_m[:, 1])
    y = (h_f + fl(h_b)).transpose(0, 2, 1, 3)
    y = _head_rms(y, g.reshape(H_C, DV_C)).reshape(B, T, V_C) * jax.nn.sigmoid(o)
    y = y.astype(h.dtype) @ w_out
    return y, (jnp.stack([cf, cb], 1), jnp.stack([nf, nb], 1), jnp.stack([mf, mb], 1))


def _swiglu(h, w1, w3, w2):
    return (jax.nn.silu(h @ w1) * (h @ w3)) @ w2


def _moe(h, router, w1, w3, w2):
    B, T, D = h.shape
    hf = h.reshape(B * T, D)
    logits = (hf @ router).astype(jnp.float32)
    top_v, top_i = lax.top_k(logits, TOP_K)
    gates = jax.nn.softmax(top_v, axis=-1)
    dense_gate = jnp.sum(jax.nn.one_hot(top_i, N_EXPERTS, dtype=jnp.float32) * gates[..., None], axis=1)
    y = jnp.zeros((B * T, D), jnp.float32)
    for e in range(N_EXPERTS):
        y = y + dense_gate[:, e:e + 1] * _swiglu(hf, w1[e], w3[e], w2[e]).astype(jnp.float32)
    return y.reshape(B, T, D).astype(h.dtype)


def setup_inputs(seed: int = 0) -> dict:
    key = jax.random.key(seed)
    ks = iter(jax.random.split(key, 64))

    def nrm(shape, scale):
        return jax.random.normal(next(ks), shape, jnp.float32) * scale

    D = D_MODEL
    return {
        'x_prompt': nrm((BATCH, SEQ, D), 1.0),
        'x_sample': nrm((DEC_BATCH, DEC_SEQ, D), 1.0),
        'state_rwkv': nrm((DEC_BATCH, N_EVEN, 2, H_A, HD_A, HD_A), 0.5),
        'state_gla': nrm((DEC_BATCH, N_EVEN, 2, H_B, DK_B, DV_B), 0.5),
        'state_mlstm_c': nrm((DEC_BATCH, N_ODD, 2, H_C, DK_C, DV_C), 0.1),
        'state_mlstm_n': nrm((DEC_BATCH, N_ODD, 2, H_C, DK_C), 0.1),
        'state_mlstm_m': nrm((DEC_BATCH, N_ODD, 2, H_C), 1.0),
        'c': nrm((DEC_BATCH, D), 1.0),
        'c_ctx': nrm((D,), 1.0),
        'mod_w': nrm((DEPTH, D, 6 * D), 0.5 * D ** -0.5),
        'mod_b': nrm((DEPTH, 6 * D), 0.02),
        'norm_g': 1.0 + nrm((DEPTH, 2, D), 0.02),
        'final_g': 1.0 + nrm((D,), 0.02),
        'ev_w_in': nrm((N_EVEN, D, EV_IN), D ** -0.5),
        'ev_mu': jax.random.uniform(next(ks), (N_EVEN, A_IN), jnp.float32),
        'rw_w0': nrm((N_EVEN, 2, D_A), 0.5),
        'rw_w2': nrm((N_EVEN, 2, LORA_W, D_A), 0.5 * LORA_W ** -0.5),
        'rw_a0': nrm((N_EVEN, 2, D_A), 0.5),
        'rw_a2': nrm((N_EVEN, 2, LORA_A, D_A), 0.5 * LORA_A ** -0.5),
        'rw_g2': nrm((N_EVEN, LORA_G, D_A), LORA_G ** -0.5),
        'rw_kk': 0.85 + nrm((N_EVEN, D_A), 0.02),
        'rw_ka': 1.0 + nrm((N_EVEN, D_A), 0.02),
        'rw_rk': nrm((N_EVEN, D_A), 0.1),
        'rw_ln_g': 1.0 + nrm((N_EVEN, D_A), 0.02),
        'rw_ln_b': nrm((N_EVEN, D_A), 0.02),
        'gla_w2': nrm((N_EVEN, 2, LORA_GK, QK_B), LORA_GK ** -0.5),
        'gla_b': nrm((N_EVEN, 2, QK_B), 0.1),
        'gla_norm_g': 1.0 + nrm((N_EVEN, V_B), 0.02),
        'ev_w_out': nrm((N_EVEN, D_A + V_B, D), (D_A + V_B) ** -0.5),
        'ffn_w1': nrm((N_EVEN, D, D_FF), D ** -0.5),
        'ffn_w3': nrm((N_EVEN, D, D_FF), D ** -0.5),
        'ffn_w2': nrm((N_EVEN, D_FF, D), D_FF ** -0.5),
        'od_w_in': nrm((N_ODD, D, OD_IN), D ** -0.5),
        'ml_b_i': nrm((N_ODD, 2, H_C), 0.1),
        'ml_b_f': 3.0 + nrm((N_ODD, 2, H_C), 0.5),
        'ml_norm_g': 1.0 + nrm((N_ODD, V_C), 0.02),
        'od_w_out': nrm((N_ODD, V_C, D), V_C ** -0.5),
        'moe_router': nrm((N_ODD, D, N_EXPERTS), D ** -0.5),
        'moe_w1': nrm((N_ODD, N_EXPERTS, D, D_FF), D ** -0.5),
        'moe_w3': nrm((N_ODD, N_EXPERTS, D, D_FF), D ** -0.5),
        'moe_w2': nrm((N_ODD, N_EXPERTS, D_FF, D), D_FF ** -0.5),
    }


def reference(x_prompt, x_sample, state_rwkv, state_gla, state_mlstm_c, state_mlstm_n, state_mlstm_m,
              c, c_ctx, mod_w, mod_b, norm_g, final_g,
              ev_w_in, ev_mu, rw_w0, rw_w2, rw_a0, rw_a2, rw_g2, rw_kk, rw_ka, rw_rk, rw_ln_g, rw_ln_b,
              gla_w2, gla_b, gla_norm_g, ev_w_out, ffn_w1, ffn_w3, ffn_w2,
              od_w_in, ml_b_i, ml_b_f, ml_norm_g, od_w_out, moe_router, moe_w1, moe_w3, moe_w2):
    f32 = jnp.float32
    rows = x_sample.shape[1] // GRID_W
    Bp = x_prompt.shape[0]
    silu_c = jax.nn.silu(c.astype(f32))
    silu_ctx = jax.nn.silu(c_ctx.astype(f32))

    def layer(l, x, mod, rows_, init):
        i = l // 2
        sh1, sc1, gt1, sh2, sc2, gt2 = jnp.split(mod, 6, axis=-1)
        h = (_rmsnorm(x, norm_g[l, 0]) * (1.0 + sc1) + sh1).astype(x.dtype)
        if l % 2 == 0:
            y, st = _even_mixer(h, init[0], init[1], rows_, ev_w_in[i], ev_mu[i], rw_w0[i], rw_w2[i],
                                rw_a0[i], rw_a2[i], rw_g2[i], rw_kk[i], rw_ka[i], rw_rk[i], rw_ln_g[i],
                                rw_ln_b[i], gla_w2[i], gla_b[i], gla_norm_g[i], ev_w_out[i])
        else:
            y, st = _odd_mixer(h, init[0], init[1], init[2], od_w_in[i], ml_b_i[i], ml_b_f[i],
                               ml_norm_g[i], od_w_out[i])
        x = x + (gt1 * y).astype(x.dtype)
        h = (_rmsnorm(x, norm_g[l, 1]) * (1.0 + sc2) + sh2).astype(x.dtype)
        if l % 2 == 0:
            f = _swiglu(h, ffn_w1[i], ffn_w3[i], ffn_w2[i])
        else:
            f = _moe(h, moe_router[i], moe_w1[i], moe_w3[i], moe_w2[i])
        x = x + (gt2 * f).astype(x.dtype)
        return x, st

    xp, xs = x_prompt, x_sample
    rw_l, gla_l, mc_l, mn_l, mm_l = [], [], [], [], []
    for l in range(DEPTH):
        i = l // 2
        mod_ctx = (silu_ctx @ mod_w[l] + mod_b[l])[None, None, :]
        mod_lat = (silu_c @ mod_w[l] + mod_b[l])[:, None, :]
        if l % 2 == 0:
            init_ctx = (jnp.zeros((Bp, 2, H_A, HD_A, HD_A), f32), jnp.zeros((Bp, 2, H_B, DK_B, DV_B), f32))
            init_lat = (state_rwkv[:, i], state_gla[:, i])
        else:
            init_ctx = (jnp.zeros((Bp, 2, H_C, DK_C, DV_C), f32), jnp.zeros((Bp, 2, H_C, DK_C), f32),
                        jnp.zeros((Bp, 2, H_C), f32))
            init_lat = (state_mlstm_c[:, i], state_mlstm_n[:, i], state_mlstm_m[:, i])
        xp, st = layer(l, xp, mod_ctx, None, init_ctx)
        if l % 2 == 0:
            rw_l.append(st[0])
            gla_l.append(st[1])
        else:
            mc_l.append(st[0])
            mn_l.append(st[1])
            mm_l.append(st[2])
        xs, _ = layer(l, xs, mod_lat, rows, init_lat)

    y_prompt = _rmsnorm(xp, final_g)
    y_sample = _rmsnorm(xs, final_g)
    dt = x_prompt.dtype
    new_rwkv = jnp.stack(rw_l, axis=1).astype(dt)
    new_gla = jnp.stack(gla_l, axis=1).astype(dt)
    new_mlstm_c = jnp.stack(mc_l, axis=1).astype(dt)
    new_mlstm_n = jnp.stack(mn_l, axis=1).astype(dt)
    new_mlstm_m = jnp.stack(mm_l, axis=1).astype(dt)
    return (y_prompt, y_sample, new_rwkv, new_gla, new_mlstm_c, new_mlstm_n, new_mlstm_m)
```

```python
import functools

import jax
import jax.numpy as jnp
import numpy as np
from jax import lax
from jax.experimental import pallas as pl
from jax.experimental.pallas import tpu as pltpu

D_MODEL = 2048
DEPTH = 4
GRID_W = 64
EPS = 1e-6
CHUNK = 64
H_A, HD_A = 16, 64
D_A = H_A * HD_A
LORA_W, LORA_A, LORA_G = 64, 64, 128
A_IN = 3 * D_A + 2 * LORA_W + 2 * LORA_A + LORA_G
H_B, DK_B, DV_B = 4, 128, 256
QK_B, V_B = H_B * DK_B, H_B * DV_B
LORA_GK = 16
GATE_NORM = 16.0
H_C, DK_C, DV_C = 8, 128, 256
QK_C, V_C = H_C * DK_C, H_C * DV_C
D_FF = 5632
N_EXPERTS = 8
TOP_K = 2

MOD_ROWS = 8
TOKENS_PER_MOD = 1024
VMEM_LIMIT = 48 * 1024 * 1024
NEG_BIG = -1e30

_f32 = jnp.float32
_bf16 = jnp.bfloat16


def _cparams(n_axes):
    return pltpu.CompilerParams(dimension_semantics=("arbitrary",) * n_axes,
                                vmem_limit_bytes=VMEM_LIMIT)


def _dot(a, b):
    return jnp.dot(a.astype(_bf16), b.astype(_bf16), preferred_element_type=_f32)


def _dot_nt(a, b):
    return lax.dot_general(a.astype(_bf16), b.astype(_bf16), (((1,), (1,)), ((), ())),
                           preferred_element_type=_f32)


def _dot_tn(a, b):
    return lax.dot_general(a.astype(_bf16), b.astype(_bf16), (((0,), (0,)), ((), ())),
                           preferred_element_type=_f32)


def _dot_f32(a, b):
    return jnp.dot(a, b, preferred_element_type=_f32, precision=lax.Precision.HIGHEST)


def _mod_row(block_idx, tokens_per_block):
    grp = (block_idx * tokens_per_block) // TOKENS_PER_MOD
    return jnp.maximum(grp - 3, 0)


def _mod_kernel(a_ref, w_ref, b_ref, o_ref):
    o_ref[...] = _dot(a_ref[...], w_ref[...]) + b_ref[...]


def _mod_table(silu_rows, mod_w, mod_b, tn=1024):
    n_l, k, n = mod_w.shape
    return pl.pallas_call(
        _mod_kernel,
        grid=(n_l, n // tn),
        in_specs=[pl.BlockSpec((MOD_ROWS, k), lambda l, j: (0, 0)),
                  pl.BlockSpec((None, k, tn), lambda l, j: (l, 0, j)),
                  pl.BlockSpec((None, 1, tn), lambda l, j: (l, 0, j))],
        out_specs=pl.BlockSpec((None, MOD_ROWS, tn), lambda l, j: (l, 0, j)),
        out_shape=jax.ShapeDtypeStruct((n_l, MOD_ROWS, n), _f32),
        compiler_params=_cparams(2),
    )(silu_rows, mod_w, mod_b.reshape(n_l, 1, n))


def _norm_mod_kernel(x_ref, g_ref, sc_ref, sh_ref, o_ref, *, tm):
    row = _mod_row(pl.program_id(0), tm)
    x = x_ref[...]
    inv = lax.rsqrt(jnp.mean(x * x, axis=-1, keepdims=True) + EPS)
    sc = sc_ref[pl.ds(row, 1), :]
    sh = sh_ref[pl.ds(row, 1), :]
    o_ref[...] = (x * inv * g_ref[...] * (1.0 + sc) + sh).astype(o_ref.dtype)


def _norm_mod(x, g, sc, sh, tm=256):
    m, d = x.shape
    return pl.pallas_call(
        functools.partial(_norm_mod_kernel, tm=tm),
        grid=(m // tm,),
        in_specs=[pl.BlockSpec((tm, d), lambda i: (i, 0)),
                  pl.BlockSpec((1, d), lambda i: (0, 0)),
                  pl.BlockSpec((MOD_ROWS, d), lambda i: (0, 0)),
                  pl.BlockSpec((MOD_ROWS, d), lambda i: (0, 0))],
        out_specs=pl.BlockSpec((tm, d), lambda i: (i, 0)),
        out_shape=jax.ShapeDtypeStruct((m, d), _bf16),
        compiler_params=_cparams(1),
    )(x, g.reshape(1, d), sc, sh)


def _rmsnorm_kernel(x_ref, g_ref, o_ref):
    x = x_ref[...]
    o_ref[...] = x * lax.rsqrt(jnp.mean(x * x, axis=-1, keepdims=True) + EPS) * g_ref[...]


def _rmsnorm(x, g, tm=256):
    m, d = x.shape
    return pl.pallas_call(
        _rmsnorm_kernel,
        grid=(m // tm,),
        in_specs=[pl.BlockSpec((tm, d), lambda i: (i, 0)), pl.BlockSpec((1, d), lambda i: (0, 0))],
        out_specs=pl.BlockSpec((tm, d), lambda i: (i, 0)),
        out_shape=jax.ShapeDtypeStruct((m, d), _f32),
        compiler_params=_cparams(1),
    )(x, g.reshape(1, d))


def _mm_kernel(a_ref, w_ref, o_ref):
    o_ref[...] = _dot(a_ref[...], w_ref[...])


def _mm_res_kernel(a_ref, w_ref, res_ref, gate_ref, o_ref, *, tm):
    row = _mod_row(pl.program_id(1), tm)
    gate = gate_ref[pl.ds(row, 1), :]
    o_ref[...] = res_ref[...] + gate * _dot(a_ref[...], w_ref[...])


def _matmul(a, w, tm=512, tn=512):
    m, k = a.shape
    n = w.shape[1]
    return pl.pallas_call(
        _mm_kernel,
        grid=(n // tn, m // tm),
        in_specs=[pl.BlockSpec((tm, k), lambda j, i: (i, 0)),
                  pl.BlockSpec((k, tn), lambda j, i: (0, j))],
        out_specs=pl.BlockSpec((tm, tn), lambda j, i: (i, j)),
        out_shape=jax.ShapeDtypeStruct((m, n), _f32),
        compiler_params=_cparams(2),
    )(a, w)


def _matmul_residual(a, w, res, gate, tm=512, tn=512):
    m, k = a.shape
    n = w.shape[1]
    return pl.pallas_call(
        functools.partial(_mm_res_kernel, tm=tm),
        grid=(n // tn, m // tm),
        in_specs=[pl.BlockSpec((tm, k), lambda j, i: (i, 0)),
                  pl.BlockSpec((k, tn), lambda j, i: (0, j)),
                  pl.BlockSpec((tm, tn), lambda j, i: (i, j)),
                  pl.BlockSpec((MOD_ROWS, tn), lambda j, i: (0, j))],
        out_specs=pl.BlockSpec((tm, tn), lambda j, i: (i, j)),
        out_shape=jax.ShapeDtypeStruct((m, n), _f32),
        compiler_params=_cparams(2),
    )(a, w, res, gate)


def _ffn_kernel(te_ref, tv_ref, x_ref, w1_ref, w3_ref, w2_ref, o_ref):
    i, j = pl.program_id(0), pl.program_id(1)

    @pl.when(j == 0)
    def _():
        o_ref[...] = jnp.zeros_like(o_ref)

    @pl.when(tv_ref[i] > 0)
    def _():
        x = x_ref[...]
        a = _dot(x, w1_ref[...])
        b = _dot(x, w3_ref[...])
        o_ref[...] += _dot(a * jax.nn.sigmoid(a) * b, w2_ref[...])


def _grouped_ffn(x, w1, w3, w2, tile_expert, tile_valid, tm=512, tf=512):
    m, d = x.shape
    f = w1.shape[2]
    nj = f // tf
    n_tiles = m // tm

    def jeff(i, j, tv):
        return jnp.where(tv[i] > 0, j, nj - 1)

    return pl.pallas_call(
        _ffn_kernel,
        grid_spec=pltpu.PrefetchScalarGridSpec(
            num_scalar_prefetch=2,
            grid=(n_tiles, nj),
            in_specs=[pl.BlockSpec((tm, d), lambda i, j, te, tv: (i, 0)),
                      pl.BlockSpec((None, d, tf), lambda i, j, te, tv: (te[i], 0, jeff(i, j, tv))),
                      pl.BlockSpec((None, d, tf), lambda i, j, te, tv: (te[i], 0, jeff(i, j, tv))),
                      pl.BlockSpec((None, tf, d), lambda i, j, te, tv: (te[i], jeff(i, j, tv), 0))],
            out_specs=pl.BlockSpec((tm, d), lambda i, j, te, tv: (i, 0))),
        out_shape=jax.ShapeDtypeStruct((m, d), _f32),
        compiler_params=_cparams(2),
    )(tile_expert, tile_valid, x, w1, w3, w2)


def _causal_masks(direction):
    t = lax.broadcasted_iota(jnp.int32, (CHUNK, CHUNK), 0)
    i = lax.broadcasted_iota(jnp.int32, (CHUNK, CHUNK), 1)
    if direction == 0:
        return i <= t, i < t
    return i >= t, i > t


def _rwkv_head_chunk(rt, kt, at, bt, v, pl_row, s0, incl, strict):
    L = CHUNK
    ar = jnp.concatenate([at, rt], axis=0)
    p_k = _dot_nt(ar, kt)
    p_b = _dot_nt(ar, bt)
    n_mat = jnp.where(strict, p_b[:L], 0.0)
    m_ak = jnp.where(strict, p_k[:L], 0.0)
    m_rk = jnp.where(incl, p_k[L:], 0.0)
    m_rb = jnp.where(incl, p_b[L:], 0.0)
    eye = jnp.where(incl & jnp.logical_not(strict), 1.0, 0.0)
    t_mat = eye - n_mat
    n_pow = n_mat
    for _ in range(5):
        n_pow = _dot_f32(n_pow, n_pow)
        t_mat = t_mat + _dot_f32(t_mat, n_pow)
    mv = _dot(jnp.concatenate([m_ak, m_rk], axis=0), v)
    x = _dot(t_mat, at)
    z = _dot(t_mat, mv[:L])
    q_eff = rt - _dot(m_rb, x)
    y0 = mv[L:] - _dot(m_rb, z)
    k_end = kt * pl_row
    b_end = bt * pl_row
    y = _dot_nt(q_eff, s0) + y0
    s_new = s0 * pl_row - _dot(s0, _dot_tn(x, b_end)) + _dot_tn(v, k_end) - _dot_tn(z, b_end)
    return y, s_new


def _rwkv_kernel(rt_f, kt_f, at_f, bt_f, pl_f, v_f, rt_b, kt_b, at_b, bt_b, pl_b, v_b, s0_ref,
                 yf_ref, yb_ref, sout_ref, s_scr, *, n_heads, n_chunks):
    c = pl.program_id(2)

    @pl.when(c == 0)
    def _():
        s_scr[...] = s0_ref[...]

    dirs = ((rt_f, kt_f, at_f, bt_f, pl_f, v_f, yf_ref), (rt_b, kt_b, at_b, bt_b, pl_b, v_b, yb_ref))
    for d, (rt, kt, at, bt, plr, v, y_ref) in enumerate(dirs):
        incl, strict = _causal_masks(d)
        for g in range(n_heads):
            y, s_new = _rwkv_head_chunk(rt[g], kt[g], at[g], bt[g], v[g], plr[g], s_scr[d, g],
                                        incl, strict)
            y_ref[g] = y
            s_scr[d, g] = s_new

    @pl.when(c == n_chunks - 1)
    def _():
        sout_ref[...] = s_scr[...]


def _rwkv_scan(rt, kt, at, bt, p_last, v, s0, heads_per_step=4):
    _, b, h, t, k = rt.shape
    nc = t // CHUNK
    g = heads_per_step
    fwd = lambda bi, gi, ci: (0, bi, gi, ci, 0)
    bwd = lambda bi, gi, ci: (1, bi, gi, nc - 1 - ci, 0)
    seq_spec = lambda im: pl.BlockSpec((None, None, g, CHUNK, k), im)
    pl_spec = lambda d: pl.BlockSpec(
        (None, None, g, None, 1, k),
        (lambda bi, gi, ci: (0, bi, gi, ci, 0, 0)) if d == 0 else
        (lambda bi, gi, ci: (1, bi, gi, nc - 1 - ci, 0, 0)))
    v_spec = lambda d: pl.BlockSpec(
        (None, g, CHUNK, k),
        (lambda bi, gi, ci: (bi, gi, ci, 0)) if d == 0 else (lambda bi, gi, ci: (bi, gi, nc - 1 - ci, 0)))
    st_spec = pl.BlockSpec((None, 2, g, k, k), lambda bi, gi, ci: (bi, 0, gi, 0, 0))
    y_shape = jax.ShapeDtypeStruct((b, h, t, k), _f32)
    return pl.pallas_call(
        functools.partial(_rwkv_kernel, n_heads=g, n_chunks=nc),
        grid=(b, h // g, nc),
        in_specs=[seq_spec(fwd)] * 4 + [pl_spec(0), v_spec(0)] + [seq_spec(bwd)] * 4 + [pl_spec(1), v_spec(1), st_spec],
        out_specs=[v_spec(0), v_spec(1), st_spec],
        out_shape=[y_shape, y_shape, jax.ShapeDtypeStruct(s0.shape, _f32)],
        scratch_shapes=[pltpu.VMEM((2, g, k, k), _f32)],
        compiler_params=_cparams(3),
    )(rt, kt, at, bt, p_last, v, rt, kt, at, bt, p_last, v, s0)


def _gla_kernel(q_f, k_f, v_f, la_f, q_b, k_b, v_b, la_b, s0_ref, of_ref, ob_ref, sout_ref, s_scr,
                *, n_chunks):
    c = pl.program_id(2)

    @pl.when(c == 0)
    def _():
        s_scr[...] = s0_ref[...]

    dirs = ((q_f, k_f, v_f, la_f, of_ref), (q_b, k_b, v_b, la_b, ob_ref))
    for d, (q_ref, k_ref, v_ref, la_ref, o_ref) in enumerate(dirs):
        incl, _ = _causal_masks(d)
        la = la_ref[...]
        cum = _dot_f32(jnp.where(incl, 1.0, 0.0), la)
        tot = jnp.sum(la, axis=0, keepdims=True)
        k = k_ref[...]
        q_in = q_ref[...] * (DK_B ** -0.5) * jnp.exp(cum)
        k_in = k * jnp.exp(-cum)
        k_end = k * jnp.exp(tot - cum)
        v = v_ref[...]
        s_t = s_scr[d]
        att = jnp.where(incl, _dot_nt(q_in, k_in), 0.0)
        o_ref[...] = _dot(att, v) + _dot_nt(q_in, s_t)
        s_scr[d] = s_t * jnp.exp(tot) + _dot_tn(v, k_end)

    @pl.when(c == n_chunks - 1)
    def _():
        sout_ref[...] = s_scr[...]


def _gla_scan(q, k, v, la, s0_t):
    b, h, t, dk = q.shape
    dv = v.shape[-1]
    nc = t // CHUNK
    fwd = lambda bi, hi, ci: (bi, hi, ci, 0)
    bwd = lambda bi, hi, ci: (bi, hi, nc - 1 - ci, 0)
    spec = lambda w, im: pl.BlockSpec((None, None, CHUNK, w), im)
    la_spec = lambda d: pl.BlockSpec(
        (None, None, None, CHUNK, dk),
        (lambda bi, hi, ci: (0, bi, hi, ci, 0)) if d == 0 else (lambda bi, hi, ci: (1, bi, hi, nc - 1 - ci, 0)))
    st_spec = pl.BlockSpec((None, 2, None, dv, dk), lambda bi, hi, ci: (bi, 0, hi, 0, 0))
    o_shape = jax.ShapeDtypeStruct((b, h, t, dv), _f32)
    return pl.pallas_call(
        functools.partial(_gla_kernel, n_chunks=nc),
        grid=(b, h, nc),
        in_specs=[spec(dk, fwd), spec(dk, fwd), spec(dv, fwd), la_spec(0),
                  spec(dk, bwd), spec(dk, bwd), spec(dv, bwd), la_spec(1), st_spec],
        out_specs=[spec(dv, fwd), spec(dv, bwd), st_spec],
        out_shape=[o_shape, o_shape, jax.ShapeDtypeStruct(s0_t.shape, _f32)],
        scratch_shapes=[pltpu.VMEM((2, dv, dk), _f32)],
        compiler_params=_cparams(3),
    )(q, k, v, la, q, k, v, la, s0_t)


def _mlstm_kernel(q_f, k_f, v_f, ir_f, fr_f, ic_f, fc_f, q_b, k_b, v_b, ir_b, fr_b, ic_b, fc_b,
                  c0_ref, n0_ref, m0_ref, hf_ref, hb_ref, cout_ref, nout_ref, mout_ref,
                  c_scr, n_scr, m_scr, *, n_chunks):
    c = pl.program_id(2)

    @pl.when(c == 0)
    def _():
        c_scr[...] = c0_ref[...]
        n_scr[...] = n0_ref[...]
        m_scr[...] = m0_ref[...]

    L = CHUNK
    dirs = ((q_f, k_f, v_f, ir_f, fr_f, ic_f, fc_f, hf_ref), (q_b, k_b, v_b, ir_b, fr_b, ic_b, fc_b, hb_ref))
    for d, (q_ref, k_ref, v_ref, ir_ref, fr_ref, ic_ref, fc_ref, h_ref) in enumerate(dirs):
        incl, _ = _causal_masks(d)
        incl_t = _causal_masks(1 - d)[0]
        i_row, f_row = ir_ref[...], fr_ref[...]
        i_col, f_col = ic_ref[...], fc_ref[...]
        cum_col = jnp.sum(jnp.where(incl, f_row, 0.0), axis=1, keepdims=True)
        cum_row = jnp.sum(jnp.where(incl_t, f_col, 0.0), axis=0, keepdims=True)
        f_tot = jnp.sum(f_row, axis=1, keepdims=True)
        m_prev = m_scr[d]
        log_d = jnp.where(incl, cum_col - cum_row + i_row, NEG_BIG)
        log_inter = cum_col + m_prev
        m_t = jnp.maximum(log_inter, jnp.max(log_d, axis=1, keepdims=True))
        d_mat = jnp.exp(log_d - m_t)
        w_inter = jnp.exp(log_inter - m_t)
        q = q_ref[...] * (DK_C ** -0.5)
        k = k_ref[...]
        v = v_ref[...]
        c_t = c_scr[d]
        n_row = n_scr[d]
        s = _dot_nt(q, k) * d_mat
        num = w_inter * _dot_nt(q, c_t) + _dot(s, v)
        den = w_inter * jnp.sum(q * n_row, axis=1, keepdims=True) + jnp.sum(s, axis=1, keepdims=True)
        h_ref[...] = num / jnp.maximum(jnp.abs(den), jnp.exp(-m_t))
        m_new = m_t[L - 1:L, :] if d == 0 else m_t[0:1, :]
        w_s = jnp.exp(f_tot - cum_col + i_col - m_new)
        dec = jnp.exp(f_tot + m_prev - m_new)
        kw = k * w_s
        c_scr[d] = dec * c_t + _dot_tn(v, kw)
        n_scr[d] = dec * n_row + jnp.sum(kw, axis=0, keepdims=True)
        m_scr[d] = m_new

    @pl.when(c == n_chunks - 1)
    def _():
        cout_ref[...] = c_scr[...]
        nout_ref[...] = n_scr[...]
        mout_ref[...] = m_scr[...]


def _mlstm_scan(q, k, v, i_row, f_row, i_col, f_col, c0_t, n0, m0):
    b, h, t, dk = q.shape
    dv = v.shape[-1]
    nc = t // CHUNK
    fwd = lambda bi, hi, ci: (bi, hi, ci, 0)
    bwd = lambda bi, hi, ci: (bi, hi, nc - 1 - ci, 0)
    spec = lambda w, im: pl.BlockSpec((None, None, CHUNK, w), im)
    row_spec = lambda d: pl.BlockSpec(
        (None, None, None, None, 1, CHUNK),
        (lambda bi, hi, ci: (0, bi, hi, ci, 0, 0)) if d == 0 else
        (lambda bi, hi, ci: (1, bi, hi, nc - 1 - ci, 0, 0)))
    col_spec = lambda d: pl.BlockSpec(
        (None, None, None, CHUNK, 1),
        (lambda bi, hi, ci: (0, bi, hi, ci, 0)) if d == 0 else (lambda bi, hi, ci: (1, bi, hi, nc - 1 - ci, 0)))
    st = lambda r, w: pl.BlockSpec((None, 2, None, r, w), lambda bi, hi, ci: (bi, 0, hi, 0, 0))
    h_shape = jax.ShapeDtypeStruct((b, h, t, dv), _f32)
    return pl.pallas_call(
        functools.partial(_mlstm_kernel, n_chunks=nc),
        grid=(b, h, nc),
        in_specs=[spec(dk, fwd), spec(dk, fwd), spec(dv, fwd), row_spec(0), row_spec(0), col_spec(0), col_spec(0),
                  spec(dk, bwd), spec(dk, bwd), spec(dv, bwd), row_spec(1), row_spec(1), col_spec(1), col_spec(1),
                  st(dv, dk), st(1, dk), st(1, 1)],
        out_specs=[spec(dv, fwd), spec(dv, bwd), st(dv, dk), st(1, dk), st(1, 1)],
        out_shape=[h_shape, h_shape, jax.ShapeDtypeStruct(c0_t.shape, _f32),
                   jax.ShapeDtypeStruct(n0.shape, _f32), jax.ShapeDtypeStruct(m0.shape, _f32)],
        scratch_shapes=[pltpu.VMEM((2, dv, dk), _f32), pltpu.VMEM((2, 1, dk), _f32), pltpu.VMEM((2, 1, 1), _f32)],
        compiler_params=_cparams(3),
    )(q, k, v, i_row, f_row, i_col, f_col, q, k, v, i_row, f_row, i_col, f_col, c0_t, n0, m0)


def _split(x, sizes):
    idx = [int(i) for i in np.cumsum(sizes[:-1])]
    return jnp.split(x, idx, axis=-1)


def _token_shift(p, rows):
    B, T, C = p.shape
    if rows is None:
        half = C // 2
        prev = jnp.pad(p[:, :-1, :half], ((0, 0), (1, 0), (0, 0)))
        nxt = jnp.pad(p[:, 1:, half:], ((0, 0), (0, 1), (0, 0)))
        return jnp.concatenate([prev, nxt], -1)
    g = p.reshape(B, rows, GRID_W, C)
    q = C // 4
    left = jnp.pad(g[:, :, :-1, :q], ((0, 0), (0, 0), (1, 0), (0, 0)))
    right = jnp.pad(g[:, :, 1:, q:2 * q], ((0, 0), (0, 0), (0, 1), (0, 0)))
    up = jnp.pad(g[:, :-1, :, 2 * q:3 * q], ((0, 0), (1, 0), (0, 0), (0, 0)))
    down = jnp.pad(g[:, 1:, :, 3 * q:], ((0, 0), (0, 1), (0, 0), (0, 0)))
    return jnp.concatenate([left, right, up, down], -1).reshape(B, T, C)


def _head_major(x, n_heads):
    d = x.shape[-1] // n_heads
    x = x.reshape(x.shape[:-1] + (n_heads, d))
    return jnp.swapaxes(x, -2, -3)


def _rwkv_mixer(pa, s0, rows, mu, w0, w2, a0, a2, g2, k_k, k_a, r_k, ln_g, ln_b):
    B, T, _ = pa.shape
    nc = T // CHUNK
    pa = pa + mu * (_token_shift(pa, rows) - pa)
    r, k, v, wl, al, gl = _split(pa, [D_A, D_A, D_A, 2 * LORA_W, 2 * LORA_A, LORA_G])
    wl = wl.reshape(B, T, 2, LORA_W)
    al = al.reshape(B, T, 2, LORA_A)
    w = -jax.nn.softplus(-(w0 + jnp.einsum('btdr,drc->btdc', jnp.tanh(wl), w2))) - 0.5
    log_decay = -jnp.exp(w)
    a = jax.nn.sigmoid(a0 + jnp.einsum('btdr,drc->btdc', al, a2))
    g = jax.nn.sigmoid(gl) @ g2
    heads = lambda t: t.reshape(t.shape[:-1] + (H_A, HD_A))
    kk = heads(k * k_k)
    kk = (kk / jnp.maximum(jnp.sqrt(jnp.sum(kk * kk, -1, keepdims=True)), 1e-12)).reshape(B, T, 1, D_A)
    k_dir = k[:, :, None, :] * (1.0 + (a - 1.0) * k_a)
    ld_c = log_decay.reshape(B, nc, CHUNK, 2, D_A)
    tot = jnp.sum(ld_c, axis=2, keepdims=True)
    cum_f = jnp.cumsum(ld_c[:, :, :, 0], axis=2)
    cum_b = tot[:, :, :, 1] - jnp.cumsum(ld_c[:, :, :, 1], axis=2) + ld_c[:, :, :, 1]
    cum = jnp.stack([cum_f, cum_b], axis=3).reshape(B, T, 2, D_A)
    rt = r[:, :, None, :] * jnp.exp(cum)
    kt = k_dir * jnp.exp(-cum)
    at = kk * jnp.exp(cum - log_decay)
    bt = kk * a * jnp.exp(-cum)
    to_k = lambda x: _head_major(jnp.moveaxis(x, 2, 0), H_A)
    p_last = jnp.exp(tot)[:, :, 0]
    p_last = jnp.moveaxis(p_last, 2, 0).reshape(2, B, nc, H_A, 1, HD_A)
    p_last = jnp.swapaxes(p_last, 2, 3)
    y_f, y_b, s_new = _rwkv_scan(to_k(rt), to_k(kt), to_k(at), to_k(bt), p_last,
                                 _head_major(v, H_A), s0)
    y = jnp.swapaxes(y_f + y_b, 1, 2)
    rh, vh = heads(r), heads(v)
    bonus = jnp.sum(jnp.sum(rh[:, :, None] * heads(k_dir) * r_k.reshape(H_A, HD_A), -1, keepdims=True), 2) * vh
    mu_ = jnp.mean(y, -1, keepdims=True)
    yc = y - mu_
    y = yc * lax.rsqrt(jnp.mean(yc * yc, -1, keepdims=True) + EPS) * ln_g.reshape(H_A, HD_A) + ln_b.reshape(H_A, HD_A)
    y = y + bonus
    return y.reshape(B, T, D_A) * g, s_new


def _gla_mixer(pb, s0, w2, bias, g):
    B, T, _ = pb.shape
    q, k, v, og, gl = _split(pb, [QK_B, QK_B, V_B, V_B, 2 * LORA_GK])
    la = jax.nn.log_sigmoid(jnp.einsum('btdr,drc->btdc', gl.reshape(B, T, 2, LORA_GK), w2) + bias) / GATE_NORM
    la = _head_major(jnp.moveaxis(la, 2, 0), H_B)
    o_f, o_b, s_t = _gla_scan(_head_major(q, H_B), _head_major(k, H_B), _head_major(v, H_B), la,
                              jnp.swapaxes(s0, -1, -2))
    o = jnp.swapaxes(o_f + o_b, 1, 2)
    o = o * lax.rsqrt(jnp.mean(o * o, -1, keepdims=True) + EPS) * g.reshape(H_B, DV_B)
    return o.reshape(B, T, V_B) * jax.nn.silu(og), jnp.swapaxes(s_t, -1, -2)


def _mlstm_mixer(p, s_c, s_n, s_m, b_i, b_f, g):
    B, T, _ = p.shape
    nc = T // CHUNK
    q, k, v, o, ig, fg = _split(p, [QK_C, QK_C, V_C, V_C, 2 * H_C, 2 * H_C])
    ig = ig.reshape(B, T, 2, H_C) + b_i
    lf = jax.nn.log_sigmoid(fg.reshape(B, T, 2, H_C) + b_f)
    gate = lambda t: jnp.transpose(t, (2, 0, 3, 1))
    i_g, f_g = gate(ig), gate(lf)
    row = lambda t: t.reshape(2, B, H_C, nc, 1, CHUNK)
    col = lambda t: t.reshape(2, B, H_C, T, 1)
    h_f, h_b, c_t, n_new, m_new = _mlstm_scan(
        _head_major(q, H_C), _head_major(k, H_C), _head_major(v, H_C),
        row(i_g), row(f_g), col(i_g), col(f_g),
        jnp.swapaxes(s_c, -1, -2), s_n[:, :, :, None, :], s_m[:, :, :, None, None])
    y = jnp.swapaxes(h_f + h_b, 1, 2)
    y = y * lax.rsqrt(jnp.mean(y * y, -1, keepdims=True) + EPS) * g.reshape(H_C, DV_C)
    y = y.reshape(B, T, V_C) * jax.nn.sigmoid(o)
    return y, jnp.swapaxes(c_t, -1, -2), n_new[:, :, :, 0, :], m_new[:, :, :, 0, 0]


def _moe(h_bf16, router, w1, w3, w2, tm=512):
    n_tok = h_bf16.shape[0]
    n_pairs = n_tok * TOP_K
    n_tiles = n_pairs // tm + N_EXPERTS
    rw = jnp.pad(router, ((0, 0), (0, 128 - N_EXPERTS)))
    r_hi = rw.astype(_bf16)
    r_lo = (rw - r_hi.astype(_f32)).astype(_bf16)
    logits = (_matmul(h_bf16, r_hi, tn=128) + _matmul(h_bf16, r_lo, tn=128))[:, :N_EXPERTS]
    top_v, top_i = lax.top_k(logits, TOP_K)
    gates = jax.nn.softmax(top_v, axis=-1)
    pair_e = top_i.reshape(-1)
    order = jnp.argsort(pair_e, stable=True)
    sorted_e = pair_e[order]
    counts = jnp.sum(jax.nn.one_hot(pair_e, N_EXPERTS, dtype=jnp.int32), axis=0)
    tiles_per_e = (counts + tm - 1) // tm
    tile_start = jnp.cumsum(tiles_per_e) - tiles_per_e
    group_start = jnp.cumsum(counts) - counts
    rank = jnp.arange(n_pairs, dtype=jnp.int32) - group_start[sorted_e]
    dest = tile_start[sorted_e] * tm + rank
    slot_token = jnp.zeros((n_tiles * tm,), jnp.int32).at[dest].set((order // TOP_K).astype(jnp.int32))
    pair_slot = jnp.zeros((n_pairs,), jnp.int32).at[order].set(dest.astype(jnp.int32))
    n_valid = jnp.sum(tiles_per_e)
    tile_ids = jnp.arange(n_tiles, dtype=jnp.int32)
    tile_e = jnp.sum((tile_ids[:, None] >= tile_start[None, :]).astype(jnp.int32), axis=1) - 1
    last_e = jnp.max(jnp.where(counts > 0, jnp.arange(N_EXPERTS), 0))
    tile_valid = (tile_ids < n_valid).astype(jnp.int32)
    tile_e = jnp.where(tile_valid > 0, tile_e, last_e).astype(jnp.int32)
    xg = jnp.take(h_bf16, slot_token, axis=0)
    yg = _grouped_ffn(xg, w1, w3, w2, tile_e, tile_valid, tm=tm)
    yp = jnp.take(yg, pair_slot, axis=0).reshape(n_tok, TOP_K, -1)
    return jnp.sum(yp * gates[..., None], axis=1)


def kernel(x_prompt, x_sample, state_rwkv, state_gla, state_mlstm_c, state_mlstm_n, state_mlstm_m, c, c_ctx, mod_w, mod_b, norm_g, final_g, ev_w_in, ev_mu, rw_w0, rw_w2, rw_a0, rw_a2, rw_g2, rw_kk, rw_ka, rw_rk, rw_ln_g, rw_ln_b, gla_w2, gla_b, gla_norm_g, ev_w_out, ffn_w1, ffn_w3, ffn_w2, od_w_in, ml_b_i, ml_b_f, ml_norm_g, od_w_out, moe_router, moe_w1, moe_w3, moe_w2):
    D = D_MODEL
    Bp, Tp, _ = x_prompt.shape
    Bs, Ts, _ = x_sample.shape
    n_ctx = Bp * Tp
    rows = Ts // GRID_W
    x = jnp.concatenate([x_prompt.reshape(n_ctx, D), x_sample.reshape(Bs * Ts, D)], axis=0)
    n_tok = x.shape[0]

    cond = jnp.concatenate([c_ctx[None, :], c, jnp.zeros((MOD_ROWS - 1 - Bs, D), _f32)], axis=0)
    mod = _mod_table(jax.nn.silu(cond).astype(_bf16), mod_w.astype(_bf16), mod_b)

    dense_te = jnp.zeros((n_tok // 512,), jnp.int32)
    dense_tv = jnp.ones((n_tok // 512,), jnp.int32)
    zeros = lambda *s: jnp.zeros(s, _f32)
    new_rw, new_gla, new_c, new_n, new_m = [], [], [], [], []

    for l in range(DEPTH):
        i = l // 2
        sh1, sc1, gt1, sh2, sc2, gt2 = [mod[l, :, j * D:(j + 1) * D] for j in range(6)]
        h = _norm_mod(x, norm_g[l, 0], sc1, sh1)
        if l % 2 == 0:
            w = ev_w_in[i]
            b0 = A_IN + 2 * QK_B + V_B
            w_in = jnp.concatenate([w[:, :b0], w[:, b0 + 2 * LORA_GK:], w[:, b0:b0 + 2 * LORA_GK],
                                    jnp.zeros((D, 96), _f32)], axis=1).astype(_bf16)
            p = _matmul(h, w_in)
            n_in = A_IN + 2 * QK_B + 2 * V_B + 2 * LORA_GK
            ys = []
            for (lo, hi, B, T, rws, s_rw, s_gl) in (
                    (0, n_ctx, Bp, Tp, None, zeros(Bp, 2, H_A, HD_A, HD_A), zeros(Bp, 2, H_B, DK_B, DV_B)),
                    (n_ctx, n_tok, Bs, Ts, rows, state_rwkv[:, i], state_gla[:, i])):
                ps = p[lo:hi].reshape(B, T, -1)
                ya, sa = _rwkv_mixer(ps[..., :A_IN], s_rw, rws, ev_mu[i], rw_w0[i], rw_w2[i], rw_a0[i],
                                     rw_a2[i], rw_g2[i], rw_kk[i], rw_ka[i], rw_rk[i], rw_ln_g[i], rw_ln_b[i])
                yb, sb = _gla_mixer(ps[..., A_IN:n_in], s_gl, gla_w2[i], gla_b[i], gla_norm_g[i])
                ys.append(jnp.concatenate([ya, yb], -1).reshape(B * T, D_A + V_B))
                if rws is None:
                    new_rw.append(sa)
                    new_gla.append(sb)
            y_in = jnp.concatenate(ys, axis=0).astype(_bf16)
            x = _matmul_residual(y_in, ev_w_out[i].astype(_bf16), x, gt1)
        else:
            w_in = jnp.pad(od_w_in[i], ((0, 0), (0, 6656 - od_w_in.shape[2]))).astype(_bf16)
            p = _matmul(h, w_in)
            n_in = od_w_in.shape[2]
            ys = []
            for (lo, hi, B, T, s_c, s_n, s_m) in (
                    (0, n_ctx, Bp, Tp, zeros(Bp, 2, H_C, DK_C, DV_C), zeros(Bp, 2, H_C, DK_C), zeros(Bp, 2, H_C)),
                    (n_ctx, n_tok, Bs, Ts, state_mlstm_c[:, i], state_mlstm_n[:, i], state_mlstm_m[:, i])):
                ps = p[lo:hi, :n_in].reshape(B, T, -1)
                y, c_new, n_new, m_new = _mlstm_mixer(ps, s_c, s_n, s_m, ml_b_i[i], ml_b_f[i], ml_norm_g[i])
                ys.append(y.reshape(B * T, V_C))
                if lo == 0:
                    new_c.append(c_new)
                    new_n.append(n_new)
                    new_m.append(m_new)
            y_in = jnp.concatenate(ys, axis=0).astype(_bf16)
            x = _matmul_residual(y_in, od_w_out[i].astype(_bf16), x, gt1)

        h = _norm_mod(x, norm_g[l, 1], sc2, sh2)
        if l % 2 == 0:
            f = _grouped_ffn(h, ffn_w1[i][None].astype(_bf16), ffn_w3[i][None].astype(_bf16),
                             ffn_w2[i][None].astype(_bf16), dense_te, dense_tv)
        else:
            f = _moe(h, moe_router[i], moe_w1[i].astype(_bf16), moe_w3[i].astype(_bf16),
                     moe_w2[i].astype(_bf16))
        gate_rows = jnp.repeat(jnp.maximum(jnp.arange(n_tok // TOKENS_PER_MOD) - 3, 0), TOKENS_PER_MOD)
        x = x + gt2[gate_rows] * f

    y = _rmsnorm(x, final_g)
    return (y[:n_ctx].reshape(Bp, Tp, D), y[n_ctx:].reshape(Bs, Ts, D),
            jnp.stack(new_rw, axis=1), jnp.stack(new_gla, axis=1),
            jnp.stack(new_c, axis=1), jnp.stack(new_n, axis=1), jnp.stack(new_m, axis=1))
```

```python
import functools
from typing import NamedTuple

import jax
import jax.numpy as jnp
import numpy as np
from jax import lax
from jax.experimental import pallas as pl
from jax.experimental.pallas import tpu as pltpu

D_MODEL = 2048
DEPTH = 4
GRID_W = 64
EPS = 1e-6
CHUNK = 64
H_A, HD_A = 16, 64
D_A = H_A * HD_A
LORA_W, LORA_A, LORA_G = 64, 64, 128
A_IN = 3 * D_A + 2 * LORA_W + 2 * LORA_A + LORA_G
H_B, DK_B, DV_B = 4, 128, 256
QK_B, V_B = H_B * DK_B, H_B * DV_B
LORA_GK = 16
GATE_NORM = 16.0
H_C, DK_C, DV_C = 8, 128, 256
QK_C, V_C = H_C * DK_C, H_C * DV_C
D_FF = 5632
N_EXPERTS = 8
TOP_K = 2

LANES = 128
MOD_ROWS = 8
VMEM_LIMIT = 48 * 1024 * 1024
NEG_BIG = -1e30

EV_MAIN = 3 * D_A
EV_V_BLK = EV_MAIN // V_B
EV_OG_BLK = EV_V_BLK + 1
EV_Q_BLK = (EV_MAIN + 2 * V_B) // QK_B
EV_K_BLK = EV_Q_BLK + 1
EV_LORA_W = 512
EV_LORA_BLK = (EV_MAIN + 2 * V_B + 2 * QK_B) // EV_LORA_W
EV_N = EV_MAIN + 2 * V_B + 2 * QK_B + EV_LORA_W
GLA_LORA_OFF = 2 * LORA_W + 2 * LORA_A + LORA_G
OD_GATE_BLK = (2 * QK_C + 2 * V_C) // LANES
OD_N = 2 * QK_C + 2 * V_C + LANES

_f32 = jnp.float32
_bf16 = jnp.bfloat16


class _Streams(NamedTuple):
    n_ctx_chunks: int
    ctx_cps: int
    lat_cps: int
    n_chunks: int
    n_ctx_seq: int
    n_seq: int


def _chunk_pos(s, st):
    is_lat = s >= st.n_ctx_chunks
    sl = s - st.n_ctx_chunks
    nc = jnp.where(is_lat, st.lat_cps, st.ctx_cps)
    c = jnp.where(is_lat, sl % st.lat_cps, s % st.ctx_cps)
    seq = jnp.where(is_lat, st.n_ctx_seq + sl // st.lat_cps, s // st.ctx_cps)
    return is_lat, nc, c, seq


def _bwd_chunk(s, st):
    _, nc, c, _ = _chunk_pos(s, st)
    return s - c + (nc - 1 - c)


def _seq_of(s, st):
    return _chunk_pos(s, st)[3]


def _mod_row_of_chunk(s, st):
    is_lat, _, _, seq = _chunk_pos(s, st)
    return jnp.where(is_lat, 1 + seq - st.n_ctx_seq, 0)


def _cparams(n_axes):
    return pltpu.CompilerParams(dimension_semantics=("arbitrary",) * n_axes,
                                vmem_limit_bytes=VMEM_LIMIT)


def _dot(a, b):
    return jnp.dot(a.astype(_bf16), b.astype(_bf16), preferred_element_type=_f32)


def _dot_nt(a, b):
    return lax.dot_general(a.astype(_bf16), b.astype(_bf16), (((1,), (1,)), ((), ())),
                           preferred_element_type=_f32)


def _dot_tn(a, b):
    return lax.dot_general(a.astype(_bf16), b.astype(_bf16), (((0,), (0,)), ((), ())),
                           preferred_element_type=_f32)


def _split2(x):
    hi = x.astype(_bf16)
    return hi, (x - hi.astype(_f32)).astype(_bf16)


def _dot_exact_lhs(tri, x):
    hi = x.astype(_bf16)
    r1 = x - hi.astype(_f32)
    mid = r1.astype(_bf16)
    lo = (r1 - mid.astype(_f32)).astype(_bf16)
    t = tri.astype(_bf16)
    d = lambda v: jnp.dot(t, v, preferred_element_type=_f32)
    return d(hi) + d(mid) + d(lo)


def _dot_exact_rhs(x, ones):
    hi = x.astype(_bf16)
    r1 = x - hi.astype(_f32)
    mid = r1.astype(_bf16)
    lo = (r1 - mid.astype(_f32)).astype(_bf16)
    d = lambda v: jnp.dot(v, ones, preferred_element_type=_f32)
    return d(hi) + d(mid) + d(lo)


def _dot3(a, b_hi, b_lo):
    a_hi, a_lo = _split2(a)
    d = lambda u, v: jnp.dot(u, v, preferred_element_type=_f32)
    return d(a_hi, b_hi) + d(a_lo, b_hi) + d(a_hi, b_lo)


def _tri_masks(direction, n=CHUNK, period=CHUNK):
    t = lax.broadcasted_iota(jnp.int32, (n, n), 0)
    i = lax.broadcasted_iota(jnp.int32, (n, n), 1)
    same = (t // period) == (i // period) if n != period else True
    if direction == 0:
        return (i <= t) & same, (i < t) & same
    return (i >= t) & same, (i > t) & same


def _mod_kernel(a_ref, w_ref, b_ref, o_ref):
    o_ref[...] = _dot(a_ref[...], w_ref[...]) + b_ref[...]


def _mod_table(silu_rows, mod_w, mod_b, tn=1024):
    n_l, k, n = mod_w.shape
    return pl.pallas_call(
        _mod_kernel,
        grid=(n_l, n // tn),
        in_specs=[pl.BlockSpec((MOD_ROWS, k), lambda l, j: (0, 0)),
                  pl.BlockSpec((None, k, tn), lambda l, j: (l, 0, j)),
                  pl.BlockSpec((None, 1, tn), lambda l, j: (l, 0, j))],
        out_specs=pl.BlockSpec((None, MOD_ROWS, tn), lambda l, j: (l, 0, j)),
        out_shape=jax.ShapeDtypeStruct((n_l, MOD_ROWS, n), _f32),
        compiler_params=_cparams(2), name="mod_table",
    )(silu_rows, mod_w, mod_b.reshape(n_l, 1, n))


def _norm_mod_kernel(x_ref, g_ref, sc_ref, sh_ref, o_ref, *, tm, st):
    row = _mod_row_of_chunk(pl.program_id(0) * (tm // CHUNK), st)
    x = x_ref[...]
    inv = lax.rsqrt(jnp.mean(x * x, axis=-1, keepdims=True) + EPS)
    sc = sc_ref[pl.ds(row, 1), :]
    sh = sh_ref[pl.ds(row, 1), :]
    o_ref[...] = (x * inv * g_ref[...] * (1.0 + sc) + sh).astype(o_ref.dtype)


def _norm_mod(x, g, sc, sh, st, tm=256):
    m, d = x.shape
    return pl.pallas_call(
        functools.partial(_norm_mod_kernel, tm=tm, st=st),
        grid=(m // tm,),
        in_specs=[pl.BlockSpec((tm, d), lambda i: (i, 0)),
                  pl.BlockSpec((1, d), lambda i: (0, 0)),
                  pl.BlockSpec((MOD_ROWS, d), lambda i: (0, 0)),
                  pl.BlockSpec((MOD_ROWS, d), lambda i: (0, 0))],
        out_specs=pl.BlockSpec((tm, d), lambda i: (i, 0)),
        out_shape=jax.ShapeDtypeStruct((m, d), _bf16),
        compiler_params=_cparams(1), name="norm_mod",
    )(x, g.reshape(1, d), sc, sh)


def _rmsnorm_kernel(x_ref, g_ref, o_ref):
    x = x_ref[...]
    o_ref[...] = x * lax.rsqrt(jnp.mean(x * x, axis=-1, keepdims=True) + EPS) * g_ref[...]


def _rmsnorm(x, g, tm=256):
    m, d = x.shape
    return pl.pallas_call(
        _rmsnorm_kernel,
        grid=(m // tm,),
        in_specs=[pl.BlockSpec((tm, d), lambda i: (i, 0)), pl.BlockSpec((1, d), lambda i: (0, 0))],
        out_specs=pl.BlockSpec((tm, d), lambda i: (i, 0)),
        out_shape=jax.ShapeDtypeStruct((m, d), _f32),
        compiler_params=_cparams(1), name="final_norm",
    )(x, g.reshape(1, d))


def _mm_kernel(a_ref, w_ref, o_ref):
    o_ref[...] = _dot(a_ref[...], w_ref[...])


def _mm_res_kernel(a_ref, w_ref, res_ref, gate_ref, o_ref, *, tm, st):
    row = _mod_row_of_chunk(pl.program_id(1) * (tm // CHUNK), st)
    gate = gate_ref[pl.ds(row, 1), :]
    o_ref[...] = res_ref[...] + gate * _dot(a_ref[...], w_ref[...])


def _matmul(a, w, tm=512, tn=512):
    m, k = a.shape
    n = w.shape[1]
    tm = min(tm, m)
    return pl.pallas_call(
        _mm_kernel,
        grid=(n // tn, m // tm),
        in_specs=[pl.BlockSpec((tm, k), lambda j, i: (i, 0)),
                  pl.BlockSpec((k, tn), lambda j, i: (0, j))],
        out_specs=pl.BlockSpec((tm, tn), lambda j, i: (i, j)),
        out_shape=jax.ShapeDtypeStruct((m, n), _f32),
        compiler_params=_cparams(2), name="matmul",
    )(a, w)


def _matmul_residual(a, w, res, gate, st, tm=512, tn=512):
    m, k = a.shape
    n = w.shape[1]
    tm = min(tm, m)
    return pl.pallas_call(
        functools.partial(_mm_res_kernel, tm=tm, st=st),
        grid=(n // tn, m // tm),
        in_specs=[pl.BlockSpec((tm, k), lambda j, i: (i, 0)),
                  pl.BlockSpec((k, tn), lambda j, i: (0, j)),
                  pl.BlockSpec((tm, tn), lambda j, i: (i, j)),
                  pl.BlockSpec((MOD_ROWS, tn), lambda j, i: (0, j))],
        out_specs=pl.BlockSpec((tm, tn), lambda j, i: (i, j)),
        out_shape=jax.ShapeDtypeStruct((m, n), _f32),
        compiler_params=_cparams(2), name="matmul_residual",
    )(a, w, res, gate)


def _ffn_kernel(te_ref, tv_ref, x_ref, w1_ref, w3_ref, w2_ref, o_ref):
    i, j = pl.program_id(0), pl.program_id(1)

    @pl.when(j == 0)
    def _():
        o_ref[...] = jnp.zeros_like(o_ref)

    @pl.when(tv_ref[i] > 0)
    def _():
        x = x_ref[...]
        a = _dot(x, w1_ref[...])
        b = _dot(x, w3_ref[...])
        o_ref[...] += _dot(a * jax.nn.sigmoid(a) * b, w2_ref[...])


def _grouped_ffn(x, w1, w3, w2, tile_expert, tile_valid, tm=512, tf=512):
    m, d = x.shape
    f = w1.shape[2]
    nj = f // tf
    n_tiles = m // tm

    def jeff(i, j, tv):
        return jnp.where(tv[i] > 0, j, nj - 1)

    return pl.pallas_call(
        _ffn_kernel,
        grid_spec=pltpu.PrefetchScalarGridSpec(
            num_scalar_prefetch=2,
            grid=(n_tiles, nj),
            in_specs=[pl.BlockSpec((tm, d), lambda i, j, te, tv: (i, 0)),
                      pl.BlockSpec((None, d, tf), lambda i, j, te, tv: (te[i], 0, jeff(i, j, tv))),
                      pl.BlockSpec((None, d, tf), lambda i, j, te, tv: (te[i], 0, jeff(i, j, tv))),
                      pl.BlockSpec((None, tf, d), lambda i, j, te, tv: (te[i], jeff(i, j, tv), 0))],
            out_specs=pl.BlockSpec((tm, d), lambda i, j, te, tv: (i, 0))),
        out_shape=jax.ShapeDtypeStruct((m, d), _f32),
        compiler_params=_cparams(2), name="grouped_ffn",
    )(tile_expert, tile_valid, x, w1, w3, w2)


def _token_shift_chunk(x, prev, nxt, lane0, is_lat, first, last):
    n = x.shape[1]
    row = lax.broadcasted_iota(jnp.int32, x.shape, 0)
    ch = lax.broadcasted_iota(jnp.int32, x.shape, 1) + lane0
    down = pltpu.roll(x, 1, 0)
    up = pltpu.roll(x, CHUNK - 1, 0)
    zero_row = jnp.zeros((1, n), _f32)
    row_prev = jnp.where(first, zero_row, prev[CHUNK - 1:CHUNK, :])
    row_next = jnp.where(last, zero_row, nxt[0:1, :])
    ctx = jnp.where(ch < A_IN // 2,
                    jnp.where(row == 0, row_prev, down),
                    jnp.where(row == CHUNK - 1, row_next, up))
    q = A_IN // 4
    lat = jnp.where(ch < q, jnp.where(row == 0, 0.0, down),
                    jnp.where(ch < 2 * q, jnp.where(row == CHUNK - 1, 0.0, up),
                              jnp.where(ch < 3 * q, jnp.where(first, 0.0, prev),
                                        jnp.where(last, 0.0, nxt))))
    return jnp.where(is_lat, lat, ctx)


def _rwkv_prep_kernel(xm_ref, pm_ref, nm_ref, xl_ref, pl_ref, nl_ref, mum_ref, mul_ref,
                      ww_ref, wa_ref, wg_ref, par_ref, bd_ref,
                      rt_ref, kt_ref, at_ref, bt_ref, kh_ref, bh_ref, v_ref, plast_ref, bonus_ref, g_ref,
                      *, st):
    s = pl.program_id(0)
    is_lat, nc, c, _ = _chunk_pos(s, st)
    first, last = c == 0, c == nc - 1
    xm, xl = xm_ref[...], xl_ref[...]
    sm = _token_shift_chunk(xm, pm_ref[...], nm_ref[...], 0, is_lat, first, last)
    sl = _token_shift_chunk(xl, pl_ref[...], nl_ref[...], EV_MAIN, is_lat, first, last)
    pm = xm + mum_ref[...] * (sm - xm)
    lora = xl + mul_ref[...] * (sl - xl)
    r, k, v = pm[:, :D_A], pm[:, D_A:2 * D_A], pm[:, 2 * D_A:]
    par = par_ref[...]
    k_k, k_a, r_k = par[4:5], par[5:6], par[6:7]
    bd = bd_ref[...]
    kk = k * k_k
    kk = kk / jnp.maximum(jnp.sqrt(_dot_exact_rhs(kk * kk, bd)), 1e-12)
    g_ref[...] = _dot(jax.nn.sigmoid(lora), wg_ref[...])
    v_ref[...] = v.astype(_bf16)
    tanh_l = jnp.tanh(lora)
    kdir_sum = jnp.zeros_like(k)
    for d in range(2):
        w = -jax.nn.softplus(-(par[d:d + 1] + _dot3(tanh_l, ww_ref[d, 0], ww_ref[d, 1]))) - 0.5
        ld = -jnp.exp(w)
        a = jax.nn.sigmoid(par[2 + d:3 + d] + _dot3(lora, wa_ref[d, 0], wa_ref[d, 1]))
        kdir = k * (1.0 + (a - 1.0) * k_a)
        kdir_sum = kdir_sum + kdir
        incl, _ = _tri_masks(d)
        cum = _dot_exact_lhs(jnp.where(incl, 1.0, 0.0), ld)
        tot = jnp.sum(ld, axis=0, keepdims=True)
        e_pos, e_neg, e_end = jnp.exp(cum), jnp.exp(-cum), jnp.exp(tot - cum)
        kka = kk * a
        rt_ref[d] = (r * e_pos).astype(_bf16)
        kt_ref[d] = (kdir * e_neg).astype(_bf16)
        at_ref[d] = (kk * jnp.exp(cum - ld)).astype(_bf16)
        bt_ref[d] = (kka * e_neg).astype(_bf16)
        kh_ref[d] = (kdir * e_end).astype(_bf16)
        bh_ref[d] = (kka * e_end).astype(_bf16)
        plast_ref[d] = jnp.exp(tot)
    bonus_ref[...] = _dot_exact_rhs(r * kdir_sum * r_k, bd) * v


def _rwkv_prep(p, st, mu_main, mu_lora, ww, wa, wg, par, bd):
    n = p.shape[0]
    nch = st.n_chunks
    cur = lambda s: (s, 0)
    prv = lambda s: (jnp.maximum(s - 1, 0), 0)
    nxt = lambda s: (jnp.minimum(s + 1, nch - 1), 0)
    lo = lambda f: (lambda s: (f(s)[0], EV_LORA_BLK))
    main = lambda f: pl.BlockSpec((CHUNK, EV_MAIN), f)
    lora = lambda f: pl.BlockSpec((CHUNK, EV_LORA_W), lo(f))
    full = lambda a: pl.BlockSpec(a.shape, lambda s: (0,) * a.ndim)
    seq2 = pl.BlockSpec((2, CHUNK, D_A), lambda s: (0, s, 0))
    seq1 = pl.BlockSpec((CHUNK, D_A), lambda s: (s, 0))
    bf2 = jax.ShapeDtypeStruct((2, n, D_A), _bf16)
    return pl.pallas_call(
        functools.partial(_rwkv_prep_kernel, st=st),
        grid=(nch,),
        in_specs=[main(cur), main(prv), main(nxt), lora(cur), lora(prv), lora(nxt),
                  full(mu_main), full(mu_lora), full(ww), full(wa), full(wg), full(par), full(bd)],
        out_specs=[seq2] * 6 + [seq1, pl.BlockSpec((2, None, 1, D_A), lambda s: (0, s, 0, 0)), seq1, seq1],
        out_shape=[bf2] * 6 + [jax.ShapeDtypeStruct((n, D_A), _bf16),
                               jax.ShapeDtypeStruct((2, nch, 1, D_A), _f32),
                               jax.ShapeDtypeStruct((n, D_A), _f32),
                               jax.ShapeDtypeStruct((n, D_A), _f32)],
        compiler_params=_cparams(1), name="rwkv_prep",
    )(p, p, p, p, p, p, mu_main, mu_lora, ww, wa, wg, par, bd)


N_PAIRS = H_A // 2
PAIR = 2 * CHUNK


def _stack_masked(x, keep_first):
    return jnp.concatenate([x * keep_first, x * (1 - keep_first)], axis=0)


def _rwkv_scan_kernel(*refs, st):
    (rt_f, kt_f, at_f, bt_f, kh_f, bh_f, v_f, pl_f,
     rt_b, kt_b, at_b, bt_b, kh_b, bh_b, v_b, pl_b, s0_ref, yf_ref, yb_ref, sout_ref, s_scr) = refs
    s = pl.program_id(0)
    _, nc, c, _ = _chunk_pos(s, st)

    @pl.when(c == 0)
    def _():
        s_scr[...] = s0_ref[...]

    dirs = ((rt_f, kt_f, at_f, bt_f, kh_f, bh_f, v_f, pl_f, yf_ref),
            (rt_b, kt_b, at_b, bt_b, kh_b, bh_b, v_b, pl_b, yb_ref))
    chains = [(d, p) for d in range(2) for p in range(N_PAIRS)]
    masks = [_tri_masks(d, PAIR, CHUNK) for d in range(2)]
    eye = jnp.where(masks[0][0] & masks[1][0], 1.0, 0.0)

    lane = lax.broadcasted_iota(jnp.int32, (CHUNK, LANES), 1)
    keep_first = jnp.where(lane < HD_A, 1.0, 0.0).astype(_bf16)

    def load(ref, p):
        return _stack_masked(ref[:, p * LANES:(p + 1) * LANES], keep_first)

    ops = {}
    for d, p in chains:
        rt, kt, at, bt, kh, bh, v = (load(dirs[d][j], p) for j in range(7))
        ops[d, p] = dict(rt=rt, at=at, bh=bh, kh=kh, v=v,
                         big=_dot_nt(jnp.concatenate([at, rt], axis=0), jnp.concatenate([kt, bt], axis=0)))
    for d, p in chains:
        o = ops[d, p]
        incl, strict = masks[d]
        big = o.pop("big")
        o["mak"] = jnp.where(strict, big[:PAIR, :PAIR], 0.0)
        o["n"] = jnp.where(strict, big[:PAIR, PAIR:], 0.0)
        o["mrk"] = jnp.where(incl, big[PAIR:, :PAIR], 0.0)
        o["mrb"] = jnp.where(incl, big[PAIR:, PAIR:], 0.0)
        o["t"] = eye - o["n"]
    for _ in range(5):
        for ch in chains:
            ops[ch]["n"] = _dot(ops[ch]["n"], ops[ch]["n"])
        for ch in chains:
            ops[ch]["t"] = ops[ch]["t"] + _dot(ops[ch]["t"], ops[ch]["n"])
    for ch in chains:
        o = ops[ch]
        o["mv"] = _dot(jnp.concatenate([o.pop("mak"), o.pop("mrk")], axis=0), o["v"])
    for ch in chains:
        o = ops[ch]
        o["xz"] = _dot(o.pop("t"), jnp.concatenate([o["at"], o["mv"][:PAIR].astype(_bf16)], axis=1))
    for ch in chains:
        o = ops[ch]
        o["qy"] = _dot(o.pop("mrb"), o["xz"])
    for ch in chains:
        o = ops[ch]
        o["upd"] = _dot_tn(o["xz"], o["bh"])
        o["vtk"] = _dot_tn(o["v"], o["kh"])
    for d, p in chains:
        o = ops[d, p]
        s0 = s_scr[d, p]
        q_eff = o["rt"].astype(_f32) - o["qy"][:, :LANES]
        y = _dot_nt(q_eff, s0) + o["mv"][PAIR:] - o["qy"][:, LANES:]
        dirs[d][8][:, p * LANES:(p + 1) * LANES] = y[:CHUNK] + y[CHUNK:]
        p_last = dirs[d][7][:, p * LANES:(p + 1) * LANES]
        s_scr[d, p] = s0 * p_last - _dot(s0, o["upd"][:LANES]) + o["vtk"] - o["upd"][LANES:]

    @pl.when(c == nc - 1)
    def _():
        sout_ref[...] = s_scr[...]


def _rwkv_scan(prep, s0_bd, st):
    rt, kt, at, bt, kh, bh, v, p_last = prep
    n = v.shape[0]
    fwd = lambda s: s
    bwd = lambda s: _bwd_chunk(s, st)
    seq2 = lambda d, f: pl.BlockSpec((None, CHUNK, D_A), lambda s: (d, f(s), 0))
    seq1 = lambda f: pl.BlockSpec((CHUNK, D_A), lambda s: (f(s), 0))
    pls = lambda d, f: pl.BlockSpec((None, None, 1, D_A), lambda s: (d, f(s), 0, 0))
    st_spec = pl.BlockSpec((None, 2, N_PAIRS, LANES, LANES), lambda s: (_seq_of(s, st), 0, 0, 0, 0))
    side = lambda d, f: [seq2(d, f)] * 6 + [seq1(f), pls(d, f)]
    y_shape = jax.ShapeDtypeStruct((n, D_A), _f32)
    args = (rt, kt, at, bt, kh, bh, v, p_last)
    return pl.pallas_call(
        functools.partial(_rwkv_scan_kernel, st=st),
        grid=(st.n_chunks,),
        in_specs=side(0, fwd) + side(1, bwd) + [st_spec],
        out_specs=[seq1(fwd), seq1(bwd), st_spec],
        out_shape=[y_shape, y_shape, jax.ShapeDtypeStruct(s0_bd.shape, _f32)],
        scratch_shapes=[pltpu.VMEM((2, N_PAIRS, LANES, LANES), _f32)],
        compiler_params=_cparams(1), name="rwkv_scan",
    )(*args, *args, s0_bd)


def _gla_kernel(q_f, k_f, v_f, l_f, q_b, k_b, v_b, l_b, w_ref, bias_ref, s0_ref,
                of_ref, ob_ref, sout_ref, s_scr, *, st):
    s = pl.program_id(0)
    _, nc, c, _ = _chunk_pos(s, st)

    @pl.when(c == 0)
    def _():
        s_scr[...] = s0_ref[...]

    dirs = ((q_f, k_f, v_f, l_f, of_ref), (q_b, k_b, v_b, l_b, ob_ref))
    work = []
    for d, (q_ref, k_ref, v_ref, l_ref, o_ref) in enumerate(dirs):
        incl, _ = _tri_masks(d)
        la = jax.nn.log_sigmoid(_dot3(l_ref[...], w_ref[d, 0], w_ref[d, 1]) + bias_ref[d:d + 1]) / GATE_NORM
        cum = _dot_exact_lhs(jnp.where(incl, 1.0, 0.0), la)
        tot = jnp.sum(la, axis=0, keepdims=True)
        k = k_ref[...]
        q_in = q_ref[...] * (DK_B ** -0.5) * jnp.exp(cum)
        k_in = k * jnp.exp(-cum)
        k_end = k * jnp.exp(tot - cum)
        e_tot = jnp.exp(tot)
        for h in range(H_B):
            ks = slice(h * DK_B, (h + 1) * DK_B)
            work.append((d, h, incl, q_in[:, ks], k_in[:, ks], k_end[:, ks], e_tot[:, ks]))
    att = [jnp.where(incl, _dot_nt(q_in, k_in), 0.0) for (_, _, incl, q_in, k_in, _, _) in work]
    for (d, h, _, q_in, _, k_end, e_tot), a in zip(work, att):
        v_ref, o_ref = dirs[d][2], dirs[d][4]
        vs = slice(h * DV_B, (h + 1) * DV_B)
        v = v_ref[:, vs]
        s_t = s_scr[d, h]
        o_ref[:, vs] = _dot(a, v) + _dot_nt(q_in, s_t)
        s_scr[d, h] = s_t * e_tot + _dot_tn(v, k_end)

    @pl.when(c == nc - 1)
    def _():
        sout_ref[...] = s_scr[...]


def _gla_scan(p, w, bias, s0_t, st):
    n = p.shape[0]
    fwd = lambda s: s
    bwd = lambda s: _bwd_chunk(s, st)
    blk = lambda width, col, f: pl.BlockSpec((CHUNK, width), lambda s: (f(s), col))
    side = lambda f: [blk(QK_B, EV_Q_BLK, f), blk(QK_B, EV_K_BLK, f), blk(V_B, EV_V_BLK, f),
                      blk(EV_LORA_W, EV_LORA_BLK, f)]
    full = lambda a: pl.BlockSpec(a.shape, lambda s: (0,) * a.ndim)
    st_spec = pl.BlockSpec((None, 2, H_B, DV_B, DK_B), lambda s: (_seq_of(s, st), 0, 0, 0, 0))
    o_spec = lambda f: pl.BlockSpec((CHUNK, V_B), lambda s: (f(s), 0))
    o_shape = jax.ShapeDtypeStruct((n, V_B), _f32)
    return pl.pallas_call(
        functools.partial(_gla_kernel, st=st),
        grid=(st.n_chunks,),
        in_specs=side(fwd) + side(bwd) + [full(w), full(bias), st_spec],
        out_specs=[o_spec(fwd), o_spec(bwd), st_spec],
        out_shape=[o_shape, o_shape, jax.ShapeDtypeStruct(s0_t.shape, _f32)],
        scratch_shapes=[pltpu.VMEM((2, H_B, DV_B, DK_B), _f32)],
        compiler_params=_cparams(1), name="gla_scan",
    )(p, p, p, p, p, p, p, p, w, bias, s0_t)


def _mlstm_kernel(q_f, k_f, v_f, g_f, q_b, k_b, v_b, g_b, bias_ref, c0_ref, n0_ref, m0_ref,
                  hf_ref, hb_ref, cout_ref, nout_ref, mout_ref, c_scr, n_scr, m_scr, *, st):
    s = pl.program_id(0)
    _, nc, c, _ = _chunk_pos(s, st)

    @pl.when(c == 0)
    def _():
        c_scr[...] = c0_ref[...]
        n_scr[...] = n0_ref[...]
        m_scr[...] = m0_ref[...]

    L = CHUNK
    lane = lax.broadcasted_iota(jnp.int32, (L, LANES), 1)
    dirs = ((q_f, k_f, v_f, g_f, hf_ref), (q_b, k_b, v_b, g_b, hb_ref))
    work = []
    for d, (q_ref, k_ref, v_ref, g_ref, h_ref) in enumerate(dirs):
        gates = g_ref[...] + bias_ref[0:1]
        gates = jnp.where(lane < 2 * H_C, gates, jax.nn.log_sigmoid(gates))
        gates_t = gates.T
        incl, _ = _tri_masks(d)
        incl_t = _tri_masks(1 - d)[0]
        for h in range(H_C):
            ji, jf = d * H_C + h, 2 * H_C + d * H_C + h
            i_col, f_col = gates[:, ji:ji + 1], gates[:, jf:jf + 1]
            i_row, f_row = gates_t[ji:ji + 1, :], gates_t[jf:jf + 1, :]
            cum_col = jnp.sum(jnp.where(incl, f_row, 0.0), axis=1, keepdims=True)
            cum_row = jnp.sum(jnp.where(incl_t, f_col, 0.0), axis=0, keepdims=True)
            f_tot = jnp.sum(f_row, axis=1, keepdims=True)
            m_prev = m_scr[d, h]
            log_d = jnp.where(incl, cum_col - cum_row + i_row, NEG_BIG)
            log_inter = cum_col + m_prev
            m_t = jnp.maximum(log_inter, jnp.max(log_d, axis=1, keepdims=True))
            d_mat = jnp.exp(log_d - m_t)
            w_inter = jnp.exp(log_inter - m_t)
            m_new = m_t[L - 1:L, :] if d == 0 else m_t[0:1, :]
            w_s = jnp.exp(f_tot - cum_col + i_col - m_new)
            dec = jnp.exp(f_tot + m_prev - m_new)
            ks = slice(h * DK_C, (h + 1) * DK_C)
            q = q_ref[:, ks] * (DK_C ** -0.5)
            k = k_ref[:, ks]
            work.append(dict(d=d, h=h, q=q, k=k, d_mat=d_mat, w_inter=w_inter, m_t=m_t, m_new=m_new,
                             kw=k * w_s, dec=dec, s=_dot_nt(q, k)))
    for o in work:
        d, h = o["d"], o["h"]
        v_ref, h_ref = dirs[d][2], dirs[d][4]
        vs = slice(h * DV_C, (h + 1) * DV_C)
        v = v_ref[:, vs]
        c_t = c_scr[d, h]
        n_row = n_scr[d, h]
        sm = o["s"] * o["d_mat"]
        num = o["w_inter"] * _dot_nt(o["q"], c_t) + _dot(sm, v)
        den = o["w_inter"] * jnp.sum(o["q"] * n_row, axis=1, keepdims=True) + jnp.sum(sm, axis=1, keepdims=True)
        h_ref[:, vs] = num / jnp.maximum(jnp.abs(den), jnp.exp(-o["m_t"]))
        c_scr[d, h] = o["dec"] * c_t + _dot_tn(v, o["kw"])
        n_scr[d, h] = o["dec"] * n_row + jnp.sum(o["kw"], axis=0, keepdims=True)
        m_scr[d, h] = o["m_new"]

    @pl.when(c == nc - 1)
    def _():
        cout_ref[...] = c_scr[...]
        nout_ref[...] = n_scr[...]
        mout_ref[...] = m_scr[...]


def _mlstm_scan(p, bias, c0_t, n0, m0, st):
    n = p.shape[0]
    fwd = lambda s: s
    bwd = lambda s: _bwd_chunk(s, st)
    blk = lambda width, col, f: pl.BlockSpec((CHUNK, width), lambda s: (f(s), col))
    side = lambda f: [blk(QK_C, 0, f), blk(QK_C, 1, f), blk(V_C, 1, f), blk(LANES, OD_GATE_BLK, f)]
    stt = lambda r, w: pl.BlockSpec((None, 2, H_C, r, w), lambda s: (_seq_of(s, st), 0, 0, 0, 0))
    o_spec = lambda f: pl.BlockSpec((CHUNK, V_C), lambda s: (f(s), 0))
    o_shape = jax.ShapeDtypeStruct((n, V_C), _f32)
    return pl.pallas_call(
        functools.partial(_mlstm_kernel, st=st),
        grid=(st.n_chunks,),
        in_specs=side(fwd) + side(bwd) + [pl.BlockSpec(bias.shape, lambda s: (0, 0)),
                                          stt(DV_C, DK_C), stt(1, DK_C), stt(1, 1)],
        out_specs=[o_spec(fwd), o_spec(bwd), stt(DV_C, DK_C), stt(1, DK_C), stt(1, 1)],
        out_shape=[o_shape, o_shape, jax.ShapeDtypeStruct(c0_t.shape, _f32),
                   jax.ShapeDtypeStruct(n0.shape, _f32), jax.ShapeDtypeStruct(m0.shape, _f32)],
        scratch_shapes=[pltpu.VMEM((2, H_C, DV_C, DK_C), _f32), pltpu.VMEM((2, H_C, 1, DK_C), _f32),
                        pltpu.VMEM((2, H_C, 1, 1), _f32)],
        compiler_params=_cparams(1), name="mlstm_scan",
    )(p, p, p, p, p, p, p, p, bias, c0_t, n0, m0)


def _head_rms_gate(o, g_row, gate, n_heads, width):
    outs = []
    for h in range(n_heads):
        hs = slice(h * width, (h + 1) * width)
        oh = o[:, hs]
        outs.append(oh * lax.rsqrt(jnp.mean(oh * oh, axis=-1, keepdims=True) + EPS) * g_row[:, hs] * gate[:, hs])
    return jnp.concatenate(outs, axis=1)


def _even_post_kernel(yf_ref, yb_ref, bonus_ref, g_ref, of_ref, ob_ref, og_ref, par_ref, gn_ref, bd_ref, o_ref):
    y = yf_ref[...] + yb_ref[...]
    bd = bd_ref[...]
    mean = _dot_exact_rhs(y, bd) * (1.0 / HD_A)
    yc = y - mean
    var = _dot_exact_rhs(yc * yc, bd) * (1.0 / HD_A)
    par = par_ref[...]
    ya = (yc * lax.rsqrt(var + EPS) * par[0:1] + par[1:2] + bonus_ref[...]) * g_ref[...]
    og = og_ref[...]
    yb = _head_rms_gate(of_ref[...] + ob_ref[...], gn_ref[...], og * jax.nn.sigmoid(og), H_B, DV_B)
    o_ref[:, :D_A] = ya.astype(_bf16)
    o_ref[:, D_A:] = yb.astype(_bf16)


def _even_post(yf, yb, bonus, g, of, ob, p, ln_par, gla_g, bd, tm=256):
    n = yf.shape[0]
    row = lambda w, col=0: pl.BlockSpec((tm, w), lambda i: (i, col))
    full = lambda a: pl.BlockSpec(a.shape, lambda i: (0,) * a.ndim)
    return pl.pallas_call(
        _even_post_kernel,
        grid=(n // tm,),
        in_specs=[row(D_A)] * 4 + [row(V_B), row(V_B), row(V_B, EV_OG_BLK), full(ln_par), full(gla_g), full(bd)],
        out_specs=row(D_A + V_B),
        out_shape=jax.ShapeDtypeStruct((n, D_A + V_B), _bf16),
        compiler_params=_cparams(1), name="even_post",
    )(yf, yb, bonus, g, of, ob, p, ln_par, gla_g, bd)


def _odd_post_kernel(hf_ref, hb_ref, o_ref_in, gn_ref, out_ref):
    out_ref[...] = _head_rms_gate(hf_ref[...] + hb_ref[...], gn_ref[...], jax.nn.sigmoid(o_ref_in[...]),
                                  H_C, DV_C).astype(_bf16)


def _odd_post(hf, hb, p, gn, tm=256):
    n = hf.shape[0]
    row = lambda col=0: pl.BlockSpec((tm, V_C), lambda i: (i, col))
    return pl.pallas_call(
        _odd_post_kernel,
        grid=(n // tm,),
        in_specs=[row(), row(), row(2), pl.BlockSpec(gn.shape, lambda i: (0, 0))],
        out_specs=row(),
        out_shape=jax.ShapeDtypeStruct((n, V_C), _bf16),
        compiler_params=_cparams(1), name="odd_post",
    )(hf, hb, p, gn)


def _pair_block_diag(s):
    lead = s.shape[:-3]
    s = s.reshape(lead + (N_PAIRS, 2, HD_A, HD_A))
    bd = jnp.einsum('...avk,ab->...avbk', s, jnp.eye(2, dtype=s.dtype))
    return bd.reshape(lead + (N_PAIRS, LANES, LANES))


def _pair_block_diag_inv(bd):
    lead = bd.shape[:-3]
    bd = bd.reshape(lead + (N_PAIRS, 2, HD_A, 2, HD_A))
    s = jnp.einsum('...avbk,ab->...avk', bd, jnp.eye(2, dtype=bd.dtype))
    return s.reshape(lead + (H_A, HD_A, HD_A))


def _even_mixer(p, st, s_rwkv, s_gla, mu, w0, w2, a0, a2, g2, k_k, k_a, r_k, ln_g, ln_b, gla_w2, gla_b, gla_g):
    zpad = lambda x, rows_before, total: jnp.pad(x, ((rows_before, total - rows_before - x.shape[0]), (0, 0)))
    mu_main = mu[:EV_MAIN].reshape(1, EV_MAIN)
    mu_lora = jnp.pad(mu[EV_MAIN:], (0, EV_LORA_W - (A_IN - EV_MAIN))).reshape(1, EV_LORA_W)
    split = lambda w: jnp.stack(_split2(w), axis=0)
    ww = jnp.stack([split(zpad(w2[d], d * LORA_W, EV_LORA_W)) for d in range(2)])
    wa = jnp.stack([split(zpad(a2[d], 2 * LORA_W + d * LORA_A, EV_LORA_W)) for d in range(2)])
    wg = zpad(g2, 2 * LORA_W + 2 * LORA_A, EV_LORA_W).astype(_bf16)
    par = jnp.stack([w0[0], w0[1], a0[0], a0[1], k_k, k_a, r_k, jnp.zeros_like(k_k)])
    head = np.arange(D_A) // HD_A
    bd = jnp.asarray(head[:, None] == head[None, :], _bf16)
    *prep, bonus, g = _rwkv_prep(p, st, mu_main, mu_lora, ww, wa, wg, par, bd)
    yf, yb, s_rw = _rwkv_scan(prep, _pair_block_diag(s_rwkv), st)
    wl = jnp.stack([split(zpad(gla_w2[d], GLA_LORA_OFF + d * LORA_GK, EV_LORA_W)) for d in range(2)])
    bias = jnp.pad(gla_b, ((0, MOD_ROWS - 2), (0, 0)))
    of, ob, s_gl = _gla_scan(p, wl, bias, jnp.swapaxes(s_gla, -1, -2), st)
    ln_par = jnp.pad(jnp.stack([ln_g, ln_b]), ((0, MOD_ROWS - 2), (0, 0)))
    y_in = _even_post(yf, yb, bonus, g, of, ob, p, ln_par, gla_g.reshape(1, V_B), bd)
    return y_in, _pair_block_diag_inv(s_rw), jnp.swapaxes(s_gl, -1, -2)


def _odd_mixer(p, st, s_c, s_n, s_m, b_i, b_f, g):
    bias = jnp.concatenate([b_i.reshape(-1), b_f.reshape(-1), jnp.zeros((LANES - 4 * H_C,), _f32)])
    bias = jnp.pad(bias[None, :], ((0, MOD_ROWS - 1), (0, 0)))
    hf, hb, c_t, n_new, m_new = _mlstm_scan(p, bias, jnp.swapaxes(s_c, -1, -2), s_n[..., None, :],
                                            s_m[..., None, None], st)
    y_in = _odd_post(hf, hb, p, g.reshape(1, V_C))
    return y_in, jnp.swapaxes(c_t, -1, -2), n_new[..., 0, :], m_new[..., 0, 0]


def _moe(h_bf16, router, w1, w3, w2, tm=512):
    n_tok = h_bf16.shape[0]
    n_pairs = n_tok * TOP_K
    n_tiles = n_pairs // tm + N_EXPERTS
    rw = jnp.pad(router, ((0, 0), (0, LANES - N_EXPERTS)))
    r_hi, r_lo = _split2(rw)
    logits = (_matmul(h_bf16, r_hi, tn=LANES) + _matmul(h_bf16, r_lo, tn=LANES))[:, :N_EXPERTS]
    top_v, top_i = lax.top_k(logits, TOP_K)
    gates = jax.nn.softmax(top_v, axis=-1)
    pair_e = top_i.reshape(-1)
    order = jnp.argsort(pair_e, stable=True)
    sorted_e = pair_e[order]
    counts = jnp.sum(jax.nn.one_hot(pair_e, N_EXPERTS, dtype=jnp.int32), axis=0)
    tiles_per_e = (counts + tm - 1) // tm
    tile_start = jnp.cumsum(tiles_per_e) - tiles_per_e
    group_start = jnp.cumsum(counts) - counts
    rank = jnp.arange(n_pairs, dtype=jnp.int32) - group_start[sorted_e]
    dest = tile_start[sorted_e] * tm + rank
    slot_token = jnp.zeros((n_tiles * tm,), jnp.int32).at[dest].set((order // TOP_K).astype(jnp.int32))
    pair_slot = jnp.zeros((n_pairs,), jnp.int32).at[order].set(dest.astype(jnp.int32))
    n_valid = jnp.sum(tiles_per_e)
    tile_ids = jnp.arange(n_tiles, dtype=jnp.int32)
    tile_e = jnp.sum((tile_ids[:, None] >= tile_start[None, :]).astype(jnp.int32), axis=1) - 1
    last_e = jnp.max(jnp.where(counts > 0, jnp.arange(N_EXPERTS), 0))
    tile_valid = (tile_ids < n_valid).astype(jnp.int32)
    tile_e = jnp.where(tile_valid > 0, tile_e, last_e).astype(jnp.int32)
    xg = jnp.take(h_bf16, slot_token, axis=0)
    yg = _grouped_ffn(xg, w1, w3, w2, tile_e, tile_valid, tm=tm)
    yp = jnp.take(yg, pair_slot, axis=0).reshape(n_tok, TOP_K, -1)
    return jnp.sum(yp * gates[..., None], axis=1)


def kernel(x_prompt, x_sample, state_rwkv, state_gla, state_mlstm_c, state_mlstm_n, state_mlstm_m, c, c_ctx, mod_w, mod_b, norm_g, final_g, ev_w_in, ev_mu, rw_w0, rw_w2, rw_a0, rw_a2, rw_g2, rw_kk, rw_ka, rw_rk, rw_ln_g, rw_ln_b, gla_w2, gla_b, gla_norm_g, ev_w_out, ffn_w1, ffn_w3, ffn_w2, od_w_in, ml_b_i, ml_b_f, ml_norm_g, od_w_out, moe_router, moe_w1, moe_w3, moe_w2):
    D = D_MODEL
    Bp, Tp, _ = x_prompt.shape
    Bs, Ts, _ = x_sample.shape
    assert Ts // GRID_W * GRID_W == Ts and GRID_W == CHUNK and Tp % CHUNK == 0
    n_ctx = Bp * Tp
    st = _Streams(n_ctx_chunks=n_ctx // CHUNK, ctx_cps=Tp // CHUNK, lat_cps=Ts // CHUNK,
                  n_chunks=(n_ctx + Bs * Ts) // CHUNK, n_ctx_seq=Bp, n_seq=Bp + Bs)
    x = jnp.concatenate([x_prompt.reshape(n_ctx, D), x_sample.reshape(Bs * Ts, D)], axis=0)
    n_tok = x.shape[0]

    cond = jnp.concatenate([c_ctx[None, :], c, jnp.zeros((MOD_ROWS - 1 - Bs, D), _f32)], axis=0)
    mod = _mod_table(jax.nn.silu(cond).astype(_bf16), mod_w.astype(_bf16), mod_b)

    dense_te = jnp.zeros((n_tok // 512,), jnp.int32)
    dense_tv = jnp.ones((n_tok // 512,), jnp.int32)
    with_ctx = lambda s: jnp.concatenate([jnp.zeros((Bp,) + s.shape[1:], _f32), s], axis=0)
    new_rw, new_gla, new_c, new_n, new_m = [], [], [], [], []

    for l in range(DEPTH):
        i = l // 2
        sh1, sc1, gt1, sh2, sc2, gt2 = [mod[l, :, j * D:(j + 1) * D] for j in range(6)]
        h = _norm_mod(x, norm_g[l, 0], sc1, sh1, st)
        if l % 2 == 0:
            w = ev_w_in[i]
            b0 = A_IN
            w_in = jnp.concatenate([
                w[:, :EV_MAIN], w[:, b0 + 2 * QK_B:b0 + 2 * QK_B + V_B], w[:, b0 + 2 * QK_B + V_B + 2 * LORA_GK:],
                w[:, b0:b0 + 2 * QK_B], w[:, EV_MAIN:A_IN],
                w[:, b0 + 2 * QK_B + V_B:b0 + 2 * QK_B + V_B + 2 * LORA_GK],
                jnp.zeros((D, EV_LORA_W - (A_IN - EV_MAIN) - 2 * LORA_GK), _f32)], axis=1).astype(_bf16)
            p = _matmul(h, w_in)
            y_in, s_rw, s_gl = _even_mixer(
                p, st, with_ctx(state_rwkv[:, i]), with_ctx(state_gla[:, i]), ev_mu[i], rw_w0[i], rw_w2[i],
                rw_a0[i], rw_a2[i], rw_g2[i], rw_kk[i], rw_ka[i], rw_rk[i], rw_ln_g[i], rw_ln_b[i],
                gla_w2[i], gla_b[i], gla_norm_g[i])
            new_rw.append(s_rw[:Bp])
            new_gla.append(s_gl[:Bp])
            x = _matmul_residual(y_in, ev_w_out[i].astype(_bf16), x, gt1, st)
        else:
            w_in = jnp.pad(od_w_in[i], ((0, 0), (0, OD_N - od_w_in.shape[2]))).astype(_bf16)
            p = _matmul(h, w_in, tn=OD_N // 7)
            y_in, c_new, n_new, m_new = _odd_mixer(
                p, st, with_ctx(state_mlstm_c[:, i]), with_ctx(state_mlstm_n[:, i]), with_ctx(state_mlstm_m[:, i]),
                ml_b_i[i], ml_b_f[i], ml_norm_g[i])
            new_c.append(c_new[:Bp])
            new_n.append(n_new[:Bp])
            new_m.append(m_new[:Bp])
            x = _matmul_residual(y_in, od_w_out[i].astype(_bf16), x, gt1, st)

        h = _norm_mod(x, norm_g[l, 1], sc2, sh2, st)
        if l % 2 == 0:
            f = _grouped_ffn(h, ffn_w1[i][None].astype(_bf16), ffn_w3[i][None].astype(_bf16),
                             ffn_w2[i][None].astype(_bf16), dense_te, dense_tv)
        else:
            f = _moe(h, moe_router[i], moe_w1[i].astype(_bf16), moe_w3[i].astype(_bf16),
                     moe_w2[i].astype(_bf16))
        tok_row = jnp.concatenate([jnp.zeros((n_ctx,), jnp.int32),
                                   1 + jnp.arange(Bs * Ts, dtype=jnp.int32) // Ts])
        x = x + gt2[tok_row] * f

    y = _rmsnorm(x, final_g)
    return (y[:n_ctx].reshape(Bp, Tp, D), y[n_ctx:].reshape(Bs, Ts, D),
            jnp.stack(new_rw, axis=1), jnp.stack(new_gla, axis=1),
            jnp.stack(new_c, axis=1), jnp.stack(new_n, axis=1), jnp.stack(new_m, axis=1))
```

```python
import functools
from typing import NamedTuple

import jax
import jax.numpy as jnp
import numpy as np
from jax import lax
from jax.experimental import pallas as pl
from jax.experimental.pallas import tpu as pltpu

D_MODEL = 2048
DEPTH = 4
GRID_W = 64
EPS = 1e-6
CHUNK = 64
H_A, HD_A = 16, 64
D_A = H_A * HD_A
LORA_W, LORA_A, LORA_G = 64, 64, 128
A_IN = 3 * D_A + 2 * LORA_W + 2 * LORA_A + LORA_G
H_B, DK_B, DV_B = 4, 128, 256
QK_B, V_B = H_B * DK_B, H_B * DV_B
LORA_GK = 16
GATE_NORM = 16.0
H_C, DK_C, DV_C = 8, 128, 256
QK_C, V_C = H_C * DK_C, H_C * DV_C
D_FF = 5632
N_EXPERTS = 8
TOP_K = 2

LANES = 128
MOD_ROWS = 8
VMEM_LIMIT = 48 * 1024 * 1024
FFN_VMEM_LIMIT = 56 * 1024 * 1024
NEG_BIG = -1e30

EV_MAIN = 3 * D_A
EV_V_BLK = EV_MAIN // V_B
EV_OG_BLK = EV_V_BLK + 1
EV_Q_BLK = (EV_MAIN + 2 * V_B) // QK_B
EV_K_BLK = EV_Q_BLK + 1
EV_LORA_W = 512
EV_LORA_BLK = (EV_MAIN + 2 * V_B + 2 * QK_B) // EV_LORA_W
EV_N = EV_MAIN + 2 * V_B + 2 * QK_B + EV_LORA_W
GLA_LORA_OFF = 2 * LORA_W + 2 * LORA_A + LORA_G
OD_GATE_BLK = (2 * QK_C + 2 * V_C) // LANES
OD_N = 2 * QK_C + 2 * V_C + LANES

_f32 = jnp.float32
_bf16 = jnp.bfloat16


class _Streams(NamedTuple):
    n_ctx_chunks: int
    ctx_cps: int
    lat_cps: int
    n_chunks: int
    n_ctx_seq: int
    n_seq: int


def _chunk_pos(s, st):
    is_lat = s >= st.n_ctx_chunks
    sl = s - st.n_ctx_chunks
    nc = jnp.where(is_lat, st.lat_cps, st.ctx_cps)
    c = jnp.where(is_lat, sl % st.lat_cps, s % st.ctx_cps)
    seq = jnp.where(is_lat, st.n_ctx_seq + sl // st.lat_cps, s // st.ctx_cps)
    return is_lat, nc, c, seq


def _bwd_chunk(s, st):
    _, nc, c, _ = _chunk_pos(s, st)
    return s - c + (nc - 1 - c)


def _seq_of(s, st):
    return _chunk_pos(s, st)[3]


def _mod_row_of_chunk(s, st):
    is_lat, _, _, seq = _chunk_pos(s, st)
    return jnp.where(is_lat, 1 + seq - st.n_ctx_seq, 0)


def _cparams(n_axes):
    return pltpu.CompilerParams(dimension_semantics=("arbitrary",) * n_axes,
                                vmem_limit_bytes=VMEM_LIMIT)


def _dot(a, b):
    return jnp.dot(a.astype(_bf16), b.astype(_bf16), preferred_element_type=_f32)


def _dot_nt(a, b):
    return lax.dot_general(a.astype(_bf16), b.astype(_bf16), (((1,), (1,)), ((), ())),
                           preferred_element_type=_f32)


def _dot_tn(a, b):
    return lax.dot_general(a.astype(_bf16), b.astype(_bf16), (((0,), (0,)), ((), ())),
                           preferred_element_type=_f32)


def _split2(x):
    hi = x.astype(_bf16)
    return hi, (x - hi.astype(_f32)).astype(_bf16)


def _dot_exact_lhs(tri, x):
    hi = x.astype(_bf16)
    r1 = x - hi.astype(_f32)
    mid = r1.astype(_bf16)
    lo = (r1 - mid.astype(_f32)).astype(_bf16)
    t = tri.astype(_bf16)
    d = lambda v: jnp.dot(t, v, preferred_element_type=_f32)
    return d(hi) + d(mid) + d(lo)


def _dot_exact_rhs(x, ones):
    hi = x.astype(_bf16)
    r1 = x - hi.astype(_f32)
    mid = r1.astype(_bf16)
    lo = (r1 - mid.astype(_f32)).astype(_bf16)
    d = lambda v: jnp.dot(v, ones, preferred_element_type=_f32)
    return d(hi) + d(mid) + d(lo)


def _dot3(a, b_hi, b_lo):
    a_hi, a_lo = _split2(a)
    d = lambda u, v: jnp.dot(u, v, preferred_element_type=_f32)
    return d(a_hi, b_hi) + d(a_lo, b_hi) + d(a_hi, b_lo)


def _tri_masks(direction, n=CHUNK, period=CHUNK):
    t = lax.broadcasted_iota(jnp.int32, (n, n), 0)
    i = lax.broadcasted_iota(jnp.int32, (n, n), 1)
    same = (t // period) == (i // period) if n != period else True
    if direction == 0:
        return (i <= t) & same, (i < t) & same
    return (i >= t) & same, (i > t) & same


def _mod_kernel(a_ref, w_ref, b_ref, o_ref):
    o_ref[...] = _dot(a_ref[...], w_ref[...]) + b_ref[...]


def _mod_table(silu_rows, mod_w, mod_b, tn=1024):
    n_l, k, n = mod_w.shape
    return pl.pallas_call(
        _mod_kernel,
        grid=(n_l, n // tn),
        in_specs=[pl.BlockSpec((MOD_ROWS, k), lambda l, j: (0, 0)),
                  pl.BlockSpec((None, k, tn), lambda l, j: (l, 0, j)),
                  pl.BlockSpec((None, 1, tn), lambda l, j: (l, 0, j))],
        out_specs=pl.BlockSpec((None, MOD_ROWS, tn), lambda l, j: (l, 0, j)),
        out_shape=jax.ShapeDtypeStruct((n_l, MOD_ROWS, n), _f32),
        compiler_params=_cparams(2), name="mod_table",
    )(silu_rows, mod_w, mod_b.reshape(n_l, 1, n))


def _norm_mod_kernel(*refs, tm, st, has_res, has_router):
    refs = list(refs)
    x_ref = refs.pop(0)
    f_ref, gate_ref = (refs.pop(0), refs.pop(0)) if has_res else (None, None)
    g_ref, sc_ref, sh_ref = refs.pop(0), refs.pop(0), refs.pop(0)
    r_hi, r_lo = (refs.pop(0), refs.pop(0)) if has_router else (None, None)
    xo_ref = refs.pop(0) if has_res else None
    h_ref = refs.pop(0)
    row = _mod_row_of_chunk(pl.program_id(0) * (tm // CHUNK), st)
    x = x_ref[...]
    if has_res:
        x = x + gate_ref[pl.ds(row, 1), :] * f_ref[...]
        xo_ref[...] = x
    inv = lax.rsqrt(jnp.mean(x * x, axis=-1, keepdims=True) + EPS)
    h = x * inv * g_ref[...] * (1.0 + sc_ref[pl.ds(row, 1), :]) + sh_ref[pl.ds(row, 1), :]
    h_ref[...] = h.astype(h_ref.dtype)
    if has_router:
        refs.pop(0)[...] = _dot3(h, r_hi[...], r_lo[...])


def _norm_mod(x, g, sc, sh, st, res=None, router=None, tm=256):
    m, d = x.shape
    row = pl.BlockSpec((tm, d), lambda i: (i, 0))
    tab = pl.BlockSpec((MOD_ROWS, d), lambda i: (0, 0))
    args, specs = [x], [row]
    outs, out_specs = [], []
    if res is not None:
        args += list(res)
        specs += [row, tab]
        outs.append(jax.ShapeDtypeStruct((m, d), _f32))
        out_specs.append(row)
    args += [g.reshape(1, d), sc, sh]
    specs += [pl.BlockSpec((1, d), lambda i: (0, 0)), tab, tab]
    outs.append(jax.ShapeDtypeStruct((m, d), _bf16))
    out_specs.append(row)
    if router is not None:
        args += list(router)
        specs += [pl.BlockSpec((d, LANES), lambda i: (0, 0))] * 2
        outs.append(jax.ShapeDtypeStruct((m, LANES), _f32))
        out_specs.append(pl.BlockSpec((tm, LANES), lambda i: (i, 0)))
    return pl.pallas_call(
        functools.partial(_norm_mod_kernel, tm=tm, st=st, has_res=res is not None, has_router=router is not None),
        grid=(m // tm,), in_specs=specs, out_specs=out_specs, out_shape=outs,
        compiler_params=_cparams(1), name="norm_mod",
    )(*args)


def _final_norm_kernel(x_ref, f_ref, gate_ref, g_ref, o_ref, *, tm, st):
    row = _mod_row_of_chunk(pl.program_id(0) * (tm // CHUNK), st)
    x = x_ref[...] + gate_ref[pl.ds(row, 1), :] * f_ref[...]
    o_ref[...] = x * lax.rsqrt(jnp.mean(x * x, axis=-1, keepdims=True) + EPS) * g_ref[...]


def _final_norm(x, f, gate, g, st, tm=256):
    m, d = x.shape
    row = pl.BlockSpec((tm, d), lambda i: (i, 0))
    return pl.pallas_call(
        functools.partial(_final_norm_kernel, tm=tm, st=st),
        grid=(m // tm,),
        in_specs=[row, row, pl.BlockSpec((MOD_ROWS, d), lambda i: (0, 0)), pl.BlockSpec((1, d), lambda i: (0, 0))],
        out_specs=row,
        out_shape=jax.ShapeDtypeStruct((m, d), _f32),
        compiler_params=_cparams(1), name="final_norm",
    )(x, f, gate, g.reshape(1, d))


def _mm_kernel(a_ref, w_ref, o_ref):
    o_ref[...] = _dot(a_ref[...], w_ref[...])


def _mm_res_kernel(a_ref, w_ref, res_ref, gate_ref, o_ref, *, tm, st):
    row = _mod_row_of_chunk(pl.program_id(1) * (tm // CHUNK), st)
    gate = gate_ref[pl.ds(row, 1), :]
    o_ref[...] = res_ref[...] + gate * _dot(a_ref[...], w_ref[...])


def _matmul(a, w, tm=512, tn=512):
    m, k = a.shape
    n = w.shape[1]
    tm = min(tm, m)
    return pl.pallas_call(
        _mm_kernel,
        grid=(n // tn, m // tm),
        in_specs=[pl.BlockSpec((tm, k), lambda j, i: (i, 0)),
                  pl.BlockSpec((k, tn), lambda j, i: (0, j))],
        out_specs=pl.BlockSpec((tm, tn), lambda j, i: (i, j)),
        out_shape=jax.ShapeDtypeStruct((m, n), _f32),
        compiler_params=_cparams(2), name="matmul",
    )(a, w)


def _matmul_residual(a, w, res, gate, st, tm=512, tn=512):
    m, k = a.shape
    n = w.shape[1]
    tm = min(tm, m)
    return pl.pallas_call(
        functools.partial(_mm_res_kernel, tm=tm, st=st),
        grid=(n // tn, m // tm),
        in_specs=[pl.BlockSpec((tm, k), lambda j, i: (i, 0)),
                  pl.BlockSpec((k, tn), lambda j, i: (0, j)),
                  pl.BlockSpec((tm, tn), lambda j, i: (i, j)),
                  pl.BlockSpec((MOD_ROWS, tn), lambda j, i: (0, j))],
        out_specs=pl.BlockSpec((tm, tn), lambda j, i: (i, j)),
        out_shape=jax.ShapeDtypeStruct((m, n), _f32),
        compiler_params=_cparams(2), name="matmul_residual",
    )(a, w, res, gate)


FFN_TM = 1024
FFN_TF = 256
FFN_HALF = 512


def _ffn_kernel(te_ref, tr_ref, x_ref, w1_ref, w3_ref, w2_ref, o_ref, *, tm):
    i, j = pl.program_id(0), pl.program_id(1)
    rows = tr_ref[i]

    @pl.when(j == 0)
    def _():
        o_ref[...] = jnp.zeros_like(o_ref)

    def swiglu(n_rows):
        x = x_ref[0:n_rows, :]
        a = _dot(x, w1_ref[...])
        b = _dot(x, w3_ref[...])
        o_ref[0:n_rows, :] += _dot(a * jax.nn.sigmoid(a) * b, w2_ref[...])

    @pl.when(rows > FFN_HALF)
    def _():
        swiglu(tm)

    @pl.when((rows > 0) & (rows <= FFN_HALF))
    def _():
        swiglu(FFN_HALF)


def _grouped_ffn(x, w1, w3, w2, layer, tile_expert, tile_rows, tm=FFN_TM, tf=FFN_TF):
    m, d = x.shape
    f = w1.shape[3]
    nj = f // tf
    n_tiles = m // tm

    def jeff(i, j, tr):
        return jnp.where(tr[i] > 0, j, nj - 1)

    return pl.pallas_call(
        functools.partial(_ffn_kernel, tm=tm),
        grid_spec=pltpu.PrefetchScalarGridSpec(
            num_scalar_prefetch=2,
            grid=(n_tiles, nj),
            in_specs=[pl.BlockSpec((tm, d), lambda i, j, te, tr: (i, 0)),
                      pl.BlockSpec((None, None, d, tf), lambda i, j, te, tr: (layer, te[i], 0, jeff(i, j, tr))),
                      pl.BlockSpec((None, None, d, tf), lambda i, j, te, tr: (layer, te[i], 0, jeff(i, j, tr))),
                      pl.BlockSpec((None, None, tf, d), lambda i, j, te, tr: (layer, te[i], jeff(i, j, tr), 0))],
            out_specs=pl.BlockSpec((tm, d), lambda i, j, te, tr: (i, 0))),
        out_shape=jax.ShapeDtypeStruct((m, d), _f32),
        compiler_params=pltpu.CompilerParams(dimension_semantics=("arbitrary", "arbitrary"),
                                             vmem_limit_bytes=FFN_VMEM_LIMIT),
        name="grouped_ffn",
    )(tile_expert, tile_rows, x, w1, w3, w2)


def _token_shift_chunk(x, prev, nxt, lane0, is_lat, first, last):
    n = x.shape[1]
    row = lax.broadcasted_iota(jnp.int32, x.shape, 0)
    ch = lax.broadcasted_iota(jnp.int32, x.shape, 1) + lane0
    down = pltpu.roll(x, 1, 0)
    up = pltpu.roll(x, CHUNK - 1, 0)
    zero_row = jnp.zeros((1, n), _f32)
    row_prev = jnp.where(first, zero_row, prev[CHUNK - 1:CHUNK, :])
    row_next = jnp.where(last, zero_row, nxt[0:1, :])
    ctx = jnp.where(ch < A_IN // 2,
                    jnp.where(row == 0, row_prev, down),
                    jnp.where(row == CHUNK - 1, row_next, up))
    q = A_IN // 4
    lat = jnp.where(ch < q, jnp.where(row == 0, 0.0, down),
                    jnp.where(ch < 2 * q, jnp.where(row == CHUNK - 1, 0.0, up),
                              jnp.where(ch < 3 * q, jnp.where(first, 0.0, prev),
                                        jnp.where(last, 0.0, nxt))))
    return jnp.where(is_lat, lat, ctx)


def _rwkv_prep_kernel(xm_ref, pm_ref, nm_ref, xl_ref, pl_ref, nl_ref, mum_ref, mul_ref,
                      ww_ref, wa_ref, wg_ref, par_ref, bd_ref,
                      rt_ref, kt_ref, at_ref, bt_ref, kh_ref, bh_ref, v_ref, plast_ref, bonus_ref, g_ref,
                      *, st):
    s = pl.program_id(0)
    is_lat, nc, c, _ = _chunk_pos(s, st)
    first, last = c == 0, c == nc - 1
    xm, xl = xm_ref[...], xl_ref[...]
    sm = _token_shift_chunk(xm, pm_ref[...], nm_ref[...], 0, is_lat, first, last)
    sl = _token_shift_chunk(xl, pl_ref[...], nl_ref[...], EV_MAIN, is_lat, first, last)
    pm = xm + mum_ref[...] * (sm - xm)
    lora = xl + mul_ref[...] * (sl - xl)
    r, k, v = pm[:, :D_A], pm[:, D_A:2 * D_A], pm[:, 2 * D_A:]
    par = par_ref[...]
    k_k, k_a, r_k = par[4:5], par[5:6], par[6:7]
    bd = bd_ref[...]
    kk = k * k_k
    kk = kk / jnp.maximum(jnp.sqrt(_dot_exact_rhs(kk * kk, bd)), 1e-12)
    g_ref[...] = _dot(jax.nn.sigmoid(lora), wg_ref[...])
    v_ref[...] = v.astype(_bf16)
    tanh_l = jnp.tanh(lora)
    kdir_sum = jnp.zeros_like(k)
    for d in range(2):
        w = -jax.nn.softplus(-(par[d:d + 1] + _dot3(tanh_l, ww_ref[d, 0], ww_ref[d, 1]))) - 0.5
        ld = -jnp.exp(w)
        a = jax.nn.sigmoid(par[2 + d:3 + d] + _dot3(lora, wa_ref[d, 0], wa_ref[d, 1]))
        kdir = k * (1.0 + (a - 1.0) * k_a)
        kdir_sum = kdir_sum + kdir
        incl, _ = _tri_masks(d)
        cum = _dot_exact_lhs(jnp.where(incl, 1.0, 0.0), ld)
        tot = jnp.sum(ld, axis=0, keepdims=True)
        e_pos, e_neg, e_end = jnp.exp(cum), jnp.exp(-cum), jnp.exp(tot - cum)
        kka = kk * a
        rt_ref[d] = (r * e_pos).astype(_bf16)
        kt_ref[d] = (kdir * e_neg).astype(_bf16)
        at_ref[d] = (kk * jnp.exp(cum - ld)).astype(_bf16)
        bt_ref[d] = (kka * e_neg).astype(_bf16)
        kh_ref[d] = (kdir * e_end).astype(_bf16)
        bh_ref[d] = (kka * e_end).astype(_bf16)
        plast_ref[d] = jnp.exp(tot)
    bonus_ref[...] = _dot_exact_rhs(r * kdir_sum * r_k, bd) * v


def _rwkv_prep(p, st, mu_main, mu_lora, ww, wa, wg, par, bd):
    n = p.shape[0]
    nch = st.n_chunks
    cur = lambda s: (s, 0)
    prv = lambda s: (jnp.maximum(s - 1, 0), 0)
    nxt = lambda s: (jnp.minimum(s + 1, nch - 1), 0)
    lo = lambda f: (lambda s: (f(s)[0], EV_LORA_BLK))
    main = lambda f: pl.BlockSpec((CHUNK, EV_MAIN), f)
    lora = lambda f: pl.BlockSpec((CHUNK, EV_LORA_W), lo(f))
    full = lambda a: pl.BlockSpec(a.shape, lambda s: (0,) * a.ndim)
    seq2 = pl.BlockSpec((2, CHUNK, D_A), lambda s: (0, s, 0))
    seq1 = pl.BlockSpec((CHUNK, D_A), lambda s: (s, 0))
    bf2 = jax.ShapeDtypeStruct((2, n, D_A), _bf16)
    return pl.pallas_call(
        functools.partial(_rwkv_prep_kernel, st=st),
        grid=(nch,),
        in_specs=[main(cur), main(prv), main(nxt), lora(cur), lora(prv), lora(nxt),
                  full(mu_main), full(mu_lora), full(ww), full(wa), full(wg), full(par), full(bd)],
        out_specs=[seq2] * 6 + [seq1, pl.BlockSpec((2, None, 1, D_A), lambda s: (0, s, 0, 0)), seq1, seq1],
        out_shape=[bf2] * 6 + [jax.ShapeDtypeStruct((n, D_A), _bf16),
                               jax.ShapeDtypeStruct((2, nch, 1, D_A), _f32),
                               jax.ShapeDtypeStruct((n, D_A), _f32),
                               jax.ShapeDtypeStruct((n, D_A), _f32)],
        compiler_params=_cparams(1), name="rwkv_prep",
    )(p, p, p, p, p, p, mu_main, mu_lora, ww, wa, wg, par, bd)


N_PAIRS = H_A // 2
PAIR = 2 * CHUNK


def _stack_masked(x, keep_first):
    return jnp.concatenate([x * keep_first, x * (1 - keep_first)], axis=0)


def _rwkv_scan_kernel(*refs, st):
    (rt_f, kt_f, at_f, bt_f, kh_f, bh_f, v_f, pl_f,
     rt_b, kt_b, at_b, bt_b, kh_b, bh_b, v_b, pl_b, s0_ref, yf_ref, yb_ref, sout_ref, s_scr) = refs
    s = pl.program_id(0)
    is_lat, nc, c, _ = _chunk_pos(s, st)

    @pl.when(c == 0)
    def _():
        keep = jnp.where(is_lat, 1.0, 0.0)
        zero = jnp.zeros((HD_A, HD_A), _f32)
        for d in range(2):
            for p in range(N_PAIRS):
                top = jnp.concatenate([s0_ref[d, 2 * p], zero], axis=1)
                bot = jnp.concatenate([zero, s0_ref[d, 2 * p + 1]], axis=1)
                s_scr[d, p] = jnp.concatenate([top, bot], axis=0) * keep

    dirs = ((rt_f, kt_f, at_f, bt_f, kh_f, bh_f, v_f, pl_f, yf_ref),
            (rt_b, kt_b, at_b, bt_b, kh_b, bh_b, v_b, pl_b, yb_ref))
    chains = [(d, p) for d in range(2) for p in range(N_PAIRS)]
    masks = [_tri_masks(d, PAIR, CHUNK) for d in range(2)]
    eye = jnp.where(masks[0][0] & masks[1][0], 1.0, 0.0)

    lane = lax.broadcasted_iota(jnp.int32, (CHUNK, LANES), 1)
    keep_first = jnp.where(lane < HD_A, 1.0, 0.0).astype(_bf16)

    def load(ref, p):
        return _stack_masked(ref[:, p * LANES:(p + 1) * LANES], keep_first)

    ops = {}
    for d, p in chains:
        rt, kt, at, bt, kh, bh, v = (load(dirs[d][j], p) for j in range(7))
        ops[d, p] = dict(rt=rt, at=at, bh=bh, kh=kh, v=v,
                         big=_dot_nt(jnp.concatenate([at, rt], axis=0), jnp.concatenate([kt, bt], axis=0)))
    for d, p in chains:
        o = ops[d, p]
        incl, strict = masks[d]
        big = o.pop("big")
        o["mak"] = jnp.where(strict, big[:PAIR, :PAIR], 0.0)
        o["n"] = jnp.where(strict, big[:PAIR, PAIR:], 0.0)
        o["mrk"] = jnp.where(incl, big[PAIR:, :PAIR], 0.0)
        o["mrb"] = jnp.where(incl, big[PAIR:, PAIR:], 0.0)
        o["t"] = eye - o["n"]
    for _ in range(5):
        for ch in chains:
            ops[ch]["n"] = _dot(ops[ch]["n"], ops[ch]["n"])
        for ch in chains:
            ops[ch]["t"] = ops[ch]["t"] + _dot(ops[ch]["t"], ops[ch]["n"])
    for ch in chains:
        o = ops[ch]
        o["mv"] = _dot(jnp.concatenate([o.pop("mak"), o.pop("mrk")], axis=0), o["v"])
    for ch in chains:
        o = ops[ch]
        o["xz"] = _dot(o.pop("t"), jnp.concatenate([o["at"], o["mv"][:PAIR].astype(_bf16)], axis=1))
    for ch in chains:
        o = ops[ch]
        o["qy"] = _dot(o.pop("mrb"), o["xz"])
    for ch in chains:
        o = ops[ch]
        o["upd"] = _dot_tn(o["xz"], o["bh"])
        o["vtk"] = _dot_tn(o["v"], o["kh"])
    for d, p in chains:
        o = ops[d, p]
        s0 = s_scr[d, p]
        q_eff = o["rt"].astype(_f32) - o["qy"][:, :LANES]
        y = _dot_nt(q_eff, s0) + o["mv"][PAIR:] - o["qy"][:, LANES:]
        dirs[d][8][:, p * LANES:(p + 1) * LANES] = y[:CHUNK] + y[CHUNK:]
        p_last = dirs[d][7][:, p * LANES:(p + 1) * LANES]
        s_scr[d, p] = s0 * p_last - _dot(s0, o["upd"][:LANES]) + o["vtk"] - o["upd"][LANES:]

    @pl.when((c == nc - 1) & jnp.logical_not(is_lat))
    def _():
        for d in range(2):
            for p in range(N_PAIRS):
                sp = s_scr[d, p]
                sout_ref[d, 2 * p] = sp[:HD_A, :HD_A]
                sout_ref[d, 2 * p + 1] = sp[HD_A:, HD_A:]


def _lat_state_spec(shape, st):
    nd = len(shape)
    return pl.BlockSpec((None,) + tuple(shape[1:]),
                        lambda s: (jnp.maximum(_seq_of(s, st) - st.n_ctx_seq, 0),) + (0,) * (nd - 1))


def _ctx_state_spec(shape, st):
    nd = len(shape)
    return pl.BlockSpec((None,) + tuple(shape[1:]),
                        lambda s: (jnp.minimum(_seq_of(s, st), st.n_ctx_seq - 1),) + (0,) * (nd - 1))


def _rwkv_scan(prep, s0, st):
    rt, kt, at, bt, kh, bh, v, p_last = prep
    n = v.shape[0]
    fwd = lambda s: s
    bwd = lambda s: _bwd_chunk(s, st)
    seq2 = lambda d, f: pl.BlockSpec((None, CHUNK, D_A), lambda s: (d, f(s), 0))
    seq1 = lambda f: pl.BlockSpec((CHUNK, D_A), lambda s: (f(s), 0))
    pls = lambda d, f: pl.BlockSpec((None, None, 1, D_A), lambda s: (d, f(s), 0, 0))
    side = lambda d, f: [seq2(d, f)] * 6 + [seq1(f), pls(d, f)]
    y_shape = jax.ShapeDtypeStruct((n, D_A), _f32)
    out_state = (st.n_ctx_seq,) + s0.shape[1:]
    args = (rt, kt, at, bt, kh, bh, v, p_last)
    return pl.pallas_call(
        functools.partial(_rwkv_scan_kernel, st=st),
        grid=(st.n_chunks,),
        in_specs=side(0, fwd) + side(1, bwd) + [_lat_state_spec(s0.shape, st)],
        out_specs=[seq1(fwd), seq1(bwd), _ctx_state_spec(out_state, st)],
        out_shape=[y_shape, y_shape, jax.ShapeDtypeStruct(out_state, _f32)],
        scratch_shapes=[pltpu.VMEM((2, N_PAIRS, LANES, LANES), _f32)],
        compiler_params=_cparams(1), name="rwkv_scan",
    )(*args, *args, s0)


def _gla_kernel(q_f, k_f, v_f, l_f, q_b, k_b, v_b, l_b, w_ref, bias_ref, s0_ref,
                of_ref, ob_ref, sout_ref, s_scr, *, st):
    s = pl.program_id(0)
    is_lat, nc, c, _ = _chunk_pos(s, st)

    @pl.when(c == 0)
    def _():
        keep = jnp.where(is_lat, 1.0, 0.0)
        for d in range(2):
            for h in range(H_B):
                s_scr[d, h] = s0_ref[d, h].T * keep

    dirs = ((q_f, k_f, v_f, l_f, of_ref), (q_b, k_b, v_b, l_b, ob_ref))
    work = []
    for d, (q_ref, k_ref, v_ref, l_ref, o_ref) in enumerate(dirs):
        incl, _ = _tri_masks(d)
        la = jax.nn.log_sigmoid(_dot3(l_ref[...], w_ref[d, 0], w_ref[d, 1]) + bias_ref[d:d + 1]) / GATE_NORM
        cum = _dot_exact_lhs(jnp.where(incl, 1.0, 0.0), la)
        tot = jnp.sum(la, axis=0, keepdims=True)
        k = k_ref[...]
        q_in = q_ref[...] * (DK_B ** -0.5) * jnp.exp(cum)
        k_in = k * jnp.exp(-cum)
        k_end = k * jnp.exp(tot - cum)
        e_tot = jnp.exp(tot)
        for h in range(H_B):
            ks = slice(h * DK_B, (h + 1) * DK_B)
            work.append((d, h, incl, q_in[:, ks], k_in[:, ks], k_end[:, ks], e_tot[:, ks]))
    att = [jnp.where(incl, _dot_nt(q_in, k_in), 0.0) for (_, _, incl, q_in, k_in, _, _) in work]
    for (d, h, _, q_in, _, k_end, e_tot), a in zip(work, att):
        v_ref, o_ref = dirs[d][2], dirs[d][4]
        vs = slice(h * DV_B, (h + 1) * DV_B)
        v = v_ref[:, vs]
        s_t = s_scr[d, h]
        o_ref[:, vs] = _dot(a, v) + _dot_nt(q_in, s_t)
        s_scr[d, h] = s_t * e_tot + _dot_tn(v, k_end)

    @pl.when((c == nc - 1) & jnp.logical_not(is_lat))
    def _():
        for d in range(2):
            for h in range(H_B):
                sout_ref[d, h] = s_scr[d, h].T


def _gla_scan(p, w, bias, s0, st):
    n = p.shape[0]
    out_state = (st.n_ctx_seq,) + s0.shape[1:]
    fwd = lambda s: s
    bwd = lambda s: _bwd_chunk(s, st)
    blk = lambda width, col, f: pl.BlockSpec((CHUNK, width), lambda s: (f(s), col))
    side = lambda f: [blk(QK_B, EV_Q_BLK, f), blk(QK_B, EV_K_BLK, f), blk(V_B, EV_V_BLK, f),
                      blk(EV_LORA_W, EV_LORA_BLK, f)]
    full = lambda a: pl.BlockSpec(a.shape, lambda s: (0,) * a.ndim)
    o_spec = lambda f: pl.BlockSpec((CHUNK, V_B), lambda s: (f(s), 0))
    o_shape = jax.ShapeDtypeStruct((n, V_B), _f32)
    return pl.pallas_call(
        functools.partial(_gla_kernel, st=st),
        grid=(st.n_chunks,),
        in_specs=side(fwd) + side(bwd) + [full(w), full(bias), _lat_state_spec(s0.shape, st)],
        out_specs=[o_spec(fwd), o_spec(bwd), _ctx_state_spec(out_state, st)],
        out_shape=[o_shape, o_shape, jax.ShapeDtypeStruct(out_state, _f32)],
        scratch_shapes=[pltpu.VMEM((2, H_B, DV_B, DK_B), _f32)],
        compiler_params=_cparams(1), name="gla_scan",
    )(p, p, p, p, p, p, p, p, w, bias, s0)


def _mlstm_kernel(q_f, k_f, v_f, g_f, q_b, k_b, v_b, g_b, bias_ref, c0_ref, n0_ref, m0_ref,
                  hf_ref, hb_ref, cout_ref, nout_ref, mout_ref, c_scr, n_scr, m_scr, *, st):
    s = pl.program_id(0)
    is_lat, nc, c, _ = _chunk_pos(s, st)

    @pl.when(c == 0)
    def _():
        keep = jnp.where(is_lat, 1.0, 0.0)
        for d in range(2):
            for h in range(H_C):
                c_scr[d, h] = c0_ref[d, h].T * keep
        n_scr[...] = n0_ref[...] * keep
        m_scr[...] = m0_ref[...] * keep

    L = CHUNK
    lane = lax.broadcasted_iota(jnp.int32, (L, LANES), 1)
    dirs = ((q_f, k_f, v_f, g_f, hf_ref), (q_b, k_b, v_b, g_b, hb_ref))
    work = []
    for d, (q_ref, k_ref, v_ref, g_ref, h_ref) in enumerate(dirs):
        gates = g_ref[...] + bias_ref[0:1]
        gates = jnp.where(lane < 2 * H_C, gates, jax.nn.log_sigmoid(gates))
        gates_t = gates.T
        incl, _ = _tri_masks(d)
        incl_t = _tri_masks(1 - d)[0]
        for h in range(H_C):
            ji, jf = d * H_C + h, 2 * H_C + d * H_C + h
            i_col, f_col = gates[:, ji:ji + 1], gates[:, jf:jf + 1]
            i_row, f_row = gates_t[ji:ji + 1, :], gates_t[jf:jf + 1, :]
            cum_col = jnp.sum(jnp.where(incl, f_row, 0.0), axis=1, keepdims=True)
            cum_row = jnp.sum(jnp.where(incl_t, f_col, 0.0), axis=0, keepdims=True)
            f_tot = jnp.sum(f_row, axis=1, keepdims=True)
            m_prev = m_scr[d, h]
            log_d = jnp.where(incl, cum_col - cum_row + i_row, NEG_BIG)
            log_inter = cum_col + m_prev
            m_t = jnp.maximum(log_inter, jnp.max(log_d, axis=1, keepdims=True))
            d_mat = jnp.exp(log_d - m_t)
            w_inter = jnp.exp(log_inter - m_t)
            m_new = m_t[L - 1:L, :] if d == 0 else m_t[0:1, :]
            w_s = jnp.exp(f_tot - cum_col + i_col - m_new)
            dec = jnp.exp(f_tot + m_prev - m_new)
            ks = slice(h * DK_C, (h + 1) * DK_C)
            q = q_ref[:, ks] * (DK_C ** -0.5)
            k = k_ref[:, ks]
            work.append(dict(d=d, h=h, q=q, k=k, d_mat=d_mat, w_inter=w_inter, m_t=m_t, m_new=m_new,
                             kw=k * w_s, dec=dec, s=_dot_nt(q, k)))
    for o in work:
        d, h = o["d"], o["h"]
        v_ref, h_ref = dirs[d][2], dirs[d][4]
        vs = slice(h * DV_C, (h + 1) * DV_C)
        v = v_ref[:, vs]
        c_t = c_scr[d, h]
        n_row = n_scr[d, h]
        sm = o["s"] * o["d_mat"]
        num = o["w_inter"] * _dot_nt(o["q"], c_t) + _dot(sm, v)
        den = o["w_inter"] * jnp.sum(o["q"] * n_row, axis=1, keepdims=True) + jnp.sum(sm, axis=1, keepdims=True)
        h_ref[:, vs] = num / jnp.maximum(jnp.abs(den), jnp.exp(-o["m_t"]))
        c_scr[d, h] = o["dec"] * c_t + _dot_tn(v, o["kw"])
        n_scr[d, h] = o["dec"] * n_row + jnp.sum(o["kw"], axis=0, keepdims=True)
        m_scr[d, h] = o["m_new"]

    @pl.when((c == nc - 1) & jnp.logical_not(is_lat))
    def _():
        for d in range(2):
            for h in range(H_C):
                cout_ref[d, h] = c_scr[d, h].T
        nout_ref[...] = n_scr[...]
        mout_ref[...] = m_scr[...]


def _mlstm_scan(p, bias, c0, n0, m0, st):
    n = p.shape[0]
    fwd = lambda s: s
    bwd = lambda s: _bwd_chunk(s, st)
    blk = lambda width, col, f: pl.BlockSpec((CHUNK, width), lambda s: (f(s), col))
    side = lambda f: [blk(QK_C, 0, f), blk(QK_C, 1, f), blk(V_C, 1, f), blk(LANES, OD_GATE_BLK, f)]
    o_spec = lambda f: pl.BlockSpec((CHUNK, V_C), lambda s: (f(s), 0))
    o_shape = jax.ShapeDtypeStruct((n, V_C), _f32)
    states = (c0, n0, m0)
    out_states = [(st.n_ctx_seq,) + a.shape[1:] for a in states]
    return pl.pallas_call(
        functools.partial(_mlstm_kernel, st=st),
        grid=(st.n_chunks,),
        in_specs=side(fwd) + side(bwd) + [pl.BlockSpec(bias.shape, lambda s: (0, 0))]
        + [_lat_state_spec(a.shape, st) for a in states],
        out_specs=[o_spec(fwd), o_spec(bwd)] + [_ctx_state_spec(sh, st) for sh in out_states],
        out_shape=[o_shape, o_shape] + [jax.ShapeDtypeStruct(sh, _f32) for sh in out_states],
        scratch_shapes=[pltpu.VMEM((2, H_C, DV_C, DK_C), _f32), pltpu.VMEM((2, H_C, 1, DK_C), _f32),
                        pltpu.VMEM((2, H_C, 1, 1), _f32)],
        compiler_params=_cparams(1), name="mlstm_scan",
    )(p, p, p, p, p, p, p, p, bias, c0, n0, m0)


def _head_rms_gate(o, g_row, gate, n_heads, width):
    outs = []
    for h in range(n_heads):
        hs = slice(h * width, (h + 1) * width)
        oh = o[:, hs]
        outs.append(oh * lax.rsqrt(jnp.mean(oh * oh, axis=-1, keepdims=True) + EPS) * g_row[:, hs] * gate[:, hs])
    return jnp.concatenate(outs, axis=1)


def _even_post_kernel(yf_ref, yb_ref, bonus_ref, g_ref, of_ref, ob_ref, og_ref, par_ref, gn_ref, bd_ref, o_ref):
    y = yf_ref[...] + yb_ref[...]
    bd = bd_ref[...]
    mean = _dot_exact_rhs(y, bd) * (1.0 / HD_A)
    yc = y - mean
    var = _dot_exact_rhs(yc * yc, bd) * (1.0 / HD_A)
    par = par_ref[...]
    ya = (yc * lax.rsqrt(var + EPS) * par[0:1] + par[1:2] + bonus_ref[...]) * g_ref[...]
    og = og_ref[...]
    yb = _head_rms_gate(of_ref[...] + ob_ref[...], gn_ref[...], og * jax.nn.sigmoid(og), H_B, DV_B)
    o_ref[:, :D_A] = ya.astype(_bf16)
    o_ref[:, D_A:] = yb.astype(_bf16)


def _even_post(yf, yb, bonus, g, of, ob, p, ln_par, gla_g, bd, tm=256):
    n = yf.shape[0]
    row = lambda w, col=0: pl.BlockSpec((tm, w), lambda i: (i, col))
    full = lambda a: pl.BlockSpec(a.shape, lambda i: (0,) * a.ndim)
    return pl.pallas_call(
        _even_post_kernel,
        grid=(n // tm,),
        in_specs=[row(D_A)] * 4 + [row(V_B), row(V_B), row(V_B, EV_OG_BLK), full(ln_par), full(gla_g), full(bd)],
        out_specs=row(D_A + V_B),
        out_shape=jax.ShapeDtypeStruct((n, D_A + V_B), _bf16),
        compiler_params=_cparams(1), name="even_post",
    )(yf, yb, bonus, g, of, ob, p, ln_par, gla_g, bd)


def _odd_post_kernel(hf_ref, hb_ref, o_ref_in, gn_ref, out_ref):
    out_ref[...] = _head_rms_gate(hf_ref[...] + hb_ref[...], gn_ref[...], jax.nn.sigmoid(o_ref_in[...]),
                                  H_C, DV_C).astype(_bf16)


def _odd_post(hf, hb, p, gn, tm=256):
    n = hf.shape[0]
    row = lambda col=0: pl.BlockSpec((tm, V_C), lambda i: (i, col))
    return pl.pallas_call(
        _odd_post_kernel,
        grid=(n // tm,),
        in_specs=[row(), row(), row(2), pl.BlockSpec(gn.shape, lambda i: (0, 0))],
        out_specs=row(),
        out_shape=jax.ShapeDtypeStruct((n, V_C), _bf16),
        compiler_params=_cparams(1), name="odd_post",
    )(hf, hb, p, gn)


def _even_mixer(p, st, s_rwkv, s_gla, mu, w0, w2, a0, a2, g2, k_k, k_a, r_k, ln_g, ln_b, gla_w2, gla_b, gla_g):
    zpad = lambda x, rows_before, total: jnp.pad(x, ((rows_before, total - rows_before - x.shape[0]), (0, 0)))
    mu_main = mu[:EV_MAIN].reshape(1, EV_MAIN)
    mu_lora = jnp.pad(mu[EV_MAIN:], (0, EV_LORA_W - (A_IN - EV_MAIN))).reshape(1, EV_LORA_W)
    split = lambda w: jnp.stack(_split2(w), axis=0)
    ww = jnp.stack([split(zpad(w2[d], d * LORA_W, EV_LORA_W)) for d in range(2)])
    wa = jnp.stack([split(zpad(a2[d], 2 * LORA_W + d * LORA_A, EV_LORA_W)) for d in range(2)])
    wg = zpad(g2, 2 * LORA_W + 2 * LORA_A, EV_LORA_W).astype(_bf16)
    par = jnp.stack([w0[0], w0[1], a0[0], a0[1], k_k, k_a, r_k, jnp.zeros_like(k_k)])
    head = np.arange(D_A) // HD_A
    bd = jnp.asarray(head[:, None] == head[None, :], _bf16)
    *prep, bonus, g = _rwkv_prep(p, st, mu_main, mu_lora, ww, wa, wg, par, bd)
    yf, yb, s_rw = _rwkv_scan(prep, s_rwkv, st)
    wl = jnp.stack([split(zpad(gla_w2[d], GLA_LORA_OFF + d * LORA_GK, EV_LORA_W)) for d in range(2)])
    bias = jnp.pad(gla_b, ((0, MOD_ROWS - 2), (0, 0)))
    of, ob, s_gl = _gla_scan(p, wl, bias, s_gla, st)
    ln_par = jnp.pad(jnp.stack([ln_g, ln_b]), ((0, MOD_ROWS - 2), (0, 0)))
    y_in = _even_post(yf, yb, bonus, g, of, ob, p, ln_par, gla_g.reshape(1, V_B), bd)
    return y_in, s_rw, s_gl


def _odd_mixer(p, st, s_c, s_n, s_m, b_i, b_f, g):
    bias = jnp.concatenate([b_i.reshape(-1), b_f.reshape(-1), jnp.zeros((LANES - 4 * H_C,), _f32)])
    bias = jnp.pad(bias[None, :], ((0, MOD_ROWS - 1), (0, 0)))
    hf, hb, c_new, n_new, m_new = _mlstm_scan(p, bias, s_c, s_n[..., None, :], s_m[..., None, None], st)
    y_in = _odd_post(hf, hb, p, g.reshape(1, V_C))
    return y_in, c_new, n_new[..., 0, :], m_new[..., 0, 0]


def _moe(h_bf16, logits, w1, w3, w2, layer, tm=FFN_TM):
    n_tok = h_bf16.shape[0]
    n_pairs = n_tok * TOP_K
    n_tiles = n_pairs // tm + N_EXPERTS
    top_v, top_i = lax.top_k(logits[:, :N_EXPERTS], TOP_K)
    gates = jax.nn.softmax(top_v, axis=-1)
    pair_e = top_i.reshape(-1)
    order = jnp.argsort(pair_e, stable=True)
    sorted_e = pair_e[order]
    counts = jnp.sum(jax.nn.one_hot(pair_e, N_EXPERTS, dtype=jnp.int32), axis=0)
    tiles_per_e = (counts + tm - 1) // tm
    tile_start = jnp.cumsum(tiles_per_e) - tiles_per_e
    group_start = jnp.cumsum(counts) - counts
    rank = jnp.arange(n_pairs, dtype=jnp.int32) - group_start[sorted_e]
    dest = tile_start[sorted_e] * tm + rank
    slot_token = jnp.zeros((n_tiles * tm,), jnp.int32).at[dest].set((order // TOP_K).astype(jnp.int32))
    pair_slot = jnp.zeros((n_pairs,), jnp.int32).at[order].set(dest.astype(jnp.int32))
    n_valid = jnp.sum(tiles_per_e)
    tile_ids = jnp.arange(n_tiles, dtype=jnp.int32)
    tile_e = jnp.sum((tile_ids[:, None] >= tile_start[None, :]).astype(jnp.int32), axis=1) - 1
    last_e = jnp.max(jnp.where(counts > 0, jnp.arange(N_EXPERTS), 0))
    tile_valid = tile_ids < n_valid
    tile_e = jnp.where(tile_valid, tile_e, last_e).astype(jnp.int32)
    tile_rows = jnp.clip(counts[tile_e] - (tile_ids - tile_start[tile_e]) * tm, 0, tm)
    tile_rows = jnp.where(tile_valid, tile_rows, 0).astype(jnp.int32)
    xg = jnp.take(h_bf16, slot_token, axis=0)
    yg = _grouped_ffn(xg, w1, w3, w2, layer, tile_e, tile_rows, tm=tm)
    yp = jnp.take(yg, pair_slot, axis=0).reshape(n_tok, TOP_K, -1)
    return jnp.sum(yp * gates[..., None], axis=1)


def kernel(x_prompt, x_sample, state_rwkv, state_gla, state_mlstm_c, state_mlstm_n, state_mlstm_m, c, c_ctx, mod_w, mod_b, norm_g, final_g, ev_w_in, ev_mu, rw_w0, rw_w2, rw_a0, rw_a2, rw_g2, rw_kk, rw_ka, rw_rk, rw_ln_g, rw_ln_b, gla_w2, gla_b, gla_norm_g, ev_w_out, ffn_w1, ffn_w3, ffn_w2, od_w_in, ml_b_i, ml_b_f, ml_norm_g, od_w_out, moe_router, moe_w1, moe_w3, moe_w2):
    D = D_MODEL
    Bp, Tp, _ = x_prompt.shape
    Bs, Ts, _ = x_sample.shape
    assert Ts // GRID_W * GRID_W == Ts and GRID_W == CHUNK and Tp % CHUNK == 0
    n_ctx = Bp * Tp
    st = _Streams(n_ctx_chunks=n_ctx // CHUNK, ctx_cps=Tp // CHUNK, lat_cps=Ts // CHUNK,
                  n_chunks=(n_ctx + Bs * Ts) // CHUNK, n_ctx_seq=Bp, n_seq=Bp + Bs)
    x = jnp.concatenate([x_prompt.reshape(n_ctx, D), x_sample.reshape(Bs * Ts, D)], axis=0)
    n_tok = x.shape[0]

    cond = jnp.concatenate([c_ctx[None, :], c, jnp.zeros((MOD_ROWS - 1 - Bs, D), _f32)], axis=0)
    mod = _mod_table(jax.nn.silu(cond).astype(_bf16), mod_w, mod_b)

    dense_te = jnp.zeros((n_tok // FFN_TM,), jnp.int32)
    dense_rows = jnp.full((n_tok // FFN_TM,), FFN_TM, jnp.int32)
    new_rw, new_gla, new_c, new_n, new_m = [], [], [], [], []
    pending = None

    for l in range(DEPTH):
        i = l // 2
        sh1, sc1, gt1, sh2, sc2, gt2 = [mod[l, :, j * D:(j + 1) * D] for j in range(6)]
        if pending is None:
            h = _norm_mod(x, norm_g[l, 0], sc1, sh1, st)[0]
        else:
            x, h = _norm_mod(x, norm_g[l, 0], sc1, sh1, st, res=pending)
        if l % 2 == 0:
            w = ev_w_in[i]
            b0 = A_IN
            w_in = jnp.concatenate([
                w[:, :EV_MAIN], w[:, b0 + 2 * QK_B:b0 + 2 * QK_B + V_B], w[:, b0 + 2 * QK_B + V_B + 2 * LORA_GK:],
                w[:, b0:b0 + 2 * QK_B], w[:, EV_MAIN:A_IN],
                w[:, b0 + 2 * QK_B + V_B:b0 + 2 * QK_B + V_B + 2 * LORA_GK],
                jnp.zeros((D, EV_LORA_W - (A_IN - EV_MAIN) - 2 * LORA_GK), _f32)], axis=1).astype(_bf16)
            p = _matmul(h, w_in)
            y_in, s_rw, s_gl = _even_mixer(
                p, st, state_rwkv[:, i], state_gla[:, i], ev_mu[i], rw_w0[i], rw_w2[i],
                rw_a0[i], rw_a2[i], rw_g2[i], rw_kk[i], rw_ka[i], rw_rk[i], rw_ln_g[i], rw_ln_b[i],
                gla_w2[i], gla_b[i], gla_norm_g[i])
            new_rw.append(s_rw)
            new_gla.append(s_gl)
            x = _matmul_residual(y_in, ev_w_out[i].astype(_bf16), x, gt1, st)
        else:
            w_in = jnp.pad(od_w_in[i], ((0, 0), (0, OD_N - od_w_in.shape[2]))).astype(_bf16)
            p = _matmul(h, w_in, tn=OD_N // 7)
            y_in, c_new, n_new, m_new = _odd_mixer(
                p, st, state_mlstm_c[:, i], state_mlstm_n[:, i], state_mlstm_m[:, i],
                ml_b_i[i], ml_b_f[i], ml_norm_g[i])
            new_c.append(c_new)
            new_n.append(n_new)
            new_m.append(m_new)
            x = _matmul_residual(y_in, od_w_out[i].astype(_bf16), x, gt1, st)

        if l % 2 == 0:
            h = _norm_mod(x, norm_g[l, 1], sc2, sh2, st)[0]
            f = _grouped_ffn(h, ffn_w1[:, None], ffn_w3[:, None], ffn_w2[:, None], i, dense_te, dense_rows)
        else:
            router = _split2(jnp.pad(moe_router[i], ((0, 0), (0, LANES - N_EXPERTS))))
            h, logits = _norm_mod(x, norm_g[l, 1], sc2, sh2, st, router=router)
            f = _moe(h, logits, moe_w1, moe_w3, moe_w2, i)
        pending = (f, gt2)

    y = _final_norm(x, pending[0], pending[1], final_g, st)
    return (y[:n_ctx].reshape(Bp, Tp, D), y[n_ctx:].reshape(Bs, Ts, D),
            jnp.stack(new_rw, axis=1), jnp.stack(new_gla, axis=1),
            jnp.stack(new_c, axis=1), jnp.stack(new_n, axis=1), jnp.stack(new_m, axis=1))
```

```python
import functools
from typing import NamedTuple

import jax
import jax.numpy as jnp
import numpy as np
from jax import lax
from jax.experimental import pallas as pl
from jax.experimental.pallas import tpu as pltpu

D_MODEL = 2048
DEPTH = 4
GRID_W = 64
EPS = 1e-6
CHUNK = 64
H_A, HD_A = 16, 64
D_A = H_A * HD_A
LORA_W, LORA_A, LORA_G = 64, 64, 128
A_IN = 3 * D_A + 2 * LORA_W + 2 * LORA_A + LORA_G
H_B, DK_B, DV_B = 4, 128, 256
QK_B, V_B = H_B * DK_B, H_B * DV_B
LORA_GK = 16
GATE_NORM = 16.0
H_C, DK_C, DV_C = 8, 128, 256
QK_C, V_C = H_C * DK_C, H_C * DV_C
D_FF = 5632
N_EXPERTS = 8
TOP_K = 2

LANES = 128
MOD_ROWS = 8
VMEM_LIMIT = 48 * 1024 * 1024
FFN_VMEM_LIMIT = 56 * 1024 * 1024
NEG_BIG = -1e30

EV_MAIN = 3 * D_A
EV_V_BLK = EV_MAIN // V_B
EV_OG_BLK = EV_V_BLK + 1
EV_Q_BLK = (EV_MAIN + 2 * V_B) // QK_B
EV_K_BLK = EV_Q_BLK + 1
EV_LORA_W = 512
EV_LORA_BLK = (EV_MAIN + 2 * V_B + 2 * QK_B) // EV_LORA_W
EV_N = EV_MAIN + 2 * V_B + 2 * QK_B + EV_LORA_W
GLA_LORA_OFF = 2 * LORA_W + 2 * LORA_A + LORA_G
OD_GATE_BLK = (2 * QK_C + 2 * V_C) // LANES
OD_N = 2 * QK_C + 2 * V_C + LANES

_f32 = jnp.float32
_bf16 = jnp.bfloat16


class _Streams(NamedTuple):
    n_ctx_chunks: int
    ctx_cps: int
    lat_cps: int
    n_chunks: int
    n_ctx_seq: int
    n_seq: int


def _chunk_pos(s, st):
    is_lat = s >= st.n_ctx_chunks
    sl = s - st.n_ctx_chunks
    nc = jnp.where(is_lat, st.lat_cps, st.ctx_cps)
    c = jnp.where(is_lat, sl % st.lat_cps, s % st.ctx_cps)
    seq = jnp.where(is_lat, st.n_ctx_seq + sl // st.lat_cps, s // st.ctx_cps)
    return is_lat, nc, c, seq


def _bwd_chunk(s, st):
    _, nc, c, _ = _chunk_pos(s, st)
    return s - c + (nc - 1 - c)


def _seq_of(s, st):
    return _chunk_pos(s, st)[3]


def _mod_row_of_chunk(s, st):
    is_lat, _, _, seq = _chunk_pos(s, st)
    return jnp.where(is_lat, 1 + seq - st.n_ctx_seq, 0)


def _cparams(n_axes):
    return pltpu.CompilerParams(dimension_semantics=("arbitrary",) * n_axes,
                                vmem_limit_bytes=VMEM_LIMIT)


def _dot(a, b):
    return jnp.dot(a.astype(_bf16), b.astype(_bf16), preferred_element_type=_f32)


def _dot_nt(a, b):
    return lax.dot_general(a.astype(_bf16), b.astype(_bf16), (((1,), (1,)), ((), ())),
                           preferred_element_type=_f32)


def _dot_tn(a, b):
    return lax.dot_general(a.astype(_bf16), b.astype(_bf16), (((0,), (0,)), ((), ())),
                           preferred_element_type=_f32)


def _split2(x):
    hi = x.astype(_bf16)
    return hi, (x - hi.astype(_f32)).astype(_bf16)


def _dot_exact_lhs(tri, x):
    hi = x.astype(_bf16)
    r1 = x - hi.astype(_f32)
    mid = r1.astype(_bf16)
    lo = (r1 - mid.astype(_f32)).astype(_bf16)
    t = tri.astype(_bf16)
    d = lambda v: jnp.dot(t, v, preferred_element_type=_f32)
    return d(hi) + d(mid) + d(lo)


def _dot_exact_rhs(x, ones):
    hi = x.astype(_bf16)
    r1 = x - hi.astype(_f32)
    mid = r1.astype(_bf16)
    lo = (r1 - mid.astype(_f32)).astype(_bf16)
    d = lambda v: jnp.dot(v, ones, preferred_element_type=_f32)
    return d(hi) + d(mid) + d(lo)


def _dot3(a, b_hi, b_lo):
    a_hi, a_lo = _split2(a)
    d = lambda u, v: jnp.dot(u, v, preferred_element_type=_f32)
    return d(a_hi, b_hi) + d(a_lo, b_hi) + d(a_hi, b_lo)


def _tri_masks(direction, n=CHUNK, period=CHUNK):
    t = lax.broadcasted_iota(jnp.int32, (n, n), 0)
    i = lax.broadcasted_iota(jnp.int32, (n, n), 1)
    same = (t // period) == (i // period) if n != period else True
    if direction == 0:
        return (i <= t) & same, (i < t) & same
    return (i >= t) & same, (i > t) & same


def _mod_kernel(a_ref, w_ref, b_ref, o_ref):
    o_ref[...] = _dot(a_ref[...], w_ref[...]) + b_ref[...]


def _mod_table(silu_rows, mod_w, mod_b, tn=1024):
    n_l, k, n = mod_w.shape
    return pl.pallas_call(
        _mod_kernel,
        grid=(n_l, n // tn),
        in_specs=[pl.BlockSpec((MOD_ROWS, k), lambda l, j: (0, 0)),
                  pl.BlockSpec((None, k, tn), lambda l, j: (l, 0, j)),
                  pl.BlockSpec((None, 1, tn), lambda l, j: (l, 0, j))],
        out_specs=pl.BlockSpec((None, MOD_ROWS, tn), lambda l, j: (l, 0, j)),
        out_shape=jax.ShapeDtypeStruct((n_l, MOD_ROWS, n), _f32),
        compiler_params=_cparams(2), name="mod_table",
    )(silu_rows, mod_w, mod_b.reshape(n_l, 1, n))


def _norm_mod_kernel(*refs, tm, st, has_res, has_router):
    refs = list(refs)
    x_ref = refs.pop(0)
    f_ref, gate_ref = (refs.pop(0), refs.pop(0)) if has_res else (None, None)
    g_ref, sc_ref, sh_ref = refs.pop(0), refs.pop(0), refs.pop(0)
    r_hi, r_lo = (refs.pop(0), refs.pop(0)) if has_router else (None, None)
    xo_ref = refs.pop(0) if has_res else None
    h_ref = refs.pop(0)
    row = _mod_row_of_chunk(pl.program_id(0) * (tm // CHUNK), st)
    x = x_ref[...]
    if has_res:
        x = x + gate_ref[pl.ds(row, 1), :] * f_ref[...]
        xo_ref[...] = x
    inv = lax.rsqrt(jnp.mean(x * x, axis=-1, keepdims=True) + EPS)
    h = x * inv * g_ref[...] * (1.0 + sc_ref[pl.ds(row, 1), :]) + sh_ref[pl.ds(row, 1), :]
    h_ref[...] = h.astype(h_ref.dtype)
    if has_router:
        refs.pop(0)[...] = _dot3(h, r_hi[...], r_lo[...])


def _norm_mod(x, g, sc, sh, st, res=None, router=None, h_dtype=_bf16, tm=256):
    m, d = x.shape
    row = pl.BlockSpec((tm, d), lambda i: (i, 0))
    tab = pl.BlockSpec((MOD_ROWS, d), lambda i: (0, 0))
    args, specs = [x], [row]
    outs, out_specs = [], []
    if res is not None:
        args += list(res)
        specs += [row, tab]
        outs.append(jax.ShapeDtypeStruct((m, d), _f32))
        out_specs.append(row)
    args += [g.reshape(1, d), sc, sh]
    specs += [pl.BlockSpec((1, d), lambda i: (0, 0)), tab, tab]
    outs.append(jax.ShapeDtypeStruct((m, d), h_dtype))
    out_specs.append(row)
    if router is not None:
        args += list(router)
        specs += [pl.BlockSpec((d, LANES), lambda i: (0, 0))] * 2
        outs.append(jax.ShapeDtypeStruct((m, LANES), _f32))
        out_specs.append(pl.BlockSpec((tm, LANES), lambda i: (i, 0)))
    return pl.pallas_call(
        functools.partial(_norm_mod_kernel, tm=tm, st=st, has_res=res is not None, has_router=router is not None),
        grid=(m // tm,), in_specs=specs, out_specs=out_specs, out_shape=outs,
        compiler_params=_cparams(1), name="norm_mod",
    )(*args)


def _final_norm_kernel(x_ref, f_ref, gate_ref, g_ref, o_ref, *, tm, st):
    row = _mod_row_of_chunk(pl.program_id(0) * (tm // CHUNK), st)
    x = x_ref[...] + gate_ref[pl.ds(row, 1), :] * f_ref[...]
    o_ref[...] = x * lax.rsqrt(jnp.mean(x * x, axis=-1, keepdims=True) + EPS) * g_ref[...]


def _final_norm(x, f, gate, g, st, tm=256):
    m, d = x.shape
    row = pl.BlockSpec((tm, d), lambda i: (i, 0))
    return pl.pallas_call(
        functools.partial(_final_norm_kernel, tm=tm, st=st),
        grid=(m // tm,),
        in_specs=[row, row, pl.BlockSpec((MOD_ROWS, d), lambda i: (0, 0)), pl.BlockSpec((1, d), lambda i: (0, 0))],
        out_specs=row,
        out_shape=jax.ShapeDtypeStruct((m, d), _f32),
        compiler_params=_cparams(1), name="final_norm",
    )(x, f, gate, g.reshape(1, d))


def _mm_kernel(a_ref, w_ref, o_ref):
    o_ref[...] = _dot(a_ref[...], w_ref[...])


def _mm_res_kernel(a_ref, w_ref, res_ref, gate_ref, o_ref, *, tm, st):
    row = _mod_row_of_chunk(pl.program_id(1) * (tm // CHUNK), st)
    gate = gate_ref[pl.ds(row, 1), :]
    o_ref[...] = res_ref[...] + gate * _dot(a_ref[...], w_ref[...])


def _matmul(a, w, tm=512, tn=512):
    m, k = a.shape
    n = w.shape[1]
    tm = min(tm, m)
    return pl.pallas_call(
        _mm_kernel,
        grid=(n // tn, m // tm),
        in_specs=[pl.BlockSpec((tm, k), lambda j, i: (i, 0)),
                  pl.BlockSpec((k, tn), lambda j, i: (0, j))],
        out_specs=pl.BlockSpec((tm, tn), lambda j, i: (i, j)),
        out_shape=jax.ShapeDtypeStruct((m, n), _f32),
        compiler_params=_cparams(2), name="matmul",
    )(a, w)


def _matmul_residual(a, w, res, gate, st, tm=512, tn=512):
    m, k = a.shape
    n = w.shape[1]
    tm = min(tm, m)
    return pl.pallas_call(
        functools.partial(_mm_res_kernel, tm=tm, st=st),
        grid=(n // tn, m // tm),
        in_specs=[pl.BlockSpec((tm, k), lambda j, i: (i, 0)),
                  pl.BlockSpec((k, tn), lambda j, i: (0, j)),
                  pl.BlockSpec((tm, tn), lambda j, i: (i, j)),
                  pl.BlockSpec((MOD_ROWS, tn), lambda j, i: (0, j))],
        out_specs=pl.BlockSpec((tm, tn), lambda j, i: (i, j)),
        out_shape=jax.ShapeDtypeStruct((m, n), _f32),
        compiler_params=_cparams(2), name="matmul_residual",
    )(a, w, res, gate)


FFN_TM = 1024
FFN_TF = 256
FFN_HALF = 512


def _ffn_kernel(te_ref, tr_ref, x_ref, w1_ref, w3_ref, w2_ref, o_ref, *, tm):
    i, j = pl.program_id(0), pl.program_id(1)
    rows = tr_ref[i]

    @pl.when(j == 0)
    def _():
        o_ref[...] = jnp.zeros_like(o_ref)

    def swiglu(n_rows):
        x = x_ref[0:n_rows, :]
        a = _dot(x, w1_ref[...])
        b = _dot(x, w3_ref[...])
        o_ref[0:n_rows, :] += _dot(a * jax.nn.sigmoid(a) * b, w2_ref[...])

    @pl.when(rows > FFN_HALF)
    def _():
        swiglu(tm)

    @pl.when((rows > 0) & (rows <= FFN_HALF))
    def _():
        swiglu(FFN_HALF)


def _row_copy(src_hbm, src_row, dst, dst_row, sem):
    return pltpu.make_async_copy(src_hbm.at[pl.ds(src_row, 1), :], dst.at[pl.ds(dst_row, 1), :], sem)


def _ffn_gather_kernel(te_ref, tr_ref, slot_ref, h_hbm, w1_ref, w3_ref, w2_ref, o_ref, xf_scr, xb_scr, sem,
                       *, tm, n_tiles):
    i, j = pl.program_id(0), pl.program_id(1)
    rows = tr_ref[i]

    def start_gather(tile):
        def body(r, carry):
            _row_copy(h_hbm, slot_ref[tile * tm + r], xf_scr, r, sem).start()
            return carry
        lax.fori_loop(0, tr_ref[tile], body, 0)

    def wait_gather(tile):
        def body(r, carry):
            _row_copy(h_hbm, 0, xf_scr, r, sem).wait()
            return carry
        lax.fori_loop(0, tr_ref[tile], body, 0)

    @pl.when(j == 0)
    def _():
        @pl.when(i == 0)
        def _():
            xf_scr[...] = jnp.zeros_like(xf_scr)
            start_gather(0)

        wait_gather(i)
        xb_scr[...] = xf_scr[...].astype(_bf16)
        o_ref[...] = jnp.zeros_like(o_ref)

        @pl.when(i + 1 < n_tiles)
        def _():
            start_gather(i + 1)

    def swiglu(n_rows):
        x = xb_scr[0:n_rows, :]
        a = _dot(x, w1_ref[...])
        b = _dot(x, w3_ref[...])
        o_ref[0:n_rows, :] += _dot(a * jax.nn.sigmoid(a) * b, w2_ref[...])

    @pl.when(rows > FFN_HALF)
    def _():
        swiglu(tm)

    @pl.when((rows > 0) & (rows <= FFN_HALF))
    def _():
        swiglu(FFN_HALF)


def _gathered_ffn(h, slot_token, w1, w3, w2, layer, tile_expert, tile_rows, tm=FFN_TM, tf=FFN_TF):
    d = h.shape[1]
    f = w1.shape[3]
    nj = f // tf
    n_tiles = tile_expert.shape[0]

    def jeff(i, j, tr):
        return jnp.where(tr[i] > 0, j, nj - 1)

    return pl.pallas_call(
        functools.partial(_ffn_gather_kernel, tm=tm, n_tiles=n_tiles),
        grid_spec=pltpu.PrefetchScalarGridSpec(
            num_scalar_prefetch=3,
            grid=(n_tiles, nj),
            in_specs=[pl.BlockSpec(memory_space=pl.ANY),
                      pl.BlockSpec((None, None, d, tf), lambda i, j, te, tr, sl: (layer, te[i], 0, jeff(i, j, tr))),
                      pl.BlockSpec((None, None, d, tf), lambda i, j, te, tr, sl: (layer, te[i], 0, jeff(i, j, tr))),
                      pl.BlockSpec((None, None, tf, d), lambda i, j, te, tr, sl: (layer, te[i], jeff(i, j, tr), 0))],
            out_specs=pl.BlockSpec((tm, d), lambda i, j, te, tr, sl: (i, 0)),
            scratch_shapes=[pltpu.VMEM((tm, d), _f32), pltpu.VMEM((tm, d), _bf16), pltpu.SemaphoreType.DMA]),
        out_shape=jax.ShapeDtypeStruct((n_tiles * tm, d), _f32),
        compiler_params=pltpu.CompilerParams(dimension_semantics=("arbitrary", "arbitrary"),
                                             vmem_limit_bytes=FFN_VMEM_LIMIT),
        name="gathered_ffn",
    )(tile_expert, tile_rows, slot_token, h, w1, w3, w2)


COMBINE_TM = 256


def _combine_kernel(slot_ref, yg_hbm, gates_ref, o_ref, buf, sem, *, tm):
    base = pl.program_id(0) * tm * TOP_K

    def start(t, carry):
        for k in range(TOP_K):
            _row_copy(yg_hbm, slot_ref[base + t * TOP_K + k], buf.at[k], t, sem).start()
        return carry

    def wait(t, carry):
        for k in range(TOP_K):
            _row_copy(yg_hbm, 0, buf.at[k], t, sem).wait()
        return carry

    lax.fori_loop(0, tm, start, 0)
    lax.fori_loop(0, tm, wait, 0)
    g = gates_ref[...]
    o_ref[...] = g[:, 0:1] * buf[0] + g[:, 1:2] * buf[1]


def _moe_combine(yg, pair_slot, gates, tm=COMBINE_TM):
    n_tok = gates.shape[0]
    d = yg.shape[1]
    return pl.pallas_call(
        functools.partial(_combine_kernel, tm=tm),
        grid_spec=pltpu.PrefetchScalarGridSpec(
            num_scalar_prefetch=1,
            grid=(n_tok // tm,),
            in_specs=[pl.BlockSpec(memory_space=pl.ANY),
                      pl.BlockSpec((tm, TOP_K), lambda b, sl: (b, 0))],
            out_specs=pl.BlockSpec((tm, d), lambda b, sl: (b, 0)),
            scratch_shapes=[pltpu.VMEM((TOP_K, tm, d), _f32), pltpu.SemaphoreType.DMA]),
        out_shape=jax.ShapeDtypeStruct((n_tok, d), _f32),
        compiler_params=_cparams(1), name="moe_combine",
    )(pair_slot, yg, gates)


def _grouped_ffn(x, w1, w3, w2, layer, tile_expert, tile_rows, tm=FFN_TM, tf=FFN_TF):
    m, d = x.shape
    f = w1.shape[3]
    nj = f // tf
    n_tiles = m // tm

    def jeff(i, j, tr):
        return jnp.where(tr[i] > 0, j, nj - 1)

    return pl.pallas_call(
        functools.partial(_ffn_kernel, tm=tm),
        grid_spec=pltpu.PrefetchScalarGridSpec(
            num_scalar_prefetch=2,
            grid=(n_tiles, nj),
            in_specs=[pl.BlockSpec((tm, d), lambda i, j, te, tr: (i, 0)),
                      pl.BlockSpec((None, None, d, tf), lambda i, j, te, tr: (layer, te[i], 0, jeff(i, j, tr))),
                      pl.BlockSpec((None, None, d, tf), lambda i, j, te, tr: (layer, te[i], 0, jeff(i, j, tr))),
                      pl.BlockSpec((None, None, tf, d), lambda i, j, te, tr: (layer, te[i], jeff(i, j, tr), 0))],
            out_specs=pl.BlockSpec((tm, d), lambda i, j, te, tr: (i, 0))),
        out_shape=jax.ShapeDtypeStruct((m, d), _f32),
        compiler_params=pltpu.CompilerParams(dimension_semantics=("arbitrary", "arbitrary"),
                                             vmem_limit_bytes=FFN_VMEM_LIMIT),
        name="grouped_ffn",
    )(tile_expert, tile_rows, x, w1, w3, w2)


def _token_shift_chunk(x, prev, nxt, lane0, is_lat, first, last):
    n = x.shape[1]
    row = lax.broadcasted_iota(jnp.int32, x.shape, 0)
    ch = lax.broadcasted_iota(jnp.int32, x.shape, 1) + lane0
    down = pltpu.roll(x, 1, 0)
    up = pltpu.roll(x, CHUNK - 1, 0)
    zero_row = jnp.zeros((1, n), _f32)
    row_prev = jnp.where(first, zero_row, prev[CHUNK - 1:CHUNK, :])
    row_next = jnp.where(last, zero_row, nxt[0:1, :])
    ctx = jnp.where(ch < A_IN // 2,
                    jnp.where(row == 0, row_prev, down),
                    jnp.where(row == CHUNK - 1, row_next, up))
    q = A_IN // 4
    lat = jnp.where(ch < q, jnp.where(row == 0, 0.0, down),
                    jnp.where(ch < 2 * q, jnp.where(row == CHUNK - 1, 0.0, up),
                              jnp.where(ch < 3 * q, jnp.where(first, 0.0, prev),
                                        jnp.where(last, 0.0, nxt))))
    return jnp.where(is_lat, lat, ctx)


def _rwkv_prep_kernel(xm_ref, pm_ref, nm_ref, xl_ref, pl_ref, nl_ref, mum_ref, mul_ref,
                      ww_ref, wa_ref, wg_ref, par_ref, bd_ref,
                      rt_ref, kt_ref, at_ref, bt_ref, kh_ref, bh_ref, v_ref, plast_ref, bonus_ref, g_ref,
                      *, st):
    s = pl.program_id(0)
    is_lat, nc, c, _ = _chunk_pos(s, st)
    first, last = c == 0, c == nc - 1
    xm, xl = xm_ref[...], xl_ref[...]
    sm = _token_shift_chunk(xm, pm_ref[...], nm_ref[...], 0, is_lat, first, last)
    sl = _token_shift_chunk(xl, pl_ref[...], nl_ref[...], EV_MAIN, is_lat, first, last)
    pm = xm + mum_ref[...] * (sm - xm)
    lora = xl + mul_ref[...] * (sl - xl)
    r, k, v = pm[:, :D_A], pm[:, D_A:2 * D_A], pm[:, 2 * D_A:]
    par = par_ref[...]
    k_k, k_a, r_k = par[4:5], par[5:6], par[6:7]
    bd = bd_ref[...]
    kk = k * k_k
    kk = kk / jnp.maximum(jnp.sqrt(_dot_exact_rhs(kk * kk, bd)), 1e-12)
    g_ref[...] = _dot(jax.nn.sigmoid(lora), wg_ref[...])
    v_ref[...] = v.astype(_bf16)
    tanh_l = jnp.tanh(lora)
    kdir_sum = jnp.zeros_like(k)
    for d in range(2):
        w = -jax.nn.softplus(-(par[d:d + 1] + _dot3(tanh_l, ww_ref[d, 0], ww_ref[d, 1]))) - 0.5
        ld = -jnp.exp(w)
        a = jax.nn.sigmoid(par[2 + d:3 + d] + _dot3(lora, wa_ref[d, 0], wa_ref[d, 1]))
        kdir = k * (1.0 + (a - 1.0) * k_a)
        kdir_sum = kdir_sum + kdir
        incl, _ = _tri_masks(d)
        cum = _dot_exact_lhs(jnp.where(incl, 1.0, 0.0), ld)
        tot = jnp.sum(ld, axis=0, keepdims=True)
        e_pos, e_neg, e_end = jnp.exp(cum), jnp.exp(-cum), jnp.exp(tot - cum)
        kka = kk * a
        rt_ref[d] = (r * e_pos).astype(_bf16)
        kt_ref[d] = (kdir * e_neg).astype(_bf16)
        at_ref[d] = (kk * jnp.exp(cum - ld)).astype(_bf16)
        bt_ref[d] = (kka * e_neg).astype(_bf16)
        kh_ref[d] = (kdir * e_end).astype(_bf16)
        bh_ref[d] = (kka * e_end).astype(_bf16)
        plast_ref[d] = jnp.exp(tot)
    bonus_ref[...] = _dot_exact_rhs(r * kdir_sum * r_k, bd) * v


def _rwkv_prep(p, st, mu_main, mu_lora, ww, wa, wg, par, bd):
    n = p.shape[0]
    nch = st.n_chunks
    cur = lambda s: (s, 0)
    prv = lambda s: (jnp.maximum(s - 1, 0), 0)
    nxt = lambda s: (jnp.minimum(s + 1, nch - 1), 0)
    lo = lambda f: (lambda s: (f(s)[0], EV_LORA_BLK))
    main = lambda f: pl.BlockSpec((CHUNK, EV_MAIN), f)
    lora = lambda f: pl.BlockSpec((CHUNK, EV_LORA_W), lo(f))
    full = lambda a: pl.BlockSpec(a.shape, lambda s: (0,) * a.ndim)
    seq2 = pl.BlockSpec((2, CHUNK, D_A), lambda s: (0, s, 0))
    seq1 = pl.BlockSpec((CHUNK, D_A), lambda s: (s, 0))
    bf2 = jax.ShapeDtypeStruct((2, n, D_A), _bf16)
    return pl.pallas_call(
        functools.partial(_rwkv_prep_kernel, st=st),
        grid=(nch,),
        in_specs=[main(cur), main(prv), main(nxt), lora(cur), lora(prv), lora(nxt),
                  full(mu_main), full(mu_lora), full(ww), full(wa), full(wg), full(par), full(bd)],
        out_specs=[seq2] * 6 + [seq1, pl.BlockSpec((2, None, 1, D_A), lambda s: (0, s, 0, 0)), seq1, seq1],
        out_shape=[bf2] * 6 + [jax.ShapeDtypeStruct((n, D_A), _bf16),
                               jax.ShapeDtypeStruct((2, nch, 1, D_A), _f32),
                               jax.ShapeDtypeStruct((n, D_A), _f32),
                               jax.ShapeDtypeStruct((n, D_A), _f32)],
        compiler_params=_cparams(1), name="rwkv_prep",
    )(p, p, p, p, p, p, mu_main, mu_lora, ww, wa, wg, par, bd)


N_PAIRS = H_A // 2
PAIR = 2 * CHUNK


def _stack_masked(x, keep_first):
    return jnp.concatenate([x * keep_first, x * (1 - keep_first)], axis=0)


def _rwkv_scan_kernel(*refs, st):
    (rt_f, kt_f, at_f, bt_f, kh_f, bh_f, v_f, pl_f,
     rt_b, kt_b, at_b, bt_b, kh_b, bh_b, v_b, pl_b, s0_ref, yf_ref, yb_ref, sout_ref, s_scr) = refs
    s = pl.program_id(0)
    is_lat, nc, c, _ = _chunk_pos(s, st)

    @pl.when(c == 0)
    def _():
        keep = jnp.where(is_lat, 1.0, 0.0)
        zero = jnp.zeros((HD_A, HD_A), _f32)
        for d in range(2):
            for p in range(N_PAIRS):
                top = jnp.concatenate([s0_ref[d, 2 * p], zero], axis=1)
                bot = jnp.concatenate([zero, s0_ref[d, 2 * p + 1]], axis=1)
                s_scr[d, p] = jnp.concatenate([top, bot], axis=0) * keep

    dirs = ((rt_f, kt_f, at_f, bt_f, kh_f, bh_f, v_f, pl_f, yf_ref),
            (rt_b, kt_b, at_b, bt_b, kh_b, bh_b, v_b, pl_b, yb_ref))
    chains = [(d, p) for d in range(2) for p in range(N_PAIRS)]
    masks = [_tri_masks(d, PAIR, CHUNK) for d in range(2)]
    eye = jnp.where(masks[0][0] & masks[1][0], 1.0, 0.0)

    lane = lax.broadcasted_iota(jnp.int32, (CHUNK, LANES), 1)
    keep_first = jnp.where(lane < HD_A, 1.0, 0.0).astype(_bf16)

    def load(ref, p):
        return _stack_masked(ref[:, p * LANES:(p + 1) * LANES], keep_first)

    ops = {}
    for d, p in chains:
        rt, kt, at, bt, kh, bh, v = (load(dirs[d][j], p) for j in range(7))
        ops[d, p] = dict(rt=rt, at=at, bh=bh, kh=kh, v=v,
                         big=_dot_nt(jnp.concatenate([at, rt], axis=0), jnp.concatenate([kt, bt], axis=0)))
    for d, p in chains:
        o = ops[d, p]
        incl, strict = masks[d]
        big = o.pop("big")
        o["mak"] = jnp.where(strict, big[:PAIR, :PAIR], 0.0)
        o["n"] = jnp.where(strict, big[:PAIR, PAIR:], 0.0)
        o["mrk"] = jnp.where(incl, big[PAIR:, :PAIR], 0.0)
        o["mrb"] = jnp.where(incl, big[PAIR:, PAIR:], 0.0)
        o["t"] = eye - o["n"]
    for _ in range(5):
        for ch in chains:
            ops[ch]["n"] = _dot(ops[ch]["n"], ops[ch]["n"])
        for ch in chains:
            ops[ch]["t"] = ops[ch]["t"] + _dot(ops[ch]["t"], ops[ch]["n"])
    for ch in chains:
        o = ops[ch]
        o["mv"] = _dot(jnp.concatenate([o.pop("mak"), o.pop("mrk")], axis=0), o["v"])
    for ch in chains:
        o = ops[ch]
        o["xz"] = _dot(o.pop("t"), jnp.concatenate([o["at"], o["mv"][:PAIR].astype(_bf16)], axis=1))
    for ch in chains:
        o = ops[ch]
        o["qy"] = _dot(o.pop("mrb"), o["xz"])
    for ch in chains:
        o = ops[ch]
        o["upd"] = _dot_tn(o["xz"], o["bh"])
        o["vtk"] = _dot_tn(o["v"], o["kh"])
    for d, p in chains:
        o = ops[d, p]
        s0 = s_scr[d, p]
        q_eff = o["rt"].astype(_f32) - o["qy"][:, :LANES]
        y = _dot_nt(q_eff, s0) + o["mv"][PAIR:] - o["qy"][:, LANES:]
        dirs[d][8][:, p * LANES:(p + 1) * LANES] = y[:CHUNK] + y[CHUNK:]
        p_last = dirs[d][7][:, p * LANES:(p + 1) * LANES]
        s_scr[d, p] = s0 * p_last - _dot(s0, o["upd"][:LANES]) + o["vtk"] - o["upd"][LANES:]

    @pl.when((c == nc - 1) & jnp.logical_not(is_lat))
    def _():
        for d in range(2):
            for p in range(N_PAIRS):
                sp = s_scr[d, p]
                sout_ref[d, 2 * p] = sp[:HD_A, :HD_A]
                sout_ref[d, 2 * p + 1] = sp[HD_A:, HD_A:]


def _lat_state_spec(shape, st):
    nd = len(shape)
    return pl.BlockSpec((None,) + tuple(shape[1:]),
                        lambda s: (jnp.maximum(_seq_of(s, st) - st.n_ctx_seq, 0),) + (0,) * (nd - 1))


def _ctx_state_spec(shape, st):
    nd = len(shape)
    return pl.BlockSpec((None,) + tuple(shape[1:]),
                        lambda s: (jnp.minimum(_seq_of(s, st), st.n_ctx_seq - 1),) + (0,) * (nd - 1))


def _rwkv_scan(prep, s0, st):
    rt, kt, at, bt, kh, bh, v, p_last = prep
    n = v.shape[0]
    fwd = lambda s: s
    bwd = lambda s: _bwd_chunk(s, st)
    seq2 = lambda d, f: pl.BlockSpec((None, CHUNK, D_A), lambda s: (d, f(s), 0))
    seq1 = lambda f: pl.BlockSpec((CHUNK, D_A), lambda s: (f(s), 0))
    pls = lambda d, f: pl.BlockSpec((None, None, 1, D_A), lambda s: (d, f(s), 0, 0))
    side = lambda d, f: [seq2(d, f)] * 6 + [seq1(f), pls(d, f)]
    y_shape = jax.ShapeDtypeStruct((n, D_A), _f32)
    out_state = (st.n_ctx_seq,) + s0.shape[1:]
    args = (rt, kt, at, bt, kh, bh, v, p_last)
    return pl.pallas_call(
        functools.partial(_rwkv_scan_kernel, st=st),
        grid=(st.n_chunks,),
        in_specs=side(0, fwd) + side(1, bwd) + [_lat_state_spec(s0.shape, st)],
        out_specs=[seq1(fwd), seq1(bwd), _ctx_state_spec(out_state, st)],
        out_shape=[y_shape, y_shape, jax.ShapeDtypeStruct(out_state, _f32)],
        scratch_shapes=[pltpu.VMEM((2, N_PAIRS, LANES, LANES), _f32)],
        compiler_params=_cparams(1), name="rwkv_scan",
    )(*args, *args, s0)


def _gla_kernel(q_f, k_f, v_f, l_f, q_b, k_b, v_b, l_b, w_ref, bias_ref, s0_ref,
                of_ref, ob_ref, sout_ref, s_scr, *, st):
    s = pl.program_id(0)
    is_lat, nc, c, _ = _chunk_pos(s, st)

    @pl.when(c == 0)
    def _():
        keep = jnp.where(is_lat, 1.0, 0.0)
        for d in range(2):
            for h in range(H_B):
                s_scr[d, h] = s0_ref[d, h].T * keep

    dirs = ((q_f, k_f, v_f, l_f, of_ref), (q_b, k_b, v_b, l_b, ob_ref))
    work = []
    for d, (q_ref, k_ref, v_ref, l_ref, o_ref) in enumerate(dirs):
        incl, _ = _tri_masks(d)
        la = jax.nn.log_sigmoid(_dot3(l_ref[...], w_ref[d, 0], w_ref[d, 1]) + bias_ref[d:d + 1]) / GATE_NORM
        cum = _dot_exact_lhs(jnp.where(incl, 1.0, 0.0), la)
        tot = jnp.sum(la, axis=0, keepdims=True)
        k = k_ref[...]
        q_in = q_ref[...] * (DK_B ** -0.5) * jnp.exp(cum)
        k_in = k * jnp.exp(-cum)
        k_end = k * jnp.exp(tot - cum)
        e_tot = jnp.exp(tot)
        for h in range(H_B):
            ks = slice(h * DK_B, (h + 1) * DK_B)
            work.append((d, h, incl, q_in[:, ks], k_in[:, ks], k_end[:, ks], e_tot[:, ks]))
    att = [jnp.where(incl, _dot_nt(q_in, k_in), 0.0) for (_, _, incl, q_in, k_in, _, _) in work]
    for (d, h, _, q_in, _, k_end, e_tot), a in zip(work, att):
        v_ref, o_ref = dirs[d][2], dirs[d][4]
        vs = slice(h * DV_B, (h + 1) * DV_B)
        v = v_ref[:, vs]
        s_t = s_scr[d, h]
        o_ref[:, vs] = _dot(a, v) + _dot_nt(q_in, s_t)
        s_scr[d, h] = s_t * e_tot + _dot_tn(v, k_end)

    @pl.when((c == nc - 1) & jnp.logical_not(is_lat))
    def _():
        for d in range(2):
            for h in range(H_B):
                sout_ref[d, h] = s_scr[d, h].T


def _gla_scan(p, w, bias, s0, st):
    n = p.shape[0]
    out_state = (st.n_ctx_seq,) + s0.shape[1:]
    fwd = lambda s: s
    bwd = lambda s: _bwd_chunk(s, st)
    blk = lambda width, col, f: pl.BlockSpec((CHUNK, width), lambda s: (f(s), col))
    side = lambda f: [blk(QK_B, EV_Q_BLK, f), blk(QK_B, EV_K_BLK, f), blk(V_B, EV_V_BLK, f),
                      blk(EV_LORA_W, EV_LORA_BLK, f)]
    full = lambda a: pl.BlockSpec(a.shape, lambda s: (0,) * a.ndim)
    o_spec = lambda f: pl.BlockSpec((CHUNK, V_B), lambda s: (f(s), 0))
    o_shape = jax.ShapeDtypeStruct((n, V_B), _f32)
    return pl.pallas_call(
        functools.partial(_gla_kernel, st=st),
        grid=(st.n_chunks,),
        in_specs=side(fwd) + side(bwd) + [full(w), full(bias), _lat_state_spec(s0.shape, st)],
        out_specs=[o_spec(fwd), o_spec(bwd), _ctx_state_spec(out_state, st)],
        out_shape=[o_shape, o_shape, jax.ShapeDtypeStruct(out_state, _f32)],
        scratch_shapes=[pltpu.VMEM((2, H_B, DV_B, DK_B), _f32)],
        compiler_params=_cparams(1), name="gla_scan",
    )(p, p, p, p, p, p, p, p, w, bias, s0)


def _mlstm_kernel(q_f, k_f, v_f, g_f, q_b, k_b, v_b, g_b, bias_ref, c0_ref, n0_ref, m0_ref,
                  hf_ref, hb_ref, cout_ref, nout_ref, mout_ref, c_scr, n_scr, m_scr, *, st):
    s = pl.program_id(0)
    is_lat, nc, c, _ = _chunk_pos(s, st)

    @pl.when(c == 0)
    def _():
        keep = jnp.where(is_lat, 1.0, 0.0)
        for d in range(2):
            for h in range(H_C):
                c_scr[d, h] = c0_ref[d, h].T * keep
        n_scr[...] = n0_ref[...] * keep
        m_scr[...] = m0_ref[...] * keep

    L = CHUNK
    lane = lax.broadcasted_iota(jnp.int32, (L, LANES), 1)
    dirs = ((q_f, k_f, v_f, g_f, hf_ref), (q_b, k_b, v_b, g_b, hb_ref))
    work = []
    for d, (q_ref, k_ref, v_ref, g_ref, h_ref) in enumerate(dirs):
        gates = g_ref[...] + bias_ref[0:1]
        gates = jnp.where(lane < 2 * H_C, gates, jax.nn.log_sigmoid(gates))
        gates_t = gates.T
        for h in range(H_C):
            ji, jf = d * H_C + h, 2 * H_C + d * H_C + h
            ks = slice(h * DK_C, (h + 1) * DK_C)
            q = q_ref[:, ks] * (DK_C ** -0.5)
            k = k_ref[:, ks]
            work.append(dict(d=d, h=h, q=q, k=k, incl=_tri_masks(d)[0], incl_t=_tri_masks(1 - d)[0],
                             i_col=gates[:, ji:ji + 1], f_col=gates[:, jf:jf + 1],
                             i_row=gates_t[ji:ji + 1, :], f_row=gates_t[jf:jf + 1, :],
                             m_prev=m_scr[d, h], c_t=c_scr[d, h], n_row=n_scr[d, h]))
    for o in work:
        o["s"] = _dot_nt(o["q"], o["k"])
        o["qc"] = _dot_nt(o["q"], o["c_t"])
    for o in work:
        o["cum_col"] = jnp.sum(jnp.where(o["incl"], o["f_row"], 0.0), axis=1, keepdims=True)
        o["cum_row"] = jnp.sum(jnp.where(o["incl_t"], o["f_col"], 0.0), axis=0, keepdims=True)
        o["f_tot"] = jnp.sum(o["f_row"], axis=1, keepdims=True)
        o["qn"] = jnp.sum(o["q"] * o["n_row"], axis=1, keepdims=True)
    for o in work:
        o["log_d"] = jnp.where(o["incl"], o["cum_col"] - o["cum_row"] + o["i_row"], NEG_BIG)
        o["log_inter"] = o["cum_col"] + o["m_prev"]
        o["row_max"] = jnp.max(o["log_d"], axis=1, keepdims=True)
    for o in work:
        m_t = jnp.maximum(o["log_inter"], o["row_max"])
        o["m_t"] = m_t
        o["sm"] = o["s"] * jnp.exp(o.pop("log_d") - m_t)
        o["w_inter"] = jnp.exp(o["log_inter"] - m_t)
        o["m_new"] = m_t[L - 1:L, :] if o["d"] == 0 else m_t[0:1, :]
        o["kw"] = o["k"] * jnp.exp(o["f_tot"] - o["cum_col"] + o["i_col"] - o["m_new"])
        o["dec"] = jnp.exp(o["f_tot"] + o["m_prev"] - o["m_new"])
    for o in work:
        vs = slice(o["h"] * DV_C, (o["h"] + 1) * DV_C)
        o["v"] = dirs[o["d"]][2][:, vs]
        o["sv"] = _dot(o["sm"], o["v"])
        o["kv"] = _dot_tn(o["v"], o["kw"])
        o["s_sum"] = jnp.sum(o["sm"], axis=1, keepdims=True)
    for o in work:
        d, h = o["d"], o["h"]
        vs = slice(h * DV_C, (h + 1) * DV_C)
        num = o["w_inter"] * o["qc"] + o["sv"]
        den = o["w_inter"] * o["qn"] + o["s_sum"]
        dirs[d][4][:, vs] = num / jnp.maximum(jnp.abs(den), jnp.exp(-o["m_t"]))
        c_scr[d, h] = o["dec"] * o["c_t"] + o["kv"]
        n_scr[d, h] = o["dec"] * o["n_row"] + jnp.sum(o["kw"], axis=0, keepdims=True)
        m_scr[d, h] = o["m_new"]

    @pl.when((c == nc - 1) & jnp.logical_not(is_lat))
    def _():
        for d in range(2):
            for h in range(H_C):
                cout_ref[d, h] = c_scr[d, h].T
        nout_ref[...] = n_scr[...]
        mout_ref[...] = m_scr[...]


def _mlstm_scan(p, bias, c0, n0, m0, st):
    n = p.shape[0]
    fwd = lambda s: s
    bwd = lambda s: _bwd_chunk(s, st)
    blk = lambda width, col, f: pl.BlockSpec((CHUNK, width), lambda s: (f(s), col))
    side = lambda f: [blk(QK_C, 0, f), blk(QK_C, 1, f), blk(V_C, 1, f), blk(LANES, OD_GATE_BLK, f)]
    o_spec = lambda f: pl.BlockSpec((CHUNK, V_C), lambda s: (f(s), 0))
    o_shape = jax.ShapeDtypeStruct((n, V_C), _f32)
    states = (c0, n0, m0)
    out_states = [(st.n_ctx_seq,) + a.shape[1:] for a in states]
    return pl.pallas_call(
        functools.partial(_mlstm_kernel, st=st),
        grid=(st.n_chunks,),
        in_specs=side(fwd) + side(bwd) + [pl.BlockSpec(bias.shape, lambda s: (0, 0))]
        + [_lat_state_spec(a.shape, st) for a in states],
        out_specs=[o_spec(fwd), o_spec(bwd)] + [_ctx_state_spec(sh, st) for sh in out_states],
        out_shape=[o_shape, o_shape] + [jax.ShapeDtypeStruct(sh, _f32) for sh in out_states],
        scratch_shapes=[pltpu.VMEM((2, H_C, DV_C, DK_C), _f32), pltpu.VMEM((2, H_C, 1, DK_C), _f32),
                        pltpu.VMEM((2, H_C, 1, 1), _f32)],
        compiler_params=_cparams(1), name="mlstm_scan",
    )(p, p, p, p, p, p, p, p, bias, c0, n0, m0)


def _head_rms_gate(o, g_row, gate, n_heads, width):
    outs = []
    for h in range(n_heads):
        hs = slice(h * width, (h + 1) * width)
        oh = o[:, hs]
        outs.append(oh * lax.rsqrt(jnp.mean(oh * oh, axis=-1, keepdims=True) + EPS) * g_row[:, hs] * gate[:, hs])
    return jnp.concatenate(outs, axis=1)


def _even_post_kernel(yf_ref, yb_ref, bonus_ref, g_ref, of_ref, ob_ref, og_ref, par_ref, gn_ref, bd_ref, o_ref):
    y = yf_ref[...] + yb_ref[...]
    bd = bd_ref[...]
    mean = _dot_exact_rhs(y, bd) * (1.0 / HD_A)
    yc = y - mean
    var = _dot_exact_rhs(yc * yc, bd) * (1.0 / HD_A)
    par = par_ref[...]
    ya = (yc * lax.rsqrt(var + EPS) * par[0:1] + par[1:2] + bonus_ref[...]) * g_ref[...]
    og = og_ref[...]
    yb = _head_rms_gate(of_ref[...] + ob_ref[...], gn_ref[...], og * jax.nn.sigmoid(og), H_B, DV_B)
    o_ref[:, :D_A] = ya.astype(_bf16)
    o_ref[:, D_A:] = yb.astype(_bf16)


def _even_post(yf, yb, bonus, g, of, ob, p, ln_par, gla_g, bd, tm=256):
    n = yf.shape[0]
    row = lambda w, col=0: pl.BlockSpec((tm, w), lambda i: (i, col))
    full = lambda a: pl.BlockSpec(a.shape, lambda i: (0,) * a.ndim)
    return pl.pallas_call(
        _even_post_kernel,
        grid=(n // tm,),
        in_specs=[row(D_A)] * 4 + [row(V_B), row(V_B), row(V_B, EV_OG_BLK), full(ln_par), full(gla_g), full(bd)],
        out_specs=row(D_A + V_B),
        out_shape=jax.ShapeDtypeStruct((n, D_A + V_B), _bf16),
        compiler_params=_cparams(1), name="even_post",
    )(yf, yb, bonus, g, of, ob, p, ln_par, gla_g, bd)


def _odd_post_kernel(hf_ref, hb_ref, o_ref_in, gn_ref, out_ref):
    out_ref[...] = _head_rms_gate(hf_ref[...] + hb_ref[...], gn_ref[...], jax.nn.sigmoid(o_ref_in[...]),
                                  H_C, DV_C).astype(_bf16)


def _odd_post(hf, hb, p, gn, tm=256):
    n = hf.shape[0]
    row = lambda col=0: pl.BlockSpec((tm, V_C), lambda i: (i, col))
    return pl.pallas_call(
        _odd_post_kernel,
        grid=(n // tm,),
        in_specs=[row(), row(), row(2), pl.BlockSpec(gn.shape, lambda i: (0, 0))],
        out_specs=row(),
        out_shape=jax.ShapeDtypeStruct((n, V_C), _bf16),
        compiler_params=_cparams(1), name="odd_post",
    )(hf, hb, p, gn)


def _even_mixer(p, st, s_rwkv, s_gla, mu, w0, w2, a0, a2, g2, k_k, k_a, r_k, ln_g, ln_b, gla_w2, gla_b, gla_g):
    zpad = lambda x, rows_before, total: jnp.pad(x, ((rows_before, total - rows_before - x.shape[0]), (0, 0)))
    mu_main = mu[:EV_MAIN].reshape(1, EV_MAIN)
    mu_lora = jnp.pad(mu[EV_MAIN:], (0, EV_LORA_W - (A_IN - EV_MAIN))).reshape(1, EV_LORA_W)
    split = lambda w: jnp.stack(_split2(w), axis=0)
    ww = jnp.stack([split(zpad(w2[d], d * LORA_W, EV_LORA_W)) for d in range(2)])
    wa = jnp.stack([split(zpad(a2[d], 2 * LORA_W + d * LORA_A, EV_LORA_W)) for d in range(2)])
    wg = zpad(g2, 2 * LORA_W + 2 * LORA_A, EV_LORA_W).astype(_bf16)
    par = jnp.stack([w0[0], w0[1], a0[0], a0[1], k_k, k_a, r_k, jnp.zeros_like(k_k)])
    head = np.arange(D_A) // HD_A
    bd = jnp.asarray(head[:, None] == head[None, :], _bf16)
    *prep, bonus, g = _rwkv_prep(p, st, mu_main, mu_lora, ww, wa, wg, par, bd)
    yf, yb, s_rw = _rwkv_scan(prep, s_rwkv, st)
    wl = jnp.stack([split(zpad(gla_w2[d], GLA_LORA_OFF + d * LORA_GK, EV_LORA_W)) for d in range(2)])
    bias = jnp.pad(gla_b, ((0, MOD_ROWS - 2), (0, 0)))
    of, ob, s_gl = _gla_scan(p, wl, bias, s_gla, st)
    ln_par = jnp.pad(jnp.stack([ln_g, ln_b]), ((0, MOD_ROWS - 2), (0, 0)))
    y_in = _even_post(yf, yb, bonus, g, of, ob, p, ln_par, gla_g.reshape(1, V_B), bd)
    return y_in, s_rw, s_gl


def _odd_mixer(p, st, s_c, s_n, s_m, b_i, b_f, g):
    bias = jnp.concatenate([b_i.reshape(-1), b_f.reshape(-1), jnp.zeros((LANES - 4 * H_C,), _f32)])
    bias = jnp.pad(bias[None, :], ((0, MOD_ROWS - 1), (0, 0)))
    hf, hb, c_new, n_new, m_new = _mlstm_scan(p, bias, s_c, s_n[..., None, :], s_m[..., None, None], st)
    y_in = _odd_post(hf, hb, p, g.reshape(1, V_C))
    return y_in, c_new, n_new[..., 0, :], m_new[..., 0, 0]


def _moe(h, logits, w1, w3, w2, layer, tm=FFN_TM):
    n_tok = h.shape[0]
    n_pairs = n_tok * TOP_K
    n_tiles = n_pairs // tm + N_EXPERTS
    top_v, top_i = lax.top_k(logits[:, :N_EXPERTS], TOP_K)
    gates = jax.nn.softmax(top_v, axis=-1)
    pair_e = top_i.reshape(-1)
    order = jnp.argsort(pair_e, stable=True)
    sorted_e = pair_e[order]
    counts = jnp.sum(jax.nn.one_hot(pair_e, N_EXPERTS, dtype=jnp.int32), axis=0)
    tiles_per_e = (counts + tm - 1) // tm
    tile_start = jnp.cumsum(tiles_per_e) - tiles_per_e
    group_start = jnp.cumsum(counts) - counts
    rank = jnp.arange(n_pairs, dtype=jnp.int32) - group_start[sorted_e]
    dest = tile_start[sorted_e] * tm + rank
    slot_token = jnp.zeros((n_tiles * tm,), jnp.int32).at[dest].set((order // TOP_K).astype(jnp.int32))
    pair_slot = jnp.zeros((n_pairs,), jnp.int32).at[order].set(dest.astype(jnp.int32))
    n_valid = jnp.sum(tiles_per_e)
    tile_ids = jnp.arange(n_tiles, dtype=jnp.int32)
    tile_e = jnp.sum((tile_ids[:, None] >= tile_start[None, :]).astype(jnp.int32), axis=1) - 1
    last_e = jnp.max(jnp.where(counts > 0, jnp.arange(N_EXPERTS), 0))
    tile_valid = tile_ids < n_valid
    tile_e = jnp.where(tile_valid, tile_e, last_e).astype(jnp.int32)
    tile_rows = jnp.clip(counts[tile_e] - (tile_ids - tile_start[tile_e]) * tm, 0, tm)
    tile_rows = jnp.where(tile_valid, tile_rows, 0).astype(jnp.int32)
    yg = _gathered_ffn(h, slot_token, w1, w3, w2, layer, tile_e, tile_rows, tm=tm)
    return _moe_combine(yg, pair_slot, gates)


def kernel(x_prompt, x_sample, state_rwkv, state_gla, state_mlstm_c, state_mlstm_n, state_mlstm_m, c, c_ctx, mod_w, mod_b, norm_g, final_g, ev_w_in, ev_mu, rw_w0, rw_w2, rw_a0, rw_a2, rw_g2, rw_kk, rw_ka, rw_rk, rw_ln_g, rw_ln_b, gla_w2, gla_b, gla_norm_g, ev_w_out, ffn_w1, ffn_w3, ffn_w2, od_w_in, ml_b_i, ml_b_f, ml_norm_g, od_w_out, moe_router, moe_w1, moe_w3, moe_w2):
    D = D_MODEL
    Bp, Tp, _ = x_prompt.shape
    Bs, Ts, _ = x_sample.shape
    assert Ts // GRID_W * GRID_W == Ts and GRID_W == CHUNK and Tp % CHUNK == 0
    n_ctx = Bp * Tp
    st = _Streams(n_ctx_chunks=n_ctx // CHUNK, ctx_cps=Tp // CHUNK, lat_cps=Ts // CHUNK,
                  n_chunks=(n_ctx + Bs * Ts) // CHUNK, n_ctx_seq=Bp, n_seq=Bp + Bs)
    x = jnp.concatenate([x_prompt.reshape(n_ctx, D), x_sample.reshape(Bs * Ts, D)], axis=0)
    n_tok = x.shape[0]

    cond = jnp.concatenate([c_ctx[None, :], c, jnp.zeros((MOD_ROWS - 1 - Bs, D), _f32)], axis=0)
    mod = _mod_table(jax.nn.silu(cond).astype(_bf16), mod_w, mod_b)

    dense_te = jnp.zeros((n_tok // FFN_TM,), jnp.int32)
    dense_rows = jnp.full((n_tok // FFN_TM,), FFN_TM, jnp.int32)
    new_rw, new_gla, new_c, new_n, new_m = [], [], [], [], []
    pending = None

    for l in range(DEPTH):
        i = l // 2
        sh1, sc1, gt1, sh2, sc2, gt2 = [mod[l, :, j * D:(j + 1) * D] for j in range(6)]
        if pending is None:
            h = _norm_mod(x, norm_g[l, 0], sc1, sh1, st)[0]
        else:
            x, h = _norm_mod(x, norm_g[l, 0], sc1, sh1, st, res=pending)
        if l % 2 == 0:
            w = ev_w_in[i]
            b0 = A_IN
            w_in = jnp.concatenate([
                w[:, :EV_MAIN], w[:, b0 + 2 * QK_B:b0 + 2 * QK_B + V_B], w[:, b0 + 2 * QK_B + V_B + 2 * LORA_GK:],
                w[:, b0:b0 + 2 * QK_B], w[:, EV_MAIN:A_IN],
                w[:, b0 + 2 * QK_B + V_B:b0 + 2 * QK_B + V_B + 2 * LORA_GK],
                jnp.zeros((D, EV_LORA_W - (A_IN - EV_MAIN) - 2 * LORA_GK), _f32)], axis=1).astype(_bf16)
            p = _matmul(h, w_in)
            y_in, s_rw, s_gl = _even_mixer(
                p, st, state_rwkv[:, i], state_gla[:, i], ev_mu[i], rw_w0[i], rw_w2[i],
                rw_a0[i], rw_a2[i], rw_g2[i], rw_kk[i], rw_ka[i], rw_rk[i], rw_ln_g[i], rw_ln_b[i],
                gla_w2[i], gla_b[i], gla_norm_g[i])
            new_rw.append(s_rw)
            new_gla.append(s_gl)
            x = _matmul_residual(y_in, ev_w_out[i].astype(_bf16), x, gt1, st)
        else:
            w_in = jnp.pad(od_w_in[i], ((0, 0), (0, OD_N - od_w_in.shape[2]))).astype(_bf16)
            p = _matmul(h, w_in, tn=OD_N // 7)
            y_in, c_new, n_new, m_new = _odd_mixer(
                p, st, state_mlstm_c[:, i], state_mlstm_n[:, i], state_mlstm_m[:, i],
                ml_b_i[i], ml_b_f[i], ml_norm_g[i])
            new_c.append(c_new)
            new_n.append(n_new)
            new_m.append(m_new)
            x = _matmul_residual(y_in, od_w_out[i].astype(_bf16), x, gt1, st)

        if l % 2 == 0:
            h = _norm_mod(x, norm_g[l, 1], sc2, sh2, st)[0]
            f = _grouped_ffn(h, ffn_w1[:, None], ffn_w3[:, None], ffn_w2[:, None], i, dense_te, dense_rows)
        else:
            router = _split2(jnp.pad(moe_router[i], ((0, 0), (0, LANES - N_EXPERTS))))
            h, logits = _norm_mod(x, norm_g[l, 1], sc2, sh2, st, router=router, h_dtype=_f32)
            f = _moe(h, logits, moe_w1, moe_w3, moe_w2, i)
        pending = (f, gt2)

    y = _final_norm(x, pending[0], pending[1], final_g, st)
    return (y[:n_ctx].reshape(Bp, Tp, D), y[n_ctx:].reshape(Bs, Ts, D),
            jnp.stack(new_rw, axis=1), jnp.stack(new_gla, axis=1),
            jnp.stack(new_c, axis=1), jnp.stack(new_n, axis=1), jnp.stack(new_m, axis=1))
```

```python
import functools
from typing import NamedTuple

import jax
import jax.numpy as jnp
import numpy as np
from jax import lax
from jax.experimental import pallas as pl
from jax.experimental.pallas import tpu as pltpu

D_MODEL = 2048
DEPTH = 4
GRID_W = 64
EPS = 1e-6
CHUNK = 64
H_A, HD_A = 16, 64
D_A = H_A * HD_A
LORA_W, LORA_A, LORA_G = 64, 64, 128
A_IN = 3 * D_A + 2 * LORA_W + 2 * LORA_A + LORA_G
H_B, DK_B, DV_B = 4, 128, 256
QK_B, V_B = H_B * DK_B, H_B * DV_B
LORA_GK = 16
GATE_NORM = 16.0
H_C, DK_C, DV_C = 8, 128, 256
QK_C, V_C = H_C * DK_C, H_C * DV_C
D_FF = 5632
N_EXPERTS = 8
TOP_K = 2

LANES = 128
MOD_ROWS = 8
VMEM_LIMIT = 48 * 1024 * 1024
FFN_VMEM_LIMIT = 56 * 1024 * 1024
NEG_BIG = -1e30

EV_MAIN = 3 * D_A
EV_V_BLK = EV_MAIN // V_B
EV_OG_BLK = EV_V_BLK + 1
EV_Q_BLK = (EV_MAIN + 2 * V_B) // QK_B
EV_K_BLK = EV_Q_BLK + 1
EV_LORA_W = 512
EV_LORA_BLK = (EV_MAIN + 2 * V_B + 2 * QK_B) // EV_LORA_W
EV_N = EV_MAIN + 2 * V_B + 2 * QK_B + EV_LORA_W
GLA_LORA_OFF = 2 * LORA_W + 2 * LORA_A + LORA_G
OD_GATE_BLK = (2 * QK_C + 2 * V_C) // LANES
OD_N = 2 * QK_C + 2 * V_C + LANES

_f32 = jnp.float32
_bf16 = jnp.bfloat16


class _Streams(NamedTuple):
    n_ctx_chunks: int
    ctx_cps: int
    lat_cps: int
    n_chunks: int
    n_ctx_seq: int
    n_seq: int


def _chunk_pos(s, st):
    is_lat = s >= st.n_ctx_chunks
    sl = s - st.n_ctx_chunks
    nc = jnp.where(is_lat, st.lat_cps, st.ctx_cps)
    c = jnp.where(is_lat, sl % st.lat_cps, s % st.ctx_cps)
    seq = jnp.where(is_lat, st.n_ctx_seq + sl // st.lat_cps, s // st.ctx_cps)
    return is_lat, nc, c, seq


def _bwd_chunk(s, st):
    _, nc, c, _ = _chunk_pos(s, st)
    return s - c + (nc - 1 - c)


def _seq_of(s, st):
    return _chunk_pos(s, st)[3]


def _mod_row_of_chunk(s, st):
    is_lat, _, _, seq = _chunk_pos(s, st)
    return jnp.where(is_lat, 1 + seq - st.n_ctx_seq, 0)


def _cparams(n_axes):
    return pltpu.CompilerParams(dimension_semantics=("arbitrary",) * n_axes,
                                vmem_limit_bytes=VMEM_LIMIT)


def _dot(a, b):
    return jnp.dot(a.astype(_bf16), b.astype(_bf16), preferred_element_type=_f32)


def _dot_nt(a, b):
    return lax.dot_general(a.astype(_bf16), b.astype(_bf16), (((1,), (1,)), ((), ())),
                           preferred_element_type=_f32)


def _dot_tn(a, b):
    return lax.dot_general(a.astype(_bf16), b.astype(_bf16), (((0,), (0,)), ((), ())),
                           preferred_element_type=_f32)


def _split2(x):
    hi = x.astype(_bf16)
    return hi, (x - hi.astype(_f32)).astype(_bf16)


def _dot_exact_lhs(tri, x):
    hi, lo = _split2(x)
    t = tri.astype(_bf16)
    return jnp.dot(t, hi, preferred_element_type=_f32) + jnp.dot(t, lo, preferred_element_type=_f32)


def _dot_exact_rhs(x, ones):
    hi, lo = _split2(x)
    return jnp.dot(hi, ones, preferred_element_type=_f32) + jnp.dot(lo, ones, preferred_element_type=_f32)


def _head_sum(x, seg, seg_t):
    return _dot_exact_rhs(_dot_exact_rhs(x, seg), seg_t)


def _dot3(a, b_hi, b_lo):
    a_hi, a_lo = _split2(a)
    d = lambda u, v: jnp.dot(u, v, preferred_element_type=_f32)
    return d(a_hi, b_hi) + d(a_lo, b_hi) + d(a_hi, b_lo)


def _tri_masks(direction, n=CHUNK, period=CHUNK):
    t = lax.broadcasted_iota(jnp.int32, (n, n), 0)
    i = lax.broadcasted_iota(jnp.int32, (n, n), 1)
    same = (t // period) == (i // period) if n != period else True
    if direction == 0:
        return (i <= t) & same, (i < t) & same
    return (i >= t) & same, (i > t) & same


def _mod_kernel(a_ref, w_ref, b_ref, o_ref):
    o_ref[...] = _dot(a_ref[...], w_ref[...]) + b_ref[...]


def _mod_table(silu_rows, mod_w, mod_b, tn=1024):
    n_l, k, n = mod_w.shape
    return pl.pallas_call(
        _mod_kernel,
        grid=(n_l, n // tn),
        in_specs=[pl.BlockSpec((MOD_ROWS, k), lambda l, j: (0, 0)),
                  pl.BlockSpec((None, k, tn), lambda l, j: (l, 0, j)),
                  pl.BlockSpec((None, 1, tn), lambda l, j: (l, 0, j))],
        out_specs=pl.BlockSpec((None, MOD_ROWS, tn), lambda l, j: (l, 0, j)),
        out_shape=jax.ShapeDtypeStruct((n_l, MOD_ROWS, n), _f32),
        compiler_params=_cparams(2), name="mod_table",
    )(silu_rows, mod_w, mod_b.reshape(n_l, 1, n))


def _norm_mod_kernel(*refs, tm, st, has_res, has_router):
    refs = list(refs)
    x_ref = refs.pop(0)
    f_ref, gate_ref = (refs.pop(0), refs.pop(0)) if has_res else (None, None)
    g_ref, sc_ref, sh_ref = refs.pop(0), refs.pop(0), refs.pop(0)
    r_hi, r_lo = (refs.pop(0), refs.pop(0)) if has_router else (None, None)
    xo_ref = refs.pop(0) if has_res else None
    h_ref = refs.pop(0)
    row = _mod_row_of_chunk(pl.program_id(0) * (tm // CHUNK), st)
    x = x_ref[...]
    if has_res:
        x = x + gate_ref[pl.ds(row, 1), :] * f_ref[...]
        xo_ref[...] = x
    inv = lax.rsqrt(jnp.mean(x * x, axis=-1, keepdims=True) + EPS)
    h = x * inv * g_ref[...] * (1.0 + sc_ref[pl.ds(row, 1), :]) + sh_ref[pl.ds(row, 1), :]
    h_ref[...] = h.astype(h_ref.dtype)
    if has_router:
        refs.pop(0)[...] = _dot3(h, r_hi[...], r_lo[...])


def _norm_mod(x, g, sc, sh, st, res=None, router=None, h_dtype=_bf16, tm=256):
    m, d = x.shape
    row = pl.BlockSpec((tm, d), lambda i: (i, 0))
    tab = pl.BlockSpec((MOD_ROWS, d), lambda i: (0, 0))
    args, specs = [x], [row]
    outs, out_specs = [], []
    if res is not None:
        args += list(res)
        specs += [row, tab]
        outs.append(jax.ShapeDtypeStruct((m, d), _f32))
        out_specs.append(row)
    args += [g.reshape(1, d), sc, sh]
    specs += [pl.BlockSpec((1, d), lambda i: (0, 0)), tab, tab]
    outs.append(jax.ShapeDtypeStruct((m, d), h_dtype))
    out_specs.append(row)
    if router is not None:
        args += list(router)
        specs += [pl.BlockSpec((d, LANES), lambda i: (0, 0))] * 2
        outs.append(jax.ShapeDtypeStruct((m, LANES), _f32))
        out_specs.append(pl.BlockSpec((tm, LANES), lambda i: (i, 0)))
    return pl.pallas_call(
        functools.partial(_norm_mod_kernel, tm=tm, st=st, has_res=res is not None, has_router=router is not None),
        grid=(m // tm,), in_specs=specs, out_specs=out_specs, out_shape=outs,
        compiler_params=_cparams(1), name="norm_mod",
    )(*args)


def _final_norm_kernel(x_ref, f_ref, gate_ref, g_ref, o_ref, *, tm, st):
    row = _mod_row_of_chunk(pl.program_id(0) * (tm // CHUNK), st)
    x = x_ref[...] + gate_ref[pl.ds(row, 1), :] * f_ref[...]
    o_ref[...] = x * lax.rsqrt(jnp.mean(x * x, axis=-1, keepdims=True) + EPS) * g_ref[...]


def _final_norm(x, f, gate, g, st, tm=256):
    m, d = x.shape
    row = pl.BlockSpec((tm, d), lambda i: (i, 0))
    return pl.pallas_call(
        functools.partial(_final_norm_kernel, tm=tm, st=st),
        grid=(m // tm,),
        in_specs=[row, row, pl.BlockSpec((MOD_ROWS, d), lambda i: (0, 0)), pl.BlockSpec((1, d), lambda i: (0, 0))],
        out_specs=row,
        out_shape=jax.ShapeDtypeStruct((m, d), _f32),
        compiler_params=_cparams(1), name="final_norm",
    )(x, f, gate, g.reshape(1, d))


def _mm_kernel(a_ref, w_ref, o_ref):
    o_ref[...] = _dot(a_ref[...], w_ref[...])


def _mm_res_kernel(a_ref, w_ref, res_ref, gate_ref, o_ref, *, tm, st):
    row = _mod_row_of_chunk(pl.program_id(1) * (tm // CHUNK), st)
    gate = gate_ref[pl.ds(row, 1), :]
    o_ref[...] = res_ref[...] + gate * _dot(a_ref[...], w_ref[...])


def _matmul(a, w, tm=512, tn=512):
    m, k = a.shape
    n = w.shape[1]
    tm = min(tm, m)
    return pl.pallas_call(
        _mm_kernel,
        grid=(n // tn, m // tm),
        in_specs=[pl.BlockSpec((tm, k), lambda j, i: (i, 0)),
                  pl.BlockSpec((k, tn), lambda j, i: (0, j))],
        out_specs=pl.BlockSpec((tm, tn), lambda j, i: (i, j)),
        out_shape=jax.ShapeDtypeStruct((m, n), _f32),
        compiler_params=_cparams(2), name="matmul",
    )(a, w)


def _matmul_residual(a, w, res, gate, st, tm=512, tn=1024):
    m, k = a.shape
    n = w.shape[1]
    tm = min(tm, m)
    return pl.pallas_call(
        functools.partial(_mm_res_kernel, tm=tm, st=st),
        grid=(n // tn, m // tm),
        in_specs=[pl.BlockSpec((tm, k), lambda j, i: (i, 0)),
                  pl.BlockSpec((k, tn), lambda j, i: (0, j)),
                  pl.BlockSpec((tm, tn), lambda j, i: (i, j)),
                  pl.BlockSpec((MOD_ROWS, tn), lambda j, i: (0, j))],
        out_specs=pl.BlockSpec((tm, tn), lambda j, i: (i, j)),
        out_shape=jax.ShapeDtypeStruct((m, n), _f32),
        compiler_params=_cparams(2), name="matmul_residual",
    )(a, w, res, gate)


FFN_TM = 1024
FFN_TF = 256
FFN_HALF = 512


def _ffn_kernel(te_ref, tr_ref, x_ref, w1_ref, w3_ref, w2_ref, o_ref, *, tm):
    i, j = pl.program_id(0), pl.program_id(1)
    rows = tr_ref[i]

    @pl.when(j == 0)
    def _():
        o_ref[...] = jnp.zeros_like(o_ref)

    def swiglu(n_rows):
        x = x_ref[0:n_rows, :]
        a = _dot(x, w1_ref[...])
        b = _dot(x, w3_ref[...])
        o_ref[0:n_rows, :] += _dot(a * jax.nn.sigmoid(a) * b, w2_ref[...])

    @pl.when(rows > FFN_HALF)
    def _():
        swiglu(tm)

    @pl.when((rows > 0) & (rows <= FFN_HALF))
    def _():
        swiglu(FFN_HALF)


def _row_copy(src_hbm, src_row, dst, dst_row, sem):
    return pltpu.make_async_copy(src_hbm.at[pl.ds(src_row, 1), :], dst.at[pl.ds(dst_row, 1), :], sem)


def _ffn_gather_kernel(te_ref, tr_ref, slot_ref, h_hbm, w1_ref, w3_ref, w2_ref, o_ref, xf_scr, xb_scr, sem,
                       *, tm, n_tiles):
    i, j = pl.program_id(0), pl.program_id(1)
    rows = tr_ref[i]

    def start_gather(tile):
        def body(r, carry):
            _row_copy(h_hbm, slot_ref[tile * tm + r], xf_scr, r, sem).start()
            return carry
        lax.fori_loop(0, tr_ref[tile], body, 0)

    def wait_gather(tile):
        def body(r, carry):
            _row_copy(h_hbm, 0, xf_scr, r, sem).wait()
            return carry
        lax.fori_loop(0, tr_ref[tile], body, 0)

    @pl.when(j == 0)
    def _():
        @pl.when(i == 0)
        def _():
            xf_scr[...] = jnp.zeros_like(xf_scr)
            start_gather(0)

        wait_gather(i)
        xb_scr[...] = xf_scr[...].astype(_bf16)
        o_ref[...] = jnp.zeros_like(o_ref)

        @pl.when(i + 1 < n_tiles)
        def _():
            start_gather(i + 1)

    def swiglu(n_rows):
        x = xb_scr[0:n_rows, :]
        a = _dot(x, w1_ref[...])
        b = _dot(x, w3_ref[...])
        o_ref[0:n_rows, :] += _dot(a * jax.nn.sigmoid(a) * b, w2_ref[...])

    @pl.when(rows > FFN_HALF)
    def _():
        swiglu(tm)

    @pl.when((rows > 0) & (rows <= FFN_HALF))
    def _():
        swiglu(FFN_HALF)


def _gathered_ffn(h, slot_token, w1, w3, w2, layer, tile_expert, tile_rows, tm=FFN_TM, tf=FFN_TF):
    d = h.shape[1]
    f = w1.shape[3]
    nj = f // tf
    n_tiles = tile_expert.shape[0]

    def jeff(i, j, tr):
        return jnp.where(tr[i] > 0, j, nj - 1)

    return pl.pallas_call(
        functools.partial(_ffn_gather_kernel, tm=tm, n_tiles=n_tiles),
        grid_spec=pltpu.PrefetchScalarGridSpec(
            num_scalar_prefetch=3,
            grid=(n_tiles, nj),
            in_specs=[pl.BlockSpec(memory_space=pl.ANY),
                      pl.BlockSpec((None, None, d, tf), lambda i, j, te, tr, sl: (layer, te[i], 0, jeff(i, j, tr))),
                      pl.BlockSpec((None, None, d, tf), lambda i, j, te, tr, sl: (layer, te[i], 0, jeff(i, j, tr))),
                      pl.BlockSpec((None, None, tf, d), lambda i, j, te, tr, sl: (layer, te[i], jeff(i, j, tr), 0))],
            out_specs=pl.BlockSpec((tm, d), lambda i, j, te, tr, sl: (i, 0)),
            scratch_shapes=[pltpu.VMEM((tm, d), _f32), pltpu.VMEM((tm, d), _bf16), pltpu.SemaphoreType.DMA]),
        out_shape=jax.ShapeDtypeStruct((n_tiles * tm, d), _f32),
        compiler_params=pltpu.CompilerParams(dimension_semantics=("arbitrary", "arbitrary"),
                                             vmem_limit_bytes=FFN_VMEM_LIMIT),
        name="gathered_ffn",
    )(tile_expert, tile_rows, slot_token, h, w1, w3, w2)


COMBINE_TM = 256


def _combine_kernel(slot_ref, yg_hbm, gates_ref, o_ref, buf, sem, *, tm):
    base = pl.program_id(0) * tm * TOP_K

    def start(t, carry):
        for k in range(TOP_K):
            _row_copy(yg_hbm, slot_ref[base + t * TOP_K + k], buf.at[k], t, sem).start()
        return carry

    def wait(t, carry):
        for k in range(TOP_K):
            _row_copy(yg_hbm, 0, buf.at[k], t, sem).wait()
        return carry

    lax.fori_loop(0, tm, start, 0)
    lax.fori_loop(0, tm, wait, 0)
    g = gates_ref[...]
    o_ref[...] = g[:, 0:1] * buf[0] + g[:, 1:2] * buf[1]


def _moe_combine(yg, pair_slot, gates, tm=COMBINE_TM):
    n_tok = gates.shape[0]
    d = yg.shape[1]
    return pl.pallas_call(
        functools.partial(_combine_kernel, tm=tm),
        grid_spec=pltpu.PrefetchScalarGridSpec(
            num_scalar_prefetch=1,
            grid=(n_tok // tm,),
            in_specs=[pl.BlockSpec(memory_space=pl.ANY),
                      pl.BlockSpec((tm, TOP_K), lambda b, sl: (b, 0))],
            out_specs=pl.BlockSpec((tm, d), lambda b, sl: (b, 0)),
            scratch_shapes=[pltpu.VMEM((TOP_K, tm, d), _f32), pltpu.SemaphoreType.DMA]),
        out_shape=jax.ShapeDtypeStruct((n_tok, d), _f32),
        compiler_params=_cparams(1), name="moe_combine",
    )(pair_slot, yg, gates)


def _grouped_ffn(x, w1, w3, w2, layer, tile_expert, tile_rows, tm=FFN_TM, tf=FFN_TF):
    m, d = x.shape
    f = w1.shape[3]
    nj = f // tf
    n_tiles = m // tm

    def jeff(i, j, tr):
        return jnp.where(tr[i] > 0, j, nj - 1)

    return pl.pallas_call(
        functools.partial(_ffn_kernel, tm=tm),
        grid_spec=pltpu.PrefetchScalarGridSpec(
            num_scalar_prefetch=2,
            grid=(n_tiles, nj),
            in_specs=[pl.BlockSpec((tm, d), lambda i, j, te, tr: (i, 0)),
                      pl.BlockSpec((None, None, d, tf), lambda i, j, te, tr: (layer, te[i], 0, jeff(i, j, tr))),
                      pl.BlockSpec((None, None, d, tf), lambda i, j, te, tr: (layer, te[i], 0, jeff(i, j, tr))),
                      pl.BlockSpec((None, None, tf, d), lambda i, j, te, tr: (layer, te[i], jeff(i, j, tr), 0))],
            out_specs=pl.BlockSpec((tm, d), lambda i, j, te, tr: (i, 0))),
        out_shape=jax.ShapeDtypeStruct((m, d), _f32),
        compiler_params=pltpu.CompilerParams(dimension_semantics=("arbitrary", "arbitrary"),
                                             vmem_limit_bytes=FFN_VMEM_LIMIT),
        name="grouped_ffn",
    )(tile_expert, tile_rows, x, w1, w3, w2)


PREP_CHUNKS = 4
PREP_ROWS = PREP_CHUNKS * CHUNK


def _token_shift_block(x, prev, nxt, lane0, is_lat, first, last):
    rows, n = x.shape
    row = lax.broadcasted_iota(jnp.int32, x.shape, 0)
    ch = lax.broadcasted_iota(jnp.int32, x.shape, 1) + lane0
    down = pltpu.roll(x, 1, 0)
    up = pltpu.roll(x, rows - 1, 0)

    def context():
        zero_row = jnp.zeros((1, n), _f32)
        row_prev = jnp.where(first, zero_row, prev[CHUNK - 1:CHUNK, :])
        row_next = jnp.where(last, zero_row, nxt[0:1, :])
        return jnp.where(ch < A_IN // 2,
                         jnp.where(row == 0, row_prev, down),
                         jnp.where(row == rows - 1, row_next, up))

    def latent():
        q = A_IN // 4
        col = row % GRID_W
        above = jnp.concatenate([jnp.where(first, 0.0, prev), x[:rows - CHUNK]], axis=0)
        below = jnp.concatenate([x[CHUNK:], jnp.where(last, 0.0, nxt)], axis=0)
        return jnp.where(ch < q, jnp.where(col == 0, 0.0, down),
                         jnp.where(ch < 2 * q, jnp.where(col == GRID_W - 1, 0.0, up),
                                   jnp.where(ch < 3 * q, above, below)))

    return lax.cond(is_lat, latent, context)


def _rwkv_prep_kernel(xm_ref, pm_ref, nm_ref, xl_ref, pl_ref, nl_ref, mum_ref, mul_ref,
                      ww_ref, wa_ref, wg_ref, par_ref, seg_ref, segt_ref,
                      rt_ref, kt_ref, at_ref, bt_ref, kh_ref, bh_ref, v_ref, plast_ref, bonus_ref, g_ref,
                      *, st):
    s = pl.program_id(0) * PREP_CHUNKS
    is_lat, nc, c, _ = _chunk_pos(s, st)
    first, last = c == 0, c + PREP_CHUNKS == nc
    xm, xl = xm_ref[...], xl_ref[...]
    sm = _token_shift_block(xm, pm_ref[...], nm_ref[...], 0, is_lat, first, last)
    sl = _token_shift_block(xl, pl_ref[...], nl_ref[...], EV_MAIN, is_lat, first, last)
    pm = xm + mum_ref[...] * (sm - xm)
    lora = xl + mul_ref[...] * (sl - xl)
    r, k, v = pm[:, :D_A], pm[:, D_A:2 * D_A], pm[:, 2 * D_A:]
    par = par_ref[...]
    k_k, k_a, r_k = par[4:5], par[5:6], par[6:7]
    seg, seg_t = seg_ref[...], segt_ref[...]
    kk = k * k_k
    kk = kk / jnp.maximum(jnp.sqrt(_head_sum(kk * kk, seg, seg_t)), 1e-12)
    g_ref[...] = _dot(jax.nn.sigmoid(lora[:, 2 * LORA_W + 2 * LORA_A:GLA_LORA_OFF]), wg_ref[...])
    v_ref[...] = v.astype(_bf16)
    tanh_w = jnp.tanh(lora[:, :2 * LORA_W])
    lora_a = lora[:, 2 * LORA_W:2 * LORA_W + 2 * LORA_A]
    kdir_sum = jnp.zeros_like(k)
    t = lax.broadcasted_iota(jnp.int32, (PREP_ROWS, PREP_ROWS), 0)
    i = lax.broadcasted_iota(jnp.int32, (PREP_ROWS, PREP_ROWS), 1)
    same_chunk = jnp.where((t // CHUNK) == (i // CHUNK), 1.0, 0.0)
    for d in range(2):
        w = -jax.nn.softplus(-(par[d:d + 1] + _dot3(tanh_w, ww_ref[d, 0], ww_ref[d, 1]))) - 0.5
        ld = -jnp.exp(w)
        a = jax.nn.sigmoid(par[2 + d:3 + d] + _dot3(lora_a, wa_ref[d, 0], wa_ref[d, 1]))
        kdir = k * (1.0 + (a - 1.0) * k_a)
        kdir_sum = kdir_sum + kdir
        incl, _ = _tri_masks(d, PREP_ROWS, CHUNK)
        cum = _dot_exact_lhs(jnp.where(incl, 1.0, 0.0), ld)
        tot = _dot_exact_lhs(same_chunk, ld)
        e_pos, e_neg, e_end = jnp.exp(cum), jnp.exp(-cum), jnp.exp(tot - cum)
        kka = kk * a
        rt_ref[d] = (r * e_pos).astype(_bf16)
        kt_ref[d] = (kdir * e_neg).astype(_bf16)
        at_ref[d] = (kk * jnp.exp(cum - ld)).astype(_bf16)
        bt_ref[d] = (kka * e_neg).astype(_bf16)
        kh_ref[d] = (kdir * e_end).astype(_bf16)
        bh_ref[d] = (kka * e_end).astype(_bf16)
        e_tot = jnp.exp(tot)
        for j in range(PREP_CHUNKS):
            plast_ref[d, j] = e_tot[j * CHUNK:j * CHUNK + 1, :]
    bonus_ref[...] = _head_sum(r * kdir_sum * r_k, seg, seg_t) * v


def _rwkv_prep(p, st, mu_main, mu_lora, ww, wa, wg, par, seg, seg_t):
    n = p.shape[0]
    nch = st.n_chunks
    assert st.ctx_cps % PREP_CHUNKS == 0 and st.lat_cps % PREP_CHUNKS == 0
    cur = lambda b: (b, 0)
    prv = lambda b: (jnp.maximum(b * PREP_CHUNKS - 1, 0), 0)
    nxt = lambda b: (jnp.minimum((b + 1) * PREP_CHUNKS, nch - 1), 0)
    lo = lambda f: (lambda b: (f(b)[0], EV_LORA_BLK))
    main = lambda rows, f: pl.BlockSpec((rows, EV_MAIN), f)
    lora = lambda rows, f: pl.BlockSpec((rows, EV_LORA_W), lo(f))
    full = lambda a: pl.BlockSpec(a.shape, lambda b: (0,) * a.ndim)
    seq2 = pl.BlockSpec((2, PREP_ROWS, D_A), lambda b: (0, b, 0))
    seq1 = pl.BlockSpec((PREP_ROWS, D_A), lambda b: (b, 0))
    bf2 = jax.ShapeDtypeStruct((2, n, D_A), _bf16)
    return pl.pallas_call(
        functools.partial(_rwkv_prep_kernel, st=st),
        grid=(nch // PREP_CHUNKS,),
        in_specs=[main(PREP_ROWS, cur), main(CHUNK, prv), main(CHUNK, nxt),
                  lora(PREP_ROWS, cur), lora(CHUNK, prv), lora(CHUNK, nxt),
                  full(mu_main), full(mu_lora), full(ww), full(wa), full(wg), full(par), full(seg), full(seg_t)],
        out_specs=[seq2] * 6 + [seq1, pl.BlockSpec((2, PREP_CHUNKS, 1, D_A), lambda b: (0, b, 0, 0)), seq1, seq1],
        out_shape=[bf2] * 6 + [jax.ShapeDtypeStruct((n, D_A), _bf16),
                               jax.ShapeDtypeStruct((2, nch, 1, D_A), _f32),
                               jax.ShapeDtypeStruct((n, D_A), _f32),
                               jax.ShapeDtypeStruct((n, D_A), _f32)],
        compiler_params=_cparams(1), name="rwkv_prep",
    )(p, p, p, p, p, p, mu_main, mu_lora, ww, wa, wg, par, seg, seg_t)


N_PAIRS = H_A // 2
PAIR = 2 * CHUNK


def _stack_masked(x, keep_first):
    return jnp.concatenate([x * keep_first, x * (1 - keep_first)], axis=0)


def _rwkv_scan_kernel(*refs, st):
    (rt_f, kt_f, at_f, bt_f, kh_f, bh_f, v_f, pl_f,
     rt_b, kt_b, at_b, bt_b, kh_b, bh_b, v_b, pl_b, s0_ref, yf_ref, yb_ref, sout_ref, s_scr) = refs
    s = pl.program_id(0)
    is_lat, nc, c, _ = _chunk_pos(s, st)

    @pl.when(c == 0)
    def _():
        keep = jnp.where(is_lat, 1.0, 0.0)
        zero = jnp.zeros((HD_A, HD_A), _f32)
        for d in range(2):
            for p in range(N_PAIRS):
                top = jnp.concatenate([s0_ref[d, 2 * p], zero], axis=1)
                bot = jnp.concatenate([zero, s0_ref[d, 2 * p + 1]], axis=1)
                s_scr[d, p] = jnp.concatenate([top, bot], axis=0) * keep

    dirs = ((rt_f, kt_f, at_f, bt_f, kh_f, bh_f, v_f, pl_f, yf_ref),
            (rt_b, kt_b, at_b, bt_b, kh_b, bh_b, v_b, pl_b, yb_ref))
    chains = [(d, p) for d in range(2) for p in range(N_PAIRS)]
    masks = [_tri_masks(d, PAIR, CHUNK) for d in range(2)]
    eye = jnp.where(masks[0][0] & masks[1][0], 1.0, 0.0)

    lane = lax.broadcasted_iota(jnp.int32, (CHUNK, LANES), 1)
    keep_first = jnp.where(lane < HD_A, 1.0, 0.0).astype(_bf16)

    def load(ref, p):
        return _stack_masked(ref[:, p * LANES:(p + 1) * LANES], keep_first)

    ops = {}
    for d, p in chains:
        rt, kt, at, bt, kh, bh, v = (load(dirs[d][j], p) for j in range(7))
        incl, strict = masks[d]
        big = _dot_nt(jnp.concatenate([at, rt], axis=0), jnp.concatenate([kt, bt], axis=0))
        n = jnp.where(strict, big[:PAIR, PAIR:], 0.0)
        ops[d, p] = dict(rt=rt, at=at, bh=bh, kh=kh, v=v, t=eye - n, n=n.astype(_bf16),
                         mak=jnp.where(strict, big[:PAIR, :PAIR], 0.0).astype(_bf16),
                         mrk=jnp.where(incl, big[PAIR:, :PAIR], 0.0).astype(_bf16),
                         mrb=jnp.where(incl, big[PAIR:, PAIR:], 0.0).astype(_bf16))
    for _ in range(5):
        for ch in chains:
            ops[ch]["n"] = _dot(ops[ch]["n"], ops[ch]["n"]).astype(_bf16)
        for ch in chains:
            ops[ch]["t"] = ops[ch]["t"] + _dot(ops[ch]["t"], ops[ch]["n"])
    for ch in chains:
        o = ops[ch]
        o["mv"] = _dot(jnp.concatenate([o.pop("mak"), o.pop("mrk")], axis=0), o["v"])
    for ch in chains:
        o = ops[ch]
        o["xz"] = _dot(o.pop("t"), jnp.concatenate([o["at"], o["mv"][:PAIR].astype(_bf16)], axis=1))
    for ch in chains:
        o = ops[ch]
        o["qy"] = _dot(o.pop("mrb"), o["xz"])
    for ch in chains:
        o = ops[ch]
        o["upd"] = _dot_tn(o["xz"], o["bh"])
        o["vtk"] = _dot_tn(o["v"], o["kh"])
    for d, p in chains:
        o = ops[d, p]
        s0 = s_scr[d, p]
        q_eff = o["rt"].astype(_f32) - o["qy"][:, :LANES]
        y = _dot_nt(q_eff, s0) + o["mv"][PAIR:] - o["qy"][:, LANES:]
        dirs[d][8][:, p * LANES:(p + 1) * LANES] = y[:CHUNK] + y[CHUNK:]
        p_last = dirs[d][7][:, p * LANES:(p + 1) * LANES]
        s_scr[d, p] = s0 * p_last - _dot(s0, o["upd"][:LANES]) + o["vtk"] - o["upd"][LANES:]

    @pl.when((c == nc - 1) & jnp.logical_not(is_lat))
    def _():
        for d in range(2):
            for p in range(N_PAIRS):
                sp = s_scr[d, p]
                sout_ref[d, 2 * p] = sp[:HD_A, :HD_A]
                sout_ref[d, 2 * p + 1] = sp[HD_A:, HD_A:]


def _lat_state_spec(shape, st):
    nd = len(shape)
    return pl.BlockSpec((None,) + tuple(shape[1:]),
                        lambda s: (jnp.maximum(_seq_of(s, st) - st.n_ctx_seq, 0),) + (0,) * (nd - 1))


def _ctx_state_spec(shape, st):
    nd = len(shape)
    return pl.BlockSpec((None,) + tuple(shape[1:]),
                        lambda s: (jnp.minimum(_seq_of(s, st), st.n_ctx_seq - 1),) + (0,) * (nd - 1))


def _rwkv_scan(prep, s0, st):
    rt, kt, at, bt, kh, bh, v, p_last = prep
    n = v.shape[0]
    fwd = lambda s: s
    bwd = lambda s: _bwd_chunk(s, st)
    seq2 = lambda d, f: pl.BlockSpec((None, CHUNK, D_A), lambda s: (d, f(s), 0))
    seq1 = lambda f: pl.BlockSpec((CHUNK, D_A), lambda s: (f(s), 0))
    pls = lambda d, f: pl.BlockSpec((None, None, 1, D_A), lambda s: (d, f(s), 0, 0))
    side = lambda d, f: [seq2(d, f)] * 6 + [seq1(f), pls(d, f)]
    y_shape = jax.ShapeDtypeStruct((n, D_A), _f32)
    out_state = (st.n_ctx_seq,) + s0.shape[1:]
    args = (rt, kt, at, bt, kh, bh, v, p_last)
    return pl.pallas_call(
        functools.partial(_rwkv_scan_kernel, st=st),
        grid=(st.n_chunks,),
        in_specs=side(0, fwd) + side(1, bwd) + [_lat_state_spec(s0.shape, st)],
        out_specs=[seq1(fwd), seq1(bwd), _ctx_state_spec(out_state, st)],
        out_shape=[y_shape, y_shape, jax.ShapeDtypeStruct(out_state, _f32)],
        scratch_shapes=[pltpu.VMEM((2, N_PAIRS, LANES, LANES), _f32)],
        compiler_params=_cparams(1), name="rwkv_scan",
    )(*args, *args, s0)


def _gla_kernel(q_f, k_f, v_f, l_f, q_b, k_b, v_b, l_b, w_ref, bias_ref, s0_ref,
                of_ref, ob_ref, sout_ref, s_scr, *, st):
    s = pl.program_id(0)
    is_lat, nc, c, _ = _chunk_pos(s, st)

    @pl.when(c == 0)
    def _():
        keep = jnp.where(is_lat, 1.0, 0.0)
        for d in range(2):
            for h in range(H_B):
                s_scr[d, h] = s0_ref[d, h].T * keep

    dirs = ((q_f, k_f, v_f, l_f, of_ref), (q_b, k_b, v_b, l_b, ob_ref))
    work = []
    for d, (q_ref, k_ref, v_ref, l_ref, o_ref) in enumerate(dirs):
        incl, _ = _tri_masks(d)
        la = jax.nn.log_sigmoid(_dot3(l_ref[...], w_ref[d, 0], w_ref[d, 1]) + bias_ref[d:d + 1]) / GATE_NORM
        cum = _dot_exact_lhs(jnp.where(incl, 1.0, 0.0), la)
        tot = jnp.sum(la, axis=0, keepdims=True)
        k = k_ref[...]
        q_in = q_ref[...] * (DK_B ** -0.5) * jnp.exp(cum)
        k_in = k * jnp.exp(-cum)
        k_end = k * jnp.exp(tot - cum)
        e_tot = jnp.exp(tot)
        for h in range(H_B):
            ks = slice(h * DK_B, (h + 1) * DK_B)
            work.append((d, h, incl, q_in[:, ks], k_in[:, ks], k_end[:, ks], e_tot[:, ks]))
    att = [jnp.where(incl, _dot_nt(q_in, k_in), 0.0) for (_, _, incl, q_in, k_in, _, _) in work]
    for (d, h, _, q_in, _, k_end, e_tot), a in zip(work, att):
        v_ref, o_ref = dirs[d][2], dirs[d][4]
        vs = slice(h * DV_B, (h + 1) * DV_B)
        v = v_ref[:, vs]
        s_t = s_scr[d, h]
        o_ref[:, vs] = _dot(a, v) + _dot_nt(q_in, s_t)
        s_scr[d, h] = s_t * e_tot + _dot_tn(v, k_end)

    @pl.when((c == nc - 1) & jnp.logical_not(is_lat))
    def _():
        for d in range(2):
            for h in range(H_B):
                sout_ref[d, h] = s_scr[d, h].T


def _gla_scan(p, w, bias, s0, st):
    n = p.shape[0]
    out_state = (st.n_ctx_seq,) + s0.shape[1:]
    fwd = lambda s: s
    bwd = lambda s: _bwd_chunk(s, st)
    blk = lambda width, col, f: pl.BlockSpec((CHUNK, width), lambda s: (f(s), col))
    side = lambda f: [blk(QK_B, EV_Q_BLK, f), blk(QK_B, EV_K_BLK, f), blk(V_B, EV_V_BLK, f),
                      blk(EV_LORA_W, EV_LORA_BLK, f)]
    full = lambda a: pl.BlockSpec(a.shape, lambda s: (0,) * a.ndim)
    o_spec = lambda f: pl.BlockSpec((CHUNK, V_B), lambda s: (f(s), 0))
    o_shape = jax.ShapeDtypeStruct((n, V_B), _f32)
    return pl.pallas_call(
        functools.partial(_gla_kernel, st=st),
        grid=(st.n_chunks,),
        in_specs=side(fwd) + side(bwd) + [full(w), full(bias), _lat_state_spec(s0.shape, st)],
        out_specs=[o_spec(fwd), o_spec(bwd), _ctx_state_spec(out_state, st)],
        out_shape=[o_shape, o_shape, jax.ShapeDtypeStruct(out_state, _f32)],
        scratch_shapes=[pltpu.VMEM((2, H_B, DV_B, DK_B), _f32)],
        compiler_params=_cparams(1), name="gla_scan",
    )(p, p, p, p, p, p, p, p, w, bias, s0)


def _mlstm_kernel(q_f, k_f, v_f, g_f, q_b, k_b, v_b, g_b, bias_ref, c0_ref, n0_ref, m0_ref,
                  hf_ref, hb_ref, cout_ref, nout_ref, mout_ref, c_scr, n_scr, m_scr, *, st):
    s = pl.program_id(0)
    is_lat, nc, c, _ = _chunk_pos(s, st)

    @pl.when(c == 0)
    def _():
        keep = jnp.where(is_lat, 1.0, 0.0)
        for d in range(2):
            for h in range(H_C):
                c_scr[d, h] = c0_ref[d, h].T * keep
        n_scr[...] = n0_ref[...] * keep
        m_scr[...] = m0_ref[...] * keep

    L = CHUNK
    lane = lax.broadcasted_iota(jnp.int32, (L, LANES), 1)
    dirs = ((q_f, k_f, v_f, g_f, hf_ref), (q_b, k_b, v_b, g_b, hb_ref))
    work = []
    for d, (q_ref, k_ref, v_ref, g_ref, h_ref) in enumerate(dirs):
        gates = g_ref[...] + bias_ref[0:1]
        gates = jnp.where(lane < 2 * H_C, gates, jax.nn.log_sigmoid(gates))
        gates_t = gates.T
        for h in range(H_C):
            ji, jf = d * H_C + h, 2 * H_C + d * H_C + h
            ks = slice(h * DK_C, (h + 1) * DK_C)
            q = q_ref[:, ks] * (DK_C ** -0.5)
            k = k_ref[:, ks]
            work.append(dict(d=d, h=h, q=q, k=k, incl=_tri_masks(d)[0], incl_t=_tri_masks(1 - d)[0],
                             i_col=gates[:, ji:ji + 1], f_col=gates[:, jf:jf + 1],
                             i_row=gates_t[ji:ji + 1, :], f_row=gates_t[jf:jf + 1, :],
                             m_prev=m_scr[d, h], c_t=c_scr[d, h], n_row=n_scr[d, h]))
    for o in work:
        o["s"] = _dot_nt(o["q"], o["k"])
        o["qc"] = _dot_nt(o["q"], o["c_t"])
    for o in work:
        o["cum_col"] = jnp.sum(jnp.where(o["incl"], o["f_row"], 0.0), axis=1, keepdims=True)
        o["cum_row"] = jnp.sum(jnp.where(o["incl_t"], o["f_col"], 0.0), axis=0, keepdims=True)
        o["f_tot"] = jnp.sum(o["f_row"], axis=1, keepdims=True)
        o["qn"] = jnp.sum(o["q"] * o["n_row"], axis=1, keepdims=True)
    for o in work:
        o["log_d"] = jnp.where(o["incl"], o["cum_col"] - o["cum_row"] + o["i_row"], NEG_BIG)
        o["log_inter"] = o["cum_col"] + o["m_prev"]
        o["row_max"] = jnp.max(o["log_d"], axis=1, keepdims=True)
    for o in work:
        m_t = jnp.maximum(o["log_inter"], o["row_max"])
        o["m_t"] = m_t
        o["sm"] = o["s"] * jnp.exp(o.pop("log_d") - m_t)
        o["w_inter"] = jnp.exp(o["log_inter"] - m_t)
        o["m_new"] = m_t[L - 1:L, :] if o["d"] == 0 else m_t[0:1, :]
        o["kw"] = o["k"] * jnp.exp(o["f_tot"] - o["cum_col"] + o["i_col"] - o["m_new"])
        o["dec"] = jnp.exp(o["f_tot"] + o["m_prev"] - o["m_new"])
    for o in work:
        vs = slice(o["h"] * DV_C, (o["h"] + 1) * DV_C)
        o["v"] = dirs[o["d"]][2][:, vs]
        o["sv"] = _dot(o["sm"], o["v"])
        o["kv"] = _dot_tn(o["v"], o["kw"])
        o["s_sum"] = jnp.sum(o["sm"], axis=1, keepdims=True)
    for o in work:
        d, h = o["d"], o["h"]
        vs = slice(h * DV_C, (h + 1) * DV_C)
        num = o["w_inter"] * o["qc"] + o["sv"]
        den = o["w_inter"] * o["qn"] + o["s_sum"]
        dirs[d][4][:, vs] = num / jnp.maximum(jnp.abs(den), jnp.exp(-o["m_t"]))
        c_scr[d, h] = o["dec"] * o["c_t"] + o["kv"]
        n_scr[d, h] = o["dec"] * o["n_row"] + jnp.sum(o["kw"], axis=0, keepdims=True)
        m_scr[d, h] = o["m_new"]

    @pl.when((c == nc - 1) & jnp.logical_not(is_lat))
    def _():
        for d in range(2):
            for h in range(H_C):
                cout_ref[d, h] = c_scr[d, h].T
        nout_ref[...] = n_scr[...]
        mout_ref[...] = m_scr[...]


def _mlstm_scan(p, bias, c0, n0, m0, st):
    n = p.shape[0]
    fwd = lambda s: s
    bwd = lambda s: _bwd_chunk(s, st)
    blk = lambda width, col, f: pl.BlockSpec((CHUNK, width), lambda s: (f(s), col))
    side = lambda f: [blk(QK_C, 0, f), blk(QK_C, 1, f), blk(V_C, 1, f), blk(LANES, OD_GATE_BLK, f)]
    o_spec = lambda f: pl.BlockSpec((CHUNK, V_C), lambda s: (f(s), 0))
    o_shape = jax.ShapeDtypeStruct((n, V_C), _f32)
    states = (c0, n0, m0)
    out_states = [(st.n_ctx_seq,) + a.shape[1:] for a in states]
    return pl.pallas_call(
        functools.partial(_mlstm_kernel, st=st),
        grid=(st.n_chunks,),
        in_specs=side(fwd) + side(bwd) + [pl.BlockSpec(bias.shape, lambda s: (0, 0))]
        + [_lat_state_spec(a.shape, st) for a in states],
        out_specs=[o_spec(fwd), o_spec(bwd)] + [_ctx_state_spec(sh, st) for sh in out_states],
        out_shape=[o_shape, o_shape] + [jax.ShapeDtypeStruct(sh, _f32) for sh in out_states],
        scratch_shapes=[pltpu.VMEM((2, H_C, DV_C, DK_C), _f32), pltpu.VMEM((2, H_C, 1, DK_C), _f32),
                        pltpu.VMEM((2, H_C, 1, 1), _f32)],
        compiler_params=_cparams(1), name="mlstm_scan",
    )(p, p, p, p, p, p, p, p, bias, c0, n0, m0)


def _head_rms_gate(o, g_row, gate, n_heads, width):
    outs = []
    for h in range(n_heads):
        hs = slice(h * width, (h + 1) * width)
        oh = o[:, hs]
        outs.append(oh * lax.rsqrt(jnp.mean(oh * oh, axis=-1, keepdims=True) + EPS) * g_row[:, hs] * gate[:, hs])
    return jnp.concatenate(outs, axis=1)


def _even_post_kernel(yf_ref, yb_ref, bonus_ref, g_ref, of_ref, ob_ref, og_ref, par_ref, gn_ref, seg_ref, segt_ref,
                      o_ref):
    y = yf_ref[...] + yb_ref[...]
    seg, seg_t = seg_ref[...], segt_ref[...]
    mean = _head_sum(y, seg, seg_t) * (1.0 / HD_A)
    yc = y - mean
    var = _head_sum(yc * yc, seg, seg_t) * (1.0 / HD_A)
    par = par_ref[...]
    ya = (yc * lax.rsqrt(var + EPS) * par[0:1] + par[1:2] + bonus_ref[...]) * g_ref[...]
    og = og_ref[...]
    yb = _head_rms_gate(of_ref[...] + ob_ref[...], gn_ref[...], og * jax.nn.sigmoid(og), H_B, DV_B)
    o_ref[:, :D_A] = ya.astype(_bf16)
    o_ref[:, D_A:] = yb.astype(_bf16)


def _even_post(yf, yb, bonus, g, of, ob, p, ln_par, gla_g, seg, seg_t, tm=256):
    n = yf.shape[0]
    row = lambda w, col=0: pl.BlockSpec((tm, w), lambda i: (i, col))
    full = lambda a: pl.BlockSpec(a.shape, lambda i: (0,) * a.ndim)
    return pl.pallas_call(
        _even_post_kernel,
        grid=(n // tm,),
        in_specs=[row(D_A)] * 4 + [row(V_B), row(V_B), row(V_B, EV_OG_BLK), full(ln_par), full(gla_g),
                  full(seg), full(seg_t)],
        out_specs=row(D_A + V_B),
        out_shape=jax.ShapeDtypeStruct((n, D_A + V_B), _bf16),
        compiler_params=_cparams(1), name="even_post",
    )(yf, yb, bonus, g, of, ob, p, ln_par, gla_g, seg, seg_t)


def _odd_post_kernel(hf_ref, hb_ref, o_ref_in, gn_ref, out_ref):
    out_ref[...] = _head_rms_gate(hf_ref[...] + hb_ref[...], gn_ref[...], jax.nn.sigmoid(o_ref_in[...]),
                                  H_C, DV_C).astype(_bf16)


def _odd_post(hf, hb, p, gn, tm=256):
    n = hf.shape[0]
    row = lambda col=0: pl.BlockSpec((tm, V_C), lambda i: (i, col))
    return pl.pallas_call(
        _odd_post_kernel,
        grid=(n // tm,),
        in_specs=[row(), row(), row(2), pl.BlockSpec(gn.shape, lambda i: (0, 0))],
        out_specs=row(),
        out_shape=jax.ShapeDtypeStruct((n, V_C), _bf16),
        compiler_params=_cparams(1), name="odd_post",
    )(hf, hb, p, gn)


def _even_mixer(p, st, s_rwkv, s_gla, mu, w0, w2, a0, a2, g2, k_k, k_a, r_k, ln_g, ln_b, gla_w2, gla_b, gla_g):
    zpad = lambda x, rows_before, total: jnp.pad(x, ((rows_before, total - rows_before - x.shape[0]), (0, 0)))
    mu_main = mu[:EV_MAIN].reshape(1, EV_MAIN)
    mu_lora = jnp.pad(mu[EV_MAIN:], (0, EV_LORA_W - (A_IN - EV_MAIN))).reshape(1, EV_LORA_W)
    split = lambda w: jnp.stack(_split2(w), axis=0)
    ww = jnp.stack([split(zpad(w2[d], d * LORA_W, 2 * LORA_W)) for d in range(2)])
    wa = jnp.stack([split(zpad(a2[d], d * LORA_A, 2 * LORA_A)) for d in range(2)])
    wg = g2.astype(_bf16)
    par = jnp.stack([w0[0], w0[1], a0[0], a0[1], k_k, k_a, r_k, jnp.zeros_like(k_k)])
    head = np.arange(D_A) // HD_A
    seg_np = head[:, None] == np.arange(LANES)[None, :]
    seg, seg_t = jnp.asarray(seg_np, _bf16), jnp.asarray(seg_np.T, _bf16)
    *prep, bonus, g = _rwkv_prep(p, st, mu_main, mu_lora, ww, wa, wg, par, seg, seg_t)
    yf, yb, s_rw = _rwkv_scan(prep, s_rwkv, st)
    wl = jnp.stack([split(zpad(gla_w2[d], GLA_LORA_OFF + d * LORA_GK, EV_LORA_W)) for d in range(2)])
    bias = jnp.pad(gla_b, ((0, MOD_ROWS - 2), (0, 0)))
    of, ob, s_gl = _gla_scan(p, wl, bias, s_gla, st)
    ln_par = jnp.pad(jnp.stack([ln_g, ln_b]), ((0, MOD_ROWS - 2), (0, 0)))
    y_in = _even_post(yf, yb, bonus, g, of, ob, p, ln_par, gla_g.reshape(1, V_B), seg, seg_t)
    return y_in, s_rw, s_gl


def _odd_mixer(p, st, s_c, s_n, s_m, b_i, b_f, g):
    bias = jnp.concatenate([b_i.reshape(-1), b_f.reshape(-1), jnp.zeros((LANES - 4 * H_C,), _f32)])
    bias = jnp.pad(bias[None, :], ((0, MOD_ROWS - 1), (0, 0)))
    hf, hb, c_new, n_new, m_new = _mlstm_scan(p, bias, s_c, s_n[..., None, :], s_m[..., None, None], st)
    y_in = _odd_post(hf, hb, p, g.reshape(1, V_C))
    return y_in, c_new, n_new[..., 0, :], m_new[..., 0, 0]


def _moe(h, logits, w1, w3, w2, layer, tm=FFN_TM):
    n_tok = h.shape[0]
    n_pairs = n_tok * TOP_K
    n_tiles = n_pairs // tm + N_EXPERTS
    top_v, top_i = lax.top_k(logits[:, :N_EXPERTS], TOP_K)
    gates = jax.nn.softmax(top_v, axis=-1)
    pair_e = top_i.reshape(-1)
    order = jnp.argsort(pair_e, stable=True)
    sorted_e = pair_e[order]
    counts = jnp.sum(jax.nn.one_hot(pair_e, N_EXPERTS, dtype=jnp.int32), axis=0)
    tiles_per_e = (counts + tm - 1) // tm
    tile_start = jnp.cumsum(tiles_per_e) - tiles_per_e
    group_start = jnp.cumsum(counts) - counts
    rank = jnp.arange(n_pairs, dtype=jnp.int32) - group_start[sorted_e]
    dest = tile_start[sorted_e] * tm + rank
    slot_token = jnp.zeros((n_tiles * tm,), jnp.int32).at[dest].set((order // TOP_K).astype(jnp.int32))
    pair_slot = jnp.zeros((n_pairs,), jnp.int32).at[order].set(dest.astype(jnp.int32))
    n_valid = jnp.sum(tiles_per_e)
    tile_ids = jnp.arange(n_tiles, dtype=jnp.int32)
    tile_e = jnp.sum((tile_ids[:, None] >= tile_start[None, :]).astype(jnp.int32), axis=1) - 1
    last_e = jnp.max(jnp.where(counts > 0, jnp.arange(N_EXPERTS), 0))
    tile_valid = tile_ids < n_valid
    tile_e = jnp.where(tile_valid, tile_e, last_e).astype(jnp.int32)
    tile_rows = jnp.clip(counts[tile_e] - (tile_ids - tile_start[tile_e]) * tm, 0, tm)
    tile_rows = jnp.where(tile_valid, tile_rows, 0).astype(jnp.int32)
    yg = _gathered_ffn(h, slot_token, w1, w3, w2, layer, tile_e, tile_rows, tm=tm)
    return _moe_combine(yg, pair_slot, gates)


def kernel(x_prompt, x_sample, state_rwkv, state_gla, state_mlstm_c, state_mlstm_n, state_mlstm_m, c, c_ctx, mod_w, mod_b, norm_g, final_g, ev_w_in, ev_mu, rw_w0, rw_w2, rw_a0, rw_a2, rw_g2, rw_kk, rw_ka, rw_rk, rw_ln_g, rw_ln_b, gla_w2, gla_b, gla_norm_g, ev_w_out, ffn_w1, ffn_w3, ffn_w2, od_w_in, ml_b_i, ml_b_f, ml_norm_g, od_w_out, moe_router, moe_w1, moe_w3, moe_w2):
    D = D_MODEL
    Bp, Tp, _ = x_prompt.shape
    Bs, Ts, _ = x_sample.shape
    assert Ts // GRID_W * GRID_W == Ts and GRID_W == CHUNK and Tp % CHUNK == 0
    n_ctx = Bp * Tp
    st = _Streams(n_ctx_chunks=n_ctx // CHUNK, ctx_cps=Tp // CHUNK, lat_cps=Ts // CHUNK,
                  n_chunks=(n_ctx + Bs * Ts) // CHUNK, n_ctx_seq=Bp, n_seq=Bp + Bs)
    x = jnp.concatenate([x_prompt.reshape(n_ctx, D), x_sample.reshape(Bs * Ts, D)], axis=0)
    n_tok = x.shape[0]

    cond = jnp.concatenate([c_ctx[None, :], c, jnp.zeros((MOD_ROWS - 1 - Bs, D), _f32)], axis=0)
    mod = _mod_table(jax.nn.silu(cond).astype(_bf16), mod_w, mod_b)

    dense_te = jnp.zeros((n_tok // FFN_TM,), jnp.int32)
    dense_rows = jnp.full((n_tok // FFN_TM,), FFN_TM, jnp.int32)
    new_rw, new_gla, new_c, new_n, new_m = [], [], [], [], []
    pending = None

    for l in range(DEPTH):
        i = l // 2
        sh1, sc1, gt1, sh2, sc2, gt2 = [mod[l, :, j * D:(j + 1) * D] for j in range(6)]
        if pending is None:
            h = _norm_mod(x, norm_g[l, 0], sc1, sh1, st)[0]
        else:
            x, h = _norm_mod(x, norm_g[l, 0], sc1, sh1, st, res=pending)
        if l % 2 == 0:
            w = ev_w_in[i]
            b0 = A_IN
            w_in = jnp.concatenate([
                w[:, :EV_MAIN], w[:, b0 + 2 * QK_B:b0 + 2 * QK_B + V_B], w[:, b0 + 2 * QK_B + V_B + 2 * LORA_GK:],
                w[:, b0:b0 + 2 * QK_B], w[:, EV_MAIN:A_IN],
                w[:, b0 + 2 * QK_B + V_B:b0 + 2 * QK_B + V_B + 2 * LORA_GK],
                jnp.zeros((D, EV_LORA_W - (A_IN - EV_MAIN) - 2 * LORA_GK), _f32)], axis=1).astype(_bf16)
            p = _matmul(h, w_in, tn=EV_N // 4)
            y_in, s_rw, s_gl = _even_mixer(
                p, st, state_rwkv[:, i], state_gla[:, i], ev_mu[i], rw_w0[i], rw_w2[i],
                rw_a0[i], rw_a2[i], rw_g2[i], rw_kk[i], rw_ka[i], rw_rk[i], rw_ln_g[i], rw_ln_b[i],
                gla_w2[i], gla_b[i], gla_norm_g[i])
            new_rw.append(s_rw)
            new_gla.append(s_gl)
            x = _matmul_residual(y_in, ev_w_out[i].astype(_bf16), x, gt1, st)
        else:
            w_in = jnp.pad(od_w_in[i], ((0, 0), (0, OD_N - od_w_in.shape[2]))).astype(_bf16)
            p = _matmul(h, w_in, tn=OD_N // 7)
            y_in, c_new, n_new, m_new = _odd_mixer(
                p, st, state_mlstm_c[:, i], state_mlstm_n[:, i], state_mlstm_m[:, i],
                ml_b_i[i], ml_b_f[i], ml_norm_g[i])
            new_c.append(c_new)
            new_n.append(n_new)
            new_m.append(m_new)
            x = _matmul_residual(y_in, od_w_out[i].astype(_bf16), x, gt1, st)

        if l % 2 == 0:
            h = _norm_mod(x, norm_g[l, 1], sc2, sh2, st)[0]
            f = _grouped_ffn(h, ffn_w1[:, None], ffn_w3[:, None], ffn_w2[:, None], i, dense_te, dense_rows)
        else:
            router = _split2(jnp.pad(moe_router[i], ((0, 0), (0, LANES - N_EXPERTS))))
            h, logits = _norm_mod(x, norm_g[l, 1], sc2, sh2, st, router=router, h_dtype=_f32)
            f = _moe(h, logits, moe_w1, moe_w3, moe_w2, i)
        pending = (f, gt2)

    y = _final_norm(x, pending[0], pending[1], final_g, st)
    return (y[:n_ctx].reshape(Bp, Tp, D), y[n_ctx:].reshape(Bs, Ts, D),
            jnp.stack(new_rw, axis=1), jnp.stack(new_gla, axis=1),
            jnp.stack(new_c, axis=1), jnp.stack(new_n, axis=1), jnp.stack(new_m, axis=1))
```

```python
import functools
from typing import NamedTuple

import jax
import jax.numpy as jnp
import numpy as np
from jax import lax
from jax.experimental import pallas as pl
from jax.experimental.pallas import tpu as pltpu

D_MODEL = 2048
DEPTH = 4
GRID_W = 64
EPS = 1e-6
CHUNK = 64
H_A, HD_A = 16, 64
D_A = H_A * HD_A
LORA_W, LORA_A, LORA_G = 64, 64, 128
A_IN = 3 * D_A + 2 * LORA_W + 2 * LORA_A + LORA_G
H_B, DK_B, DV_B = 4, 128, 256
QK_B, V_B = H_B * DK_B, H_B * DV_B
LORA_GK = 16
GATE_NORM = 16.0
H_C, DK_C, DV_C = 8, 128, 256
QK_C, V_C = H_C * DK_C, H_C * DV_C
D_FF = 5632
N_EXPERTS = 8
TOP_K = 2

LANES = 128
MOD_ROWS = 8
VMEM_LIMIT = 48 * 1024 * 1024
FFN_VMEM_LIMIT = 56 * 1024 * 1024
NEG_BIG = -1e30

EV_MAIN = 3 * D_A
EV_V_BLK = EV_MAIN // V_B
EV_OG_BLK = EV_V_BLK + 1
EV_Q_BLK = (EV_MAIN + 2 * V_B) // QK_B
EV_K_BLK = EV_Q_BLK + 1
EV_LORA_W = 512
EV_LORA_BLK = (EV_MAIN + 2 * V_B + 2 * QK_B) // EV_LORA_W
EV_N = EV_MAIN + 2 * V_B + 2 * QK_B + EV_LORA_W
GLA_LORA_OFF = 2 * LORA_W + 2 * LORA_A + LORA_G
OD_GATE_BLK = (2 * QK_C + 2 * V_C) // LANES
OD_N = 2 * QK_C + 2 * V_C + LANES

_f32 = jnp.float32
_bf16 = jnp.bfloat16


class _Streams(NamedTuple):
    n_ctx_chunks: int
    ctx_cps: int
    lat_cps: int
    n_chunks: int
    n_ctx_seq: int
    n_seq: int


def _chunk_pos(s, st):
    is_lat = s >= st.n_ctx_chunks
    sl = s - st.n_ctx_chunks
    nc = jnp.where(is_lat, st.lat_cps, st.ctx_cps)
    c = jnp.where(is_lat, sl % st.lat_cps, s % st.ctx_cps)
    seq = jnp.where(is_lat, st.n_ctx_seq + sl // st.lat_cps, s // st.ctx_cps)
    return is_lat, nc, c, seq


def _bwd_chunk(s, st):
    _, nc, c, _ = _chunk_pos(s, st)
    return s - c + (nc - 1 - c)


def _seq_of(s, st):
    return _chunk_pos(s, st)[3]


def _mod_row_of_chunk(s, st):
    is_lat, _, _, seq = _chunk_pos(s, st)
    return jnp.where(is_lat, 1 + seq - st.n_ctx_seq, 0)


def _cparams(n_axes):
    return pltpu.CompilerParams(dimension_semantics=("arbitrary",) * n_axes,
                                vmem_limit_bytes=VMEM_LIMIT)


def _dot(a, b):
    return jnp.dot(a.astype(_bf16), b.astype(_bf16), preferred_element_type=_f32)


def _dot_nt(a, b):
    return lax.dot_general(a.astype(_bf16), b.astype(_bf16), (((1,), (1,)), ((), ())),
                           preferred_element_type=_f32)


def _dot_tn(a, b):
    return lax.dot_general(a.astype(_bf16), b.astype(_bf16), (((0,), (0,)), ((), ())),
                           preferred_element_type=_f32)


def _split2(x):
    hi = x.astype(_bf16)
    return hi, (x - hi.astype(_f32)).astype(_bf16)


def _dot_exact_lhs(tri, x):
    hi, lo = _split2(x)
    t = tri.astype(_bf16)
    return jnp.dot(t, hi, preferred_element_type=_f32) + jnp.dot(t, lo, preferred_element_type=_f32)


def _dot_exact_rhs(x, ones):
    hi, lo = _split2(x)
    return jnp.dot(hi, ones, preferred_element_type=_f32) + jnp.dot(lo, ones, preferred_element_type=_f32)


def _head_sum(x, seg, seg_t):
    return _dot_exact_rhs(_dot_exact_rhs(x, seg), seg_t)


def _dot3(a, b_hi, b_lo):
    a_hi, a_lo = _split2(a)
    d = lambda u, v: jnp.dot(u, v, preferred_element_type=_f32)
    return d(a_hi, b_hi) + d(a_lo, b_hi) + d(a_hi, b_lo)


def _tri_masks(direction, n=CHUNK, period=CHUNK):
    t = lax.broadcasted_iota(jnp.int32, (n, n), 0)
    i = lax.broadcasted_iota(jnp.int32, (n, n), 1)
    same = (t // period) == (i // period) if n != period else True
    if direction == 0:
        return (i <= t) & same, (i < t) & same
    return (i >= t) & same, (i > t) & same


def _mod_kernel(a_ref, w_ref, b_ref, o_ref):
    o_ref[...] = _dot(a_ref[...], w_ref[...]) + b_ref[...]


def _mod_table(silu_rows, mod_w, mod_b, tn=1024):
    n_l, k, n = mod_w.shape
    return pl.pallas_call(
        _mod_kernel,
        grid=(n_l, n // tn),
        in_specs=[pl.BlockSpec((MOD_ROWS, k), lambda l, j: (0, 0)),
                  pl.BlockSpec((None, k, tn), lambda l, j: (l, 0, j)),
                  pl.BlockSpec((None, 1, tn), lambda l, j: (l, 0, j))],
        out_specs=pl.BlockSpec((None, MOD_ROWS, tn), lambda l, j: (l, 0, j)),
        out_shape=jax.ShapeDtypeStruct((n_l, MOD_ROWS, n), _f32),
        compiler_params=_cparams(2), name="mod_table",
    )(silu_rows, mod_w, mod_b.reshape(n_l, 1, n))


def _norm_mod_kernel(*refs, tm, st, has_res, has_router):
    refs = list(refs)
    x_ref = refs.pop(0)
    f_ref, gate_ref = (refs.pop(0), refs.pop(0)) if has_res else (None, None)
    g_ref, sc_ref, sh_ref = refs.pop(0), refs.pop(0), refs.pop(0)
    r_hi, r_lo = (refs.pop(0), refs.pop(0)) if has_router else (None, None)
    xo_ref = refs.pop(0) if has_res else None
    h_ref = refs.pop(0)
    row = _mod_row_of_chunk(pl.program_id(0) * (tm // CHUNK), st)
    x = x_ref[...]
    if has_res:
        x = x + gate_ref[pl.ds(row, 1), :] * f_ref[...]
        xo_ref[...] = x
    inv = lax.rsqrt(jnp.mean(x * x, axis=-1, keepdims=True) + EPS)
    h = x * inv * g_ref[...] * (1.0 + sc_ref[pl.ds(row, 1), :]) + sh_ref[pl.ds(row, 1), :]
    h_ref[...] = h.astype(h_ref.dtype)
    if has_router:
        refs.pop(0)[...] = _dot3(h, r_hi[...], r_lo[...])


def _norm_mod(x, g, sc, sh, st, res=None, router=None, h_dtype=_bf16, tm=256):
    m, d = x.shape
    row = pl.BlockSpec((tm, d), lambda i: (i, 0))
    tab = pl.BlockSpec((MOD_ROWS, d), lambda i: (0, 0))
    args, specs = [x], [row]
    outs, out_specs = [], []
    if res is not None:
        args += list(res)
        specs += [row, tab]
        outs.append(jax.ShapeDtypeStruct((m, d), _f32))
        out_specs.append(row)
    args += [g.reshape(1, d), sc, sh]
    specs += [pl.BlockSpec((1, d), lambda i: (0, 0)), tab, tab]
    outs.append(jax.ShapeDtypeStruct((m, d), h_dtype))
    out_specs.append(row)
    if router is not None:
        args += list(router)
        specs += [pl.BlockSpec((d, LANES), lambda i: (0, 0))] * 2
        outs.append(jax.ShapeDtypeStruct((m, LANES), _f32))
        out_specs.append(pl.BlockSpec((tm, LANES), lambda i: (i, 0)))
    return pl.pallas_call(
        functools.partial(_norm_mod_kernel, tm=tm, st=st, has_res=res is not None, has_router=router is not None),
        grid=(m // tm,), in_specs=specs, out_specs=out_specs, out_shape=outs,
        compiler_params=_cparams(1), name="norm_mod",
    )(*args)


def _final_norm_kernel(x_ref, f_ref, gate_ref, g_ref, o_ref, *, tm, st):
    row = _mod_row_of_chunk(pl.program_id(0) * (tm // CHUNK), st)
    x = x_ref[...] + gate_ref[pl.ds(row, 1), :] * f_ref[...]
    o_ref[...] = x * lax.rsqrt(jnp.mean(x * x, axis=-1, keepdims=True) + EPS) * g_ref[...]


def _final_norm(x, f, gate, g, st, tm=256):
    m, d = x.shape
    row = pl.BlockSpec((tm, d), lambda i: (i, 0))
    return pl.pallas_call(
        functools.partial(_final_norm_kernel, tm=tm, st=st),
        grid=(m // tm,),
        in_specs=[row, row, pl.BlockSpec((MOD_ROWS, d), lambda i: (0, 0)), pl.BlockSpec((1, d), lambda i: (0, 0))],
        out_specs=row,
        out_shape=jax.ShapeDtypeStruct((m, d), _f32),
        compiler_params=_cparams(1), name="final_norm",
    )(x, f, gate, g.reshape(1, d))


def _mm_kernel(a_ref, w_ref, o_ref):
    o_ref[...] = _dot(a_ref[...], w_ref[...])


def _mm_res_kernel(a_ref, w_ref, res_ref, gate_ref, o_ref, *, tm, st):
    row = _mod_row_of_chunk(pl.program_id(1) * (tm // CHUNK), st)
    gate = gate_ref[pl.ds(row, 1), :]
    o_ref[...] = res_ref[...] + gate * _dot(a_ref[...], w_ref[...])


def _matmul(a, w, tm=512, tn=512):
    m, k = a.shape
    n = w.shape[1]
    tm = min(tm, m)
    return pl.pallas_call(
        _mm_kernel,
        grid=(n // tn, m // tm),
        in_specs=[pl.BlockSpec((tm, k), lambda j, i: (i, 0)),
                  pl.BlockSpec((k, tn), lambda j, i: (0, j))],
        out_specs=pl.BlockSpec((tm, tn), lambda j, i: (i, j)),
        out_shape=jax.ShapeDtypeStruct((m, n), _f32),
        compiler_params=_cparams(2), name="matmul",
    )(a, w)


def _matmul_residual(a, w, res, gate, st, tm=512, tn=1024):
    m, k = a.shape
    n = w.shape[1]
    tm = min(tm, m)
    return pl.pallas_call(
        functools.partial(_mm_res_kernel, tm=tm, st=st),
        grid=(n // tn, m // tm),
        in_specs=[pl.BlockSpec((tm, k), lambda j, i: (i, 0)),
                  pl.BlockSpec((k, tn), lambda j, i: (0, j)),
                  pl.BlockSpec((tm, tn), lambda j, i: (i, j)),
                  pl.BlockSpec((MOD_ROWS, tn), lambda j, i: (0, j))],
        out_specs=pl.BlockSpec((tm, tn), lambda j, i: (i, j)),
        out_shape=jax.ShapeDtypeStruct((m, n), _f32),
        compiler_params=_cparams(2), name="matmul_residual",
    )(a, w, res, gate)


FFN_TM = 1024
FFN_TF = 256
FFN_ROW_STEP = 256


def _swiglu_accumulate(x_ref, w1_ref, w3_ref, w2_ref, o_ref, n_rows):
    x = x_ref[0:n_rows, :]
    a = _dot(x, w1_ref[...])
    b = _dot(x, w3_ref[...])
    o_ref[0:n_rows, :] += _dot(a * jax.nn.sigmoid(a) * b, w2_ref[...])


def _ffn_kernel(x_ref, w1_ref, w3_ref, w2_ref, o_ref, *, tm):
    @pl.when(pl.program_id(1) == 0)
    def _():
        o_ref[...] = jnp.zeros_like(o_ref)

    _swiglu_accumulate(x_ref, w1_ref, w3_ref, w2_ref, o_ref, tm)


def _row_copy(src_hbm, src_row, dst, dst_row, sem):
    return pltpu.make_async_copy(src_hbm.at[pl.ds(src_row, 1), :], dst.at[pl.ds(dst_row, 1), :], sem)


def _ffn_gather_kernel(te_ref, tr_ref, tb_ref, tok_ref, h_hbm, w1_ref, w3_ref, w2_ref, o_ref, xf_scr, xb_scr, sem,
                       *, tm, n_tiles):
    i, j = pl.program_id(0), pl.program_id(1)
    rows = tr_ref[i]

    def start_gather(tile):
        def body(r, carry):
            _row_copy(h_hbm, tok_ref[tb_ref[tile] + r], xf_scr, r, sem).start()
            return carry
        lax.fori_loop(0, tr_ref[tile], body, 0)

    def wait_gather(tile):
        def body(r, carry):
            _row_copy(h_hbm, 0, xf_scr, r, sem).wait()
            return carry
        lax.fori_loop(0, tr_ref[tile], body, 0)

    @pl.when(j == 0)
    def _():
        @pl.when(i == 0)
        def _():
            xf_scr[...] = jnp.zeros_like(xf_scr)
            start_gather(0)

        wait_gather(i)
        xb_scr[...] = xf_scr[...].astype(_bf16)
        o_ref[...] = jnp.zeros_like(o_ref)

        @pl.when(i + 1 < n_tiles)
        def _():
            start_gather(i + 1)

    for n_rows in range(FFN_ROW_STEP, tm + 1, FFN_ROW_STEP):
        @pl.when((rows > n_rows - FFN_ROW_STEP) & (rows <= n_rows))
        def _():
            _swiglu_accumulate(xb_scr, w1_ref, w3_ref, w2_ref, o_ref, n_rows)


def _gathered_ffn(h, pair_token, w1, w3, w2, layer, tile_expert, tile_rows, tile_base, tm=FFN_TM, tf=FFN_TF):
    d = h.shape[1]
    f = w1.shape[3]
    nj = f // tf
    n_tiles = tile_expert.shape[0]

    def jeff(i, j, tr):
        return jnp.where(tr[i] > 0, j, nj - 1)

    return pl.pallas_call(
        functools.partial(_ffn_gather_kernel, tm=tm, n_tiles=n_tiles),
        grid_spec=pltpu.PrefetchScalarGridSpec(
            num_scalar_prefetch=4,
            grid=(n_tiles, nj),
            in_specs=[pl.BlockSpec(memory_space=pl.ANY),
                      pl.BlockSpec((None, None, d, tf), lambda i, j, te, tr, *_: (layer, te[i], 0, jeff(i, j, tr))),
                      pl.BlockSpec((None, None, d, tf), lambda i, j, te, tr, *_: (layer, te[i], 0, jeff(i, j, tr))),
                      pl.BlockSpec((None, None, tf, d), lambda i, j, te, tr, *_: (layer, te[i], jeff(i, j, tr), 0))],
            out_specs=pl.BlockSpec((tm, d), lambda i, j, *_: (i, 0)),
            scratch_shapes=[pltpu.VMEM((tm, d), _f32), pltpu.VMEM((tm, d), _bf16), pltpu.SemaphoreType.DMA]),
        out_shape=jax.ShapeDtypeStruct((n_tiles * tm, d), _f32),
        compiler_params=pltpu.CompilerParams(dimension_semantics=("arbitrary", "arbitrary"),
                                             vmem_limit_bytes=FFN_VMEM_LIMIT),
        name="gathered_ffn",
    )(tile_expert, tile_rows, tile_base, pair_token, h, w1, w3, w2)


COMBINE_TM = 256


def _combine_kernel(slot_ref, yg_hbm, gates_ref, o_ref, buf, sem, *, tm):
    base = pl.program_id(0) * tm * TOP_K

    def start(t, carry):
        for k in range(TOP_K):
            _row_copy(yg_hbm, slot_ref[base + t * TOP_K + k], buf.at[k], t, sem).start()
        return carry

    def wait(t, carry):
        for k in range(TOP_K):
            _row_copy(yg_hbm, 0, buf.at[k], t, sem).wait()
        return carry

    lax.fori_loop(0, tm, start, 0)
    lax.fori_loop(0, tm, wait, 0)
    g = gates_ref[...]
    o_ref[...] = g[:, 0:1] * buf[0] + g[:, 1:2] * buf[1]


def _moe_combine(yg, pair_slot, gates, tm=COMBINE_TM):
    n_tok = gates.shape[0]
    d = yg.shape[1]
    return pl.pallas_call(
        functools.partial(_combine_kernel, tm=tm),
        grid_spec=pltpu.PrefetchScalarGridSpec(
            num_scalar_prefetch=1,
            grid=(n_tok // tm,),
            in_specs=[pl.BlockSpec(memory_space=pl.ANY),
                      pl.BlockSpec((tm, TOP_K), lambda b, sl: (b, 0))],
            out_specs=pl.BlockSpec((tm, d), lambda b, sl: (b, 0)),
            scratch_shapes=[pltpu.VMEM((TOP_K, tm, d), _f32), pltpu.SemaphoreType.DMA]),
        out_shape=jax.ShapeDtypeStruct((n_tok, d), _f32),
        compiler_params=_cparams(1), name="moe_combine",
    )(pair_slot, yg, gates)


def _dense_ffn(x, w1, w3, w2, layer, tm=FFN_TM, tf=FFN_TF):
    m, d = x.shape
    f = w1.shape[2]
    return pl.pallas_call(
        functools.partial(_ffn_kernel, tm=tm),
        grid=(m // tm, f // tf),
        in_specs=[pl.BlockSpec((tm, d), lambda i, j: (i, 0)),
                  pl.BlockSpec((None, d, tf), lambda i, j: (layer, 0, j)),
                  pl.BlockSpec((None, d, tf), lambda i, j: (layer, 0, j)),
                  pl.BlockSpec((None, tf, d), lambda i, j: (layer, j, 0))],
        out_specs=pl.BlockSpec((tm, d), lambda i, j: (i, 0)),
        out_shape=jax.ShapeDtypeStruct((m, d), _f32),
        compiler_params=pltpu.CompilerParams(dimension_semantics=("arbitrary", "arbitrary"),
                                             vmem_limit_bytes=FFN_VMEM_LIMIT),
        name="dense_ffn",
    )(x, w1, w3, w2)


PREP_CHUNKS = 4
PREP_ROWS = PREP_CHUNKS * CHUNK


def _token_shift_block(x, prev, nxt, lane0, is_lat, first, last):
    rows, n = x.shape
    row = lax.broadcasted_iota(jnp.int32, x.shape, 0)
    ch = lax.broadcasted_iota(jnp.int32, x.shape, 1) + lane0
    down = pltpu.roll(x, 1, 0)
    up = pltpu.roll(x, rows - 1, 0)

    def context():
        zero_row = jnp.zeros((1, n), _f32)
        row_prev = jnp.where(first, zero_row, prev[CHUNK - 1:CHUNK, :])
        row_next = jnp.where(last, zero_row, nxt[0:1, :])
        return jnp.where(ch < A_IN // 2,
                         jnp.where(row == 0, row_prev, down),
                         jnp.where(row == rows - 1, row_next, up))

    def latent():
        q = A_IN // 4
        col = row % GRID_W
        above = jnp.concatenate([jnp.where(first, 0.0, prev), x[:rows - CHUNK]], axis=0)
        below = jnp.concatenate([x[CHUNK:], jnp.where(last, 0.0, nxt)], axis=0)
        return jnp.where(ch < q, jnp.where(col == 0, 0.0, down),
                         jnp.where(ch < 2 * q, jnp.where(col == GRID_W - 1, 0.0, up),
                                   jnp.where(ch < 3 * q, above, below)))

    return lax.cond(is_lat, latent, context)


def _rwkv_prep_kernel(xm_ref, pm_ref, nm_ref, xl_ref, pl_ref, nl_ref, mum_ref, mul_ref,
                      ww_ref, wa_ref, wg_ref, par_ref, seg_ref, segt_ref,
                      rt_ref, kt_ref, at_ref, bt_ref, kh_ref, bh_ref, v_ref, plast_ref, bonus_ref, g_ref,
                      *, st):
    s = pl.program_id(0) * PREP_CHUNKS
    is_lat, nc, c, _ = _chunk_pos(s, st)
    first, last = c == 0, c + PREP_CHUNKS == nc
    xm, xl = xm_ref[...], xl_ref[...]
    sm = _token_shift_block(xm, pm_ref[...], nm_ref[...], 0, is_lat, first, last)
    sl = _token_shift_block(xl, pl_ref[...], nl_ref[...], EV_MAIN, is_lat, first, last)
    pm = xm + mum_ref[...] * (sm - xm)
    lora = xl + mul_ref[...] * (sl - xl)
    r, k, v = pm[:, :D_A], pm[:, D_A:2 * D_A], pm[:, 2 * D_A:]
    par = par_ref[...]
    k_k, k_a, r_k = par[4:5], par[5:6], par[6:7]
    seg, seg_t = seg_ref[...], segt_ref[...]
    kk = k * k_k
    kk = kk / jnp.maximum(jnp.sqrt(_head_sum(kk * kk, seg, seg_t)), 1e-12)
    g_ref[...] = _dot(jax.nn.sigmoid(lora[:, 2 * LORA_W + 2 * LORA_A:GLA_LORA_OFF]), wg_ref[...])
    v_ref[...] = v.astype(_bf16)
    tanh_w = jnp.tanh(lora[:, :2 * LORA_W])
    lora_a = lora[:, 2 * LORA_W:2 * LORA_W + 2 * LORA_A]
    kdir_sum = jnp.zeros_like(k)
    t = lax.broadcasted_iota(jnp.int32, (PREP_ROWS, PREP_ROWS), 0)
    i = lax.broadcasted_iota(jnp.int32, (PREP_ROWS, PREP_ROWS), 1)
    same_chunk = jnp.where((t // CHUNK) == (i // CHUNK), 1.0, 0.0)
    for d in range(2):
        w = -jax.nn.softplus(-(par[d:d + 1] + _dot3(tanh_w, ww_ref[d, 0], ww_ref[d, 1]))) - 0.5
        ld = -jnp.exp(w)
        a = jax.nn.sigmoid(par[2 + d:3 + d] + _dot3(lora_a, wa_ref[d, 0], wa_ref[d, 1]))
        kdir = k * (1.0 + (a - 1.0) * k_a)
        kdir_sum = kdir_sum + kdir
        incl, _ = _tri_masks(d, PREP_ROWS, CHUNK)
        cum = _dot_exact_lhs(jnp.where(incl, 1.0, 0.0), ld)
        tot = _dot_exact_lhs(same_chunk, ld)
        e_pos, e_neg, e_end = jnp.exp(cum), jnp.exp(-cum), jnp.exp(tot - cum)
        kka = kk * a
        rt_ref[d] = (r * e_pos).astype(_bf16)
        kt_ref[d] = (kdir * e_neg).astype(_bf16)
        at_ref[d] = (kk * jnp.exp(cum - ld)).astype(_bf16)
        bt_ref[d] = (kka * e_neg).astype(_bf16)
        kh_ref[d] = (kdir * e_end).astype(_bf16)
        bh_ref[d] = (kka * e_end).astype(_bf16)
        e_tot = jnp.exp(tot)
        for j in range(PREP_CHUNKS):
            plast_ref[d, j] = e_tot[j * CHUNK:j * CHUNK + 1, :]
    bonus_ref[...] = _head_sum(r * kdir_sum * r_k, seg, seg_t) * v


def _rwkv_prep(p, st, mu_main, mu_lora, ww, wa, wg, par, seg, seg_t):
    n = p.shape[0]
    nch = st.n_chunks
    assert st.ctx_cps % PREP_CHUNKS == 0 and st.lat_cps % PREP_CHUNKS == 0
    cur = lambda b: (b, 0)
    prv = lambda b: (jnp.maximum(b * PREP_CHUNKS - 1, 0), 0)
    nxt = lambda b: (jnp.minimum((b + 1) * PREP_CHUNKS, nch - 1), 0)
    lo = lambda f: (lambda b: (f(b)[0], EV_LORA_BLK))
    main = lambda rows, f: pl.BlockSpec((rows, EV_MAIN), f)
    lora = lambda rows, f: pl.BlockSpec((rows, EV_LORA_W), lo(f))
    full = lambda a: pl.BlockSpec(a.shape, lambda b: (0,) * a.ndim)
    seq2 = pl.BlockSpec((2, PREP_ROWS, D_A), lambda b: (0, b, 0))
    seq1 = pl.BlockSpec((PREP_ROWS, D_A), lambda b: (b, 0))
    bf2 = jax.ShapeDtypeStruct((2, n, D_A), _bf16)
    return pl.pallas_call(
        functools.partial(_rwkv_prep_kernel, st=st),
        grid=(nch // PREP_CHUNKS,),
        in_specs=[main(PREP_ROWS, cur), main(CHUNK, prv), main(CHUNK, nxt),
                  lora(PREP_ROWS, cur), lora(CHUNK, prv), lora(CHUNK, nxt),
                  full(mu_main), full(mu_lora), full(ww), full(wa), full(wg), full(par), full(seg), full(seg_t)],
        out_specs=[seq2] * 6 + [seq1, pl.BlockSpec((2, PREP_CHUNKS, 1, D_A), lambda b: (0, b, 0, 0)), seq1, seq1],
        out_shape=[bf2] * 6 + [jax.ShapeDtypeStruct((n, D_A), _bf16),
                               jax.ShapeDtypeStruct((2, nch, 1, D_A), _f32),
                               jax.ShapeDtypeStruct((n, D_A), _f32),
                               jax.ShapeDtypeStruct((n, D_A), _f32)],
        compiler_params=_cparams(1), name="rwkv_prep",
    )(p, p, p, p, p, p, mu_main, mu_lora, ww, wa, wg, par, seg, seg_t)


N_PAIRS = H_A // 2
PAIR = 2 * CHUNK


def _stack_masked(x, keep_first):
    return jnp.concatenate([x * keep_first, x * (1 - keep_first)], axis=0)


def _rwkv_scan_kernel(*refs, st):
    (rt_f, kt_f, at_f, bt_f, kh_f, bh_f, v_f, pl_f,
     rt_b, kt_b, at_b, bt_b, kh_b, bh_b, v_b, pl_b, s0_ref, _acc, yf_ref, yb_ref, sout_ref, s_scr) = refs
    s = pl.program_id(0)
    is_lat, nc, c, _ = _chunk_pos(s, st)

    @pl.when(c == 0)
    def _():
        keep = jnp.where(is_lat, 1.0, 0.0)
        zero = jnp.zeros((HD_A, HD_A), _f32)
        for d in range(2):
            for p in range(N_PAIRS):
                top = jnp.concatenate([s0_ref[d, 2 * p], zero], axis=1)
                bot = jnp.concatenate([zero, s0_ref[d, 2 * p + 1]], axis=1)
                s_scr[d, p] = jnp.concatenate([top, bot], axis=0) * keep

    dirs = ((rt_f, kt_f, at_f, bt_f, kh_f, bh_f, v_f, pl_f, yf_ref),
            (rt_b, kt_b, at_b, bt_b, kh_b, bh_b, v_b, pl_b, yb_ref))
    chains = [(d, p) for d in range(2) for p in range(N_PAIRS)]
    masks = [_tri_masks(d, PAIR, CHUNK) for d in range(2)]
    eye = jnp.where(masks[0][0] & masks[1][0], 1.0, 0.0)

    lane = lax.broadcasted_iota(jnp.int32, (CHUNK, LANES), 1)
    keep_first = jnp.where(lane < HD_A, 1.0, 0.0).astype(_bf16)

    def load(ref, p):
        return _stack_masked(ref[:, p * LANES:(p + 1) * LANES], keep_first)

    ops = {}
    for d, p in chains:
        rt, kt, at, bt, kh, bh, v = (load(dirs[d][j], p) for j in range(7))
        incl, strict = masks[d]
        big = _dot_nt(jnp.concatenate([at, rt], axis=0), jnp.concatenate([kt, bt], axis=0))
        n = jnp.where(strict, big[:PAIR, PAIR:], 0.0)
        ops[d, p] = dict(rt=rt, at=at, bh=bh, kh=kh, v=v, t=eye - n, n=n.astype(_bf16),
                         mak=jnp.where(strict, big[:PAIR, :PAIR], 0.0).astype(_bf16),
                         mrk=jnp.where(incl, big[PAIR:, :PAIR], 0.0).astype(_bf16),
                         mrb=jnp.where(incl, big[PAIR:, PAIR:], 0.0).astype(_bf16))
    for _ in range(5):
        for ch in chains:
            ops[ch]["n"] = _dot(ops[ch]["n"], ops[ch]["n"]).astype(_bf16)
        for ch in chains:
            ops[ch]["t"] = ops[ch]["t"] + _dot(ops[ch]["t"], ops[ch]["n"])
    for ch in chains:
        o = ops[ch]
        o["mv"] = _dot(jnp.concatenate([o.pop("mak"), o.pop("mrk")], axis=0), o["v"])
    for ch in chains:
        o = ops[ch]
        o["xz"] = _dot(o.pop("t"), jnp.concatenate([o["at"], o["mv"][:PAIR].astype(_bf16)], axis=1))
    for ch in chains:
        o = ops[ch]
        o["qy"] = _dot(o.pop("mrb"), o["xz"])
    for ch in chains:
        o = ops[ch]
        o["upd"] = _dot_tn(o["xz"], o["bh"])
        o["vtk"] = _dot_tn(o["v"], o["kh"])
    for d, p in chains:
        o = ops[d, p]
        s0 = s_scr[d, p]
        q_eff = o["rt"].astype(_f32) - o["qy"][:, :LANES]
        y = _dot_nt(q_eff, s0) + o["mv"][PAIR:] - o["qy"][:, LANES:]
        dirs[d][8][:, p * LANES:(p + 1) * LANES] = y[:CHUNK] + y[CHUNK:]
        p_last = dirs[d][7][:, p * LANES:(p + 1) * LANES]
        s_scr[d, p] = s0 * p_last - _dot(s0, o["upd"][:LANES]) + o["vtk"] - o["upd"][LANES:]

    @pl.when((c == nc - 1) & jnp.logical_not(is_lat))
    def _():
        for d in range(2):
            for p in range(N_PAIRS):
                sp = s_scr[d, p]
                sout_ref[d, 2 * p] = sp[:HD_A, :HD_A]
                sout_ref[d, 2 * p + 1] = sp[HD_A:, HD_A:]


def _lat_state_spec(shape, st):
    nd = len(shape)
    return pl.BlockSpec((None,) + tuple(shape[1:]),
                        lambda s: (jnp.maximum(_seq_of(s, st) - st.n_ctx_seq, 0),) + (0,) * (nd - 1))


def _ctx_state_spec(shape, st, layer):
    nd = len(shape)
    return pl.BlockSpec((None, None) + tuple(shape[2:]),
                        lambda s: (jnp.minimum(_seq_of(s, st), st.n_ctx_seq - 1), layer) + (0,) * (nd - 2))


_ANY = pl.BlockSpec(memory_space=pl.ANY)


def _rwkv_scan(prep, s0, st, acc, layer):
    rt, kt, at, bt, kh, bh, v, p_last = prep
    n = v.shape[0]
    fwd = lambda s: s
    bwd = lambda s: _bwd_chunk(s, st)
    seq2 = lambda d, f: pl.BlockSpec((None, CHUNK, D_A), lambda s: (d, f(s), 0))
    seq1 = lambda f: pl.BlockSpec((CHUNK, D_A), lambda s: (f(s), 0))
    pls = lambda d, f: pl.BlockSpec((None, None, 1, D_A), lambda s: (d, f(s), 0, 0))
    side = lambda d, f: [seq2(d, f)] * 6 + [seq1(f), pls(d, f)]
    y_shape = jax.ShapeDtypeStruct((n, D_A), _f32)
    args = (rt, kt, at, bt, kh, bh, v, p_last)
    return pl.pallas_call(
        functools.partial(_rwkv_scan_kernel, st=st),
        grid=(st.n_chunks,),
        in_specs=side(0, fwd) + side(1, bwd) + [_lat_state_spec(s0.shape, st), _ANY],
        out_specs=[seq1(fwd), seq1(bwd), _ctx_state_spec(acc.shape, st, layer)],
        out_shape=[y_shape, y_shape, jax.ShapeDtypeStruct(acc.shape, _f32)],
        input_output_aliases={2 * len(args) + 1: 2},
        scratch_shapes=[pltpu.VMEM((2, N_PAIRS, LANES, LANES), _f32)],
        compiler_params=_cparams(1), name="rwkv_scan",
    )(*args, *args, s0, acc)


def _gla_kernel(q_f, k_f, v_f, l_f, q_b, k_b, v_b, l_b, w_ref, bias_ref, s0_ref, _acc,
                of_ref, ob_ref, sout_ref, s_scr, *, st):
    s = pl.program_id(0)
    is_lat, nc, c, _ = _chunk_pos(s, st)

    @pl.when(c == 0)
    def _():
        keep = jnp.where(is_lat, 1.0, 0.0)
        for d in range(2):
            for h in range(H_B):
                s_scr[d, h] = s0_ref[d, h].T * keep

    dirs = ((q_f, k_f, v_f, l_f, of_ref), (q_b, k_b, v_b, l_b, ob_ref))
    work = []
    for d, (q_ref, k_ref, v_ref, l_ref, o_ref) in enumerate(dirs):
        incl, _ = _tri_masks(d)
        la = jax.nn.log_sigmoid(_dot3(l_ref[...], w_ref[d, 0], w_ref[d, 1]) + bias_ref[d:d + 1]) / GATE_NORM
        cum = _dot_exact_lhs(jnp.where(incl, 1.0, 0.0), la)
        tot = jnp.sum(la, axis=0, keepdims=True)
        k = k_ref[...]
        q_in = q_ref[...] * (DK_B ** -0.5) * jnp.exp(cum)
        k_in = k * jnp.exp(-cum)
        k_end = k * jnp.exp(tot - cum)
        e_tot = jnp.exp(tot)
        for h in range(H_B):
            ks = slice(h * DK_B, (h + 1) * DK_B)
            work.append((d, h, incl, q_in[:, ks], k_in[:, ks], k_end[:, ks], e_tot[:, ks]))
    att = [jnp.where(incl, _dot_nt(q_in, k_in), 0.0) for (_, _, incl, q_in, k_in, _, _) in work]
    for (d, h, _, q_in, _, k_end, e_tot), a in zip(work, att):
        v_ref, o_ref = dirs[d][2], dirs[d][4]
        vs = slice(h * DV_B, (h + 1) * DV_B)
        v = v_ref[:, vs]
        s_t = s_scr[d, h]
        o_ref[:, vs] = _dot(a, v) + _dot_nt(q_in, s_t)
        s_scr[d, h] = s_t * e_tot + _dot_tn(v, k_end)

    @pl.when((c == nc - 1) & jnp.logical_not(is_lat))
    def _():
        for d in range(2):
            for h in range(H_B):
                sout_ref[d, h] = s_scr[d, h].T


def _gla_scan(p, w, bias, s0, st, acc, layer):
    n = p.shape[0]
    fwd = lambda s: s
    bwd = lambda s: _bwd_chunk(s, st)
    blk = lambda width, col, f: pl.BlockSpec((CHUNK, width), lambda s: (f(s), col))
    side = lambda f: [blk(QK_B, EV_Q_BLK, f), blk(QK_B, EV_K_BLK, f), blk(V_B, EV_V_BLK, f),
                      blk(EV_LORA_W, EV_LORA_BLK, f)]
    full = lambda a: pl.BlockSpec(a.shape, lambda s: (0,) * a.ndim)
    o_spec = lambda f: pl.BlockSpec((CHUNK, V_B), lambda s: (f(s), 0))
    o_shape = jax.ShapeDtypeStruct((n, V_B), _f32)
    return pl.pallas_call(
        functools.partial(_gla_kernel, st=st),
        grid=(st.n_chunks,),
        in_specs=side(fwd) + side(bwd) + [full(w), full(bias), _lat_state_spec(s0.shape, st), _ANY],
        out_specs=[o_spec(fwd), o_spec(bwd), _ctx_state_spec(acc.shape, st, layer)],
        out_shape=[o_shape, o_shape, jax.ShapeDtypeStruct(acc.shape, _f32)],
        input_output_aliases={11: 2},
        scratch_shapes=[pltpu.VMEM((2, H_B, DV_B, DK_B), _f32)],
        compiler_params=_cparams(1), name="gla_scan",
    )(p, p, p, p, p, p, p, p, w, bias, s0, acc)


def _mlstm_kernel(q_f, k_f, v_f, g_f, q_b, k_b, v_b, g_b, bias_ref, c0_ref, n0_ref, m0_ref, _acc_c, _acc_n, _acc_m,
                  hf_ref, hb_ref, cout_ref, nout_ref, mout_ref, c_scr, n_scr, m_scr, *, st):
    s = pl.program_id(0)
    is_lat, nc, c, _ = _chunk_pos(s, st)

    @pl.when(c == 0)
    def _():
        keep = jnp.where(is_lat, 1.0, 0.0)
        for d in range(2):
            for h in range(H_C):
                c_scr[d, h] = c0_ref[d, h].T * keep
        n_scr[...] = n0_ref[...] * keep
        m_scr[...] = m0_ref[...] * keep

    L = CHUNK
    lane = lax.broadcasted_iota(jnp.int32, (L, LANES), 1)
    dirs = ((q_f, k_f, v_f, g_f, hf_ref), (q_b, k_b, v_b, g_b, hb_ref))
    work = []
    for d, (q_ref, k_ref, v_ref, g_ref, h_ref) in enumerate(dirs):
        gates = g_ref[...] + bias_ref[0:1]
        gates = jnp.where(lane < 2 * H_C, gates, jax.nn.log_sigmoid(gates))
        gates_t = gates.T
        for h in range(H_C):
            ji, jf = d * H_C + h, 2 * H_C + d * H_C + h
            ks = slice(h * DK_C, (h + 1) * DK_C)
            q = q_ref[:, ks] * (DK_C ** -0.5)
            k = k_ref[:, ks]
            work.append(dict(d=d, h=h, q=q, k=k, incl=_tri_masks(d)[0], incl_t=_tri_masks(1 - d)[0],
                             i_col=gates[:, ji:ji + 1], f_col=gates[:, jf:jf + 1],
                             i_row=gates_t[ji:ji + 1, :], f_row=gates_t[jf:jf + 1, :],
                             m_prev=m_scr[d, h], c_t=c_scr[d, h], n_row=n_scr[d, h]))
    for o in work:
        o["s"] = _dot_nt(o["q"], o["k"])
        o["qc"] = _dot_nt(o["q"], o["c_t"])
    for o in work:
        o["cum_col"] = jnp.sum(jnp.where(o["incl"], o["f_row"], 0.0), axis=1, keepdims=True)
        o["cum_row"] = jnp.sum(jnp.where(o["incl_t"], o["f_col"], 0.0), axis=0, keepdims=True)
        o["f_tot"] = jnp.sum(o["f_row"], axis=1, keepdims=True)
        o["qn"] = jnp.sum(o["q"] * o["n_row"], axis=1, keepdims=True)
    for o in work:
        o["log_d"] = jnp.where(o["incl"], o["cum_col"] - o["cum_row"] + o["i_row"], NEG_BIG)
        o["log_inter"] = o["cum_col"] + o["m_prev"]
        o["row_max"] = jnp.max(o["log_d"], axis=1, keepdims=True)
    for o in work:
        m_t = jnp.maximum(o["log_inter"], o["row_max"])
        o["m_t"] = m_t
        o["sm"] = o["s"] * jnp.exp(o.pop("log_d") - m_t)
        o["w_inter"] = jnp.exp(o["log_inter"] - m_t)
        o["m_new"] = m_t[L - 1:L, :] if o["d"] == 0 else m_t[0:1, :]
        o["kw"] = o["k"] * jnp.exp(o["f_tot"] - o["cum_col"] + o["i_col"] - o["m_new"])
        o["dec"] = jnp.exp(o["f_tot"] + o["m_prev"] - o["m_new"])
    for o in work:
        vs = slice(o["h"] * DV_C, (o["h"] + 1) * DV_C)
        o["v"] = dirs[o["d"]][2][:, vs]
        o["sv"] = _dot(o["sm"], o["v"])
        o["kv"] = _dot_tn(o["v"], o["kw"])
        o["s_sum"] = jnp.sum(o["sm"], axis=1, keepdims=True)
    for o in work:
        d, h = o["d"], o["h"]
        vs = slice(h * DV_C, (h + 1) * DV_C)
        num = o["w_inter"] * o["qc"] + o["sv"]
        den = o["w_inter"] * o["qn"] + o["s_sum"]
        dirs[d][4][:, vs] = num / jnp.maximum(jnp.abs(den), jnp.exp(-o["m_t"]))
        c_scr[d, h] = o["dec"] * o["c_t"] + o["kv"]
        n_scr[d, h] = o["dec"] * o["n_row"] + jnp.sum(o["kw"], axis=0, keepdims=True)
        m_scr[d, h] = o["m_new"]

    @pl.when((c == nc - 1) & jnp.logical_not(is_lat))
    def _():
        for d in range(2):
            for h in range(H_C):
                cout_ref[d, h] = c_scr[d, h].T
        nout_ref[...] = n_scr[...]
        mout_ref[...] = m_scr[...]


def _mlstm_scan(p, bias, c0, n0, m0, st, accs, layer):
    n = p.shape[0]
    fwd = lambda s: s
    bwd = lambda s: _bwd_chunk(s, st)
    blk = lambda width, col, f: pl.BlockSpec((CHUNK, width), lambda s: (f(s), col))
    side = lambda f: [blk(QK_C, 0, f), blk(QK_C, 1, f), blk(V_C, 1, f), blk(LANES, OD_GATE_BLK, f)]
    o_spec = lambda f: pl.BlockSpec((CHUNK, V_C), lambda s: (f(s), 0))
    o_shape = jax.ShapeDtypeStruct((n, V_C), _f32)
    states = (c0, n0, m0)
    return pl.pallas_call(
        functools.partial(_mlstm_kernel, st=st),
        grid=(st.n_chunks,),
        in_specs=side(fwd) + side(bwd) + [pl.BlockSpec(bias.shape, lambda s: (0, 0))]
        + [_lat_state_spec(a.shape, st) for a in states] + [_ANY] * 3,
        out_specs=[o_spec(fwd), o_spec(bwd)] + [_ctx_state_spec(a.shape, st, layer) for a in accs],
        out_shape=[o_shape, o_shape] + [jax.ShapeDtypeStruct(a.shape, _f32) for a in accs],
        input_output_aliases={12: 2, 13: 3, 14: 4},
        scratch_shapes=[pltpu.VMEM((2, H_C, DV_C, DK_C), _f32), pltpu.VMEM((2, H_C, 1, DK_C), _f32),
                        pltpu.VMEM((2, H_C, 1, 1), _f32)],
        compiler_params=_cparams(1), name="mlstm_scan",
    )(p, p, p, p, p, p, p, p, bias, c0, n0, m0, *accs)


def _head_rms_gate(o, g_row, gate, n_heads, width):
    outs = []
    for h in range(n_heads):
        hs = slice(h * width, (h + 1) * width)
        oh = o[:, hs]
        outs.append(oh * lax.rsqrt(jnp.mean(oh * oh, axis=-1, keepdims=True) + EPS) * g_row[:, hs] * gate[:, hs])
    return jnp.concatenate(outs, axis=1)


def _even_post_kernel(yf_ref, yb_ref, bonus_ref, g_ref, of_ref, ob_ref, og_ref, par_ref, gn_ref, seg_ref, segt_ref,
                      o_ref):
    y = yf_ref[...] + yb_ref[...]
    seg, seg_t = seg_ref[...], segt_ref[...]
    mean = _head_sum(y, seg, seg_t) * (1.0 / HD_A)
    yc = y - mean
    var = _head_sum(yc * yc, seg, seg_t) * (1.0 / HD_A)
    par = par_ref[...]
    ya = (yc * lax.rsqrt(var + EPS) * par[0:1] + par[1:2] + bonus_ref[...]) * g_ref[...]
    og = og_ref[...]
    yb = _head_rms_gate(of_ref[...] + ob_ref[...], gn_ref[...], og * jax.nn.sigmoid(og), H_B, DV_B)
    o_ref[:, :D_A] = ya.astype(_bf16)
    o_ref[:, D_A:] = yb.astype(_bf16)


def _even_post(yf, yb, bonus, g, of, ob, p, ln_par, gla_g, seg, seg_t, tm=256):
    n = yf.shape[0]
    row = lambda w, col=0: pl.BlockSpec((tm, w), lambda i: (i, col))
    full = lambda a: pl.BlockSpec(a.shape, lambda i: (0,) * a.ndim)
    return pl.pallas_call(
        _even_post_kernel,
        grid=(n // tm,),
        in_specs=[row(D_A)] * 4 + [row(V_B), row(V_B), row(V_B, EV_OG_BLK), full(ln_par), full(gla_g),
                  full(seg), full(seg_t)],
        out_specs=row(D_A + V_B),
        out_shape=jax.ShapeDtypeStruct((n, D_A + V_B), _bf16),
        compiler_params=_cparams(1), name="even_post",
    )(yf, yb, bonus, g, of, ob, p, ln_par, gla_g, seg, seg_t)


def _odd_post_kernel(hf_ref, hb_ref, o_ref_in, gn_ref, out_ref):
    out_ref[...] = _head_rms_gate(hf_ref[...] + hb_ref[...], gn_ref[...], jax.nn.sigmoid(o_ref_in[...]),
                                  H_C, DV_C).astype(_bf16)


def _odd_post(hf, hb, p, gn, tm=256):
    n = hf.shape[0]
    row = lambda col=0: pl.BlockSpec((tm, V_C), lambda i: (i, col))
    return pl.pallas_call(
        _odd_post_kernel,
        grid=(n // tm,),
        in_specs=[row(), row(), row(2), pl.BlockSpec(gn.shape, lambda i: (0, 0))],
        out_specs=row(),
        out_shape=jax.ShapeDtypeStruct((n, V_C), _bf16),
        compiler_params=_cparams(1), name="odd_post",
    )(hf, hb, p, gn)


def _even_mixer(p, st, layer, acc_rw, acc_gl, s_rwkv, s_gla, mu, w0, w2, a0, a2, g2, k_k, k_a, r_k, ln_g, ln_b,
                gla_w2, gla_b, gla_g):
    zpad = lambda x, rows_before, total: jnp.pad(x, ((rows_before, total - rows_before - x.shape[0]), (0, 0)))
    mu_main = mu[:EV_MAIN].reshape(1, EV_MAIN)
    mu_lora = jnp.pad(mu[EV_MAIN:], (0, EV_LORA_W - (A_IN - EV_MAIN))).reshape(1, EV_LORA_W)
    split = lambda w: jnp.stack(_split2(w), axis=0)
    ww = jnp.stack([split(zpad(w2[d], d * LORA_W, 2 * LORA_W)) for d in range(2)])
    wa = jnp.stack([split(zpad(a2[d], d * LORA_A, 2 * LORA_A)) for d in range(2)])
    wg = g2.astype(_bf16)
    par = jnp.stack([w0[0], w0[1], a0[0], a0[1], k_k, k_a, r_k, jnp.zeros_like(k_k)])
    head = np.arange(D_A) // HD_A
    seg_np = head[:, None] == np.arange(LANES)[None, :]
    seg, seg_t = jnp.asarray(seg_np, _bf16), jnp.asarray(seg_np.T, _bf16)
    *prep, bonus, g = _rwkv_prep(p, st, mu_main, mu_lora, ww, wa, wg, par, seg, seg_t)
    yf, yb, acc_rw = _rwkv_scan(prep, s_rwkv, st, acc_rw, layer)
    wl = jnp.stack([split(zpad(gla_w2[d], GLA_LORA_OFF + d * LORA_GK, EV_LORA_W)) for d in range(2)])
    bias = jnp.pad(gla_b, ((0, MOD_ROWS - 2), (0, 0)))
    of, ob, acc_gl = _gla_scan(p, wl, bias, s_gla, st, acc_gl, layer)
    ln_par = jnp.pad(jnp.stack([ln_g, ln_b]), ((0, MOD_ROWS - 2), (0, 0)))
    y_in = _even_post(yf, yb, bonus, g, of, ob, p, ln_par, gla_g.reshape(1, V_B), seg, seg_t)
    return y_in, acc_rw, acc_gl


def _odd_mixer(p, st, layer, accs, s_c, s_n, s_m, b_i, b_f, g):
    bias = jnp.concatenate([b_i.reshape(-1), b_f.reshape(-1), jnp.zeros((LANES - 4 * H_C,), _f32)])
    bias = jnp.pad(bias[None, :], ((0, MOD_ROWS - 1), (0, 0)))
    hf, hb, *accs = _mlstm_scan(p, bias, s_c, s_n[..., None, :], s_m[..., None, None], st, accs, layer)
    return _odd_post(hf, hb, p, g.reshape(1, V_C)), tuple(accs)


def _moe(h, logits, w1, w3, w2, layer, tm=FFN_TM):
    n_tok = h.shape[0]
    n_pairs = n_tok * TOP_K
    n_tiles = n_pairs // tm + N_EXPERTS
    top_v, top_i = lax.top_k(logits[:, :N_EXPERTS], TOP_K)
    gates = jax.nn.softmax(top_v, axis=-1)
    pair_e = top_i.reshape(-1)
    order = jnp.argsort(pair_e, stable=True)
    sorted_e = pair_e[order]
    counts = jnp.sum(jax.nn.one_hot(pair_e, N_EXPERTS, dtype=jnp.int32), axis=0)
    tiles_per_e = (counts + tm - 1) // tm
    tile_start = jnp.cumsum(tiles_per_e) - tiles_per_e
    group_start = jnp.cumsum(counts) - counts
    rank = jnp.arange(n_pairs, dtype=jnp.int32) - group_start[sorted_e]
    dest = tile_start[sorted_e] * tm + rank
    pair_slot = jnp.zeros((n_pairs,), jnp.int32).at[order].set(dest.astype(jnp.int32))
    n_valid = jnp.sum(tiles_per_e)
    tile_ids = jnp.arange(n_tiles, dtype=jnp.int32)
    tile_e = jnp.sum((tile_ids[:, None] >= tile_start[None, :]).astype(jnp.int32), axis=1) - 1
    last_e = jnp.max(jnp.where(counts > 0, jnp.arange(N_EXPERTS), 0))
    tile_valid = tile_ids < n_valid
    tile_e = jnp.where(tile_valid, tile_e, last_e).astype(jnp.int32)
    tile_off = (tile_ids - tile_start[tile_e]) * tm
    tile_rows = jnp.where(tile_valid, jnp.clip(counts[tile_e] - tile_off, 0, tm), 0).astype(jnp.int32)
    tile_base = jnp.where(tile_valid, group_start[tile_e] + tile_off, 0).astype(jnp.int32)
    pair_token = (order // TOP_K).astype(jnp.int32)
    yg = _gathered_ffn(h, pair_token, w1, w3, w2, layer, tile_e, tile_rows, tile_base, tm=tm)
    return _moe_combine(yg, pair_slot, gates)


def kernel(x_prompt, x_sample, state_rwkv, state_gla, state_mlstm_c, state_mlstm_n, state_mlstm_m, c, c_ctx, mod_w, mod_b, norm_g, final_g, ev_w_in, ev_mu, rw_w0, rw_w2, rw_a0, rw_a2, rw_g2, rw_kk, rw_ka, rw_rk, rw_ln_g, rw_ln_b, gla_w2, gla_b, gla_norm_g, ev_w_out, ffn_w1, ffn_w3, ffn_w2, od_w_in, ml_b_i, ml_b_f, ml_norm_g, od_w_out, moe_router, moe_w1, moe_w3, moe_w2):
    D = D_MODEL
    Bp, Tp, _ = x_prompt.shape
    Bs, Ts, _ = x_sample.shape
    assert Ts // GRID_W * GRID_W == Ts and GRID_W == CHUNK and Tp % CHUNK == 0
    n_ctx = Bp * Tp
    st = _Streams(n_ctx_chunks=n_ctx // CHUNK, ctx_cps=Tp // CHUNK, lat_cps=Ts // CHUNK,
                  n_chunks=(n_ctx + Bs * Ts) // CHUNK, n_ctx_seq=Bp, n_seq=Bp + Bs)
    x = jnp.concatenate([x_prompt.reshape(n_ctx, D), x_sample.reshape(Bs * Ts, D)], axis=0)
    n_tok = x.shape[0]

    cond = jnp.concatenate([c_ctx[None, :], c, jnp.zeros((MOD_ROWS - 1 - Bs, D), _f32)], axis=0)
    mod = _mod_table(jax.nn.silu(cond).astype(_bf16), mod_w, mod_b)

    n_even, n_odd = (DEPTH + 1) // 2, DEPTH // 2
    acc_rw = jnp.zeros((Bp, n_even, 2, H_A, HD_A, HD_A), _f32)
    acc_gl = jnp.zeros((Bp, n_even, 2, H_B, DK_B, DV_B), _f32)
    acc_ml = (jnp.zeros((Bp, n_odd, 2, H_C, DK_C, DV_C), _f32), jnp.zeros((Bp, n_odd, 2, H_C, 1, DK_C), _f32),
              jnp.zeros((Bp, n_odd, 2, H_C, 1, 1), _f32))
    pending = None

    for l in range(DEPTH):
        i = l // 2
        sh1, sc1, gt1, sh2, sc2, gt2 = [mod[l, :, j * D:(j + 1) * D] for j in range(6)]
        if pending is None:
            h = _norm_mod(x, norm_g[l, 0], sc1, sh1, st)[0]
        else:
            x, h = _norm_mod(x, norm_g[l, 0], sc1, sh1, st, res=pending)
        if l % 2 == 0:
            w = ev_w_in[i]
            b0 = A_IN
            w_in = jnp.concatenate([
                w[:, :EV_MAIN], w[:, b0 + 2 * QK_B:b0 + 2 * QK_B + V_B], w[:, b0 + 2 * QK_B + V_B + 2 * LORA_GK:],
                w[:, b0:b0 + 2 * QK_B], w[:, EV_MAIN:A_IN],
                w[:, b0 + 2 * QK_B + V_B:b0 + 2 * QK_B + V_B + 2 * LORA_GK],
                jnp.zeros((D, EV_LORA_W - (A_IN - EV_MAIN) - 2 * LORA_GK), _f32)], axis=1).astype(_bf16)
            p = _matmul(h, w_in, tn=EV_N // 4)
            y_in, acc_rw, acc_gl = _even_mixer(
                p, st, i, acc_rw, acc_gl, state_rwkv[:, i], state_gla[:, i], ev_mu[i], rw_w0[i], rw_w2[i],
                rw_a0[i], rw_a2[i], rw_g2[i], rw_kk[i], rw_ka[i], rw_rk[i], rw_ln_g[i], rw_ln_b[i],
                gla_w2[i], gla_b[i], gla_norm_g[i])
            x = _matmul_residual(y_in, ev_w_out[i].astype(_bf16), x, gt1, st)
        else:
            w_in = jnp.pad(od_w_in[i], ((0, 0), (0, OD_N - od_w_in.shape[2]))).astype(_bf16)
            p = _matmul(h, w_in, tn=OD_N // 7)
            y_in, acc_ml = _odd_mixer(
                p, st, i, acc_ml, state_mlstm_c[:, i], state_mlstm_n[:, i], state_mlstm_m[:, i],
                ml_b_i[i], ml_b_f[i], ml_norm_g[i])
            x = _matmul_residual(y_in, od_w_out[i].astype(_bf16), x, gt1, st)

        if l % 2 == 0:
            h = _norm_mod(x, norm_g[l, 1], sc2, sh2, st)[0]
            f = _dense_ffn(h, ffn_w1, ffn_w3, ffn_w2, i)
        else:
            router = _split2(jnp.pad(moe_router[i], ((0, 0), (0, LANES - N_EXPERTS))))
            h, logits = _norm_mod(x, norm_g[l, 1], sc2, sh2, st, router=router, h_dtype=_f32)
            f = _moe(h, logits, moe_w1, moe_w3, moe_w2, i)
        pending = (f, gt2)

    y = _final_norm(x, pending[0], pending[1], final_g, st)
    return (y[:n_ctx].reshape(Bp, Tp, D), y[n_ctx:].reshape(Bs, Ts, D), acc_rw, acc_gl,
            acc_ml[0], acc_ml[1][..., 0, :], acc_ml[2][..., 0, 0])
```

```python
import functools
from typing import NamedTuple

import jax
import jax.numpy as jnp
import numpy as np
from jax import lax
from jax.experimental import pallas as pl
from jax.experimental.pallas import tpu as pltpu

D_MODEL = 2048
DEPTH = 4
GRID_W = 64
EPS = 1e-6
CHUNK = 64
H_A, HD_A = 16, 64
D_A = H_A * HD_A
LORA_W, LORA_A, LORA_G = 64, 64, 128
A_IN = 3 * D_A + 2 * LORA_W + 2 * LORA_A + LORA_G
H_B, DK_B, DV_B = 4, 128, 256
QK_B, V_B = H_B * DK_B, H_B * DV_B
LORA_GK = 16
GATE_NORM = 16.0
H_C, DK_C, DV_C = 8, 128, 256
QK_C, V_C = H_C * DK_C, H_C * DV_C
D_FF = 5632
N_EXPERTS = 8
TOP_K = 2

LANES = 128
MOD_ROWS = 8
VMEM_LIMIT = 48 * 1024 * 1024
FFN_VMEM_LIMIT = 56 * 1024 * 1024
NEG_BIG = -1e30

EV_MAIN = 3 * D_A
EV_V_BLK = EV_MAIN // V_B
EV_OG_BLK = EV_V_BLK + 1
EV_Q_BLK = (EV_MAIN + 2 * V_B) // QK_B
EV_K_BLK = EV_Q_BLK + 1
EV_LORA_W = 512
EV_LORA_BLK = (EV_MAIN + 2 * V_B + 2 * QK_B) // EV_LORA_W
EV_N = EV_MAIN + 2 * V_B + 2 * QK_B + EV_LORA_W
GLA_LORA_OFF = 2 * LORA_W + 2 * LORA_A + LORA_G
OD_GATE_BLK = (2 * QK_C + 2 * V_C) // LANES
OD_N = 2 * QK_C + 2 * V_C + LANES

_f32 = jnp.float32
_bf16 = jnp.bfloat16


class _Streams(NamedTuple):
    n_ctx_chunks: int
    ctx_cps: int
    lat_cps: int
    n_chunks: int
    n_ctx_seq: int
    n_seq: int


def _chunk_pos(s, st):
    is_lat = s >= st.n_ctx_chunks
    sl = s - st.n_ctx_chunks
    nc = jnp.where(is_lat, st.lat_cps, st.ctx_cps)
    c = jnp.where(is_lat, sl % st.lat_cps, s % st.ctx_cps)
    seq = jnp.where(is_lat, st.n_ctx_seq + sl // st.lat_cps, s // st.ctx_cps)
    return is_lat, nc, c, seq


def _bwd_chunk(s, st):
    _, nc, c, _ = _chunk_pos(s, st)
    return s - c + (nc - 1 - c)


def _seq_of(s, st):
    return _chunk_pos(s, st)[3]


def _mod_row_of_chunk(s, st):
    is_lat, _, _, seq = _chunk_pos(s, st)
    return jnp.where(is_lat, 1 + seq - st.n_ctx_seq, 0)


def _cparams(n_axes):
    return pltpu.CompilerParams(dimension_semantics=("arbitrary",) * n_axes,
                                vmem_limit_bytes=VMEM_LIMIT)


def _dot(a, b):
    return jnp.dot(a.astype(_bf16), b.astype(_bf16), preferred_element_type=_f32)


def _dot_nt(a, b):
    return lax.dot_general(a.astype(_bf16), b.astype(_bf16), (((1,), (1,)), ((), ())),
                           preferred_element_type=_f32)


def _dot_tn(a, b):
    return lax.dot_general(a.astype(_bf16), b.astype(_bf16), (((0,), (0,)), ((), ())),
                           preferred_element_type=_f32)


def _split2(x):
    hi = x.astype(_bf16)
    return hi, (x - hi.astype(_f32)).astype(_bf16)


def _dot_exact_lhs(tri, x):
    hi, lo = _split2(x)
    t = tri.astype(_bf16)
    return jnp.dot(t, hi, preferred_element_type=_f32) + jnp.dot(t, lo, preferred_element_type=_f32)


def _dot_exact_rhs(x, ones):
    hi, lo = _split2(x)
    return jnp.dot(hi, ones, preferred_element_type=_f32) + jnp.dot(lo, ones, preferred_element_type=_f32)


def _head_sum(x, seg, seg_t):
    return _dot_exact_rhs(_dot_exact_rhs(x, seg), seg_t)


def _dot3(a, b_hi, b_lo):
    a_hi, a_lo = _split2(a)
    d = lambda u, v: jnp.dot(u, v, preferred_element_type=_f32)
    return d(a_hi, b_hi) + d(a_lo, b_hi) + d(a_hi, b_lo)


def _tri_masks(direction, n=CHUNK, period=CHUNK):
    t = lax.broadcasted_iota(jnp.int32, (n, n), 0)
    i = lax.broadcasted_iota(jnp.int32, (n, n), 1)
    same = (t // period) == (i // period) if n != period else True
    if direction == 0:
        return (i <= t) & same, (i < t) & same
    return (i >= t) & same, (i > t) & same


def _mod_kernel(a_ref, w_ref, b_ref, o_ref):
    o_ref[...] = _dot(a_ref[...], w_ref[...]) + b_ref[...]


def _mod_table(silu_rows, mod_w, mod_b, tn=1024):
    n_l, k, n = mod_w.shape
    return pl.pallas_call(
        _mod_kernel,
        grid=(n_l, n // tn),
        in_specs=[pl.BlockSpec((MOD_ROWS, k), lambda l, j: (0, 0)),
                  pl.BlockSpec((None, k, tn), lambda l, j: (l, 0, j)),
                  pl.BlockSpec((None, 1, tn), lambda l, j: (l, 0, j))],
        out_specs=pl.BlockSpec((None, MOD_ROWS, tn), lambda l, j: (l, 0, j)),
        out_shape=jax.ShapeDtypeStruct((n_l, MOD_ROWS, n), _f32),
        compiler_params=_cparams(2), name="mod_table",
    )(silu_rows, mod_w, mod_b.reshape(n_l, 1, n))


def _norm_mod_kernel(*refs, tm, st, has_res, has_router):
    refs = list(refs)
    x_ref = refs.pop(0)
    f_ref, gate_ref = (refs.pop(0), refs.pop(0)) if has_res else (None, None)
    g_ref, sc_ref, sh_ref = refs.pop(0), refs.pop(0), refs.pop(0)
    r_hi, r_lo = (refs.pop(0), refs.pop(0)) if has_router else (None, None)
    xo_ref = refs.pop(0) if has_res else None
    h_ref = refs.pop(0)
    row = _mod_row_of_chunk(pl.program_id(0) * (tm // CHUNK), st)
    x = x_ref[...]
    if has_res:
        x = x + gate_ref[pl.ds(row, 1), :] * f_ref[...]
        xo_ref[...] = x
    inv = lax.rsqrt(jnp.mean(x * x, axis=-1, keepdims=True) + EPS)
    h = x * inv * g_ref[...] * (1.0 + sc_ref[pl.ds(row, 1), :]) + sh_ref[pl.ds(row, 1), :]
    h_ref[...] = h.astype(h_ref.dtype)
    if has_router:
        refs.pop(0)[...] = _dot3(h, r_hi[...], r_lo[...])


def _norm_mod(x, g, sc, sh, st, res=None, router=None, h_dtype=_bf16, tm=256):
    m, d = x.shape
    row = pl.BlockSpec((tm, d), lambda i: (i, 0))
    tab = pl.BlockSpec((MOD_ROWS, d), lambda i: (0, 0))
    args, specs = [x], [row]
    outs, out_specs = [], []
    if res is not None:
        args += list(res)
        specs += [row, tab]
        outs.append(jax.ShapeDtypeStruct((m, d), _f32))
        out_specs.append(row)
    args += [g.reshape(1, d), sc, sh]
    specs += [pl.BlockSpec((1, d), lambda i: (0, 0)), tab, tab]
    outs.append(jax.ShapeDtypeStruct((m, d), h_dtype))
    out_specs.append(row)
    if router is not None:
        args += list(router)
        specs += [pl.BlockSpec((d, LANES), lambda i: (0, 0))] * 2
        outs.append(jax.ShapeDtypeStruct((m, LANES), _f32))
        out_specs.append(pl.BlockSpec((tm, LANES), lambda i: (i, 0)))
    return pl.pallas_call(
        functools.partial(_norm_mod_kernel, tm=tm, st=st, has_res=res is not None, has_router=router is not None),
        grid=(m // tm,), in_specs=specs, out_specs=out_specs, out_shape=outs,
        compiler_params=_cparams(1), name="norm_mod",
    )(*args)


def _final_norm_kernel(x_ref, f_ref, gate_ref, g_ref, o_ref, *, tm, st):
    row = _mod_row_of_chunk(pl.program_id(0) * (tm // CHUNK), st)
    x = x_ref[...] + gate_ref[pl.ds(row, 1), :] * f_ref[...]
    o_ref[...] = x * lax.rsqrt(jnp.mean(x * x, axis=-1, keepdims=True) + EPS) * g_ref[...]


def _final_norm(x, f, gate, g, st, tm=256):
    m, d = x.shape
    row = pl.BlockSpec((tm, d), lambda i: (i, 0))
    return pl.pallas_call(
        functools.partial(_final_norm_kernel, tm=tm, st=st),
        grid=(m // tm,),
        in_specs=[row, row, pl.BlockSpec((MOD_ROWS, d), lambda i: (0, 0)), pl.BlockSpec((1, d), lambda i: (0, 0))],
        out_specs=row,
        out_shape=jax.ShapeDtypeStruct((m, d), _f32),
        compiler_params=_cparams(1), name="final_norm",
    )(x, f, gate, g.reshape(1, d))


def _mm_kernel(a_ref, w_ref, o_ref):
    o_ref[...] = _dot(a_ref[...], w_ref[...])


def _mm_res_kernel(a_ref, w_ref, res_ref, gate_ref, o_ref, *, tm, st):
    row = _mod_row_of_chunk(pl.program_id(1) * (tm // CHUNK), st)
    gate = gate_ref[pl.ds(row, 1), :]
    o_ref[...] = res_ref[...] + gate * _dot(a_ref[...], w_ref[...])


def _matmul(a, w, tm=512, tn=512):
    m, k = a.shape
    n = w.shape[1]
    tm = min(tm, m)
    return pl.pallas_call(
        _mm_kernel,
        grid=(n // tn, m // tm),
        in_specs=[pl.BlockSpec((tm, k), lambda j, i: (i, 0)),
                  pl.BlockSpec((k, tn), lambda j, i: (0, j))],
        out_specs=pl.BlockSpec((tm, tn), lambda j, i: (i, j)),
        out_shape=jax.ShapeDtypeStruct((m, n), _f32),
        compiler_params=_cparams(2), name="matmul",
    )(a, w)


def _matmul_residual(a, w, res, gate, st, tm=512, tn=1024):
    m, k = a.shape
    n = w.shape[1]
    tm = min(tm, m)
    return pl.pallas_call(
        functools.partial(_mm_res_kernel, tm=tm, st=st),
        grid=(n // tn, m // tm),
        in_specs=[pl.BlockSpec((tm, k), lambda j, i: (i, 0)),
                  pl.BlockSpec((k, tn), lambda j, i: (0, j)),
                  pl.BlockSpec((tm, tn), lambda j, i: (i, j)),
                  pl.BlockSpec((MOD_ROWS, tn), lambda j, i: (0, j))],
        out_specs=pl.BlockSpec((tm, tn), lambda j, i: (i, j)),
        out_shape=jax.ShapeDtypeStruct((m, n), _f32),
        compiler_params=_cparams(2), name="matmul_residual",
    )(a, w, res, gate)


FFN_TM = 1024
FFN_TF = 256
FFN_ROW_STEP = 256


def _swiglu_accumulate(x_ref, w1_ref, w3_ref, w2_ref, o_ref, n_rows):
    x = x_ref[0:n_rows, :]
    a = _dot(x, w1_ref[...])
    b = _dot(x, w3_ref[...])
    o_ref[0:n_rows, :] += _dot(a * jax.nn.sigmoid(a) * b, w2_ref[...])


def _ffn_kernel(x_ref, w1_ref, w3_ref, w2_ref, o_ref, *, tm):
    @pl.when(pl.program_id(1) == 0)
    def _():
        o_ref[...] = jnp.zeros_like(o_ref)

    _swiglu_accumulate(x_ref, w1_ref, w3_ref, w2_ref, o_ref, tm)


def _row_copy(src_hbm, src_row, dst, dst_row, sem):
    return pltpu.make_async_copy(src_hbm.at[pl.ds(src_row, 1), :], dst.at[pl.ds(dst_row, 1), :], sem)


ROW_DMA_UNROLL = 8


def _for_each_row(n, body):
    groups = n // ROW_DMA_UNROLL

    def group(g, carry):
        r0 = pl.multiple_of(g * ROW_DMA_UNROLL, ROW_DMA_UNROLL)
        for u in range(ROW_DMA_UNROLL):
            body(r0 + u)
        return carry

    def single(r, carry):
        body(r)
        return carry

    lax.fori_loop(0, groups, group, 0)
    lax.fori_loop(groups * ROW_DMA_UNROLL, n, single, 0)


def _ffn_gather_kernel(te_ref, tr_ref, tb_ref, tok_ref, h_hbm, w1_ref, w3_ref, w2_ref, o_ref, xf_scr, xb_scr, sem,
                       *, tm, n_tiles):
    i, j = pl.program_id(0), pl.program_id(1)
    rows = tr_ref[i]

    def start_gather(tile):
        base = tb_ref[tile]
        _for_each_row(tr_ref[tile], lambda r: _row_copy(h_hbm, tok_ref[base + r], xf_scr, r, sem).start())

    def wait_gather(tile):
        @pl.when(tr_ref[tile] == tm)
        def _():
            pltpu.make_async_copy(h_hbm.at[pl.ds(0, tm), :], xf_scr, sem).wait()

        @pl.when(tr_ref[tile] < tm)
        def _():
            _for_each_row(tr_ref[tile], lambda r: _row_copy(h_hbm, 0, xf_scr, r, sem).wait())

    @pl.when(j == 0)
    def _():
        @pl.when(i == 0)
        def _():
            xf_scr[...] = jnp.zeros_like(xf_scr)
            start_gather(0)

        wait_gather(i)
        xb_scr[...] = xf_scr[...].astype(_bf16)
        o_ref[...] = jnp.zeros_like(o_ref)

        @pl.when(i + 1 < n_tiles)
        def _():
            start_gather(i + 1)

    for n_rows in range(FFN_ROW_STEP, tm + 1, FFN_ROW_STEP):
        @pl.when((rows > n_rows - FFN_ROW_STEP) & (rows <= n_rows))
        def _():
            _swiglu_accumulate(xb_scr, w1_ref, w3_ref, w2_ref, o_ref, n_rows)


def _gathered_ffn(h, pair_token, w1, w3, w2, layer, tile_expert, tile_rows, tile_base, tm=FFN_TM, tf=FFN_TF):
    d = h.shape[1]
    f = w1.shape[3]
    nj = f // tf
    n_tiles = tile_expert.shape[0]

    def jeff(i, j, tr):
        return jnp.where(tr[i] > 0, j, nj - 1)

    return pl.pallas_call(
        functools.partial(_ffn_gather_kernel, tm=tm, n_tiles=n_tiles),
        grid_spec=pltpu.PrefetchScalarGridSpec(
            num_scalar_prefetch=4,
            grid=(n_tiles, nj),
            in_specs=[pl.BlockSpec(memory_space=pl.ANY),
                      pl.BlockSpec((None, None, d, tf), lambda i, j, te, tr, *_: (layer, te[i], 0, jeff(i, j, tr))),
                      pl.BlockSpec((None, None, d, tf), lambda i, j, te, tr, *_: (layer, te[i], 0, jeff(i, j, tr))),
                      pl.BlockSpec((None, None, tf, d), lambda i, j, te, tr, *_: (layer, te[i], jeff(i, j, tr), 0))],
            out_specs=pl.BlockSpec((tm, d), lambda i, j, *_: (i, 0)),
            scratch_shapes=[pltpu.VMEM((tm, d), _f32), pltpu.VMEM((tm, d), _bf16), pltpu.SemaphoreType.DMA]),
        out_shape=jax.ShapeDtypeStruct((n_tiles * tm, d), _f32),
        compiler_params=pltpu.CompilerParams(dimension_semantics=("arbitrary", "arbitrary"),
                                             vmem_limit_bytes=FFN_VMEM_LIMIT),
        name="gathered_ffn",
    )(tile_expert, tile_rows, tile_base, pair_token, h, w1, w3, w2)


COMBINE_TM = 256


def _combine_kernel(slot_ref, yg_hbm, gates_ref, o_ref, buf, sem, *, tm):
    base = pl.program_id(0) * tm * TOP_K

    def start(t):
        for k in range(TOP_K):
            _row_copy(yg_hbm, slot_ref[base + t * TOP_K + k], buf.at[k], t, sem).start()

    _for_each_row(tm, start)
    for k in range(TOP_K):
        pltpu.make_async_copy(yg_hbm.at[pl.ds(0, tm), :], buf.at[k], sem).wait()
    g = gates_ref[...]
    o_ref[...] = g[:, 0:1] * buf[0] + g[:, 1:2] * buf[1]


def _moe_combine(yg, pair_slot, gates, tm=COMBINE_TM):
    n_tok = gates.shape[0]
    d = yg.shape[1]
    return pl.pallas_call(
        functools.partial(_combine_kernel, tm=tm),
        grid_spec=pltpu.PrefetchScalarGridSpec(
            num_scalar_prefetch=1,
            grid=(n_tok // tm,),
            in_specs=[pl.BlockSpec(memory_space=pl.ANY),
                      pl.BlockSpec((tm, TOP_K), lambda b, sl: (b, 0))],
            out_specs=pl.BlockSpec((tm, d), lambda b, sl: (b, 0)),
            scratch_shapes=[pltpu.VMEM((TOP_K, tm, d), _f32), pltpu.SemaphoreType.DMA]),
        out_shape=jax.ShapeDtypeStruct((n_tok, d), _f32),
        compiler_params=_cparams(1), name="moe_combine",
    )(pair_slot, yg, gates)


def _dense_ffn(x, w1, w3, w2, layer, tm=FFN_TM, tf=FFN_TF):
    m, d = x.shape
    f = w1.shape[2]
    return pl.pallas_call(
        functools.partial(_ffn_kernel, tm=tm),
        grid=(m // tm, f // tf),
        in_specs=[pl.BlockSpec((tm, d), lambda i, j: (i, 0)),
                  pl.BlockSpec((None, d, tf), lambda i, j: (layer, 0, j)),
                  pl.BlockSpec((None, d, tf), lambda i, j: (layer, 0, j)),
                  pl.BlockSpec((None, tf, d), lambda i, j: (layer, j, 0))],
        out_specs=pl.BlockSpec((tm, d), lambda i, j: (i, 0)),
        out_shape=jax.ShapeDtypeStruct((m, d), _f32),
        compiler_params=pltpu.CompilerParams(dimension_semantics=("arbitrary", "arbitrary"),
                                             vmem_limit_bytes=FFN_VMEM_LIMIT),
        name="dense_ffn",
    )(x, w1, w3, w2)


PREP_CHUNKS = 4
PREP_ROWS = PREP_CHUNKS * CHUNK


def _token_shift_block(x, prev, nxt, lane0, is_lat, first, last):
    rows, n = x.shape
    row = lax.broadcasted_iota(jnp.int32, x.shape, 0)
    ch = lax.broadcasted_iota(jnp.int32, x.shape, 1) + lane0
    down = pltpu.roll(x, 1, 0)
    up = pltpu.roll(x, rows - 1, 0)

    def context():
        zero_row = jnp.zeros((1, n), _f32)
        row_prev = jnp.where(first, zero_row, prev[CHUNK - 1:CHUNK, :])
        row_next = jnp.where(last, zero_row, nxt[0:1, :])
        return jnp.where(ch < A_IN // 2,
                         jnp.where(row == 0, row_prev, down),
                         jnp.where(row == rows - 1, row_next, up))

    def latent():
        q = A_IN // 4
        col = row % GRID_W
        above = jnp.concatenate([jnp.where(first, 0.0, prev), x[:rows - CHUNK]], axis=0)
        below = jnp.concatenate([x[CHUNK:], jnp.where(last, 0.0, nxt)], axis=0)
        return jnp.where(ch < q, jnp.where(col == 0, 0.0, down),
                         jnp.where(ch < 2 * q, jnp.where(col == GRID_W - 1, 0.0, up),
                                   jnp.where(ch < 3 * q, above, below)))

    return lax.cond(is_lat, latent, context)


def _rwkv_prep_kernel(xm_ref, pm_ref, nm_ref, xl_ref, pl_ref, nl_ref, mum_ref, mul_ref,
                      ww_ref, wa_ref, wg_ref, par_ref, seg_ref, segt_ref,
                      rt_ref, kt_ref, at_ref, bt_ref, kh_ref, bh_ref, v_ref, plast_ref, bonus_ref, g_ref,
                      *, st):
    s = pl.program_id(0) * PREP_CHUNKS
    is_lat, nc, c, _ = _chunk_pos(s, st)
    first, last = c == 0, c + PREP_CHUNKS == nc
    xm, xl = xm_ref[...], xl_ref[...]
    sm = _token_shift_block(xm, pm_ref[...], nm_ref[...], 0, is_lat, first, last)
    sl = _token_shift_block(xl, pl_ref[...], nl_ref[...], EV_MAIN, is_lat, first, last)
    pm = xm + mum_ref[...] * (sm - xm)
    lora = xl + mul_ref[...] * (sl - xl)
    r, k, v = pm[:, :D_A], pm[:, D_A:2 * D_A], pm[:, 2 * D_A:]
    par = par_ref[...]
    k_k, k_a, r_k = par[4:5], par[5:6], par[6:7]
    seg, seg_t = seg_ref[...], segt_ref[...]
    kk = k * k_k
    kk = kk / jnp.maximum(jnp.sqrt(_head_sum(kk * kk, seg, seg_t)), 1e-12)
    g_ref[...] = _dot(jax.nn.sigmoid(lora[:, 2 * LORA_W + 2 * LORA_A:GLA_LORA_OFF]), wg_ref[...])
    v_ref[...] = v.astype(_bf16)
    tanh_w = jnp.tanh(lora[:, :2 * LORA_W])
    lora_a = lora[:, 2 * LORA_W:2 * LORA_W + 2 * LORA_A]
    kdir_sum = jnp.zeros_like(k)
    t = lax.broadcasted_iota(jnp.int32, (PREP_ROWS, PREP_ROWS), 0)
    i = lax.broadcasted_iota(jnp.int32, (PREP_ROWS, PREP_ROWS), 1)
    same_chunk = jnp.where((t // CHUNK) == (i // CHUNK), 1.0, 0.0)
    for d in range(2):
        w = -jax.nn.softplus(-(par[d:d + 1] + _dot3(tanh_w, ww_ref[d, 0], ww_ref[d, 1]))) - 0.5
        ld = -jnp.exp(w)
        a = jax.nn.sigmoid(par[2 + d:3 + d] + _dot3(lora_a, wa_ref[d, 0], wa_ref[d, 1]))
        kdir = k * (1.0 + (a - 1.0) * k_a)
        kdir_sum = kdir_sum + kdir
        incl, _ = _tri_masks(d, PREP_ROWS, CHUNK)
        cum = _dot_exact_lhs(jnp.where(incl, 1.0, 0.0), ld)
        tot = _dot_exact_lhs(same_chunk, ld)
        e_pos, e_neg, e_end = jnp.exp(cum), jnp.exp(-cum), jnp.exp(tot - cum)
        kka = kk * a
        rt_ref[d] = (r * e_pos).astype(_bf16)
        kt_ref[d] = (kdir * e_neg).astype(_bf16)
        at_ref[d] = (kk * jnp.exp(cum - ld)).astype(_bf16)
        bt_ref[d] = (kka * e_neg).astype(_bf16)
        kh_ref[d] = (kdir * e_end).astype(_bf16)
        bh_ref[d] = (kka * e_end).astype(_bf16)
        e_tot = jnp.exp(tot)
        for j in range(PREP_CHUNKS):
            plast_ref[d, j] = e_tot[j * CHUNK:j * CHUNK + 1, :]
    bonus_ref[...] = _head_sum(r * kdir_sum * r_k, seg, seg_t) * v


def _rwkv_prep(p, st, mu_main, mu_lora, ww, wa, wg, par, seg, seg_t):
    n = p.shape[0]
    nch = st.n_chunks
    assert st.ctx_cps % PREP_CHUNKS == 0 and st.lat_cps % PREP_CHUNKS == 0
    cur = lambda b: (b, 0)
    prv = lambda b: (jnp.maximum(b * PREP_CHUNKS - 1, 0), 0)
    nxt = lambda b: (jnp.minimum((b + 1) * PREP_CHUNKS, nch - 1), 0)
    lo = lambda f: (lambda b: (f(b)[0], EV_LORA_BLK))
    main = lambda rows, f: pl.BlockSpec((rows, EV_MAIN), f)
    lora = lambda rows, f: pl.BlockSpec((rows, EV_LORA_W), lo(f))
    full = lambda a: pl.BlockSpec(a.shape, lambda b: (0,) * a.ndim)
    seq2 = pl.BlockSpec((2, PREP_ROWS, D_A), lambda b: (0, b, 0))
    seq1 = pl.BlockSpec((PREP_ROWS, D_A), lambda b: (b, 0))
    bf2 = jax.ShapeDtypeStruct((2, n, D_A), _bf16)
    return pl.pallas_call(
        functools.partial(_rwkv_prep_kernel, st=st),
        grid=(nch // PREP_CHUNKS,),
        in_specs=[main(PREP_ROWS, cur), main(CHUNK, prv), main(CHUNK, nxt),
                  lora(PREP_ROWS, cur), lora(CHUNK, prv), lora(CHUNK, nxt),
                  full(mu_main), full(mu_lora), full(ww), full(wa), full(wg), full(par), full(seg), full(seg_t)],
        out_specs=[seq2] * 6 + [seq1, pl.BlockSpec((2, PREP_CHUNKS, 1, D_A), lambda b: (0, b, 0, 0)), seq1, seq1],
        out_shape=[bf2] * 6 + [jax.ShapeDtypeStruct((n, D_A), _bf16),
                               jax.ShapeDtypeStruct((2, nch, 1, D_A), _f32),
                               jax.ShapeDtypeStruct((n, D_A), _f32),
                               jax.ShapeDtypeStruct((n, D_A), _f32)],
        compiler_params=_cparams(1), name="rwkv_prep",
    )(p, p, p, p, p, p, mu_main, mu_lora, ww, wa, wg, par, seg, seg_t)


N_PAIRS = H_A // 2
PAIR = 2 * CHUNK


def _stack_masked(x, keep_first):
    return jnp.concatenate([x * keep_first, x * (1 - keep_first)], axis=0)


def _rwkv_scan_kernel(*refs, st):
    (rt_f, kt_f, at_f, bt_f, kh_f, bh_f, v_f, pl_f,
     rt_b, kt_b, at_b, bt_b, kh_b, bh_b, v_b, pl_b, s0_ref, _acc, yf_ref, yb_ref, sout_ref, s_scr) = refs
    s = pl.program_id(0)
    is_lat, nc, c, _ = _chunk_pos(s, st)

    @pl.when(c == 0)
    def _():
        keep = jnp.where(is_lat, 1.0, 0.0)
        zero = jnp.zeros((HD_A, HD_A), _f32)
        for d in range(2):
            for p in range(N_PAIRS):
                top = jnp.concatenate([s0_ref[d, 2 * p], zero], axis=1)
                bot = jnp.concatenate([zero, s0_ref[d, 2 * p + 1]], axis=1)
                s_scr[d, p] = jnp.concatenate([top, bot], axis=0) * keep

    dirs = ((rt_f, kt_f, at_f, bt_f, kh_f, bh_f, v_f, pl_f, yf_ref),
            (rt_b, kt_b, at_b, bt_b, kh_b, bh_b, v_b, pl_b, yb_ref))
    chains = [(d, p) for d in range(2) for p in range(N_PAIRS)]
    masks = [_tri_masks(d, PAIR, CHUNK) for d in range(2)]
    eye = jnp.where(masks[0][0] & masks[1][0], 1.0, 0.0)

    lane = lax.broadcasted_iota(jnp.int32, (CHUNK, LANES), 1)
    keep_first = jnp.where(lane < HD_A, 1.0, 0.0).astype(_bf16)

    def load(ref, p):
        return _stack_masked(ref[:, p * LANES:(p + 1) * LANES], keep_first)

    ops = {}
    for d, p in chains:
        rt, kt, at, bt, kh, bh, v = (load(dirs[d][j], p) for j in range(7))
        incl, strict = masks[d]
        big = _dot_nt(jnp.concatenate([at, rt], axis=0), jnp.concatenate([kt, bt], axis=0))
        n = jnp.where(strict, big[:PAIR, PAIR:], 0.0)
        ops[d, p] = dict(rt=rt, at=at, bh=bh, kh=kh, v=v, t=eye - n, n=n.astype(_bf16),
                         mak=jnp.where(strict, big[:PAIR, :PAIR], 0.0).astype(_bf16),
                         mrk=jnp.where(incl, big[PAIR:, :PAIR], 0.0).astype(_bf16),
                         mrb=jnp.where(incl, big[PAIR:, PAIR:], 0.0).astype(_bf16))
    for _ in range(5):
        for ch in chains:
            ops[ch]["n"] = _dot(ops[ch]["n"], ops[ch]["n"]).astype(_bf16)
        for ch in chains:
            ops[ch]["t"] = ops[ch]["t"] + _dot(ops[ch]["t"], ops[ch]["n"])
    for ch in chains:
        o = ops[ch]
        o["mv"] = _dot(jnp.concatenate([o.pop("mak"), o.pop("mrk")], axis=0), o["v"])
    for ch in chains:
        o = ops[ch]
        o["xz"] = _dot(o.pop("t"), jnp.concatenate([o["at"], o["mv"][:PAIR].astype(_bf16)], axis=1))
    for ch in chains:
        o = ops[ch]
        o["qy"] = _dot(o.pop("mrb"), o["xz"])
    for ch in chains:
        o = ops[ch]
        o["upd"] = _dot_tn(o["xz"], o["bh"])
        o["vtk"] = _dot_tn(o["v"], o["kh"])
    for d, p in chains:
        o = ops[d, p]
        s0 = s_scr[d, p]
        q_eff = o["rt"].astype(_f32) - o["qy"][:, :LANES]
        y = _dot_nt(q_eff, s0) + o["mv"][PAIR:] - o["qy"][:, LANES:]
        dirs[d][8][:, p * LANES:(p + 1) * LANES] = y[:CHUNK] + y[CHUNK:]
        p_last = dirs[d][7][:, p * LANES:(p + 1) * LANES]
        s_scr[d, p] = s0 * p_last - _dot(s0, o["upd"][:LANES]) + o["vtk"] - o["upd"][LANES:]

    @pl.when((c == nc - 1) & jnp.logical_not(is_lat))
    def _():
        for d in range(2):
            for p in range(N_PAIRS):
                sp = s_scr[d, p]
                sout_ref[d, 2 * p] = sp[:HD_A, :HD_A]
                sout_ref[d, 2 * p + 1] = sp[HD_A:, HD_A:]


def _lat_state_spec(shape, st):
    nd = len(shape)
    return pl.BlockSpec((None,) + tuple(shape[1:]),
                        lambda s: (jnp.maximum(_seq_of(s, st) - st.n_ctx_seq, 0),) + (0,) * (nd - 1))


def _ctx_state_spec(shape, st, layer):
    nd = len(shape)
    return pl.BlockSpec((None, None) + tuple(shape[2:]),
                        lambda s: (jnp.minimum(_seq_of(s, st), st.n_ctx_seq - 1), layer) + (0,) * (nd - 2))


_ANY = pl.BlockSpec(memory_space=pl.ANY)


def _rwkv_scan(prep, s0, st, acc, layer):
    rt, kt, at, bt, kh, bh, v, p_last = prep
    n = v.shape[0]
    fwd = lambda s: s
    bwd = lambda s: _bwd_chunk(s, st)
    seq2 = lambda d, f: pl.BlockSpec((None, CHUNK, D_A), lambda s: (d, f(s), 0))
    seq1 = lambda f: pl.BlockSpec((CHUNK, D_A), lambda s: (f(s), 0))
    pls = lambda d, f: pl.BlockSpec((None, None, 1, D_A), lambda s: (d, f(s), 0, 0))
    side = lambda d, f: [seq2(d, f)] * 6 + [seq1(f), pls(d, f)]
    y_shape = jax.ShapeDtypeStruct((n, D_A), _f32)
    args = (rt, kt, at, bt, kh, bh, v, p_last)
    return pl.pallas_call(
        functools.partial(_rwkv_scan_kernel, st=st),
        grid=(st.n_chunks,),
        in_specs=side(0, fwd) + side(1, bwd) + [_lat_state_spec(s0.shape, st), _ANY],
        out_specs=[seq1(fwd), seq1(bwd), _ctx_state_spec(acc.shape, st, layer)],
        out_shape=[y_shape, y_shape, jax.ShapeDtypeStruct(acc.shape, _f32)],
        input_output_aliases={2 * len(args) + 1: 2},
        scratch_shapes=[pltpu.VMEM((2, N_PAIRS, LANES, LANES), _f32)],
        compiler_params=_cparams(1), name="rwkv_scan",
    )(*args, *args, s0, acc)


def _gla_kernel(q_f, k_f, v_f, l_f, q_b, k_b, v_b, l_b, w_ref, bias_ref, s0_ref, _acc,
                of_ref, ob_ref, sout_ref, s_scr, *, st):
    s = pl.program_id(0)
    is_lat, nc, c, _ = _chunk_pos(s, st)

    @pl.when(c == 0)
    def _():
        keep = jnp.where(is_lat, 1.0, 0.0)
        for d in range(2):
            for h in range(H_B):
                s_scr[d, h] = s0_ref[d, h].T * keep

    dirs = ((q_f, k_f, v_f, l_f, of_ref), (q_b, k_b, v_b, l_b, ob_ref))
    work = []
    for d, (q_ref, k_ref, v_ref, l_ref, o_ref) in enumerate(dirs):
        incl, _ = _tri_masks(d)
        la = jax.nn.log_sigmoid(_dot3(l_ref[...], w_ref[d, 0], w_ref[d, 1]) + bias_ref[d:d + 1]) / GATE_NORM
        cum = _dot_exact_lhs(jnp.where(incl, 1.0, 0.0), la)
        tot = jnp.sum(la, axis=0, keepdims=True)
        k = k_ref[...]
        q_in = q_ref[...] * (DK_B ** -0.5) * jnp.exp(cum)
        k_in = k * jnp.exp(-cum)
        k_end = k * jnp.exp(tot - cum)
        e_tot = jnp.exp(tot)
        for h in range(H_B):
            ks = slice(h * DK_B, (h + 1) * DK_B)
            work.append((d, h, incl, q_in[:, ks], k_in[:, ks], k_end[:, ks], e_tot[:, ks]))
    att = [jnp.where(incl, _dot_nt(q_in, k_in), 0.0) for (_, _, incl, q_in, k_in, _, _) in work]
    for (d, h, _, q_in, _, k_end, e_tot), a in zip(work, att):
        v_ref, o_ref = dirs[d][2], dirs[d][4]
        vs = slice(h * DV_B, (h + 1) * DV_B)
        v = v_ref[:, vs]
        s_t = s_scr[d, h]
        o_ref[:, vs] = _dot(a, v) + _dot_nt(q_in, s_t)
        s_scr[d, h] = s_t * e_tot + _dot_tn(v, k_end)

    @pl.when((c == nc - 1) & jnp.logical_not(is_lat))
    def _():
        for d in range(2):
            for h in range(H_B):
                sout_ref[d, h] = s_scr[d, h].T


def _gla_scan(p, w, bias, s0, st, acc, layer):
    n = p.shape[0]
    fwd = lambda s: s
    bwd = lambda s: _bwd_chunk(s, st)
    blk = lambda width, col, f: pl.BlockSpec((CHUNK, width), lambda s: (f(s), col))
    side = lambda f: [blk(QK_B, EV_Q_BLK, f), blk(QK_B, EV_K_BLK, f), blk(V_B, EV_V_BLK, f),
                      blk(EV_LORA_W, EV_LORA_BLK, f)]
    full = lambda a: pl.BlockSpec(a.shape, lambda s: (0,) * a.ndim)
    o_spec = lambda f: pl.BlockSpec((CHUNK, V_B), lambda s: (f(s), 0))
    o_shape = jax.ShapeDtypeStruct((n, V_B), _f32)
    return pl.pallas_call(
        functools.partial(_gla_kernel, st=st),
        grid=(st.n_chunks,),
        in_specs=side(fwd) + side(bwd) + [full(w), full(bias), _lat_state_spec(s0.shape, st), _ANY],
        out_specs=[o_spec(fwd), o_spec(bwd), _ctx_state_spec(acc.shape, st, layer)],
        out_shape=[o_shape, o_shape, jax.ShapeDtypeStruct(acc.shape, _f32)],
        input_output_aliases={11: 2},
        scratch_shapes=[pltpu.VMEM((2, H_B, DV_B, DK_B), _f32)],
        compiler_params=_cparams(1), name="gla_scan",
    )(p, p, p, p, p, p, p, p, w, bias, s0, acc)


def _mlstm_kernel(q_f, k_f, v_f, g_f, q_b, k_b, v_b, g_b, bias_ref, c0_ref, n0_ref, m0_ref, _acc_c, _acc_n, _acc_m,
                  hf_ref, hb_ref, cout_ref, nout_ref, mout_ref, c_scr, n_scr, m_scr, *, st):
    s = pl.program_id(0)
    is_lat, nc, c, _ = _chunk_pos(s, st)

    @pl.when(c == 0)
    def _():
        keep = jnp.where(is_lat, 1.0, 0.0)
        for d in range(2):
            for h in range(H_C):
                c_scr[d, h] = c0_ref[d, h].T * keep
        n_scr[...] = n0_ref[...] * keep
        m_scr[...] = m0_ref[...] * keep

    L = CHUNK
    lane = lax.broadcasted_iota(jnp.int32, (L, LANES), 1)
    dirs = ((q_f, k_f, v_f, g_f, hf_ref), (q_b, k_b, v_b, g_b, hb_ref))
    work = []
    for d, (q_ref, k_ref, v_ref, g_ref, h_ref) in enumerate(dirs):
        gates = g_ref[...] + bias_ref[0:1]
        gates = jnp.where(lane < 2 * H_C, gates, jax.nn.log_sigmoid(gates))
        gates_t = gates.T
        for h in range(H_C):
            ji, jf = d * H_C + h, 2 * H_C + d * H_C + h
            ks = slice(h * DK_C, (h + 1) * DK_C)
            q = q_ref[:, ks] * (DK_C ** -0.5)
            k = k_ref[:, ks]
            work.append(dict(d=d, h=h, q=q, k=k, incl=_tri_masks(d)[0], incl_t=_tri_masks(1 - d)[0],
                             i_col=gates[:, ji:ji + 1], f_col=gates[:, jf:jf + 1],
                             i_row=gates_t[ji:ji + 1, :], f_row=gates_t[jf:jf + 1, :],
                             m_prev=m_scr[d, h], c_t=c_scr[d, h], n_row=n_scr[d, h]))
    for o in work:
        o["s"] = _dot_nt(o["q"], o["k"])
        o["qc"] = _dot_nt(o["q"], o["c_t"])
    for o in work:
        o["cum_col"] = jnp.sum(jnp.where(o["incl"], o["f_row"], 0.0), axis=1, keepdims=True)
        o["cum_row"] = jnp.sum(jnp.where(o["incl_t"], o["f_col"], 0.0), axis=0, keepdims=True)
        o["f_tot"] = jnp.sum(o["f_row"], axis=1, keepdims=True)
        o["qn"] = jnp.sum(o["q"] * o["n_row"], axis=1, keepdims=True)
    for o in work:
        o["log_d"] = jnp.where(o["incl"], o["cum_col"] - o["cum_row"] + o["i_row"], NEG_BIG)
        o["log_inter"] = o["cum_col"] + o["m_prev"]
        o["row_max"] = jnp.max(o["log_d"], axis=1, keepdims=True)
    for o in work:
        m_t = jnp.maximum(o["log_inter"], o["row_max"])
        o["m_t"] = m_t
        o["sm"] = o["s"] * jnp.exp(o.pop("log_d") - m_t)
        o["w_inter"] = jnp.exp(o["log_inter"] - m_t)
        o["m_new"] = m_t[L - 1:L, :] if o["d"] == 0 else m_t[0:1, :]
        o["kw"] = o["k"] * jnp.exp(o["f_tot"] - o["cum_col"] + o["i_col"] - o["m_new"])
        o["dec"] = jnp.exp(o["f_tot"] + o["m_prev"] - o["m_new"])
    for o in work:
        vs = slice(o["h"] * DV_C, (o["h"] + 1) * DV_C)
        o["v"] = dirs[o["d"]][2][:, vs]
        o["sv"] = _dot(o["sm"], o["v"])
        o["kv"] = _dot_tn(o["v"], o["kw"])
        o["s_sum"] = jnp.sum(o["sm"], axis=1, keepdims=True)
    for o in work:
        d, h = o["d"], o["h"]
        vs = slice(h * DV_C, (h + 1) * DV_C)
        num = o["w_inter"] * o["qc"] + o["sv"]
        den = o["w_inter"] * o["qn"] + o["s_sum"]
        dirs[d][4][:, vs] = num / jnp.maximum(jnp.abs(den), jnp.exp(-o["m_t"]))
        c_scr[d, h] = o["dec"] * o["c_t"] + o["kv"]
        n_scr[d, h] = o["dec"] * o["n_row"] + jnp.sum(o["kw"], axis=0, keepdims=True)
        m_scr[d, h] = o["m_new"]

    @pl.when((c == nc - 1) & jnp.logical_not(is_lat))
    def _():
        for d in range(2):
            for h in range(H_C):
                cout_ref[d, h] = c_scr[d, h].T
        nout_ref[...] = n_scr[...]
        mout_ref[...] = m_scr[...]


def _mlstm_scan(p, bias, c0, n0, m0, st, accs, layer):
    n = p.shape[0]
    fwd = lambda s: s
    bwd = lambda s: _bwd_chunk(s, st)
    blk = lambda width, col, f: pl.BlockSpec((CHUNK, width), lambda s: (f(s), col))
    side = lambda f: [blk(QK_C, 0, f), blk(QK_C, 1, f), blk(V_C, 1, f), blk(LANES, OD_GATE_BLK, f)]
    o_spec = lambda f: pl.BlockSpec((CHUNK, V_C), lambda s: (f(s), 0))
    o_shape = jax.ShapeDtypeStruct((n, V_C), _f32)
    states = (c0, n0, m0)
    return pl.pallas_call(
        functools.partial(_mlstm_kernel, st=st),
        grid=(st.n_chunks,),
        in_specs=side(fwd) + side(bwd) + [pl.BlockSpec(bias.shape, lambda s: (0, 0))]
        + [_lat_state_spec(a.shape, st) for a in states] + [_ANY] * 3,
        out_specs=[o_spec(fwd), o_spec(bwd)] + [_ctx_state_spec(a.shape, st, layer) for a in accs],
        out_shape=[o_shape, o_shape] + [jax.ShapeDtypeStruct(a.shape, _f32) for a in accs],
        input_output_aliases={12: 2, 13: 3, 14: 4},
        scratch_shapes=[pltpu.VMEM((2, H_C, DV_C, DK_C), _f32), pltpu.VMEM((2, H_C, 1, DK_C), _f32),
                        pltpu.VMEM((2, H_C, 1, 1), _f32)],
        compiler_params=_cparams(1), name="mlstm_scan",
    )(p, p, p, p, p, p, p, p, bias, c0, n0, m0, *accs)


def _head_rms_gate(o, g_row, gate, n_heads, width):
    outs = []
    for h in range(n_heads):
        hs = slice(h * width, (h + 1) * width)
        oh = o[:, hs]
        outs.append(oh * lax.rsqrt(jnp.mean(oh * oh, axis=-1, keepdims=True) + EPS) * g_row[:, hs] * gate[:, hs])
    return jnp.concatenate(outs, axis=1)


def _even_post_kernel(yf_ref, yb_ref, bonus_ref, g_ref, of_ref, ob_ref, og_ref, par_ref, gn_ref, seg_ref, segt_ref,
                      o_ref):
    y = yf_ref[...] + yb_ref[...]
    seg, seg_t = seg_ref[...], segt_ref[...]
    mean = _head_sum(y, seg, seg_t) * (1.0 / HD_A)
    yc = y - mean
    var = _head_sum(yc * yc, seg, seg_t) * (1.0 / HD_A)
    par = par_ref[...]
    ya = (yc * lax.rsqrt(var + EPS) * par[0:1] + par[1:2] + bonus_ref[...]) * g_ref[...]
    og = og_ref[...]
    yb = _head_rms_gate(of_ref[...] + ob_ref[...], gn_ref[...], og * jax.nn.sigmoid(og), H_B, DV_B)
    o_ref[:, :D_A] = ya.astype(_bf16)
    o_ref[:, D_A:] = yb.astype(_bf16)


def _even_post(yf, yb, bonus, g, of, ob, p, ln_par, gla_g, seg, seg_t, tm=256):
    n = yf.shape[0]
    row = lambda w, col=0: pl.BlockSpec((tm, w), lambda i: (i, col))
    full = lambda a: pl.BlockSpec(a.shape, lambda i: (0,) * a.ndim)
    return pl.pallas_call(
        _even_post_kernel,
        grid=(n // tm,),
        in_specs=[row(D_A)] * 4 + [row(V_B), row(V_B), row(V_B, EV_OG_BLK), full(ln_par), full(gla_g),
                  full(seg), full(seg_t)],
        out_specs=row(D_A + V_B),
        out_shape=jax.ShapeDtypeStruct((n, D_A + V_B), _bf16),
        compiler_params=_cparams(1), name="even_post",
    )(yf, yb, bonus, g, of, ob, p, ln_par, gla_g, seg, seg_t)


def _odd_post_kernel(hf_ref, hb_ref, o_ref_in, gn_ref, out_ref):
    out_ref[...] = _head_rms_gate(hf_ref[...] + hb_ref[...], gn_ref[...], jax.nn.sigmoid(o_ref_in[...]),
                                  H_C, DV_C).astype(_bf16)


def _odd_post(hf, hb, p, gn, tm=256):
    n = hf.shape[0]
    row = lambda col=0: pl.BlockSpec((tm, V_C), lambda i: (i, col))
    return pl.pallas_call(
        _odd_post_kernel,
        grid=(n // tm,),
        in_specs=[row(), row(), row(2), pl.BlockSpec(gn.shape, lambda i: (0, 0))],
        out_specs=row(),
        out_shape=jax.ShapeDtypeStruct((n, V_C), _bf16),
        compiler_params=_cparams(1), name="odd_post",
    )(hf, hb, p, gn)


def _even_mixer(p, st, layer, acc_rw, acc_gl, s_rwkv, s_gla, mu, w0, w2, a0, a2, g2, k_k, k_a, r_k, ln_g, ln_b,
                gla_w2, gla_b, gla_g):
    zpad = lambda x, rows_before, total: jnp.pad(x, ((rows_before, total - rows_before - x.shape[0]), (0, 0)))
    mu_main = mu[:EV_MAIN].reshape(1, EV_MAIN)
    mu_lora = jnp.pad(mu[EV_MAIN:], (0, EV_LORA_W - (A_IN - EV_MAIN))).reshape(1, EV_LORA_W)
    split = lambda w: jnp.stack(_split2(w), axis=0)
    ww = jnp.stack([split(zpad(w2[d], d * LORA_W, 2 * LORA_W)) for d in range(2)])
    wa = jnp.stack([split(zpad(a2[d], d * LORA_A, 2 * LORA_A)) for d in range(2)])
    wg = g2.astype(_bf16)
    par = jnp.stack([w0[0], w0[1], a0[0], a0[1], k_k, k_a, r_k, jnp.zeros_like(k_k)])
    head = np.arange(D_A) // HD_A
    seg_np = head[:, None] == np.arange(LANES)[None, :]
    seg, seg_t = jnp.asarray(seg_np, _bf16), jnp.asarray(seg_np.T, _bf16)
    *prep, bonus, g = _rwkv_prep(p, st, mu_main, mu_lora, ww, wa, wg, par, seg, seg_t)
    yf, yb, acc_rw = _rwkv_scan(prep, s_rwkv, st, acc_rw, layer)
    wl = jnp.stack([split(zpad(gla_w2[d], GLA_LORA_OFF + d * LORA_GK, EV_LORA_W)) for d in range(2)])
    bias = jnp.pad(gla_b, ((0, MOD_ROWS - 2), (0, 0)))
    of, ob, acc_gl = _gla_scan(p, wl, bias, s_gla, st, acc_gl, layer)
    ln_par = jnp.pad(jnp.stack([ln_g, ln_b]), ((0, MOD_ROWS - 2), (0, 0)))
    y_in = _even_post(yf, yb, bonus, g, of, ob, p, ln_par, gla_g.reshape(1, V_B), seg, seg_t)
    return y_in, acc_rw, acc_gl


def _odd_mixer(p, st, layer, accs, s_c, s_n, s_m, b_i, b_f, g):
    bias = jnp.concatenate([b_i.reshape(-1), b_f.reshape(-1), jnp.zeros((LANES - 4 * H_C,), _f32)])
    bias = jnp.pad(bias[None, :], ((0, MOD_ROWS - 1), (0, 0)))
    hf, hb, *accs = _mlstm_scan(p, bias, s_c, s_n[..., None, :], s_m[..., None, None], st, accs, layer)
    return _odd_post(hf, hb, p, g.reshape(1, V_C)), tuple(accs)


def _moe(h, logits, w1, w3, w2, layer, tm=FFN_TM):
    n_tok = h.shape[0]
    n_pairs = n_tok * TOP_K
    n_tiles = n_pairs // tm + N_EXPERTS
    top_v, top_i = lax.top_k(logits[:, :N_EXPERTS], TOP_K)
    gates = jax.nn.softmax(top_v, axis=-1)
    pair_e = top_i.reshape(-1)
    order = jnp.argsort(pair_e, stable=True)
    sorted_e = pair_e[order]
    counts = jnp.sum(jax.nn.one_hot(pair_e, N_EXPERTS, dtype=jnp.int32), axis=0)
    tiles_per_e = (counts + tm - 1) // tm
    tile_start = jnp.cumsum(tiles_per_e) - tiles_per_e
    group_start = jnp.cumsum(counts) - counts
    rank = jnp.arange(n_pairs, dtype=jnp.int32) - group_start[sorted_e]
    dest = tile_start[sorted_e] * tm + rank
    pair_slot = jnp.zeros((n_pairs,), jnp.int32).at[order].set(dest.astype(jnp.int32))
    n_valid = jnp.sum(tiles_per_e)
    tile_ids = jnp.arange(n_tiles, dtype=jnp.int32)
    tile_e = jnp.sum((tile_ids[:, None] >= tile_start[None, :]).astype(jnp.int32), axis=1) - 1
    last_e = jnp.max(jnp.where(counts > 0, jnp.arange(N_EXPERTS), 0))
    tile_valid = tile_ids < n_valid
    tile_e = jnp.where(tile_valid, tile_e, last_e).astype(jnp.int32)
    tile_off = (tile_ids - tile_start[tile_e]) * tm
    tile_rows = jnp.where(tile_valid, jnp.clip(counts[tile_e] - tile_off, 0, tm), 0).astype(jnp.int32)
    tile_base = jnp.where(tile_valid, group_start[tile_e] + tile_off, 0).astype(jnp.int32)
    pair_token = (order // TOP_K).astype(jnp.int32)
    yg = _gathered_ffn(h, pair_token, w1, w3, w2, layer, tile_e, tile_rows, tile_base, tm=tm)
    return _moe_combine(yg, pair_slot, gates)


def kernel(x_prompt, x_sample, state_rwkv, state_gla, state_mlstm_c, state_mlstm_n, state_mlstm_m, c, c_ctx, mod_w, mod_b, norm_g, final_g, ev_w_in, ev_mu, rw_w0, rw_w2, rw_a0, rw_a2, rw_g2, rw_kk, rw_ka, rw_rk, rw_ln_g, rw_ln_b, gla_w2, gla_b, gla_norm_g, ev_w_out, ffn_w1, ffn_w3, ffn_w2, od_w_in, ml_b_i, ml_b_f, ml_norm_g, od_w_out, moe_router, moe_w1, moe_w3, moe_w2):
    D = D_MODEL
    Bp, Tp, _ = x_prompt.shape
    Bs, Ts, _ = x_sample.shape
    assert Ts // GRID_W * GRID_W == Ts and GRID_W == CHUNK and Tp % CHUNK == 0
    n_ctx = Bp * Tp
    st = _Streams(n_ctx_chunks=n_ctx // CHUNK, ctx_cps=Tp // CHUNK, lat_cps=Ts // CHUNK,
                  n_chunks=(n_ctx + Bs * Ts) // CHUNK, n_ctx_seq=Bp, n_seq=Bp + Bs)
    x = jnp.concatenate([x_prompt.reshape(n_ctx, D), x_sample.reshape(Bs * Ts, D)], axis=0)
    n_tok = x.shape[0]

    cond = jnp.concatenate([c_ctx[None, :], c, jnp.zeros((MOD_ROWS - 1 - Bs, D), _f32)], axis=0)
    mod = _mod_table(jax.nn.silu(cond).astype(_bf16), mod_w, mod_b)

    n_even, n_odd = (DEPTH + 1) // 2, DEPTH // 2
    acc_rw = jnp.zeros((Bp, n_even, 2, H_A, HD_A, HD_A), _f32)
    acc_gl = jnp.zeros((Bp, n_even, 2, H_B, DK_B, DV_B), _f32)
    acc_ml = (jnp.zeros((Bp, n_odd, 2, H_C, DK_C, DV_C), _f32), jnp.zeros((Bp, n_odd, 2, H_C, 1, DK_C), _f32),
              jnp.zeros((Bp, n_odd, 2, H_C, 1, 1), _f32))
    pending = None

    for l in range(DEPTH):
        i = l // 2
        sh1, sc1, gt1, sh2, sc2, gt2 = [mod[l, :, j * D:(j + 1) * D] for j in range(6)]
        if pending is None:
            h = _norm_mod(x, norm_g[l, 0], sc1, sh1, st)[0]
        else:
            x, h = _norm_mod(x, norm_g[l, 0], sc1, sh1, st, res=pending)
        if l % 2 == 0:
            w = ev_w_in[i]
            b0 = A_IN
            w_in = jnp.concatenate([
                w[:, :EV_MAIN], w[:, b0 + 2 * QK_B:b0 + 2 * QK_B + V_B], w[:, b0 + 2 * QK_B + V_B + 2 * LORA_GK:],
                w[:, b0:b0 + 2 * QK_B], w[:, EV_MAIN:A_IN],
                w[:, b0 + 2 * QK_B + V_B:b0 + 2 * QK_B + V_B + 2 * LORA_GK],
                jnp.zeros((D, EV_LORA_W - (A_IN - EV_MAIN) - 2 * LORA_GK), _f32)], axis=1).astype(_bf16)
            p = _matmul(h, w_in, tn=EV_N // 4)
            y_in, acc_rw, acc_gl = _even_mixer(
                p, st, i, acc_rw, acc_gl, state_rwkv[:, i], state_gla[:, i], ev_mu[i], rw_w0[i], rw_w2[i],
                rw_a0[i], rw_a2[i], rw_g2[i], rw_kk[i], rw_ka[i], rw_rk[i], rw_ln_g[i], rw_ln_b[i],
                gla_w2[i], gla_b[i], gla_norm_g[i])
            x = _matmul_residual(y_in, ev_w_out[i].astype(_bf16), x, gt1, st)
        else:
            w_in = jnp.pad(od_w_in[i], ((0, 0), (0, OD_N - od_w_in.shape[2]))).astype(_bf16)
            p = _matmul(h, w_in, tn=OD_N // 7)
            y_in, acc_ml = _odd_mixer(
                p, st, i, acc_ml, state_mlstm_c[:, i], state_mlstm_n[:, i], state_mlstm_m[:, i],
                ml_b_i[i], ml_b_f[i], ml_norm_g[i])
            x = _matmul_residual(y_in, od_w_out[i].astype(_bf16), x, gt1, st)

        if l % 2 == 0:
            h = _norm_mod(x, norm_g[l, 1], sc2, sh2, st)[0]
            f = _dense_ffn(h, ffn_w1, ffn_w3, ffn_w2, i)
        else:
            router = _split2(jnp.pad(moe_router[i], ((0, 0), (0, LANES - N_EXPERTS))))
            h, logits = _norm_mod(x, norm_g[l, 1], sc2, sh2, st, router=router, h_dtype=_f32)
            f = _moe(h, logits, moe_w1, moe_w3, moe_w2, i)
        pending = (f, gt2)

    y = _final_norm(x, pending[0], pending[1], final_g, st)
    return (y[:n_ctx].reshape(Bp, Tp, D), y[n_ctx:].reshape(Bs, Ts, D), acc_rw, acc_gl,
            acc_ml[0], acc_ml[1][..., 0, :], acc_ml[2][..., 0, 0])
```

```python
import functools
from typing import NamedTuple

import jax
import jax.numpy as jnp
import numpy as np
from jax import lax
from jax.experimental import pallas as pl
from jax.experimental.pallas import tpu as pltpu

D_MODEL = 2048
DEPTH = 4
GRID_W = 64
EPS = 1e-6
CHUNK = 64
H_A, HD_A = 16, 64
D_A = H_A * HD_A
LORA_W, LORA_A, LORA_G = 64, 64, 128
A_IN = 3 * D_A + 2 * LORA_W + 2 * LORA_A + LORA_G
H_B, DK_B, DV_B = 4, 128, 256
QK_B, V_B = H_B * DK_B, H_B * DV_B
LORA_GK = 16
GATE_NORM = 16.0
H_C, DK_C, DV_C = 8, 128, 256
QK_C, V_C = H_C * DK_C, H_C * DV_C
D_FF = 5632
N_EXPERTS = 8
TOP_K = 2

LANES = 128
MOD_ROWS = 8
VMEM_LIMIT = 48 * 1024 * 1024
FFN_VMEM_LIMIT = 56 * 1024 * 1024
NEG_BIG = -1e30

EV_MAIN = 3 * D_A
EV_V_BLK = EV_MAIN // V_B
EV_OG_BLK = EV_V_BLK + 1
EV_Q_BLK = (EV_MAIN + 2 * V_B) // QK_B
EV_K_BLK = EV_Q_BLK + 1
EV_LORA_W = 512
EV_LORA_BLK = (EV_MAIN + 2 * V_B + 2 * QK_B) // EV_LORA_W
EV_N = EV_MAIN + 2 * V_B + 2 * QK_B + EV_LORA_W
GLA_LORA_OFF = 2 * LORA_W + 2 * LORA_A + LORA_G
OD_GATE_BLK = (2 * QK_C + 2 * V_C) // LANES
OD_N = 2 * QK_C + 2 * V_C + LANES

_f32 = jnp.float32
_bf16 = jnp.bfloat16


class _Streams(NamedTuple):
    n_ctx_chunks: int
    ctx_cps: int
    lat_cps: int
    n_chunks: int
    n_ctx_seq: int
    n_seq: int


def _chunk_pos(s, st):
    is_lat = s >= st.n_ctx_chunks
    sl = s - st.n_ctx_chunks
    nc = jnp.where(is_lat, st.lat_cps, st.ctx_cps)
    c = jnp.where(is_lat, sl % st.lat_cps, s % st.ctx_cps)
    seq = jnp.where(is_lat, st.n_ctx_seq + sl // st.lat_cps, s // st.ctx_cps)
    return is_lat, nc, c, seq


def _bwd_chunk(s, st):
    _, nc, c, _ = _chunk_pos(s, st)
    return s - c + (nc - 1 - c)


def _seq_of(s, st):
    return _chunk_pos(s, st)[3]


def _mod_row_of_chunk(s, st):
    is_lat, _, _, seq = _chunk_pos(s, st)
    return jnp.where(is_lat, 1 + seq - st.n_ctx_seq, 0)


def _cparams(n_axes):
    return pltpu.CompilerParams(dimension_semantics=("arbitrary",) * n_axes,
                                vmem_limit_bytes=VMEM_LIMIT)


def _dot(a, b):
    return jnp.dot(a.astype(_bf16), b.astype(_bf16), preferred_element_type=_f32)


def _dot_nt(a, b):
    return lax.dot_general(a.astype(_bf16), b.astype(_bf16), (((1,), (1,)), ((), ())),
                           preferred_element_type=_f32)


def _dot_tn(a, b):
    return lax.dot_general(a.astype(_bf16), b.astype(_bf16), (((0,), (0,)), ((), ())),
                           preferred_element_type=_f32)


def _split2(x):
    hi = x.astype(_bf16)
    return hi, (x - hi.astype(_f32)).astype(_bf16)


def _dot_exact_lhs(tri, x):
    hi, lo = _split2(x)
    t = tri.astype(_bf16)
    return jnp.dot(t, hi, preferred_element_type=_f32) + jnp.dot(t, lo, preferred_element_type=_f32)


def _dot_exact_rhs(x, ones):
    hi, lo = _split2(x)
    return jnp.dot(hi, ones, preferred_element_type=_f32) + jnp.dot(lo, ones, preferred_element_type=_f32)


def _head_sum(x, seg, seg_t):
    return _dot_exact_rhs(_dot_exact_rhs(x, seg), seg_t)


def _dot3(a, b_hi, b_lo):
    a_hi, a_lo = _split2(a)
    d = lambda u, v: jnp.dot(u, v, preferred_element_type=_f32)
    return d(a_hi, b_hi) + d(a_lo, b_hi) + d(a_hi, b_lo)


def _tri_masks(direction, n=CHUNK, period=CHUNK):
    t = lax.broadcasted_iota(jnp.int32, (n, n), 0)
    i = lax.broadcasted_iota(jnp.int32, (n, n), 1)
    same = (t // period) == (i // period) if n != period else True
    if direction == 0:
        return (i <= t) & same, (i < t) & same
    return (i >= t) & same, (i > t) & same


def _mod_kernel(a_ref, w_ref, b_ref, o_ref):
    o_ref[...] = _dot(a_ref[...], w_ref[...]) + b_ref[...]


def _mod_table(silu_rows, mod_w, mod_b, tn=1024):
    n_l, k, n = mod_w.shape
    return pl.pallas_call(
        _mod_kernel,
        grid=(n_l, n // tn),
        in_specs=[pl.BlockSpec((MOD_ROWS, k), lambda l, j: (0, 0)),
                  pl.BlockSpec((None, k, tn), lambda l, j: (l, 0, j)),
                  pl.BlockSpec((None, 1, tn), lambda l, j: (l, 0, j))],
        out_specs=pl.BlockSpec((None, MOD_ROWS, tn), lambda l, j: (l, 0, j)),
        out_shape=jax.ShapeDtypeStruct((n_l, MOD_ROWS, n), _f32),
        compiler_params=_cparams(2), name="mod_table",
    )(silu_rows, mod_w, mod_b.reshape(n_l, 1, n))


def _norm_mod_kernel(*refs, tm, st, has_res, has_router):
    refs = list(refs)
    x_ref = refs.pop(0)
    f_ref, gate_ref = (refs.pop(0), refs.pop(0)) if has_res else (None, None)
    g_ref, sc_ref, sh_ref = refs.pop(0), refs.pop(0), refs.pop(0)
    r_hi, r_lo = (refs.pop(0), refs.pop(0)) if has_router else (None, None)
    xo_ref = refs.pop(0) if has_res else None
    h_ref = refs.pop(0)
    row = _mod_row_of_chunk(pl.program_id(0) * (tm // CHUNK), st)
    x = x_ref[...]
    if has_res:
        x = x + gate_ref[pl.ds(row, 1), :] * f_ref[...]
        xo_ref[...] = x
    inv = lax.rsqrt(jnp.mean(x * x, axis=-1, keepdims=True) + EPS)
    h = x * inv * g_ref[...] * (1.0 + sc_ref[pl.ds(row, 1), :]) + sh_ref[pl.ds(row, 1), :]
    h_ref[...] = h.astype(h_ref.dtype)
    if has_router:
        refs.pop(0)[...] = _dot3(h, r_hi[...], r_lo[...])


def _norm_mod(x, g, sc, sh, st, res=None, router=None, h_dtype=_bf16, tm=256):
    m, d = x.shape
    row = pl.BlockSpec((tm, d), lambda i: (i, 0))
    tab = pl.BlockSpec((MOD_ROWS, d), lambda i: (0, 0))
    args, specs = [x], [row]
    outs, out_specs = [], []
    if res is not None:
        args += list(res)
        specs += [row, tab]
        outs.append(jax.ShapeDtypeStruct((m, d), _f32))
        out_specs.append(row)
    args += [g.reshape(1, d), sc, sh]
    specs += [pl.BlockSpec((1, d), lambda i: (0, 0)), tab, tab]
    outs.append(jax.ShapeDtypeStruct((m, d), h_dtype))
    out_specs.append(row)
    if router is not None:
        args += list(router)
        specs += [pl.BlockSpec((d, LANES), lambda i: (0, 0))] * 2
        outs.append(jax.ShapeDtypeStruct((m, LANES), _f32))
        out_specs.append(pl.BlockSpec((tm, LANES), lambda i: (i, 0)))
    return pl.pallas_call(
        functools.partial(_norm_mod_kernel, tm=tm, st=st, has_res=res is not None, has_router=router is not None),
        grid=(m // tm,), in_specs=specs, out_specs=out_specs, out_shape=outs,
        compiler_params=_cparams(1), name="norm_mod",
    )(*args)


def _final_norm_kernel(x_ref, f_ref, gate_ref, g_ref, o_ref, *, tm, st):
    row = _mod_row_of_chunk(pl.program_id(0) * (tm // CHUNK), st)
    x = x_ref[...] + gate_ref[pl.ds(row, 1), :] * f_ref[...]
    o_ref[...] = x * lax.rsqrt(jnp.mean(x * x, axis=-1, keepdims=True) + EPS) * g_ref[...]


def _final_norm(x, f, gate, g, st, tm=256):
    m, d = x.shape
    row = pl.BlockSpec((tm, d), lambda i: (i, 0))
    return pl.pallas_call(
        functools.partial(_final_norm_kernel, tm=tm, st=st),
        grid=(m // tm,),
        in_specs=[row, row, pl.BlockSpec((MOD_ROWS, d), lambda i: (0, 0)), pl.BlockSpec((1, d), lambda i: (0, 0))],
        out_specs=row,
        out_shape=jax.ShapeDtypeStruct((m, d), _f32),
        compiler_params=_cparams(1), name="final_norm",
    )(x, f, gate, g.reshape(1, d))


def _mm_kernel(a_ref, w_ref, o_ref):
    o_ref[...] = _dot(a_ref[...], w_ref[...])


def _mm_res_norm_kernel(*refs, tm, st, has_router):
    a_ref, w_ref, res_ref, gate_ref, g_ref, sc_ref, sh_ref = refs[:7]
    r_hi, r_lo = refs[7:9] if has_router else (None, None)
    outs = refs[9:] if has_router else refs[7:]
    row = _mod_row_of_chunk(pl.program_id(0) * (tm // CHUNK), st)
    x = res_ref[...] + gate_ref[pl.ds(row, 1), :] * _dot(a_ref[...], w_ref[...])
    outs[0][...] = x
    inv = lax.rsqrt(jnp.mean(x * x, axis=-1, keepdims=True) + EPS)
    h = x * inv * g_ref[...] * (1.0 + sc_ref[pl.ds(row, 1), :]) + sh_ref[pl.ds(row, 1), :]
    outs[1][...] = h.astype(outs[1].dtype)
    if has_router:
        outs[2][...] = _dot3(h, r_hi[...], r_lo[...])


def _matmul(a, w, tm=512, tn=512):
    m, k = a.shape
    n = w.shape[1]
    tm = min(tm, m)
    return pl.pallas_call(
        _mm_kernel,
        grid=(n // tn, m // tm),
        in_specs=[pl.BlockSpec((tm, k), lambda j, i: (i, 0)),
                  pl.BlockSpec((k, tn), lambda j, i: (0, j))],
        out_specs=pl.BlockSpec((tm, tn), lambda j, i: (i, j)),
        out_shape=jax.ShapeDtypeStruct((m, n), _f32),
        compiler_params=_cparams(2), name="matmul",
    )(a, w)


def _matmul_residual_norm(a, w, res, gate, g, sc, sh, st, router=None, h_dtype=_bf16, tm=256):
    m, k = a.shape
    n = w.shape[1]
    row = lambda width: pl.BlockSpec((tm, width), lambda i: (i, 0))
    tab = pl.BlockSpec((MOD_ROWS, n), lambda i: (0, 0))
    args = [a, w, res, gate, g.reshape(1, n), sc, sh]
    specs = [row(k), pl.BlockSpec((k, n), lambda i: (0, 0)), row(n), tab, pl.BlockSpec((1, n), lambda i: (0, 0)), tab, tab]
    outs = [jax.ShapeDtypeStruct((m, n), _f32), jax.ShapeDtypeStruct((m, n), h_dtype)]
    out_specs = [row(n), row(n)]
    if router is not None:
        args += list(router)
        specs += [pl.BlockSpec((n, LANES), lambda i: (0, 0))] * 2
        outs.append(jax.ShapeDtypeStruct((m, LANES), _f32))
        out_specs.append(row(LANES))
    return pl.pallas_call(
        functools.partial(_mm_res_norm_kernel, tm=tm, st=st, has_router=router is not None),
        grid=(m // tm,), in_specs=specs, out_specs=out_specs, out_shape=outs,
        compiler_params=_cparams(1), name="matmul_residual_norm",
    )(*args)


FFN_TM = 1024
FFN_TF = 256
FFN_ROW_STEP = 256


def _swiglu_accumulate(x_ref, w1_ref, w3_ref, w2_ref, o_ref, n_rows):
    x = x_ref[0:n_rows, :]
    a = _dot(x, w1_ref[...])
    b = _dot(x, w3_ref[...])
    o_ref[0:n_rows, :] += _dot(a * jax.nn.sigmoid(a) * b, w2_ref[...])


def _ffn_kernel(x_ref, w1_ref, w3_ref, w2_ref, o_ref, *, tm):
    @pl.when(pl.program_id(1) == 0)
    def _():
        o_ref[...] = jnp.zeros_like(o_ref)

    _swiglu_accumulate(x_ref, w1_ref, w3_ref, w2_ref, o_ref, tm)


def _row_copy(src_hbm, src_row, dst, dst_row, sem):
    return pltpu.make_async_copy(src_hbm.at[pl.ds(src_row, 1), :], dst.at[pl.ds(dst_row, 1), :], sem)


ROW_DMA_UNROLL = 8


def _for_each_row(n, body):
    groups = n // ROW_DMA_UNROLL

    def group(g, carry):
        r0 = pl.multiple_of(g * ROW_DMA_UNROLL, ROW_DMA_UNROLL)
        for u in range(ROW_DMA_UNROLL):
            body(r0 + u)
        return carry

    def single(r, carry):
        body(r)
        return carry

    lax.fori_loop(0, groups, group, 0)
    lax.fori_loop(groups * ROW_DMA_UNROLL, n, single, 0)


def _ffn_gather_kernel(te_ref, tr_ref, tb_ref, tok_ref, h_hbm, w1_ref, w3_ref, w2_ref, o_ref, xf_scr, xb_scr, sem,
                       *, tm, n_tiles):
    i, j = pl.program_id(0), pl.program_id(1)
    rows = tr_ref[i]

    def start_gather(tile):
        base = tb_ref[tile]
        _for_each_row(tr_ref[tile], lambda r: _row_copy(h_hbm, tok_ref[base + r], xf_scr, r, sem).start())

    def wait_gather(tile):
        @pl.when(tr_ref[tile] == tm)
        def _():
            pltpu.make_async_copy(h_hbm.at[pl.ds(0, tm), :], xf_scr, sem).wait()

        @pl.when(tr_ref[tile] < tm)
        def _():
            _for_each_row(tr_ref[tile], lambda r: _row_copy(h_hbm, 0, xf_scr, r, sem).wait())

    @pl.when(j == 0)
    def _():
        @pl.when(i == 0)
        def _():
            xf_scr[...] = jnp.zeros_like(xf_scr)
            start_gather(0)

        wait_gather(i)
        xb_scr[...] = xf_scr[...].astype(_bf16)
        o_ref[...] = jnp.zeros_like(o_ref)

        @pl.when(i + 1 < n_tiles)
        def _():
            start_gather(i + 1)

    for n_rows in range(FFN_ROW_STEP, tm + 1, FFN_ROW_STEP):
        @pl.when((rows > n_rows - FFN_ROW_STEP) & (rows <= n_rows))
        def _():
            _swiglu_accumulate(xb_scr, w1_ref, w3_ref, w2_ref, o_ref, n_rows)


def _gathered_ffn(h, pair_token, w1, w3, w2, layer, tile_expert, tile_rows, tile_base, tm=FFN_TM, tf=FFN_TF):
    d = h.shape[1]
    f = w1.shape[3]
    nj = f // tf
    n_tiles = tile_expert.shape[0]

    def jeff(i, j, tr):
        return jnp.where(tr[i] > 0, j, nj - 1)

    return pl.pallas_call(
        functools.partial(_ffn_gather_kernel, tm=tm, n_tiles=n_tiles),
        grid_spec=pltpu.PrefetchScalarGridSpec(
            num_scalar_prefetch=4,
            grid=(n_tiles, nj),
            in_specs=[pl.BlockSpec(memory_space=pl.ANY),
                      pl.BlockSpec((None, None, d, tf), lambda i, j, te, tr, *_: (layer, te[i], 0, jeff(i, j, tr))),
                      pl.BlockSpec((None, None, d, tf), lambda i, j, te, tr, *_: (layer, te[i], 0, jeff(i, j, tr))),
                      pl.BlockSpec((None, None, tf, d), lambda i, j, te, tr, *_: (layer, te[i], jeff(i, j, tr), 0))],
            out_specs=pl.BlockSpec((tm, d), lambda i, j, *_: (i, 0)),
            scratch_shapes=[pltpu.VMEM((tm, d), _f32), pltpu.VMEM((tm, d), _bf16), pltpu.SemaphoreType.DMA]),
        out_shape=jax.ShapeDtypeStruct((n_tiles * tm, d), _f32),
        compiler_params=pltpu.CompilerParams(dimension_semantics=("arbitrary", "arbitrary"),
                                             vmem_limit_bytes=FFN_VMEM_LIMIT),
        name="gathered_ffn",
    )(tile_expert, tile_rows, tile_base, pair_token, h, w1, w3, w2)


COMBINE_TM = 256


def _combine_kernel(slot_ref, yg_hbm, gates_ref, o_ref, buf, sem, *, tm):
    base = pl.program_id(0) * tm * TOP_K

    def start(t):
        for k in range(TOP_K):
            _row_copy(yg_hbm, slot_ref[base + t * TOP_K + k], buf.at[k], t, sem).start()

    _for_each_row(tm, start)
    for k in range(TOP_K):
        pltpu.make_async_copy(yg_hbm.at[pl.ds(0, tm), :], buf.at[k], sem).wait()
    g = gates_ref[...]
    o_ref[...] = g[:, 0:1] * buf[0] + g[:, 1:2] * buf[1]


def _moe_combine(yg, pair_slot, gates, tm=COMBINE_TM):
    n_tok = gates.shape[0]
    d = yg.shape[1]
    return pl.pallas_call(
        functools.partial(_combine_kernel, tm=tm),
        grid_spec=pltpu.PrefetchScalarGridSpec(
            num_scalar_prefetch=1,
            grid=(n_tok // tm,),
            in_specs=[pl.BlockSpec(memory_space=pl.ANY),
                      pl.BlockSpec((tm, TOP_K), lambda b, sl: (b, 0))],
            out_specs=pl.BlockSpec((tm, d), lambda b, sl: (b, 0)),
            scratch_shapes=[pltpu.VMEM((TOP_K, tm, d), _f32), pltpu.SemaphoreType.DMA]),
        out_shape=jax.ShapeDtypeStruct((n_tok, d), _f32),
        compiler_params=_cparams(1), name="moe_combine",
    )(pair_slot, yg, gates)


def _dense_ffn(x, w1, w3, w2, layer, tm=FFN_TM, tf=FFN_TF):
    m, d = x.shape
    f = w1.shape[2]
    return pl.pallas_call(
        functools.partial(_ffn_kernel, tm=tm),
        grid=(m // tm, f // tf),
        in_specs=[pl.BlockSpec((tm, d), lambda i, j: (i, 0)),
                  pl.BlockSpec((None, d, tf), lambda i, j: (layer, 0, j)),
                  pl.BlockSpec((None, d, tf), lambda i, j: (layer, 0, j)),
                  pl.BlockSpec((None, tf, d), lambda i, j: (layer, j, 0))],
        out_specs=pl.BlockSpec((tm, d), lambda i, j: (i, 0)),
        out_shape=jax.ShapeDtypeStruct((m, d), _f32),
        compiler_params=pltpu.CompilerParams(dimension_semantics=("arbitrary", "arbitrary"),
                                             vmem_limit_bytes=FFN_VMEM_LIMIT),
        name="dense_ffn",
    )(x, w1, w3, w2)


PREP_CHUNKS = 4
PREP_ROWS = PREP_CHUNKS * CHUNK


def _token_shift_block(x, prev, nxt, lane0, is_lat, first, last):
    rows, n = x.shape
    row = lax.broadcasted_iota(jnp.int32, x.shape, 0)
    ch = lax.broadcasted_iota(jnp.int32, x.shape, 1) + lane0
    down = pltpu.roll(x, 1, 0)
    up = pltpu.roll(x, rows - 1, 0)

    def context():
        zero_row = jnp.zeros((1, n), _f32)
        row_prev = jnp.where(first, zero_row, prev[CHUNK - 1:CHUNK, :])
        row_next = jnp.where(last, zero_row, nxt[0:1, :])
        return jnp.where(ch < A_IN // 2,
                         jnp.where(row == 0, row_prev, down),
                         jnp.where(row == rows - 1, row_next, up))

    def latent():
        q = A_IN // 4
        col = row % GRID_W
        above = jnp.concatenate([jnp.where(first, 0.0, prev), x[:rows - CHUNK]], axis=0)
        below = jnp.concatenate([x[CHUNK:], jnp.where(last, 0.0, nxt)], axis=0)
        return jnp.where(ch < q, jnp.where(col == 0, 0.0, down),
                         jnp.where(ch < 2 * q, jnp.where(col == GRID_W - 1, 0.0, up),
                                   jnp.where(ch < 3 * q, above, below)))

    return lax.cond(is_lat, latent, context)


def _rwkv_prep_kernel(xm_ref, pm_ref, nm_ref, xl_ref, pl_ref, nl_ref, mum_ref, mul_ref,
                      ww_ref, wa_ref, wg_ref, par_ref, seg_ref, segt_ref,
                      rt_ref, kt_ref, at_ref, bt_ref, kh_ref, bh_ref, v_ref, plast_ref, bonus_ref, g_ref,
                      *, st):
    s = pl.program_id(0) * PREP_CHUNKS
    is_lat, nc, c, _ = _chunk_pos(s, st)
    first, last = c == 0, c + PREP_CHUNKS == nc
    xm, xl = xm_ref[...], xl_ref[...]
    sm = _token_shift_block(xm, pm_ref[...], nm_ref[...], 0, is_lat, first, last)
    sl = _token_shift_block(xl, pl_ref[...], nl_ref[...], EV_MAIN, is_lat, first, last)
    pm = xm + mum_ref[...] * (sm - xm)
    lora = xl + mul_ref[...] * (sl - xl)
    r, k, v = pm[:, :D_A], pm[:, D_A:2 * D_A], pm[:, 2 * D_A:]
    par = par_ref[...]
    k_k, k_a, r_k = par[4:5], par[5:6], par[6:7]
    seg, seg_t = seg_ref[...], segt_ref[...]
    kk = k * k_k
    kk = kk / jnp.maximum(jnp.sqrt(_head_sum(kk * kk, seg, seg_t)), 1e-12)
    g_ref[...] = _dot(jax.nn.sigmoid(lora[:, 2 * LORA_W + 2 * LORA_A:GLA_LORA_OFF]), wg_ref[...])
    v_ref[...] = v.astype(_bf16)
    tanh_w = jnp.tanh(lora[:, :2 * LORA_W])
    lora_a = lora[:, 2 * LORA_W:2 * LORA_W + 2 * LORA_A]
    kdir_sum = jnp.zeros_like(k)
    t = lax.broadcasted_iota(jnp.int32, (PREP_ROWS, PREP_ROWS), 0)
    i = lax.broadcasted_iota(jnp.int32, (PREP_ROWS, PREP_ROWS), 1)
    same_chunk = jnp.where((t // CHUNK) == (i // CHUNK), 1.0, 0.0)
    for d in range(2):
        w = -jax.nn.softplus(-(par[d:d + 1] + _dot3(tanh_w, ww_ref[d, 0], ww_ref[d, 1]))) - 0.5
        ld = -jnp.exp(w)
        a = jax.nn.sigmoid(par[2 + d:3 + d] + _dot3(lora_a, wa_ref[d, 0], wa_ref[d, 1]))
        kdir = k * (1.0 + (a - 1.0) * k_a)
        kdir_sum = kdir_sum + kdir
        incl, _ = _tri_masks(d, PREP_ROWS, CHUNK)
        cum = _dot_exact_lhs(jnp.where(incl, 1.0, 0.0), ld)
        tot = _dot_exact_lhs(same_chunk, ld)
        e_pos, e_neg, e_end = jnp.exp(cum), jnp.exp(-cum), jnp.exp(tot - cum)
        kka = kk * a
        rt_ref[d] = (r * e_pos).astype(_bf16)
        kt_ref[d] = (kdir * e_neg).astype(_bf16)
        at_ref[d] = (kk * jnp.exp(cum - ld)).astype(_bf16)
        bt_ref[d] = (kka * e_neg).astype(_bf16)
        kh_ref[d] = (kdir * e_end).astype(_bf16)
        bh_ref[d] = (kka * e_end).astype(_bf16)
        e_tot = jnp.exp(tot)
        for j in range(PREP_CHUNKS):
            plast_ref[d, j] = e_tot[j * CHUNK:j * CHUNK + 1, :]
    bonus_ref[...] = _head_sum(r * kdir_sum * r_k, seg, seg_t) * v


def _rwkv_prep(p, st, mu_main, mu_lora, ww, wa, wg, par, seg, seg_t):
    n = p.shape[0]
    nch = st.n_chunks
    assert st.ctx_cps % PREP_CHUNKS == 0 and st.lat_cps % PREP_CHUNKS == 0
    cur = lambda b: (b, 0)
    prv = lambda b: (jnp.maximum(b * PREP_CHUNKS - 1, 0), 0)
    nxt = lambda b: (jnp.minimum((b + 1) * PREP_CHUNKS, nch - 1), 0)
    lo = lambda f: (lambda b: (f(b)[0], EV_LORA_BLK))
    main = lambda rows, f: pl.BlockSpec((rows, EV_MAIN), f)
    lora = lambda rows, f: pl.BlockSpec((rows, EV_LORA_W), lo(f))
    full = lambda a: pl.BlockSpec(a.shape, lambda b: (0,) * a.ndim)
    seq2 = pl.BlockSpec((2, PREP_ROWS, D_A), lambda b: (0, b, 0))
    seq1 = pl.BlockSpec((PREP_ROWS, D_A), lambda b: (b, 0))
    bf2 = jax.ShapeDtypeStruct((2, n, D_A), _bf16)
    return pl.pallas_call(
        functools.partial(_rwkv_prep_kernel, st=st),
        grid=(nch // PREP_CHUNKS,),
        in_specs=[main(PREP_ROWS, cur), main(CHUNK, prv), main(CHUNK, nxt),
                  lora(PREP_ROWS, cur), lora(CHUNK, prv), lora(CHUNK, nxt),
                  full(mu_main), full(mu_lora), full(ww), full(wa), full(wg), full(par), full(seg), full(seg_t)],
        out_specs=[seq2] * 6 + [seq1, pl.BlockSpec((2, PREP_CHUNKS, 1, D_A), lambda b: (0, b, 0, 0)), seq1, seq1],
        out_shape=[bf2] * 6 + [jax.ShapeDtypeStruct((n, D_A), _bf16),
                               jax.ShapeDtypeStruct((2, nch, 1, D_A), _f32),
                               jax.ShapeDtypeStruct((n, D_A), _f32),
                               jax.ShapeDtypeStruct((n, D_A), _f32)],
        compiler_params=_cparams(1), name="rwkv_prep",
    )(p, p, p, p, p, p, mu_main, mu_lora, ww, wa, wg, par, seg, seg_t)


N_PAIRS = H_A // 2
PAIR = 2 * CHUNK


def _stack_masked(x, keep_first):
    return jnp.concatenate([x * keep_first, x * (1 - keep_first)], axis=0)


def _rwkv_scan_kernel(*refs, st):
    (rt_f, kt_f, at_f, bt_f, kh_f, bh_f, v_f, pl_f,
     rt_b, kt_b, at_b, bt_b, kh_b, bh_b, v_b, pl_b, s0_ref, _acc, yf_ref, yb_ref, sout_ref, s_scr) = refs
    s = pl.program_id(0)
    is_lat, nc, c, _ = _chunk_pos(s, st)

    @pl.when(c == 0)
    def _():
        keep = jnp.where(is_lat, 1.0, 0.0)
        zero = jnp.zeros((HD_A, HD_A), _f32)
        for d in range(2):
            for p in range(N_PAIRS):
                top = jnp.concatenate([s0_ref[d, 2 * p], zero], axis=1)
                bot = jnp.concatenate([zero, s0_ref[d, 2 * p + 1]], axis=1)
                s_scr[d, p] = jnp.concatenate([top, bot], axis=0) * keep

    dirs = ((rt_f, kt_f, at_f, bt_f, kh_f, bh_f, v_f, pl_f, yf_ref),
            (rt_b, kt_b, at_b, bt_b, kh_b, bh_b, v_b, pl_b, yb_ref))
    chains = [(d, p) for d in range(2) for p in range(N_PAIRS)]
    masks = [_tri_masks(d, PAIR, CHUNK) for d in range(2)]
    eye = jnp.where(masks[0][0] & masks[1][0], 1.0, 0.0)

    lane = lax.broadcasted_iota(jnp.int32, (CHUNK, LANES), 1)
    keep_first = jnp.where(lane < HD_A, 1.0, 0.0).astype(_bf16)

    def load(ref, p):
        return _stack_masked(ref[:, p * LANES:(p + 1) * LANES], keep_first)

    ops = {}
    for d, p in chains:
        rt, kt, at, bt, kh, bh, v = (load(dirs[d][j], p) for j in range(7))
        incl, strict = masks[d]
        big = _dot_nt(jnp.concatenate([at, rt], axis=0), jnp.concatenate([kt, bt], axis=0))
        n = jnp.where(strict, big[:PAIR, PAIR:], 0.0)
        ops[d, p] = dict(rt=rt, at=at, bh=bh, kh=kh, v=v, t=eye - n, n=n.astype(_bf16),
                         mak=jnp.where(strict, big[:PAIR, :PAIR], 0.0).astype(_bf16),
                         mrk=jnp.where(incl, big[PAIR:, :PAIR], 0.0).astype(_bf16),
                         mrb=jnp.where(incl, big[PAIR:, PAIR:], 0.0).astype(_bf16))
    for _ in range(5):
        for ch in chains:
            ops[ch]["n"] = _dot(ops[ch]["n"], ops[ch]["n"]).astype(_bf16)
        for ch in chains:
            ops[ch]["t"] = ops[ch]["t"] + _dot(ops[ch]["t"], ops[ch]["n"])
    for ch in chains:
        o = ops[ch]
        o["mv"] = _dot(jnp.concatenate([o.pop("mak"), o.pop("mrk")], axis=0), o["v"])
    for ch in chains:
        o = ops[ch]
        o["xz"] = _dot(o.pop("t"), jnp.concatenate([o["at"], o["mv"][:PAIR].astype(_bf16)], axis=1))
    for ch in chains:
        o = ops[ch]
        o["qy"] = _dot(o.pop("mrb"), o["xz"])
    for ch in chains:
        o = ops[ch]
        o["upd"] = _dot_tn(o["xz"], o["bh"])
        o["vtk"] = _dot_tn(o["v"], o["kh"])
    for d, p in chains:
        o = ops[d, p]
        s0 = s_scr[d, p]
        q_eff = o["rt"].astype(_f32) - o["qy"][:, :LANES]
        y = _dot_nt(q_eff, s0) + o["mv"][PAIR:] - o["qy"][:, LANES:]
        dirs[d][8][:, p * LANES:(p + 1) * LANES] = y[:CHUNK] + y[CHUNK:]
        p_last = dirs[d][7][:, p * LANES:(p + 1) * LANES]
        s_scr[d, p] = s0 * p_last - _dot(s0, o["upd"][:LANES]) + o["vtk"] - o["upd"][LANES:]

    @pl.when((c == nc - 1) & jnp.logical_not(is_lat))
    def _():
        for d in range(2):
            for p in range(N_PAIRS):
                sp = s_scr[d, p]
                sout_ref[d, 2 * p] = sp[:HD_A, :HD_A]
                sout_ref[d, 2 * p + 1] = sp[HD_A:, HD_A:]


def _lat_state_spec(shape, st):
    nd = len(shape)
    return pl.BlockSpec((None,) + tuple(shape[1:]),
                        lambda s: (jnp.maximum(_seq_of(s, st) - st.n_ctx_seq, 0),) + (0,) * (nd - 1))


def _ctx_state_spec(shape, st, layer):
    nd = len(shape)
    return pl.BlockSpec((None, None) + tuple(shape[2:]),
                        lambda s: (jnp.minimum(_seq_of(s, st), st.n_ctx_seq - 1), layer) + (0,) * (nd - 2))


_ANY = pl.BlockSpec(memory_space=pl.ANY)


def _rwkv_scan(prep, s0, st, acc, layer):
    rt, kt, at, bt, kh, bh, v, p_last = prep
    n = v.shape[0]
    fwd = lambda s: s
    bwd = lambda s: _bwd_chunk(s, st)
    seq2 = lambda d, f: pl.BlockSpec((None, CHUNK, D_A), lambda s: (d, f(s), 0))
    seq1 = lambda f: pl.BlockSpec((CHUNK, D_A), lambda s: (f(s), 0))
    pls = lambda d, f: pl.BlockSpec((None, None, 1, D_A), lambda s: (d, f(s), 0, 0))
    side = lambda d, f: [seq2(d, f)] * 6 + [seq1(f), pls(d, f)]
    y_shape = jax.ShapeDtypeStruct((n, D_A), _f32)
    args = (rt, kt, at, bt, kh, bh, v, p_last)
    return pl.pallas_call(
        functools.partial(_rwkv_scan_kernel, st=st),
        grid=(st.n_chunks,),
        in_specs=side(0, fwd) + side(1, bwd) + [_lat_state_spec(s0.shape, st), _ANY],
        out_specs=[seq1(fwd), seq1(bwd), _ctx_state_spec(acc.shape, st, layer)],
        out_shape=[y_shape, y_shape, jax.ShapeDtypeStruct(acc.shape, _f32)],
        input_output_aliases={2 * len(args) + 1: 2},
        scratch_shapes=[pltpu.VMEM((2, N_PAIRS, LANES, LANES), _f32)],
        compiler_params=_cparams(1), name="rwkv_scan",
    )(*args, *args, s0, acc)


def _gla_kernel(q_f, k_f, v_f, l_f, q_b, k_b, v_b, l_b, w_ref, bias_ref, s0_ref, _acc,
                of_ref, ob_ref, sout_ref, s_scr, *, st):
    s = pl.program_id(0)
    is_lat, nc, c, _ = _chunk_pos(s, st)

    @pl.when(c == 0)
    def _():
        keep = jnp.where(is_lat, 1.0, 0.0)
        for d in range(2):
            for h in range(H_B):
                s_scr[d, h] = s0_ref[d, h].T * keep

    dirs = ((q_f, k_f, v_f, l_f, of_ref), (q_b, k_b, v_b, l_b, ob_ref))
    work = []
    for d, (q_ref, k_ref, v_ref, l_ref, o_ref) in enumerate(dirs):
        incl, _ = _tri_masks(d)
        la = jax.nn.log_sigmoid(_dot3(l_ref[...], w_ref[d, 0], w_ref[d, 1]) + bias_ref[d:d + 1]) / GATE_NORM
        cum = _dot_exact_lhs(jnp.where(incl, 1.0, 0.0), la)
        tot = jnp.sum(la, axis=0, keepdims=True)
        k = k_ref[...]
        q_in = q_ref[...] * (DK_B ** -0.5) * jnp.exp(cum)
        k_in = k * jnp.exp(-cum)
        k_end = k * jnp.exp(tot - cum)
        e_tot = jnp.exp(tot)
        for h in range(H_B):
            ks = slice(h * DK_B, (h + 1) * DK_B)
            work.append((d, h, incl, q_in[:, ks], k_in[:, ks], k_end[:, ks], e_tot[:, ks]))
    att = [jnp.where(incl, _dot_nt(q_in, k_in), 0.0) for (_, _, incl, q_in, k_in, _, _) in work]
    for (d, h, _, q_in, _, k_end, e_tot), a in zip(work, att):
        v_ref, o_ref = dirs[d][2], dirs[d][4]
        vs = slice(h * DV_B, (h + 1) * DV_B)
        v = v_ref[:, vs]
        s_t = s_scr[d, h]
        o_ref[:, vs] = _dot(a, v) + _dot_nt(q_in, s_t)
        s_scr[d, h] = s_t * e_tot + _dot_tn(v, k_end)

    @pl.when((c == nc - 1) & jnp.logical_not(is_lat))
    def _():
        for d in range(2):
            for h in range(H_B):
                sout_ref[d, h] = s_scr[d, h].T


def _gla_scan(p, w, bias, s0, st, acc, layer):
    n = p.shape[0]
    fwd = lambda s: s
    bwd = lambda s: _bwd_chunk(s, st)
    blk = lambda width, col, f: pl.BlockSpec((CHUNK, width), lambda s: (f(s), col))
    side = lambda f: [blk(QK_B, EV_Q_BLK, f), blk(QK_B, EV_K_BLK, f), blk(V_B, EV_V_BLK, f),
                      blk(EV_LORA_W, EV_LORA_BLK, f)]
    full = lambda a: pl.BlockSpec(a.shape, lambda s: (0,) * a.ndim)
    o_spec = lambda f: pl.BlockSpec((CHUNK, V_B), lambda s: (f(s), 0))
    o_shape = jax.ShapeDtypeStruct((n, V_B), _f32)
    return pl.pallas_call(
        functools.partial(_gla_kernel, st=st),
        grid=(st.n_chunks,),
        in_specs=side(fwd) + side(bwd) + [full(w), full(bias), _lat_state_spec(s0.shape, st), _ANY],
        out_specs=[o_spec(fwd), o_spec(bwd), _ctx_state_spec(acc.shape, st, layer)],
        out_shape=[o_shape, o_shape, jax.ShapeDtypeStruct(acc.shape, _f32)],
        input_output_aliases={11: 2},
        scratch_shapes=[pltpu.VMEM((2, H_B, DV_B, DK_B), _f32)],
        compiler_params=_cparams(1), name="gla_scan",
    )(p, p, p, p, p, p, p, p, w, bias, s0, acc)


def _mlstm_kernel(q_f, k_f, v_f, g_f, q_b, k_b, v_b, g_b, bias_ref, c0_ref, n0_ref, m0_ref, _acc_c, _acc_n, _acc_m,
                  hf_ref, hb_ref, cout_ref, nout_ref, mout_ref, c_scr, n_scr, m_scr, *, st):
    s = pl.program_id(0)
    is_lat, nc, c, _ = _chunk_pos(s, st)

    @pl.when(c == 0)
    def _():
        keep = jnp.where(is_lat, 1.0, 0.0)
        for d in range(2):
            for h in range(H_C):
                c_scr[d, h] = c0_ref[d, h].T * keep
        n_scr[...] = n0_ref[...] * keep
        m_scr[...] = m0_ref[...] * keep

    L = CHUNK
    lane = lax.broadcasted_iota(jnp.int32, (L, LANES), 1)
    dirs = ((q_f, k_f, v_f, g_f, hf_ref), (q_b, k_b, v_b, g_b, hb_ref))
    work = []
    for d, (q_ref, k_ref, v_ref, g_ref, h_ref) in enumerate(dirs):
        gates = g_ref[...] + bias_ref[0:1]
        gates = jnp.where(lane < 2 * H_C, gates, jax.nn.log_sigmoid(gates))
        gates_t = gates.T
        for h in range(H_C):
            ji, jf = d * H_C + h, 2 * H_C + d * H_C + h
            ks = slice(h * DK_C, (h + 1) * DK_C)
            q = q_ref[:, ks] * (DK_C ** -0.5)
            k = k_ref[:, ks]
            work.append(dict(d=d, h=h, q=q, k=k, incl=_tri_masks(d)[0], incl_t=_tri_masks(1 - d)[0],
                             i_col=gates[:, ji:ji + 1], f_col=gates[:, jf:jf + 1],
                             i_row=gates_t[ji:ji + 1, :], f_row=gates_t[jf:jf + 1, :],
                             m_prev=m_scr[d, h], c_t=c_scr[d, h], n_row=n_scr[d, h]))
    for o in work:
        o["s"] = _dot_nt(o["q"], o["k"])
        o["qc"] = _dot_nt(o["q"], o["c_t"])
    for o in work:
        o["cum_col"] = jnp.sum(jnp.where(o["incl"], o["f_row"], 0.0), axis=1, keepdims=True)
        o["cum_row"] = jnp.sum(jnp.where(o["incl_t"], o["f_col"], 0.0), axis=0, keepdims=True)
        o["f_tot"] = jnp.sum(o["f_row"], axis=1, keepdims=True)
        o["qn"] = jnp.sum(o["q"] * o["n_row"], axis=1, keepdims=True)
    for o in work:
        o["log_d"] = jnp.where(o["incl"], o["cum_col"] - o["cum_row"] + o["i_row"], NEG_BIG)
        o["log_inter"] = o["cum_col"] + o["m_prev"]
        o["row_max"] = jnp.max(o["log_d"], axis=1, keepdims=True)
    for o in work:
        m_t = jnp.maximum(o["log_inter"], o["row_max"])
        o["m_t"] = m_t
        o["sm"] = o["s"] * jnp.exp(o.pop("log_d") - m_t)
        o["w_inter"] = jnp.exp(o["log_inter"] - m_t)
        o["m_new"] = m_t[L - 1:L, :] if o["d"] == 0 else m_t[0:1, :]
        o["kw"] = o["k"] * jnp.exp(o["f_tot"] - o["cum_col"] + o["i_col"] - o["m_new"])
        o["dec"] = jnp.exp(o["f_tot"] + o["m_prev"] - o["m_new"])
    for o in work:
        vs = slice(o["h"] * DV_C, (o["h"] + 1) * DV_C)
        o["v"] = dirs[o["d"]][2][:, vs]
        o["sv"] = _dot(o["sm"], o["v"])
        o["kv"] = _dot_tn(o["v"], o["kw"])
        o["s_sum"] = jnp.sum(o["sm"], axis=1, keepdims=True)
    for o in work:
        d, h = o["d"], o["h"]
        vs = slice(h * DV_C, (h + 1) * DV_C)
        num = o["w_inter"] * o["qc"] + o["sv"]
        den = o["w_inter"] * o["qn"] + o["s_sum"]
        dirs[d][4][:, vs] = num / jnp.maximum(jnp.abs(den), jnp.exp(-o["m_t"]))
        c_scr[d, h] = o["dec"] * o["c_t"] + o["kv"]
        n_scr[d, h] = o["dec"] * o["n_row"] + jnp.sum(o["kw"], axis=0, keepdims=True)
        m_scr[d, h] = o["m_new"]

    @pl.when((c == nc - 1) & jnp.logical_not(is_lat))
    def _():
        for d in range(2):
            for h in range(H_C):
                cout_ref[d, h] = c_scr[d, h].T
        nout_ref[...] = n_scr[...]
        mout_ref[...] = m_scr[...]


def _mlstm_scan(p, bias, c0, n0, m0, st, accs, layer):
    n = p.shape[0]
    fwd = lambda s: s
    bwd = lambda s: _bwd_chunk(s, st)
    blk = lambda width, col, f: pl.BlockSpec((CHUNK, width), lambda s: (f(s), col))
    side = lambda f: [blk(QK_C, 0, f), blk(QK_C, 1, f), blk(V_C, 1, f), blk(LANES, OD_GATE_BLK, f)]
    o_spec = lambda f: pl.BlockSpec((CHUNK, V_C), lambda s: (f(s), 0))
    o_shape = jax.ShapeDtypeStruct((n, V_C), _f32)
    states = (c0, n0, m0)
    return pl.pallas_call(
        functools.partial(_mlstm_kernel, st=st),
        grid=(st.n_chunks,),
        in_specs=side(fwd) + side(bwd) + [pl.BlockSpec(bias.shape, lambda s: (0, 0))]
        + [_lat_state_spec(a.shape, st) for a in states] + [_ANY] * 3,
        out_specs=[o_spec(fwd), o_spec(bwd)] + [_ctx_state_spec(a.shape, st, layer) for a in accs],
        out_shape=[o_shape, o_shape] + [jax.ShapeDtypeStruct(a.shape, _f32) for a in accs],
        input_output_aliases={12: 2, 13: 3, 14: 4},
        scratch_shapes=[pltpu.VMEM((2, H_C, DV_C, DK_C), _f32), pltpu.VMEM((2, H_C, 1, DK_C), _f32),
                        pltpu.VMEM((2, H_C, 1, 1), _f32)],
        compiler_params=_cparams(1), name="mlstm_scan",
    )(p, p, p, p, p, p, p, p, bias, c0, n0, m0, *accs)


def _head_rms_gate(o, g_row, gate, n_heads, width):
    outs = []
    for h in range(n_heads):
        hs = slice(h * width, (h + 1) * width)
        oh = o[:, hs]
        outs.append(oh * lax.rsqrt(jnp.mean(oh * oh, axis=-1, keepdims=True) + EPS) * g_row[:, hs] * gate[:, hs])
    return jnp.concatenate(outs, axis=1)


def _even_post_kernel(yf_ref, yb_ref, bonus_ref, g_ref, of_ref, ob_ref, og_ref, par_ref, gn_ref, seg_ref, segt_ref,
                      o_ref):
    y = yf_ref[...] + yb_ref[...]
    seg, seg_t = seg_ref[...], segt_ref[...]
    mean = _head_sum(y, seg, seg_t) * (1.0 / HD_A)
    yc = y - mean
    var = _head_sum(yc * yc, seg, seg_t) * (1.0 / HD_A)
    par = par_ref[...]
    ya = (yc * lax.rsqrt(var + EPS) * par[0:1] + par[1:2] + bonus_ref[...]) * g_ref[...]
    og = og_ref[...]
    yb = _head_rms_gate(of_ref[...] + ob_ref[...], gn_ref[...], og * jax.nn.sigmoid(og), H_B, DV_B)
    o_ref[:, :D_A] = ya.astype(_bf16)
    o_ref[:, D_A:] = yb.astype(_bf16)


def _even_post(yf, yb, bonus, g, of, ob, p, ln_par, gla_g, seg, seg_t, tm=256):
    n = yf.shape[0]
    row = lambda w, col=0: pl.BlockSpec((tm, w), lambda i: (i, col))
    full = lambda a: pl.BlockSpec(a.shape, lambda i: (0,) * a.ndim)
    return pl.pallas_call(
        _even_post_kernel,
        grid=(n // tm,),
        in_specs=[row(D_A)] * 4 + [row(V_B), row(V_B), row(V_B, EV_OG_BLK), full(ln_par), full(gla_g),
                  full(seg), full(seg_t)],
        out_specs=row(D_A + V_B),
        out_shape=jax.ShapeDtypeStruct((n, D_A + V_B), _bf16),
        compiler_params=_cparams(1), name="even_post",
    )(yf, yb, bonus, g, of, ob, p, ln_par, gla_g, seg, seg_t)


def _odd_post_kernel(hf_ref, hb_ref, o_ref_in, gn_ref, out_ref):
    out_ref[...] = _head_rms_gate(hf_ref[...] + hb_ref[...], gn_ref[...], jax.nn.sigmoid(o_ref_in[...]),
                                  H_C, DV_C).astype(_bf16)


def _odd_post(hf, hb, p, gn, tm=256):
    n = hf.shape[0]
    row = lambda col=0: pl.BlockSpec((tm, V_C), lambda i: (i, col))
    return pl.pallas_call(
        _odd_post_kernel,
        grid=(n // tm,),
        in_specs=[row(), row(), row(2), pl.BlockSpec(gn.shape, lambda i: (0, 0))],
        out_specs=row(),
        out_shape=jax.ShapeDtypeStruct((n, V_C), _bf16),
        compiler_params=_cparams(1), name="odd_post",
    )(hf, hb, p, gn)


def _even_mixer(p, st, layer, acc_rw, acc_gl, s_rwkv, s_gla, mu, w0, w2, a0, a2, g2, k_k, k_a, r_k, ln_g, ln_b,
                gla_w2, gla_b, gla_g):
    zpad = lambda x, rows_before, total: jnp.pad(x, ((rows_before, total - rows_before - x.shape[0]), (0, 0)))
    mu_main = mu[:EV_MAIN].reshape(1, EV_MAIN)
    mu_lora = jnp.pad(mu[EV_MAIN:], (0, EV_LORA_W - (A_IN - EV_MAIN))).reshape(1, EV_LORA_W)
    split = lambda w: jnp.stack(_split2(w), axis=0)
    ww = jnp.stack([split(zpad(w2[d], d * LORA_W, 2 * LORA_W)) for d in range(2)])
    wa = jnp.stack([split(zpad(a2[d], d * LORA_A, 2 * LORA_A)) for d in range(2)])
    wg = g2.astype(_bf16)
    par = jnp.stack([w0[0], w0[1], a0[0], a0[1], k_k, k_a, r_k, jnp.zeros_like(k_k)])
    head = np.arange(D_A) // HD_A
    seg_np = head[:, None] == np.arange(LANES)[None, :]
    seg, seg_t = jnp.asarray(seg_np, _bf16), jnp.asarray(seg_np.T, _bf16)
    *prep, bonus, g = _rwkv_prep(p, st, mu_main, mu_lora, ww, wa, wg, par, seg, seg_t)
    yf, yb, acc_rw = _rwkv_scan(prep, s_rwkv, st, acc_rw, layer)
    wl = jnp.stack([split(zpad(gla_w2[d], GLA_LORA_OFF + d * LORA_GK, EV_LORA_W)) for d in range(2)])
    bias = jnp.pad(gla_b, ((0, MOD_ROWS - 2), (0, 0)))
    of, ob, acc_gl = _gla_scan(p, wl, bias, s_gla, st, acc_gl, layer)
    ln_par = jnp.pad(jnp.stack([ln_g, ln_b]), ((0, MOD_ROWS - 2), (0, 0)))
    y_in = _even_post(yf, yb, bonus, g, of, ob, p, ln_par, gla_g.reshape(1, V_B), seg, seg_t)
    return y_in, acc_rw, acc_gl


def _odd_mixer(p, st, layer, accs, s_c, s_n, s_m, b_i, b_f, g):
    bias = jnp.concatenate([b_i.reshape(-1), b_f.reshape(-1), jnp.zeros((LANES - 4 * H_C,), _f32)])
    bias = jnp.pad(bias[None, :], ((0, MOD_ROWS - 1), (0, 0)))
    hf, hb, *accs = _mlstm_scan(p, bias, s_c, s_n[..., None, :], s_m[..., None, None], st, accs, layer)
    return _odd_post(hf, hb, p, g.reshape(1, V_C)), tuple(accs)


def _moe(h, logits, w1, w3, w2, layer, tm=FFN_TM):
    n_tok = h.shape[0]
    n_pairs = n_tok * TOP_K
    n_tiles = n_pairs // tm + N_EXPERTS
    top_v, top_i = lax.top_k(logits[:, :N_EXPERTS], TOP_K)
    gates = jax.nn.softmax(top_v, axis=-1)
    pair_e = top_i.reshape(-1)
    order = jnp.argsort(pair_e, stable=True)
    one_hot = jax.nn.one_hot(pair_e, N_EXPERTS, dtype=jnp.int32)
    counts = jnp.sum(one_hot, axis=0)
    tiles_per_e = (counts + tm - 1) // tm
    tile_start = jnp.cumsum(tiles_per_e) - tiles_per_e
    group_start = jnp.cumsum(counts) - counts
    rank = jnp.sum((jnp.cumsum(one_hot, axis=0) - one_hot) * one_hot, axis=1)
    pair_slot = (jnp.sum(one_hot * tile_start[None, :], axis=1) * tm + rank).astype(jnp.int32)
    n_valid = jnp.sum(tiles_per_e)
    tile_ids = jnp.arange(n_tiles, dtype=jnp.int32)
    tile_e = jnp.sum((tile_ids[:, None] >= tile_start[None, :]).astype(jnp.int32), axis=1) - 1
    last_e = jnp.max(jnp.where(counts > 0, jnp.arange(N_EXPERTS), 0))
    tile_valid = tile_ids < n_valid
    tile_e = jnp.where(tile_valid, tile_e, last_e).astype(jnp.int32)
    tile_off = (tile_ids - tile_start[tile_e]) * tm
    tile_rows = jnp.where(tile_valid, jnp.clip(counts[tile_e] - tile_off, 0, tm), 0).astype(jnp.int32)
    tile_base = jnp.where(tile_valid, group_start[tile_e] + tile_off, 0).astype(jnp.int32)
    pair_token = (order // TOP_K).astype(jnp.int32)
    yg = _gathered_ffn(h, pair_token, w1, w3, w2, layer, tile_e, tile_rows, tile_base, tm=tm)
    return _moe_combine(yg, pair_slot, gates)


def kernel(x_prompt, x_sample, state_rwkv, state_gla, state_mlstm_c, state_mlstm_n, state_mlstm_m, c, c_ctx, mod_w, mod_b, norm_g, final_g, ev_w_in, ev_mu, rw_w0, rw_w2, rw_a0, rw_a2, rw_g2, rw_kk, rw_ka, rw_rk, rw_ln_g, rw_ln_b, gla_w2, gla_b, gla_norm_g, ev_w_out, ffn_w1, ffn_w3, ffn_w2, od_w_in, ml_b_i, ml_b_f, ml_norm_g, od_w_out, moe_router, moe_w1, moe_w3, moe_w2):
    D = D_MODEL
    Bp, Tp, _ = x_prompt.shape
    Bs, Ts, _ = x_sample.shape
    assert Ts // GRID_W * GRID_W == Ts and GRID_W == CHUNK and Tp % CHUNK == 0
    n_ctx = Bp * Tp
    st = _Streams(n_ctx_chunks=n_ctx // CHUNK, ctx_cps=Tp // CHUNK, lat_cps=Ts // CHUNK,
                  n_chunks=(n_ctx + Bs * Ts) // CHUNK, n_ctx_seq=Bp, n_seq=Bp + Bs)
    x = jnp.concatenate([x_prompt.reshape(n_ctx, D), x_sample.reshape(Bs * Ts, D)], axis=0)
    n_tok = x.shape[0]

    cond = jnp.concatenate([c_ctx[None, :], c, jnp.zeros((MOD_ROWS - 1 - Bs, D), _f32)], axis=0)
    mod = _mod_table(jax.nn.silu(cond).astype(_bf16), mod_w, mod_b)

    n_even, n_odd = (DEPTH + 1) // 2, DEPTH // 2
    acc_rw = jnp.zeros((Bp, n_even, 2, H_A, HD_A, HD_A), _f32)
    acc_gl = jnp.zeros((Bp, n_even, 2, H_B, DK_B, DV_B), _f32)
    acc_ml = (jnp.zeros((Bp, n_odd, 2, H_C, DK_C, DV_C), _f32), jnp.zeros((Bp, n_odd, 2, H_C, 1, DK_C), _f32),
              jnp.zeros((Bp, n_odd, 2, H_C, 1, 1), _f32))
    pending = None

    b0 = A_IN
    ev_w_all = jnp.concatenate([
        ev_w_in[..., :EV_MAIN], ev_w_in[..., b0 + 2 * QK_B:b0 + 2 * QK_B + V_B],
        ev_w_in[..., b0 + 2 * QK_B + V_B + 2 * LORA_GK:], ev_w_in[..., b0:b0 + 2 * QK_B], ev_w_in[..., EV_MAIN:A_IN],
        ev_w_in[..., b0 + 2 * QK_B + V_B:b0 + 2 * QK_B + V_B + 2 * LORA_GK],
        jnp.zeros(ev_w_in.shape[:2] + (EV_LORA_W - (A_IN - EV_MAIN) - 2 * LORA_GK,), _f32)], axis=-1).astype(_bf16)
    od_w_all = jnp.pad(od_w_in, ((0, 0), (0, 0), (0, OD_N - od_w_in.shape[2]))).astype(_bf16)
    ev_w_out_b, od_w_out_b = ev_w_out.astype(_bf16), od_w_out.astype(_bf16)

    for l in range(DEPTH):
        i = l // 2
        sh1, sc1, gt1, sh2, sc2, gt2 = [mod[l, :, j * D:(j + 1) * D] for j in range(6)]
        if pending is None:
            h = _norm_mod(x, norm_g[l, 0], sc1, sh1, st)[0]
        else:
            x, h = _norm_mod(x, norm_g[l, 0], sc1, sh1, st, res=pending)
        if l % 2 == 0:
            p = _matmul(h, ev_w_all[i], tn=EV_N // 4)
            y_in, acc_rw, acc_gl = _even_mixer(
                p, st, i, acc_rw, acc_gl, state_rwkv[:, i], state_gla[:, i], ev_mu[i], rw_w0[i], rw_w2[i],
                rw_a0[i], rw_a2[i], rw_g2[i], rw_kk[i], rw_ka[i], rw_rk[i], rw_ln_g[i], rw_ln_b[i],
                gla_w2[i], gla_b[i], gla_norm_g[i])
            x, h = _matmul_residual_norm(y_in, ev_w_out_b[i], x, gt1, norm_g[l, 1], sc2, sh2, st)
            f = _dense_ffn(h, ffn_w1, ffn_w3, ffn_w2, i)
        else:
            p = _matmul(h, od_w_all[i], tn=OD_N // 7)
            y_in, acc_ml = _odd_mixer(
                p, st, i, acc_ml, state_mlstm_c[:, i], state_mlstm_n[:, i], state_mlstm_m[:, i],
                ml_b_i[i], ml_b_f[i], ml_norm_g[i])
            router = _split2(jnp.pad(moe_router[i], ((0, 0), (0, LANES - N_EXPERTS))))
            x, h, logits = _matmul_residual_norm(y_in, od_w_out_b[i], x, gt1, norm_g[l, 1], sc2, sh2, st,
                                                 router=router, h_dtype=_f32)
            f = _moe(h, logits, moe_w1, moe_w3, moe_w2, i)
        pending = (f, gt2)

    y = _final_norm(x, pending[0], pending[1], final_g, st)
    return (y[:n_ctx].reshape(Bp, Tp, D), y[n_ctx:].reshape(Bs, Ts, D), acc_rw, acc_gl,
            acc_ml[0], acc_ml[1][..., 0, :], acc_ml[2][..., 0, 0])
```

```python
import functools
from typing import NamedTuple

import jax
import jax.numpy as jnp
import numpy as np
from jax import lax
from jax.experimental import pallas as pl
from jax.experimental.pallas import tpu as pltpu

D_MODEL = 2048
DEPTH = 4
GRID_W = 64
EPS = 1e-6
CHUNK = 64
H_A, HD_A = 16, 64
D_A = H_A * HD_A
LORA_W, LORA_A, LORA_G = 64, 64, 128
A_IN = 3 * D_A + 2 * LORA_W + 2 * LORA_A + LORA_G
H_B, DK_B, DV_B = 4, 128, 256
QK_B, V_B = H_B * DK_B, H_B * DV_B
LORA_GK = 16
GATE_NORM = 16.0
H_C, DK_C, DV_C = 8, 128, 256
QK_C, V_C = H_C * DK_C, H_C * DV_C
D_FF = 5632
N_EXPERTS = 8
TOP_K = 2

LANES = 128
MOD_ROWS = 8
VMEM_LIMIT = 48 * 1024 * 1024
FFN_VMEM_LIMIT = 56 * 1024 * 1024
NEG_BIG = -1e30

EV_MAIN = 3 * D_A
EV_V_BLK = EV_MAIN // V_B
EV_OG_BLK = EV_V_BLK + 1
EV_Q_BLK = (EV_MAIN + 2 * V_B) // QK_B
EV_K_BLK = EV_Q_BLK + 1
EV_LORA_W = 512
EV_LORA_BLK = (EV_MAIN + 2 * V_B + 2 * QK_B) // EV_LORA_W
EV_N = EV_MAIN + 2 * V_B + 2 * QK_B + EV_LORA_W
GLA_LORA_OFF = 2 * LORA_W + 2 * LORA_A + LORA_G
OD_GATE_BLK = (2 * QK_C + 2 * V_C) // LANES
OD_N = 2 * QK_C + 2 * V_C + LANES

_f32 = jnp.float32
_bf16 = jnp.bfloat16


class _Streams(NamedTuple):
    n_ctx_chunks: int
    ctx_cps: int
    lat_cps: int
    n_chunks: int
    n_ctx_seq: int


def _chunk_pos(s, st):
    is_lat = s >= st.n_ctx_chunks
    sl = s - st.n_ctx_chunks
    nc = jnp.where(is_lat, st.lat_cps, st.ctx_cps)
    c = jnp.where(is_lat, sl % st.lat_cps, s % st.ctx_cps)
    seq = jnp.where(is_lat, st.n_ctx_seq + sl // st.lat_cps, s // st.ctx_cps)
    return is_lat, nc, c, seq


def _bwd_chunk(s, st):
    _, nc, c, _ = _chunk_pos(s, st)
    return s - c + (nc - 1 - c)


def _seq_of(s, st):
    return _chunk_pos(s, st)[3]


def _mod_row_of_chunk(s, st):
    is_lat, _, _, seq = _chunk_pos(s, st)
    return jnp.where(is_lat, 1 + seq - st.n_ctx_seq, 0)


def _cparams(n_axes):
    return pltpu.CompilerParams(dimension_semantics=("arbitrary",) * n_axes,
                                vmem_limit_bytes=VMEM_LIMIT)


def _dot(a, b):
    return jnp.dot(a.astype(_bf16), b.astype(_bf16), preferred_element_type=_f32)


def _dot_nt(a, b):
    return lax.dot_general(a.astype(_bf16), b.astype(_bf16), (((1,), (1,)), ((), ())),
                           preferred_element_type=_f32)


def _dot_tn(a, b):
    return lax.dot_general(a.astype(_bf16), b.astype(_bf16), (((0,), (0,)), ((), ())),
                           preferred_element_type=_f32)


def _split2(x):
    hi = x.astype(_bf16)
    return hi, (x - hi.astype(_f32)).astype(_bf16)


def _dot_exact_lhs(tri, x):
    hi, lo = _split2(x)
    t = tri.astype(_bf16)
    return jnp.dot(t, hi, preferred_element_type=_f32) + jnp.dot(t, lo, preferred_element_type=_f32)


def _dot_exact_rhs(x, ones):
    hi, lo = _split2(x)
    return jnp.dot(hi, ones, preferred_element_type=_f32) + jnp.dot(lo, ones, preferred_element_type=_f32)


def _head_sum(x, seg, seg_t):
    return _dot_exact_rhs(_dot_exact_rhs(x, seg), seg_t)


def _dot3(a, b_hi, b_lo):
    a_hi, a_lo = _split2(a)
    d = lambda u, v: jnp.dot(u, v, preferred_element_type=_f32)
    return d(a_hi, b_hi) + d(a_lo, b_hi) + d(a_hi, b_lo)


def _tri_masks(direction, n=CHUNK, period=CHUNK):
    t = lax.broadcasted_iota(jnp.int32, (n, n), 0)
    i = lax.broadcasted_iota(jnp.int32, (n, n), 1)
    same = (t // period) == (i // period) if n != period else True
    if direction == 0:
        return (i <= t) & same, (i < t) & same
    return (i >= t) & same, (i > t) & same


def _mod_kernel(a_ref, w_ref, b_ref, o_ref):
    o_ref[...] = _dot(a_ref[...], w_ref[...]) + b_ref[...]


def _mod_table(silu_rows, mod_w, mod_b, tn=1024):
    n_l, k, n = mod_w.shape
    return pl.pallas_call(
        _mod_kernel,
        grid=(n_l, n // tn),
        in_specs=[pl.BlockSpec((MOD_ROWS, k), lambda l, j: (0, 0)),
                  pl.BlockSpec((None, k, tn), lambda l, j: (l, 0, j)),
                  pl.BlockSpec((None, 1, tn), lambda l, j: (l, 0, j))],
        out_specs=pl.BlockSpec((None, MOD_ROWS, tn), lambda l, j: (l, 0, j)),
        out_shape=jax.ShapeDtypeStruct((n_l, MOD_ROWS, n), _f32),
        compiler_params=_cparams(2), name="mod_table",
    )(silu_rows, mod_w, mod_b.reshape(n_l, 1, n))


def _stream_specs(tm, d, st):
    n_ctx_blocks = st.n_ctx_chunks * CHUNK // tm
    return [pl.BlockSpec((tm, d), lambda i: (jnp.minimum(i, n_ctx_blocks - 1), 0)),
            pl.BlockSpec((tm, d), lambda i: (jnp.maximum(i - n_ctx_blocks, 0), 0))]


def _read_stream(i, tm, st, ctx_ref, lat_ref):
    return jnp.where(i * tm < st.n_ctx_chunks * CHUNK, ctx_ref[...], lat_ref[...])


def _norm_mod_kernel(*refs, tm, st, has_res, has_router, split_x):
    refs = list(refs)
    if split_x:
        x = _read_stream(pl.program_id(0), tm, st, refs.pop(0), refs.pop(0))
    else:
        x = refs.pop(0)[...]
    f_ref, gate_ref = (refs.pop(0), refs.pop(0)) if has_res else (None, None)
    g_ref, sc_ref, sh_ref = refs.pop(0), refs.pop(0), refs.pop(0)
    r_hi, r_lo = (refs.pop(0), refs.pop(0)) if has_router else (None, None)
    xo_ref = refs.pop(0) if has_res else None
    h_ref = refs.pop(0)
    row = _mod_row_of_chunk(pl.program_id(0) * (tm // CHUNK), st)
    if has_res:
        x = x + gate_ref[pl.ds(row, 1), :] * f_ref[...]
        xo_ref[...] = x
    inv = lax.rsqrt(jnp.mean(x * x, axis=-1, keepdims=True) + EPS)
    h = x * inv * g_ref[...] * (1.0 + sc_ref[pl.ds(row, 1), :]) + sh_ref[pl.ds(row, 1), :]
    h_ref[...] = h.astype(h_ref.dtype)
    if has_router:
        refs.pop(0)[...] = _dot3(h, r_hi[...], r_lo[...])


def _norm_mod(x, g, sc, sh, st, res=None, router=None, h_dtype=_bf16, tm=256):
    split_x = isinstance(x, tuple)
    d = x[0].shape[1] if split_x else x.shape[1]
    m = x[0].shape[0] + x[1].shape[0] if split_x else x.shape[0]
    row = pl.BlockSpec((tm, d), lambda i: (i, 0))
    tab = pl.BlockSpec((MOD_ROWS, d), lambda i: (0, 0))
    args, specs = (list(x), _stream_specs(tm, d, st)) if split_x else ([x], [row])
    outs, out_specs = [], []
    if res is not None:
        args += list(res)
        specs += [row, tab]
        outs.append(jax.ShapeDtypeStruct((m, d), _f32))
        out_specs.append(row)
    args += [g.reshape(1, d), sc, sh]
    specs += [pl.BlockSpec((1, d), lambda i: (0, 0)), tab, tab]
    outs.append(jax.ShapeDtypeStruct((m, d), h_dtype))
    out_specs.append(row)
    if router is not None:
        args += list(router)
        specs += [pl.BlockSpec((d, LANES), lambda i: (0, 0))] * 2
        outs.append(jax.ShapeDtypeStruct((m, LANES), _f32))
        out_specs.append(pl.BlockSpec((tm, LANES), lambda i: (i, 0)))
    return pl.pallas_call(
        functools.partial(_norm_mod_kernel, tm=tm, st=st, has_res=res is not None, has_router=router is not None,
                          split_x=split_x),
        grid=(m // tm,), in_specs=specs, out_specs=out_specs, out_shape=outs,
        compiler_params=_cparams(1), name="norm_mod",
    )(*args)


def _final_norm_kernel(x_ref, f_ref, gate_ref, g_ref, ctx_ref, lat_ref, *, tm, st):
    i = pl.program_id(0)
    row = _mod_row_of_chunk(i * (tm // CHUNK), st)
    x = x_ref[...] + gate_ref[pl.ds(row, 1), :] * f_ref[...]
    y = x * lax.rsqrt(jnp.mean(x * x, axis=-1, keepdims=True) + EPS) * g_ref[...]
    is_ctx = i * tm < st.n_ctx_chunks * CHUNK

    @pl.when(is_ctx)
    def _():
        ctx_ref[...] = y

    @pl.when(jnp.logical_not(is_ctx))
    def _():
        lat_ref[...] = y


def _final_norm(x, f, gate, g, st, tm=256):
    m, d = x.shape
    n_ctx = st.n_ctx_chunks * CHUNK
    row = pl.BlockSpec((tm, d), lambda i: (i, 0))
    return pl.pallas_call(
        functools.partial(_final_norm_kernel, tm=tm, st=st),
        grid=(m // tm,),
        in_specs=[row, row, pl.BlockSpec((MOD_ROWS, d), lambda i: (0, 0)), pl.BlockSpec((1, d), lambda i: (0, 0))],
        out_specs=_stream_specs(tm, d, st),
        out_shape=[jax.ShapeDtypeStruct((n_ctx, d), _f32), jax.ShapeDtypeStruct((m - n_ctx, d), _f32)],
        compiler_params=_cparams(1), name="final_norm",
    )(x, f, gate, g.reshape(1, d))


def _mm_kernel(a_ref, w_ref, o_ref):
    o_ref[...] = _dot(a_ref[...], w_ref[...])


def _mm_res_norm_kernel(*refs, tm, st, has_router, split_res):
    refs = list(refs)
    a_ref, w_ref = refs.pop(0), refs.pop(0)
    if split_res:
        res = _read_stream(pl.program_id(0), tm, st, refs.pop(0), refs.pop(0))
    else:
        res = refs.pop(0)[...]
    gate_ref, g_ref, sc_ref, sh_ref = refs.pop(0), refs.pop(0), refs.pop(0), refs.pop(0)
    r_hi, r_lo = (refs.pop(0), refs.pop(0)) if has_router else (None, None)
    outs = refs
    row = _mod_row_of_chunk(pl.program_id(0) * (tm // CHUNK), st)
    x = res + gate_ref[pl.ds(row, 1), :] * _dot(a_ref[...], w_ref[...])
    outs[0][...] = x
    inv = lax.rsqrt(jnp.mean(x * x, axis=-1, keepdims=True) + EPS)
    h = x * inv * g_ref[...] * (1.0 + sc_ref[pl.ds(row, 1), :]) + sh_ref[pl.ds(row, 1), :]
    outs[1][...] = h.astype(outs[1].dtype)
    if has_router:
        outs[2][...] = _dot3(h, r_hi[...], r_lo[...])


def _matmul(a, w, tm=512, tn=512):
    m, k = a.shape
    n = w.shape[1]
    tm = min(tm, m)
    return pl.pallas_call(
        _mm_kernel,
        grid=(n // tn, m // tm),
        in_specs=[pl.BlockSpec((tm, k), lambda j, i: (i, 0)),
                  pl.BlockSpec((k, tn), lambda j, i: (0, j))],
        out_specs=pl.BlockSpec((tm, tn), lambda j, i: (i, j)),
        out_shape=jax.ShapeDtypeStruct((m, n), _f32),
        compiler_params=_cparams(2), name="matmul",
    )(a, w)


def _matmul_residual_norm(a, w, res, gate, g, sc, sh, st, router=None, h_dtype=_bf16, tm=256):
    m, k = a.shape
    n = w.shape[1]
    split_res = isinstance(res, tuple)
    row = lambda width: pl.BlockSpec((tm, width), lambda i: (i, 0))
    tab = pl.BlockSpec((MOD_ROWS, n), lambda i: (0, 0))
    args = [a, w] + (list(res) if split_res else [res]) + [gate, g.reshape(1, n), sc, sh]
    specs = ([row(k), pl.BlockSpec((k, n), lambda i: (0, 0))] + (_stream_specs(tm, n, st) if split_res else [row(n)])
             + [tab, pl.BlockSpec((1, n), lambda i: (0, 0)), tab, tab])
    outs = [jax.ShapeDtypeStruct((m, n), _f32), jax.ShapeDtypeStruct((m, n), h_dtype)]
    out_specs = [row(n), row(n)]
    if router is not None:
        args += list(router)
        specs += [pl.BlockSpec((n, LANES), lambda i: (0, 0))] * 2
        outs.append(jax.ShapeDtypeStruct((m, LANES), _f32))
        out_specs.append(row(LANES))
    return pl.pallas_call(
        functools.partial(_mm_res_norm_kernel, tm=tm, st=st, has_router=router is not None, split_res=split_res),
        grid=(m // tm,), in_specs=specs, out_specs=out_specs, out_shape=outs,
        compiler_params=_cparams(1), name="matmul_residual_norm",
    )(*args)


FFN_TM = 1024
FFN_TF = 256
FFN_ROW_STEP = 256


def _swiglu_accumulate(x_ref, w1_ref, w3_ref, w2_ref, o_ref, n_rows):
    x = x_ref[0:n_rows, :]
    a = _dot(x, w1_ref[...])
    b = _dot(x, w3_ref[...])
    o_ref[0:n_rows, :] += _dot(a * jax.nn.sigmoid(a) * b, w2_ref[...])


def _ffn_kernel(x_ref, w1_ref, w3_ref, w2_ref, o_ref, *, tm):
    @pl.when(pl.program_id(1) == 0)
    def _():
        o_ref[...] = jnp.zeros_like(o_ref)

    _swiglu_accumulate(x_ref, w1_ref, w3_ref, w2_ref, o_ref, tm)


def _row_copy(src_hbm, src_row, dst, dst_row, sem):
    return pltpu.make_async_copy(src_hbm.at[pl.ds(src_row, 1), :], dst.at[pl.ds(dst_row, 1), :], sem)


ROW_DMA_UNROLL = 8


def _for_each_row(n, body):
    groups = n // ROW_DMA_UNROLL

    def group(g, carry):
        r0 = pl.multiple_of(g * ROW_DMA_UNROLL, ROW_DMA_UNROLL)
        for u in range(ROW_DMA_UNROLL):
            body(r0 + u)
        return carry

    def single(r, carry):
        body(r)
        return carry

    lax.fori_loop(0, groups, group, 0)
    lax.fori_loop(groups * ROW_DMA_UNROLL, n, single, 0)


def _ffn_gather_kernel(te_ref, tr_ref, tb_ref, tok_ref, h_hbm, w1_ref, w3_ref, w2_ref, o_ref, xf_scr, xb_scr, sem,
                       *, tm, n_tiles):
    i, j = pl.program_id(0), pl.program_id(1)
    rows = tr_ref[i]

    def start_gather(tile):
        base = tb_ref[tile]
        _for_each_row(tr_ref[tile], lambda r: _row_copy(h_hbm, tok_ref[base + r], xf_scr, r, sem).start())

    def wait_gather(tile):
        @pl.when(tr_ref[tile] == tm)
        def _():
            pltpu.make_async_copy(h_hbm.at[pl.ds(0, tm), :], xf_scr, sem).wait()

        @pl.when(tr_ref[tile] < tm)
        def _():
            _for_each_row(tr_ref[tile], lambda r: _row_copy(h_hbm, 0, xf_scr, r, sem).wait())

    @pl.when(j == 0)
    def _():
        @pl.when(i == 0)
        def _():
            xf_scr[...] = jnp.zeros_like(xf_scr)
            start_gather(0)

        wait_gather(i)
        xb_scr[...] = xf_scr[...].astype(_bf16)
        o_ref[...] = jnp.zeros_like(o_ref)

        @pl.when(i + 1 < n_tiles)
        def _():
            start_gather(i + 1)

    for n_rows in range(FFN_ROW_STEP, tm + 1, FFN_ROW_STEP):
        @pl.when((rows > n_rows - FFN_ROW_STEP) & (rows <= n_rows))
        def _():
            _swiglu_accumulate(xb_scr, w1_ref, w3_ref, w2_ref, o_ref, n_rows)


def _gathered_ffn(h, pair_token, w1, w3, w2, layer, tile_expert, tile_rows, tile_base, tm=FFN_TM, tf=FFN_TF):
    d = h.shape[1]
    f = w1.shape[3]
    nj = f // tf
    n_tiles = tile_expert.shape[0]

    def jeff(i, j, tr):
        return jnp.where(tr[i] > 0, j, nj - 1)

    return pl.pallas_call(
        functools.partial(_ffn_gather_kernel, tm=tm, n_tiles=n_tiles),
        grid_spec=pltpu.PrefetchScalarGridSpec(
            num_scalar_prefetch=4,
            grid=(n_tiles, nj),
            in_specs=[pl.BlockSpec(memory_space=pl.ANY),
                      pl.BlockSpec((None, None, d, tf), lambda i, j, te, tr, *_: (layer, te[i], 0, jeff(i, j, tr))),
                      pl.BlockSpec((None, None, d, tf), lambda i, j, te, tr, *_: (layer, te[i], 0, jeff(i, j, tr))),
                      pl.BlockSpec((None, None, tf, d), lambda i, j, te, tr, *_: (layer, te[i], jeff(i, j, tr), 0))],
            out_specs=pl.BlockSpec((tm, d), lambda i, j, *_: (i, 0)),
            scratch_shapes=[pltpu.VMEM((tm, d), _f32), pltpu.VMEM((tm, d), _bf16), pltpu.SemaphoreType.DMA]),
        out_shape=jax.ShapeDtypeStruct((n_tiles * tm, d), _f32),
        compiler_params=pltpu.CompilerParams(dimension_semantics=("arbitrary", "arbitrary"),
                                             vmem_limit_bytes=FFN_VMEM_LIMIT),
        name="gathered_ffn",
    )(tile_expert, tile_rows, tile_base, pair_token, h, w1, w3, w2)


COMBINE_TM = 256


def _combine_kernel(slot_ref, yg_hbm, gates_ref, o_ref, buf, sem, *, tm):
    base = pl.program_id(0) * tm * TOP_K

    def start(t):
        for k in range(TOP_K):
            _row_copy(yg_hbm, slot_ref[base + t * TOP_K + k], buf.at[k], t, sem).start()

    _for_each_row(tm, start)
    for k in range(TOP_K):
        pltpu.make_async_copy(yg_hbm.at[pl.ds(0, tm), :], buf.at[k], sem).wait()
    g = gates_ref[...]
    o_ref[...] = g[:, 0:1] * buf[0] + g[:, 1:2] * buf[1]


def _moe_combine(yg, pair_slot, gates, tm=COMBINE_TM):
    n_tok = gates.shape[0]
    d = yg.shape[1]
    return pl.pallas_call(
        functools.partial(_combine_kernel, tm=tm),
        grid_spec=pltpu.PrefetchScalarGridSpec(
            num_scalar_prefetch=1,
            grid=(n_tok // tm,),
            in_specs=[pl.BlockSpec(memory_space=pl.ANY),
                      pl.BlockSpec((tm, TOP_K), lambda b, sl: (b, 0))],
            out_specs=pl.BlockSpec((tm, d), lambda b, sl: (b, 0)),
            scratch_shapes=[pltpu.VMEM((TOP_K, tm, d), _f32), pltpu.SemaphoreType.DMA]),
        out_shape=jax.ShapeDtypeStruct((n_tok, d), _f32),
        compiler_params=_cparams(1), name="moe_combine",
    )(pair_slot, yg, gates)


def _dense_ffn(x, w1, w3, w2, layer, tm=FFN_TM, tf=FFN_TF):
    m, d = x.shape
    f = w1.shape[2]
    return pl.pallas_call(
        functools.partial(_ffn_kernel, tm=tm),
        grid=(m // tm, f // tf),
        in_specs=[pl.BlockSpec((tm, d), lambda i, j: (i, 0)),
                  pl.BlockSpec((None, d, tf), lambda i, j: (layer, 0, j)),
                  pl.BlockSpec((None, d, tf), lambda i, j: (layer, 0, j)),
                  pl.BlockSpec((None, tf, d), lambda i, j: (layer, j, 0))],
        out_specs=pl.BlockSpec((tm, d), lambda i, j: (i, 0)),
        out_shape=jax.ShapeDtypeStruct((m, d), _f32),
        compiler_params=pltpu.CompilerParams(dimension_semantics=("arbitrary", "arbitrary"),
                                             vmem_limit_bytes=FFN_VMEM_LIMIT),
        name="dense_ffn",
    )(x, w1, w3, w2)


PREP_CHUNKS = 4
PREP_ROWS = PREP_CHUNKS * CHUNK


def _token_shift_block(x, prev, nxt, lane0, is_lat, first, last):
    rows, n = x.shape
    row = lax.broadcasted_iota(jnp.int32, x.shape, 0)
    ch = lax.broadcasted_iota(jnp.int32, x.shape, 1) + lane0
    down = pltpu.roll(x, 1, 0)
    up = pltpu.roll(x, rows - 1, 0)

    def context():
        zero_row = jnp.zeros((1, n), _f32)
        row_prev = jnp.where(first, zero_row, prev[CHUNK - 1:CHUNK, :])
        row_next = jnp.where(last, zero_row, nxt[0:1, :])
        return jnp.where(ch < A_IN // 2,
                         jnp.where(row == 0, row_prev, down),
                         jnp.where(row == rows - 1, row_next, up))

    def latent():
        q = A_IN // 4
        col = row % GRID_W
        above = jnp.concatenate([jnp.where(first, 0.0, prev), x[:rows - CHUNK]], axis=0)
        below = jnp.concatenate([x[CHUNK:], jnp.where(last, 0.0, nxt)], axis=0)
        return jnp.where(ch < q, jnp.where(col == 0, 0.0, down),
                         jnp.where(ch < 2 * q, jnp.where(col == GRID_W - 1, 0.0, up),
                                   jnp.where(ch < 3 * q, above, below)))

    return lax.cond(is_lat, latent, context)


def _rwkv_prep_kernel(xm_ref, pm_ref, nm_ref, xl_ref, pl_ref, nl_ref, mum_ref, mul_ref,
                      ww_ref, wa_ref, wg_ref, par_ref, seg_ref, segt_ref,
                      rt_ref, kt_ref, at_ref, bt_ref, kh_ref, bh_ref, v_ref, plast_ref, bonus_ref, g_ref,
                      *, st):
    s = pl.program_id(0) * PREP_CHUNKS
    is_lat, nc, c, _ = _chunk_pos(s, st)
    first, last = c == 0, c + PREP_CHUNKS == nc
    xm, xl = xm_ref[...], xl_ref[...]
    sm = _token_shift_block(xm, pm_ref[...], nm_ref[...], 0, is_lat, first, last)
    sl = _token_shift_block(xl, pl_ref[...], nl_ref[...], EV_MAIN, is_lat, first, last)
    pm = xm + mum_ref[...] * (sm - xm)
    lora = xl + mul_ref[...] * (sl - xl)
    r, k, v = pm[:, :D_A], pm[:, D_A:2 * D_A], pm[:, 2 * D_A:]
    par = par_ref[...]
    k_k, k_a, r_k = par[4:5], par[5:6], par[6:7]
    seg, seg_t = seg_ref[...], segt_ref[...]
    kk = k * k_k
    kk = kk / jnp.maximum(jnp.sqrt(_head_sum(kk * kk, seg, seg_t)), 1e-12)
    g_ref[...] = _dot(jax.nn.sigmoid(lora[:, 2 * LORA_W + 2 * LORA_A:GLA_LORA_OFF]), wg_ref[...])
    v_ref[...] = v.astype(_bf16)
    tanh_w = jnp.tanh(lora[:, :2 * LORA_W])
    lora_a = lora[:, 2 * LORA_W:2 * LORA_W + 2 * LORA_A]
    kdir_sum = jnp.zeros_like(k)
    t = lax.broadcasted_iota(jnp.int32, (PREP_ROWS, PREP_ROWS), 0)
    i = lax.broadcasted_iota(jnp.int32, (PREP_ROWS, PREP_ROWS), 1)
    same_chunk = jnp.where((t // CHUNK) == (i // CHUNK), 1.0, 0.0)
    for d in range(2):
        w = -jax.nn.softplus(-(par[d:d + 1] + _dot3(tanh_w, ww_ref[d, 0], ww_ref[d, 1]))) - 0.5
        ld = -jnp.exp(w)
        a = jax.nn.sigmoid(par[2 + d:3 + d] + _dot3(lora_a, wa_ref[d, 0], wa_ref[d, 1]))
        kdir = k * (1.0 + (a - 1.0) * k_a)
        kdir_sum = kdir_sum + kdir
        incl, _ = _tri_masks(d, PREP_ROWS, CHUNK)
        cum = _dot_exact_lhs(jnp.where(incl, 1.0, 0.0), ld)
        tot = _dot_exact_lhs(same_chunk, ld)
        e_pos, e_neg, e_end = jnp.exp(cum), jnp.exp(-cum), jnp.exp(tot - cum)
        kka = kk * a
        rt_ref[d] = (r * e_pos).astype(_bf16)
        kt_ref[d] = (kdir * e_neg).astype(_bf16)
        at_ref[d] = (kk * jnp.exp(cum - ld)).astype(_bf16)
        bt_ref[d] = (kka * e_neg).astype(_bf16)
        kh_ref[d] = (kdir * e_end).astype(_bf16)
        bh_ref[d] = (kka * e_end).astype(_bf16)
        e_tot = jnp.exp(tot)
        for j in range(PREP_CHUNKS):
            plast_ref[d, j] = e_tot[j * CHUNK:j * CHUNK + 1, :]
    bonus_ref[...] = _head_sum(r * kdir_sum * r_k, seg, seg_t) * v


def _rwkv_prep(p, st, mu_main, mu_lora, ww, wa, wg, par, seg, seg_t):
    n = p.shape[0]
    nch = st.n_chunks
    assert st.ctx_cps % PREP_CHUNKS == 0 and st.lat_cps % PREP_CHUNKS == 0
    cur = lambda b: (b, 0)
    prv = lambda b: (jnp.maximum(b * PREP_CHUNKS - 1, 0), 0)
    nxt = lambda b: (jnp.minimum((b + 1) * PREP_CHUNKS, nch - 1), 0)
    lo = lambda f: (lambda b: (f(b)[0], EV_LORA_BLK))
    main = lambda rows, f: pl.BlockSpec((rows, EV_MAIN), f)
    lora = lambda rows, f: pl.BlockSpec((rows, EV_LORA_W), lo(f))
    full = lambda a: pl.BlockSpec(a.shape, lambda b: (0,) * a.ndim)
    seq2 = pl.BlockSpec((2, PREP_ROWS, D_A), lambda b: (0, b, 0))
    seq1 = pl.BlockSpec((PREP_ROWS, D_A), lambda b: (b, 0))
    bf2 = jax.ShapeDtypeStruct((2, n, D_A), _bf16)
    return pl.pallas_call(
        functools.partial(_rwkv_prep_kernel, st=st),
        grid=(nch // PREP_CHUNKS,),
        in_specs=[main(PREP_ROWS, cur), main(CHUNK, prv), main(CHUNK, nxt),
                  lora(PREP_ROWS, cur), lora(CHUNK, prv), lora(CHUNK, nxt),
                  full(mu_main), full(mu_lora), full(ww), full(wa), full(wg), full(par), full(seg), full(seg_t)],
        out_specs=[seq2] * 6 + [seq1, pl.BlockSpec((2, PREP_CHUNKS, 1, D_A), lambda b: (0, b, 0, 0)), seq1, seq1],
        out_shape=[bf2] * 6 + [jax.ShapeDtypeStruct((n, D_A), _bf16),
                               jax.ShapeDtypeStruct((2, nch, 1, D_A), _f32),
                               jax.ShapeDtypeStruct((n, D_A), _f32),
                               jax.ShapeDtypeStruct((n, D_A), _f32)],
        compiler_params=_cparams(1), name="rwkv_prep",
    )(p, p, p, p, p, p, mu_main, mu_lora, ww, wa, wg, par, seg, seg_t)


N_PAIRS = H_A // 2
PAIR = 2 * CHUNK


def _stack_masked(x, keep_first):
    return jnp.concatenate([x * keep_first, x * (1 - keep_first)], axis=0)


def _rwkv_scan_kernel(*refs, st):
    (rt_f, kt_f, at_f, bt_f, kh_f, bh_f, v_f, pl_f,
     rt_b, kt_b, at_b, bt_b, kh_b, bh_b, v_b, pl_b, s0_ref, _acc, yf_ref, yb_ref, sout_ref, s_scr) = refs
    s = pl.program_id(0)
    is_lat, nc, c, _ = _chunk_pos(s, st)

    @pl.when(c == 0)
    def _():
        keep = jnp.where(is_lat, 1.0, 0.0)
        zero = jnp.zeros((HD_A, HD_A), _f32)
        for d in range(2):
            for p in range(N_PAIRS):
                top = jnp.concatenate([s0_ref[d, 2 * p], zero], axis=1)
                bot = jnp.concatenate([zero, s0_ref[d, 2 * p + 1]], axis=1)
                s_scr[d, p] = jnp.concatenate([top, bot], axis=0) * keep

    dirs = ((rt_f, kt_f, at_f, bt_f, kh_f, bh_f, v_f, pl_f, yf_ref),
            (rt_b, kt_b, at_b, bt_b, kh_b, bh_b, v_b, pl_b, yb_ref))
    chains = [(d, p) for d in range(2) for p in range(N_PAIRS)]
    masks = [_tri_masks(d, PAIR, CHUNK) for d in range(2)]
    eye = jnp.where(masks[0][0] & masks[1][0], 1.0, 0.0)

    lane = lax.broadcasted_iota(jnp.int32, (CHUNK, LANES), 1)
    keep_first = jnp.where(lane < HD_A, 1.0, 0.0).astype(_bf16)

    def load(ref, p):
        return _stack_masked(ref[:, p * LANES:(p + 1) * LANES], keep_first)

    ops = {}
    for d, p in chains:
        rt, kt, at, bt, kh, bh, v = (load(dirs[d][j], p) for j in range(7))
        incl, strict = masks[d]
        big = _dot_nt(jnp.concatenate([at, rt], axis=0), jnp.concatenate([kt, bt], axis=0))
        n = jnp.where(strict, big[:PAIR, PAIR:], 0.0)
        ops[d, p] = dict(rt=rt, at=at, bh=bh, kh=kh, v=v, t=eye - n, n=n.astype(_bf16),
                         mak=jnp.where(strict, big[:PAIR, :PAIR], 0.0).astype(_bf16),
                         mrk=jnp.where(incl, big[PAIR:, :PAIR], 0.0).astype(_bf16),
                         mrb=jnp.where(incl, big[PAIR:, PAIR:], 0.0).astype(_bf16))
    for _ in range(5):
        for ch in chains:
            ops[ch]["n"] = _dot(ops[ch]["n"], ops[ch]["n"]).astype(_bf16)
        for ch in chains:
            ops[ch]["t"] = ops[ch]["t"] + _dot(ops[ch]["t"], ops[ch]["n"])
    for ch in chains:
        o = ops[ch]
        o["mv"] = _dot(jnp.concatenate([o.pop("mak"), o.pop("mrk")], axis=0), o["v"])
    for ch in chains:
        o = ops[ch]
        o["xz"] = _dot(o.pop("t"), jnp.concatenate([o["at"], o["mv"][:PAIR].astype(_bf16)], axis=1))
    for ch in chains:
        o = ops[ch]
        o["qy"] = _dot(o.pop("mrb"), o["xz"])
    for ch in chains:
        o = ops[ch]
        o["upd"] = _dot_tn(o["xz"], o["bh"])
        o["vtk"] = _dot_tn(o["v"], o["kh"])
    for d, p in chains:
        o = ops[d, p]
        s0 = s_scr[d, p]
        q_eff = o["rt"].astype(_f32) - o["qy"][:, :LANES]
        y = _dot_nt(q_eff, s0) + o["mv"][PAIR:] - o["qy"][:, LANES:]
        dirs[d][8][:, p * LANES:(p + 1) * LANES] = y[:CHUNK] + y[CHUNK:]
        p_last = dirs[d][7][:, p * LANES:(p + 1) * LANES]
        s_scr[d, p] = s0 * p_last - _dot(s0, o["upd"][:LANES]) + o["vtk"] - o["upd"][LANES:]

    @pl.when((c == nc - 1) & jnp.logical_not(is_lat))
    def _():
        for d in range(2):
            for p in range(N_PAIRS):
                sp = s_scr[d, p]
                sout_ref[d, 2 * p] = sp[:HD_A, :HD_A]
                sout_ref[d, 2 * p + 1] = sp[HD_A:, HD_A:]


def _lat_state_spec(shape, st):
    nd = len(shape)
    return pl.BlockSpec((None,) + tuple(shape[1:]),
                        lambda s: (jnp.maximum(_seq_of(s, st) - st.n_ctx_seq, 0),) + (0,) * (nd - 1))


def _ctx_state_spec(shape, st, layer):
    nd = len(shape)
    return pl.BlockSpec((None, None) + tuple(shape[2:]),
                        lambda s: (jnp.minimum(_seq_of(s, st), st.n_ctx_seq - 1), layer) + (0,) * (nd - 2))


_ANY = pl.BlockSpec(memory_space=pl.ANY)


def _rwkv_scan(prep, s0, st, acc, layer):
    rt, kt, at, bt, kh, bh, v, p_last = prep
    n = v.shape[0]
    fwd = lambda s: s
    bwd = lambda s: _bwd_chunk(s, st)
    seq2 = lambda d, f: pl.BlockSpec((None, CHUNK, D_A), lambda s: (d, f(s), 0))
    seq1 = lambda f: pl.BlockSpec((CHUNK, D_A), lambda s: (f(s), 0))
    pls = lambda d, f: pl.BlockSpec((None, None, 1, D_A), lambda s: (d, f(s), 0, 0))
    side = lambda d, f: [seq2(d, f)] * 6 + [seq1(f), pls(d, f)]
    y_shape = jax.ShapeDtypeStruct((n, D_A), _f32)
    args = (rt, kt, at, bt, kh, bh, v, p_last)
    return pl.pallas_call(
        functools.partial(_rwkv_scan_kernel, st=st),
        grid=(st.n_chunks,),
        in_specs=side(0, fwd) + side(1, bwd) + [_lat_state_spec(s0.shape, st), _ANY],
        out_specs=[seq1(fwd), seq1(bwd), _ctx_state_spec(acc.shape, st, layer)],
        out_shape=[y_shape, y_shape, jax.ShapeDtypeStruct(acc.shape, _f32)],
        input_output_aliases={2 * len(args) + 1: 2},
        scratch_shapes=[pltpu.VMEM((2, N_PAIRS, LANES, LANES), _f32)],
        compiler_params=_cparams(1), name="rwkv_scan",
    )(*args, *args, s0, acc)


def _gla_kernel(q_f, k_f, v_f, l_f, q_b, k_b, v_b, l_b, w_ref, bias_ref, s0_ref, _acc,
                of_ref, ob_ref, sout_ref, s_scr, *, st):
    s = pl.program_id(0)
    is_lat, nc, c, _ = _chunk_pos(s, st)

    @pl.when(c == 0)
    def _():
        keep = jnp.where(is_lat, 1.0, 0.0)
        for d in range(2):
            for h in range(H_B):
                s_scr[d, h] = s0_ref[d, h].T * keep

    dirs = ((q_f, k_f, v_f, l_f, of_ref), (q_b, k_b, v_b, l_b, ob_ref))
    work = []
    for d, (q_ref, k_ref, v_ref, l_ref, o_ref) in enumerate(dirs):
        incl, _ = _tri_masks(d)
        gate_lora = l_ref[:, GLA_LORA_OFF:GLA_LORA_OFF + LANES]
        la = jax.nn.log_sigmoid(_dot3(gate_lora, w_ref[d, 0], w_ref[d, 1]) + bias_ref[d:d + 1]) / GATE_NORM
        cum = _dot_exact_lhs(jnp.where(incl, 1.0, 0.0), la)
        tot = jnp.sum(la, axis=0, keepdims=True)
        k = k_ref[...]
        q_in = q_ref[...] * (DK_B ** -0.5) * jnp.exp(cum)
        k_in = k * jnp.exp(-cum)
        k_end = k * jnp.exp(tot - cum)
        e_tot = jnp.exp(tot)
        for h in range(H_B):
            ks = slice(h * DK_B, (h + 1) * DK_B)
            work.append((d, h, incl, q_in[:, ks], k_in[:, ks], k_end[:, ks], e_tot[:, ks]))
    att = [jnp.where(incl, _dot_nt(q_in, k_in), 0.0) for (_, _, incl, q_in, k_in, _, _) in work]
    for (d, h, _, q_in, _, k_end, e_tot), a in zip(work, att):
        v_ref, o_ref = dirs[d][2], dirs[d][4]
        vs = slice(h * DV_B, (h + 1) * DV_B)
        v = v_ref[:, vs]
        s_t = s_scr[d, h]
        o_ref[:, vs] = _dot(a, v) + _dot_nt(q_in, s_t)
        s_scr[d, h] = s_t * e_tot + _dot_tn(v, k_end)

    @pl.when((c == nc - 1) & jnp.logical_not(is_lat))
    def _():
        for d in range(2):
            for h in range(H_B):
                sout_ref[d, h] = s_scr[d, h].T


def _gla_scan(p, w, bias, s0, st, acc, layer):
    n = p.shape[0]
    fwd = lambda s: s
    bwd = lambda s: _bwd_chunk(s, st)
    blk = lambda width, col, f: pl.BlockSpec((CHUNK, width), lambda s: (f(s), col))
    side = lambda f: [blk(QK_B, EV_Q_BLK, f), blk(QK_B, EV_K_BLK, f), blk(V_B, EV_V_BLK, f),
                      blk(EV_LORA_W, EV_LORA_BLK, f)]
    full = lambda a: pl.BlockSpec(a.shape, lambda s: (0,) * a.ndim)
    o_spec = lambda f: pl.BlockSpec((CHUNK, V_B), lambda s: (f(s), 0))
    o_shape = jax.ShapeDtypeStruct((n, V_B), _f32)
    return pl.pallas_call(
        functools.partial(_gla_kernel, st=st),
        grid=(st.n_chunks,),
        in_specs=side(fwd) + side(bwd) + [full(w), full(bias), _lat_state_spec(s0.shape, st), _ANY],
        out_specs=[o_spec(fwd), o_spec(bwd), _ctx_state_spec(acc.shape, st, layer)],
        out_shape=[o_shape, o_shape, jax.ShapeDtypeStruct(acc.shape, _f32)],
        input_output_aliases={11: 2},
        scratch_shapes=[pltpu.VMEM((2, H_B, DV_B, DK_B), _f32)],
        compiler_params=_cparams(1), name="gla_scan",
    )(p, p, p, p, p, p, p, p, w, bias, s0, acc)


def _mlstm_kernel(q_f, k_f, v_f, g_f, q_b, k_b, v_b, g_b, bias_ref, c0_ref, n0_ref, m0_ref, _acc_c, _acc_n, _acc_m,
                  hf_ref, hb_ref, cout_ref, nout_ref, mout_ref, c_scr, n_scr, m_scr, *, st):
    s = pl.program_id(0)
    is_lat, nc, c, _ = _chunk_pos(s, st)

    @pl.when(c == 0)
    def _():
        keep = jnp.where(is_lat, 1.0, 0.0)
        for d in range(2):
            for h in range(H_C):
                c_scr[d, h] = c0_ref[d, h].T * keep
        n_scr[...] = n0_ref[...] * keep
        m_scr[...] = m0_ref[...] * keep

    L = CHUNK
    lane = lax.broadcasted_iota(jnp.int32, (L, LANES), 1)
    dirs = ((q_f, k_f, v_f, g_f, hf_ref), (q_b, k_b, v_b, g_b, hb_ref))
    work = []
    for d, (q_ref, k_ref, v_ref, g_ref, h_ref) in enumerate(dirs):
        gates = g_ref[...] + bias_ref[0:1]
        gates = jnp.where(lane < 2 * H_C, gates, jax.nn.log_sigmoid(gates))
        gates_t = gates.T
        for h in range(H_C):
            ji, jf = d * H_C + h, 2 * H_C + d * H_C + h
            ks = slice(h * DK_C, (h + 1) * DK_C)
            q = q_ref[:, ks] * (DK_C ** -0.5)
            k = k_ref[:, ks]
            work.append(dict(d=d, h=h, q=q, k=k, incl=_tri_masks(d)[0], incl_t=_tri_masks(1 - d)[0],
                             i_col=gates[:, ji:ji + 1], f_col=gates[:, jf:jf + 1],
                             i_row=gates_t[ji:ji + 1, :], f_row=gates_t[jf:jf + 1, :],
                             m_prev=m_scr[d, h], c_t=c_scr[d, h], n_row=n_scr[d, h]))
    for o in work:
        o["s"] = _dot_nt(o["q"], o["k"])
        o["qc"] = _dot_nt(o["q"], o["c_t"])
    for o in work:
        o["cum_col"] = jnp.sum(jnp.where(o["incl"], o["f_row"], 0.0), axis=1, keepdims=True)
        o["cum_row"] = jnp.sum(jnp.where(o["incl_t"], o["f_col"], 0.0), axis=0, keepdims=True)
        o["f_tot"] = jnp.sum(o["f_row"], axis=1, keepdims=True)
        o["qn"] = jnp.sum(o["q"] * o["n_row"], axis=1, keepdims=True)
    for o in work:
        o["log_d"] = jnp.where(o["incl"], o["cum_col"] - o["cum_row"] + o["i_row"], NEG_BIG)
        o["log_inter"] = o["cum_col"] + o["m_prev"]
        o["row_max"] = jnp.max(o["log_d"], axis=1, keepdims=True)
    for o in work:
        m_t = jnp.maximum(o["log_inter"], o["row_max"])
        o["m_t"] = m_t
        o["sm"] = o["s"] * jnp.exp(o.pop("log_d") - m_t)
        o["w_inter"] = jnp.exp(o["log_inter"] - m_t)
        o["m_new"] = m_t[L - 1:L, :] if o["d"] == 0 else m_t[0:1, :]
        o["kw"] = o["k"] * jnp.exp(o["f_tot"] - o["cum_col"] + o["i_col"] - o["m_new"])
        o["dec"] = jnp.exp(o["f_tot"] + o["m_prev"] - o["m_new"])
    for o in work:
        vs = slice(o["h"] * DV_C, (o["h"] + 1) * DV_C)
        o["v"] = dirs[o["d"]][2][:, vs]
        o["sv"] = _dot(o["sm"], o["v"])
        o["kv"] = _dot_tn(o["v"], o["kw"])
        o["s_sum"] = jnp.sum(o["sm"], axis=1, keepdims=True)
    for o in work:
        d, h = o["d"], o["h"]
        vs = slice(h * DV_C, (h + 1) * DV_C)
        num = o["w_inter"] * o["qc"] + o["sv"]
        den = o["w_inter"] * o["qn"] + o["s_sum"]
        dirs[d][4][:, vs] = num * (1.0 / jnp.maximum(jnp.abs(den), jnp.exp(-o["m_t"])))
        c_scr[d, h] = o["dec"] * o["c_t"] + o["kv"]
        n_scr[d, h] = o["dec"] * o["n_row"] + jnp.sum(o["kw"], axis=0, keepdims=True)
        m_scr[d, h] = o["m_new"]

    @pl.when((c == nc - 1) & jnp.logical_not(is_lat))
    def _():
        for d in range(2):
            for h in range(H_C):
                cout_ref[d, h] = c_scr[d, h].T
        nout_ref[...] = n_scr[...]
        mout_ref[...] = m_scr[...]


def _mlstm_scan(p, bias, c0, n0, m0, st, accs, layer):
    n = p.shape[0]
    fwd = lambda s: s
    bwd = lambda s: _bwd_chunk(s, st)
    blk = lambda width, col, f: pl.BlockSpec((CHUNK, width), lambda s: (f(s), col))
    side = lambda f: [blk(QK_C, 0, f), blk(QK_C, 1, f), blk(V_C, 1, f), blk(LANES, OD_GATE_BLK, f)]
    o_spec = lambda f: pl.BlockSpec((CHUNK, V_C), lambda s: (f(s), 0))
    o_shape = jax.ShapeDtypeStruct((n, V_C), _f32)
    states = (c0, n0, m0)
    return pl.pallas_call(
        functools.partial(_mlstm_kernel, st=st),
        grid=(st.n_chunks,),
        in_specs=side(fwd) + side(bwd) + [pl.BlockSpec(bias.shape, lambda s: (0, 0))]
        + [_lat_state_spec(a.shape, st) for a in states] + [_ANY] * 3,
        out_specs=[o_spec(fwd), o_spec(bwd)] + [_ctx_state_spec(a.shape, st, layer) for a in accs],
        out_shape=[o_shape, o_shape] + [jax.ShapeDtypeStruct(a.shape, _f32) for a in accs],
        input_output_aliases={12: 2, 13: 3, 14: 4},
        scratch_shapes=[pltpu.VMEM((2, H_C, DV_C, DK_C), _f32), pltpu.VMEM((2, H_C, 1, DK_C), _f32),
                        pltpu.VMEM((2, H_C, 1, 1), _f32)],
        compiler_params=_cparams(1), name="mlstm_scan",
    )(p, p, p, p, p, p, p, p, bias, c0, n0, m0, *accs)


def _head_rms_gate(o, g_row, gate, n_heads, width):
    outs = []
    for h in range(n_heads):
        hs = slice(h * width, (h + 1) * width)
        oh = o[:, hs]
        outs.append(oh * lax.rsqrt(jnp.mean(oh * oh, axis=-1, keepdims=True) + EPS) * g_row[:, hs] * gate[:, hs])
    return jnp.concatenate(outs, axis=1)


def _even_post_kernel(yf_ref, yb_ref, bonus_ref, g_ref, of_ref, ob_ref, og_ref, par_ref, gn_ref, seg_ref, segt_ref,
                      o_ref):
    y = yf_ref[...] + yb_ref[...]
    seg, seg_t = seg_ref[...], segt_ref[...]
    mean = _head_sum(y, seg, seg_t) * (1.0 / HD_A)
    yc = y - mean
    var = _head_sum(yc * yc, seg, seg_t) * (1.0 / HD_A)
    par = par_ref[...]
    ya = (yc * lax.rsqrt(var + EPS) * par[0:1] + par[1:2] + bonus_ref[...]) * g_ref[...]
    og = og_ref[...]
    yb = _head_rms_gate(of_ref[...] + ob_ref[...], gn_ref[...], og * jax.nn.sigmoid(og), H_B, DV_B)
    o_ref[:, :D_A] = ya.astype(_bf16)
    o_ref[:, D_A:] = yb.astype(_bf16)


def _even_post(yf, yb, bonus, g, of, ob, p, ln_par, gla_g, seg, seg_t, tm=256):
    n = yf.shape[0]
    row = lambda w, col=0: pl.BlockSpec((tm, w), lambda i: (i, col))
    full = lambda a: pl.BlockSpec(a.shape, lambda i: (0,) * a.ndim)
    return pl.pallas_call(
        _even_post_kernel,
        grid=(n // tm,),
        in_specs=[row(D_A)] * 4 + [row(V_B), row(V_B), row(V_B, EV_OG_BLK), full(ln_par), full(gla_g),
                  full(seg), full(seg_t)],
        out_specs=row(D_A + V_B),
        out_shape=jax.ShapeDtypeStruct((n, D_A + V_B), _bf16),
        compiler_params=_cparams(1), name="even_post",
    )(yf, yb, bonus, g, of, ob, p, ln_par, gla_g, seg, seg_t)


def _odd_post_kernel(hf_ref, hb_ref, o_ref_in, gn_ref, out_ref):
    out_ref[...] = _head_rms_gate(hf_ref[...] + hb_ref[...], gn_ref[...], jax.nn.sigmoid(o_ref_in[...]),
                                  H_C, DV_C).astype(_bf16)


def _odd_post(hf, hb, p, gn, tm=256):
    n = hf.shape[0]
    row = lambda col=0: pl.BlockSpec((tm, V_C), lambda i: (i, col))
    return pl.pallas_call(
        _odd_post_kernel,
        grid=(n // tm,),
        in_specs=[row(), row(), row(2), pl.BlockSpec(gn.shape, lambda i: (0, 0))],
        out_specs=row(),
        out_shape=jax.ShapeDtypeStruct((n, V_C), _bf16),
        compiler_params=_cparams(1), name="odd_post",
    )(hf, hb, p, gn)


def _even_mixer(p, st, layer, acc_rw, acc_gl, s_rwkv, s_gla, mu, w0, w2, a0, a2, g2, k_k, k_a, r_k, ln_g, ln_b,
                gla_w2, gla_b, gla_g):
    zpad = lambda x, rows_before, total: jnp.pad(x, ((rows_before, total - rows_before - x.shape[0]), (0, 0)))
    mu_main = mu[:EV_MAIN].reshape(1, EV_MAIN)
    mu_lora = jnp.pad(mu[EV_MAIN:], (0, EV_LORA_W - (A_IN - EV_MAIN))).reshape(1, EV_LORA_W)
    split = lambda w: jnp.stack(_split2(w), axis=0)
    ww = jnp.stack([split(zpad(w2[d], d * LORA_W, 2 * LORA_W)) for d in range(2)])
    wa = jnp.stack([split(zpad(a2[d], d * LORA_A, 2 * LORA_A)) for d in range(2)])
    wg = g2.astype(_bf16)
    par = jnp.stack([w0[0], w0[1], a0[0], a0[1], k_k, k_a, r_k, jnp.zeros_like(k_k)])
    head = np.arange(D_A) // HD_A
    seg_np = head[:, None] == np.arange(LANES)[None, :]
    seg, seg_t = jnp.asarray(seg_np, _bf16), jnp.asarray(seg_np.T, _bf16)
    *prep, bonus, g = _rwkv_prep(p, st, mu_main, mu_lora, ww, wa, wg, par, seg, seg_t)
    yf, yb, acc_rw = _rwkv_scan(prep, s_rwkv, st, acc_rw, layer)
    wl = jnp.stack([split(zpad(gla_w2[d], d * LORA_GK, LANES)) for d in range(2)])
    bias = jnp.pad(gla_b, ((0, MOD_ROWS - 2), (0, 0)))
    of, ob, acc_gl = _gla_scan(p, wl, bias, s_gla, st, acc_gl, layer)
    ln_par = jnp.pad(jnp.stack([ln_g, ln_b]), ((0, MOD_ROWS - 2), (0, 0)))
    y_in = _even_post(yf, yb, bonus, g, of, ob, p, ln_par, gla_g.reshape(1, V_B), seg, seg_t)
    return y_in, acc_rw, acc_gl


def _odd_mixer(p, st, layer, accs, s_c, s_n, s_m, b_i, b_f, g):
    bias = jnp.concatenate([b_i.reshape(-1), b_f.reshape(-1), jnp.zeros((LANES - 4 * H_C,), _f32)])
    bias = jnp.pad(bias[None, :], ((0, MOD_ROWS - 1), (0, 0)))
    hf, hb, *accs = _mlstm_scan(p, bias, s_c, s_n[..., None, :], s_m[..., None, None], st, accs, layer)
    return _odd_post(hf, hb, p, g.reshape(1, V_C)), tuple(accs)


def _moe(h, logits, w1, w3, w2, layer, tm=FFN_TM):
    n_tok = h.shape[0]
    n_pairs = n_tok * TOP_K
    n_tiles = n_pairs // tm + N_EXPERTS
    top_v, top_i = lax.top_k(logits[:, :N_EXPERTS], TOP_K)
    gates = jax.nn.softmax(top_v, axis=-1)
    pair_e = top_i.reshape(-1)
    order = jnp.argsort(pair_e, stable=True)
    one_hot = jax.nn.one_hot(pair_e, N_EXPERTS, dtype=jnp.int32)
    counts = jnp.sum(one_hot, axis=0)
    tiles_per_e = (counts + tm - 1) // tm
    tile_start = jnp.cumsum(tiles_per_e) - tiles_per_e
    group_start = jnp.cumsum(counts) - counts
    rank = jnp.sum((jnp.cumsum(one_hot, axis=0) - one_hot) * one_hot, axis=1)
    pair_slot = (jnp.sum(one_hot * tile_start[None, :], axis=1) * tm + rank).astype(jnp.int32)
    n_valid = jnp.sum(tiles_per_e)
    tile_ids = jnp.arange(n_tiles, dtype=jnp.int32)
    tile_e = jnp.sum((tile_ids[:, None] >= tile_start[None, :]).astype(jnp.int32), axis=1) - 1
    last_e = jnp.max(jnp.where(counts > 0, jnp.arange(N_EXPERTS), 0))
    tile_valid = tile_ids < n_valid
    tile_e = jnp.where(tile_valid, tile_e, last_e).astype(jnp.int32)
    tile_off = (tile_ids - tile_start[tile_e]) * tm
    tile_rows = jnp.where(tile_valid, jnp.clip(counts[tile_e] - tile_off, 0, tm), 0).astype(jnp.int32)
    tile_base = jnp.where(tile_valid, group_start[tile_e] + tile_off, 0).astype(jnp.int32)
    pair_token = (order // TOP_K).astype(jnp.int32)
    yg = _gathered_ffn(h, pair_token, w1, w3, w2, layer, tile_e, tile_rows, tile_base, tm=tm)
    return _moe_combine(yg, pair_slot, gates)


def kernel(x_prompt, x_sample, state_rwkv, state_gla, state_mlstm_c, state_mlstm_n, state_mlstm_m, c, c_ctx, mod_w, mod_b, norm_g, final_g, ev_w_in, ev_mu, rw_w0, rw_w2, rw_a0, rw_a2, rw_g2, rw_kk, rw_ka, rw_rk, rw_ln_g, rw_ln_b, gla_w2, gla_b, gla_norm_g, ev_w_out, ffn_w1, ffn_w3, ffn_w2, od_w_in, ml_b_i, ml_b_f, ml_norm_g, od_w_out, moe_router, moe_w1, moe_w3, moe_w2):
    D = D_MODEL
    Bp, Tp, _ = x_prompt.shape
    Bs, Ts, _ = x_sample.shape
    assert Ts // GRID_W * GRID_W == Ts and GRID_W == CHUNK and Tp % CHUNK == 0
    n_ctx = Bp * Tp
    st = _Streams(n_ctx_chunks=n_ctx // CHUNK, ctx_cps=Tp // CHUNK, lat_cps=Ts // CHUNK,
                  n_chunks=(n_ctx + Bs * Ts) // CHUNK, n_ctx_seq=Bp)
    x = (x_prompt.reshape(n_ctx, D), x_sample.reshape(Bs * Ts, D))

    cond = jnp.concatenate([c_ctx[None, :], c, jnp.zeros((MOD_ROWS - 1 - Bs, D), _f32)], axis=0)
    mod = _mod_table(jax.nn.silu(cond).astype(_bf16), mod_w, mod_b)

    n_even, n_odd = (DEPTH + 1) // 2, DEPTH // 2
    acc_rw = jnp.zeros((Bp, n_even, 2, H_A, HD_A, HD_A), _f32)
    acc_gl = jnp.zeros((Bp, n_even, 2, H_B, DK_B, DV_B), _f32)
    acc_ml = (jnp.zeros((Bp, n_odd, 2, H_C, DK_C, DV_C), _f32), jnp.zeros((Bp, n_odd, 2, H_C, 1, DK_C), _f32),
              jnp.zeros((Bp, n_odd, 2, H_C, 1, 1), _f32))
    pending = None

    b0 = A_IN
    ev_w_all = jnp.concatenate([
        ev_w_in[..., :EV_MAIN], ev_w_in[..., b0 + 2 * QK_B:b0 + 2 * QK_B + V_B],
        ev_w_in[..., b0 + 2 * QK_B + V_B + 2 * LORA_GK:], ev_w_in[..., b0:b0 + 2 * QK_B], ev_w_in[..., EV_MAIN:A_IN],
        ev_w_in[..., b0 + 2 * QK_B + V_B:b0 + 2 * QK_B + V_B + 2 * LORA_GK],
        jnp.zeros(ev_w_in.shape[:2] + (EV_LORA_W - (A_IN - EV_MAIN) - 2 * LORA_GK,), _f32)], axis=-1).astype(_bf16)
    od_w_all = jnp.pad(od_w_in, ((0, 0), (0, 0), (0, OD_N - od_w_in.shape[2]))).astype(_bf16)
    ev_w_out_b, od_w_out_b = ev_w_out.astype(_bf16), od_w_out.astype(_bf16)

    for l in range(DEPTH):
        i = l // 2
        sh1, sc1, gt1, sh2, sc2, gt2 = [mod[l, :, j * D:(j + 1) * D] for j in range(6)]
        if pending is None:
            h = _norm_mod(x, norm_g[l, 0], sc1, sh1, st)[0]
        else:
            x, h = _norm_mod(x, norm_g[l, 0], sc1, sh1, st, res=pending)
        if l % 2 == 0:
            p = _matmul(h, ev_w_all[i], tn=EV_N // 4)
            y_in, acc_rw, acc_gl = _even_mixer(
                p, st, i, acc_rw, acc_gl, state_rwkv[:, i], state_gla[:, i], ev_mu[i], rw_w0[i], rw_w2[i],
                rw_a0[i], rw_a2[i], rw_g2[i], rw_kk[i], rw_ka[i], rw_rk[i], rw_ln_g[i], rw_ln_b[i],
                gla_w2[i], gla_b[i], gla_norm_g[i])
            x, h = _matmul_residual_norm(y_in, ev_w_out_b[i], x, gt1, norm_g[l, 1], sc2, sh2, st)
            f = _dense_ffn(h, ffn_w1, ffn_w3, ffn_w2, i)
        else:
            p = _matmul(h, od_w_all[i], tn=OD_N // 7)
            y_in, acc_ml = _odd_mixer(
                p, st, i, acc_ml, state_mlstm_c[:, i], state_mlstm_n[:, i], state_mlstm_m[:, i],
                ml_b_i[i], ml_b_f[i], ml_norm_g[i])
            router = _split2(jnp.pad(moe_router[i], ((0, 0), (0, LANES - N_EXPERTS))))
            x, h, logits = _matmul_residual_norm(y_in, od_w_out_b[i], x, gt1, norm_g[l, 1], sc2, sh2, st,
                                                 router=router, h_dtype=_f32)
            f = _moe(h, logits, moe_w1, moe_w3, moe_w2, i)
        pending = (f, gt2)

    y_ctx, y_lat = _final_norm(x, pending[0], pending[1], final_g, st)
    return (y_ctx.reshape(Bp, Tp, D), y_lat.reshape(Bs, Ts, D), acc_rw, acc_gl,
            acc_ml[0], acc_ml[1][..., 0, :], acc_ml[2][..., 0, 0])
```

```python
import functools
from typing import NamedTuple

import jax
import jax.numpy as jnp
import numpy as np
from jax import lax
from jax.experimental import pallas as pl
from jax.experimental.pallas import tpu as pltpu

D_MODEL = 2048
DEPTH = 4
GRID_W = 64
EPS = 1e-6
CHUNK = 64
H_A, HD_A = 16, 64
D_A = H_A * HD_A
LORA_W, LORA_A, LORA_G = 64, 64, 128
A_IN = 3 * D_A + 2 * LORA_W + 2 * LORA_A + LORA_G
H_B, DK_B, DV_B = 4, 128, 256
QK_B, V_B = H_B * DK_B, H_B * DV_B
LORA_GK = 16
GATE_NORM = 16.0
H_C, DK_C, DV_C = 8, 128, 256
QK_C, V_C = H_C * DK_C, H_C * DV_C
D_FF = 5632
N_EXPERTS = 8
TOP_K = 2

LANES = 128
MOD_ROWS = 8
VMEM_LIMIT = 48 * 1024 * 1024
FFN_VMEM_LIMIT = 56 * 1024 * 1024
NEG_BIG = -1e30

EV_MAIN = 3 * D_A
EV_V_BLK = EV_MAIN // V_B
EV_OG_BLK = EV_V_BLK + 1
EV_Q_BLK = (EV_MAIN + 2 * V_B) // QK_B
EV_K_BLK = EV_Q_BLK + 1
EV_LORA_W = 512
EV_LORA_BLK = (EV_MAIN + 2 * V_B + 2 * QK_B) // EV_LORA_W
EV_N = EV_MAIN + 2 * V_B + 2 * QK_B + EV_LORA_W
GLA_LORA_OFF = 2 * LORA_W + 2 * LORA_A + LORA_G
OD_GATE_BLK = (2 * QK_C + 2 * V_C) // LANES
OD_N = 2 * QK_C + 2 * V_C + LANES

_f32 = jnp.float32
_bf16 = jnp.bfloat16


class _Streams(NamedTuple):
    n_ctx_chunks: int
    ctx_cps: int
    lat_cps: int
    n_chunks: int
    n_ctx_seq: int


def _chunk_pos(s, st):
    is_lat = s >= st.n_ctx_chunks
    sl = s - st.n_ctx_chunks
    nc = jnp.where(is_lat, st.lat_cps, st.ctx_cps)
    c = jnp.where(is_lat, sl % st.lat_cps, s % st.ctx_cps)
    seq = jnp.where(is_lat, st.n_ctx_seq + sl // st.lat_cps, s // st.ctx_cps)
    return is_lat, nc, c, seq


def _bwd_chunk(s, st):
    _, nc, c, _ = _chunk_pos(s, st)
    return s - c + (nc - 1 - c)


def _seq_of(s, st):
    return _chunk_pos(s, st)[3]


def _mod_row_of_chunk(s, st):
    is_lat, _, _, seq = _chunk_pos(s, st)
    return jnp.where(is_lat, 1 + seq - st.n_ctx_seq, 0)


def _cparams(n_axes):
    return pltpu.CompilerParams(dimension_semantics=("arbitrary",) * n_axes,
                                vmem_limit_bytes=VMEM_LIMIT)


def _dot(a, b):
    return jnp.dot(a.astype(_bf16), b.astype(_bf16), preferred_element_type=_f32)


def _dot_nt(a, b):
    return lax.dot_general(a.astype(_bf16), b.astype(_bf16), (((1,), (1,)), ((), ())),
                           preferred_element_type=_f32)


def _dot_tn(a, b):
    return lax.dot_general(a.astype(_bf16), b.astype(_bf16), (((0,), (0,)), ((), ())),
                           preferred_element_type=_f32)


def _split2(x):
    hi = x.astype(_bf16)
    return hi, (x - hi.astype(_f32)).astype(_bf16)


def _dot_exact_lhs(tri, x):
    hi, lo = _split2(x)
    t = tri.astype(_bf16)
    return jnp.dot(t, hi, preferred_element_type=_f32) + jnp.dot(t, lo, preferred_element_type=_f32)


def _dot_exact_rhs(x, ones):
    hi, lo = _split2(x)
    return jnp.dot(hi, ones, preferred_element_type=_f32) + jnp.dot(lo, ones, preferred_element_type=_f32)


def _head_sum(x, seg, seg_t):
    return _dot_exact_rhs(_dot_exact_rhs(x, seg), seg_t)


def _dot3(a, b_hi, b_lo):
    a_hi, a_lo = _split2(a)
    d = lambda u, v: jnp.dot(u, v, preferred_element_type=_f32)
    return d(a_hi, b_hi) + d(a_lo, b_hi) + d(a_hi, b_lo)


def _tri_masks(direction, n=CHUNK, period=CHUNK):
    t = lax.broadcasted_iota(jnp.int32, (n, n), 0)
    i = lax.broadcasted_iota(jnp.int32, (n, n), 1)
    same = (t // period) == (i // period) if n != period else True
    if direction == 0:
        return (i <= t) & same, (i < t) & same
    return (i >= t) & same, (i > t) & same


def _mod_kernel(a_ref, w_ref, b_ref, o_ref):
    o_ref[...] = _dot(a_ref[...], w_ref[...]) + b_ref[...]


def _mod_table(silu_rows, mod_w, mod_b, tn=1024):
    n_l, k, n = mod_w.shape
    return pl.pallas_call(
        _mod_kernel,
        grid=(n_l, n // tn),
        in_specs=[pl.BlockSpec((MOD_ROWS, k), lambda l, j: (0, 0)),
                  pl.BlockSpec((None, k, tn), lambda l, j: (l, 0, j)),
                  pl.BlockSpec((None, 1, tn), lambda l, j: (l, 0, j))],
        out_specs=pl.BlockSpec((None, MOD_ROWS, tn), lambda l, j: (l, 0, j)),
        out_shape=jax.ShapeDtypeStruct((n_l, MOD_ROWS, n), _f32),
        compiler_params=_cparams(2), name="mod_table",
    )(silu_rows, mod_w, mod_b.reshape(n_l, 1, n))


def _stream_specs(tm, d, st):
    n_ctx_blocks = st.n_ctx_chunks * CHUNK // tm
    return [pl.BlockSpec((tm, d), lambda i: (jnp.minimum(i, n_ctx_blocks - 1), 0)),
            pl.BlockSpec((tm, d), lambda i: (jnp.maximum(i - n_ctx_blocks, 0), 0))]


def _read_stream(i, tm, st, ctx_ref, lat_ref):
    return jnp.where(i * tm < st.n_ctx_chunks * CHUNK, ctx_ref[...], lat_ref[...])


def _norm_mod_kernel(*refs, tm, st, has_res, has_router, split_x):
    refs = list(refs)
    if split_x:
        x = _read_stream(pl.program_id(0), tm, st, refs.pop(0), refs.pop(0))
    else:
        x = refs.pop(0)[...]
    f_ref, gate_ref = (refs.pop(0), refs.pop(0)) if has_res else (None, None)
    g_ref, sc_ref, sh_ref = refs.pop(0), refs.pop(0), refs.pop(0)
    r_hi, r_lo = (refs.pop(0), refs.pop(0)) if has_router else (None, None)
    xo_ref = refs.pop(0) if has_res else None
    h_ref = refs.pop(0)
    row = _mod_row_of_chunk(pl.program_id(0) * (tm // CHUNK), st)
    if has_res:
        x = x + gate_ref[pl.ds(row, 1), :] * f_ref[...]
        xo_ref[...] = x
    inv = lax.rsqrt(jnp.mean(x * x, axis=-1, keepdims=True) + EPS)
    h = x * inv * g_ref[...] * (1.0 + sc_ref[pl.ds(row, 1), :]) + sh_ref[pl.ds(row, 1), :]
    h_ref[...] = h.astype(h_ref.dtype)
    if has_router:
        refs.pop(0)[...] = _dot3(h, r_hi[...], r_lo[...])


def _norm_mod(x, g, sc, sh, st, res=None, router=None, h_dtype=_bf16, tm=256):
    split_x = isinstance(x, tuple)
    d = x[0].shape[1] if split_x else x.shape[1]
    m = x[0].shape[0] + x[1].shape[0] if split_x else x.shape[0]
    row = pl.BlockSpec((tm, d), lambda i: (i, 0))
    tab = pl.BlockSpec((MOD_ROWS, d), lambda i: (0, 0))
    args, specs = (list(x), _stream_specs(tm, d, st)) if split_x else ([x], [row])
    outs, out_specs = [], []
    if res is not None:
        args += list(res)
        specs += [row, tab]
        outs.append(jax.ShapeDtypeStruct((m, d), _f32))
        out_specs.append(row)
    args += [g.reshape(1, d), sc, sh]
    specs += [pl.BlockSpec((1, d), lambda i: (0, 0)), tab, tab]
    outs.append(jax.ShapeDtypeStruct((m, d), h_dtype))
    out_specs.append(row)
    if router is not None:
        args += list(router)
        specs += [pl.BlockSpec((d, LANES), lambda i: (0, 0))] * 2
        outs.append(jax.ShapeDtypeStruct((m, LANES), _f32))
        out_specs.append(pl.BlockSpec((tm, LANES), lambda i: (i, 0)))
    return pl.pallas_call(
        functools.partial(_norm_mod_kernel, tm=tm, st=st, has_res=res is not None, has_router=router is not None,
                          split_x=split_x),
        grid=(m // tm,), in_specs=specs, out_specs=out_specs, out_shape=outs,
        compiler_params=_cparams(1), name="norm_mod",
    )(*args)


def _final_norm_kernel(x_ref, f_ref, gate_ref, g_ref, ctx_ref, lat_ref, *, tm, st):
    i = pl.program_id(0)
    row = _mod_row_of_chunk(i * (tm // CHUNK), st)
    x = x_ref[...] + gate_ref[pl.ds(row, 1), :] * f_ref[...]
    y = x * lax.rsqrt(jnp.mean(x * x, axis=-1, keepdims=True) + EPS) * g_ref[...]
    is_ctx = i * tm < st.n_ctx_chunks * CHUNK

    @pl.when(is_ctx)
    def _():
        ctx_ref[...] = y

    @pl.when(jnp.logical_not(is_ctx))
    def _():
        lat_ref[...] = y


def _final_norm(x, f, gate, g, st, tm=256):
    m, d = x.shape
    n_ctx = st.n_ctx_chunks * CHUNK
    row = pl.BlockSpec((tm, d), lambda i: (i, 0))
    return pl.pallas_call(
        functools.partial(_final_norm_kernel, tm=tm, st=st),
        grid=(m // tm,),
        in_specs=[row, row, pl.BlockSpec((MOD_ROWS, d), lambda i: (0, 0)), pl.BlockSpec((1, d), lambda i: (0, 0))],
        out_specs=_stream_specs(tm, d, st),
        out_shape=[jax.ShapeDtypeStruct((n_ctx, d), _f32), jax.ShapeDtypeStruct((m - n_ctx, d), _f32)],
        compiler_params=_cparams(1), name="final_norm",
    )(x, f, gate, g.reshape(1, d))


def _mm_kernel(a_ref, w_ref, o_ref):
    o_ref[...] = _dot(a_ref[...], w_ref[...])


def _mm_res_norm_kernel(*refs, tm, st, has_router, split_res):
    refs = list(refs)
    a_ref, w_ref = refs.pop(0), refs.pop(0)
    if split_res:
        res = _read_stream(pl.program_id(0), tm, st, refs.pop(0), refs.pop(0))
    else:
        res = refs.pop(0)[...]
    gate_ref, g_ref, sc_ref, sh_ref = refs.pop(0), refs.pop(0), refs.pop(0), refs.pop(0)
    r_hi, r_lo = (refs.pop(0), refs.pop(0)) if has_router else (None, None)
    outs = refs
    row = _mod_row_of_chunk(pl.program_id(0) * (tm // CHUNK), st)
    x = res + gate_ref[pl.ds(row, 1), :] * _dot(a_ref[...], w_ref[...])
    outs[0][...] = x
    inv = lax.rsqrt(jnp.mean(x * x, axis=-1, keepdims=True) + EPS)
    h = x * inv * g_ref[...] * (1.0 + sc_ref[pl.ds(row, 1), :]) + sh_ref[pl.ds(row, 1), :]
    outs[1][...] = h.astype(outs[1].dtype)
    if has_router:
        outs[2][...] = _dot3(h, r_hi[...], r_lo[...])


def _matmul(a, w, tm=512, tn=512):
    m, k = a.shape
    n = w.shape[1]
    tm = min(tm, m)
    return pl.pallas_call(
        _mm_kernel,
        grid=(n // tn, m // tm),
        in_specs=[pl.BlockSpec((tm, k), lambda j, i: (i, 0)),
                  pl.BlockSpec((k, tn), lambda j, i: (0, j))],
        out_specs=pl.BlockSpec((tm, tn), lambda j, i: (i, j)),
        out_shape=jax.ShapeDtypeStruct((m, n), _f32),
        compiler_params=_cparams(2), name="matmul",
    )(a, w)


def _matmul_residual_norm(a, w, res, gate, g, sc, sh, st, router=None, h_dtype=_bf16, tm=256):
    m, k = a.shape
    n = w.shape[1]
    split_res = isinstance(res, tuple)
    row = lambda width: pl.BlockSpec((tm, width), lambda i: (i, 0))
    tab = pl.BlockSpec((MOD_ROWS, n), lambda i: (0, 0))
    args = [a, w] + (list(res) if split_res else [res]) + [gate, g.reshape(1, n), sc, sh]
    specs = ([row(k), pl.BlockSpec((k, n), lambda i: (0, 0))] + (_stream_specs(tm, n, st) if split_res else [row(n)])
             + [tab, pl.BlockSpec((1, n), lambda i: (0, 0)), tab, tab])
    outs = [jax.ShapeDtypeStruct((m, n), _f32), jax.ShapeDtypeStruct((m, n), h_dtype)]
    out_specs = [row(n), row(n)]
    if router is not None:
        args += list(router)
        specs += [pl.BlockSpec((n, LANES), lambda i: (0, 0))] * 2
        outs.append(jax.ShapeDtypeStruct((m, LANES), _f32))
        out_specs.append(row(LANES))
    return pl.pallas_call(
        functools.partial(_mm_res_norm_kernel, tm=tm, st=st, has_router=router is not None, split_res=split_res),
        grid=(m // tm,), in_specs=specs, out_specs=out_specs, out_shape=outs,
        compiler_params=_cparams(1), name="matmul_residual_norm",
    )(*args)


FFN_TM = 1024
FFN_TF = 256
FFN_ROW_STEP = 256


def _swiglu_accumulate(x_ref, w1_ref, w3_ref, w2_ref, o_ref, n_rows):
    x = x_ref[0:n_rows, :]
    a = _dot(x, w1_ref[...])
    b = _dot(x, w3_ref[...])
    o_ref[0:n_rows, :] += _dot(a * jax.nn.sigmoid(a) * b, w2_ref[...])


def _ffn_kernel(x_ref, w1_ref, w3_ref, w2_ref, o_ref, *, tm):
    @pl.when(pl.program_id(1) == 0)
    def _():
        o_ref[...] = jnp.zeros_like(o_ref)

    _swiglu_accumulate(x_ref, w1_ref, w3_ref, w2_ref, o_ref, tm)


def _row_copy(src_hbm, src_row, dst, dst_row, sem):
    return pltpu.make_async_copy(src_hbm.at[pl.ds(src_row, 1), :], dst.at[pl.ds(dst_row, 1), :], sem)


ROW_DMA_UNROLL = 8


def _for_each_row(n, body):
    groups = n // ROW_DMA_UNROLL

    def group(g, carry):
        r0 = pl.multiple_of(g * ROW_DMA_UNROLL, ROW_DMA_UNROLL)
        for u in range(ROW_DMA_UNROLL):
            body(r0 + u)
        return carry

    def single(r, carry):
        body(r)
        return carry

    lax.fori_loop(0, groups, group, 0)
    lax.fori_loop(groups * ROW_DMA_UNROLL, n, single, 0)


def _ffn_gather_kernel(te_ref, tr_ref, tb_ref, tok_ref, h_hbm, w1_ref, w3_ref, w2_ref, o_ref, xf_scr, xb_scr, sem,
                       *, tm, n_tiles):
    i, j = pl.program_id(0), pl.program_id(1)
    rows = tr_ref[i]

    def start_gather(tile):
        base = tb_ref[tile]
        _for_each_row(tr_ref[tile], lambda r: _row_copy(h_hbm, tok_ref[base + r], xf_scr, r, sem).start())

    def wait_gather(tile):
        @pl.when(tr_ref[tile] == tm)
        def _():
            pltpu.make_async_copy(h_hbm.at[pl.ds(0, tm), :], xf_scr, sem).wait()

        @pl.when(tr_ref[tile] < tm)
        def _():
            _for_each_row(tr_ref[tile], lambda r: _row_copy(h_hbm, 0, xf_scr, r, sem).wait())

    @pl.when(j == 0)
    def _():
        @pl.when(i == 0)
        def _():
            xf_scr[...] = jnp.zeros_like(xf_scr)
            start_gather(0)

        wait_gather(i)
        xb_scr[...] = xf_scr[...].astype(_bf16)
        o_ref[...] = jnp.zeros_like(o_ref)

        @pl.when(i + 1 < n_tiles)
        def _():
            start_gather(i + 1)

    for n_rows in range(FFN_ROW_STEP, tm + 1, FFN_ROW_STEP):
        @pl.when((rows > n_rows - FFN_ROW_STEP) & (rows <= n_rows))
        def _():
            _swiglu_accumulate(xb_scr, w1_ref, w3_ref, w2_ref, o_ref, n_rows)


def _gathered_ffn(h, pair_token, w1, w3, w2, layer, tile_expert, tile_rows, tile_base, tm=FFN_TM, tf=FFN_TF):
    d = h.shape[1]
    f = w1.shape[3]
    nj = f // tf
    n_tiles = tile_expert.shape[0]

    def jeff(i, j, tr):
        return jnp.where(tr[i] > 0, j, nj - 1)

    return pl.pallas_call(
        functools.partial(_ffn_gather_kernel, tm=tm, n_tiles=n_tiles),
        grid_spec=pltpu.PrefetchScalarGridSpec(
            num_scalar_prefetch=4,
            grid=(n_tiles, nj),
            in_specs=[pl.BlockSpec(memory_space=pl.ANY),
                      pl.BlockSpec((None, None, d, tf), lambda i, j, te, tr, *_: (layer, te[i], 0, jeff(i, j, tr))),
                      pl.BlockSpec((None, None, d, tf), lambda i, j, te, tr, *_: (layer, te[i], 0, jeff(i, j, tr))),
                      pl.BlockSpec((None, None, tf, d), lambda i, j, te, tr, *_: (layer, te[i], jeff(i, j, tr), 0))],
            out_specs=pl.BlockSpec((tm, d), lambda i, j, *_: (i, 0)),
            scratch_shapes=[pltpu.VMEM((tm, d), _f32), pltpu.VMEM((tm, d), _bf16), pltpu.SemaphoreType.DMA]),
        out_shape=jax.ShapeDtypeStruct((n_tiles * tm, d), _f32),
        compiler_params=pltpu.CompilerParams(dimension_semantics=("arbitrary", "arbitrary"),
                                             vmem_limit_bytes=FFN_VMEM_LIMIT),
        name="gathered_ffn",
    )(tile_expert, tile_rows, tile_base, pair_token, h, w1, w3, w2)


COMBINE_TM = 256


def _combine_kernel(slot_ref, yg_hbm, gates_ref, o_ref, buf, sem, *, tm):
    base = pl.program_id(0) * tm * TOP_K

    def start(t):
        for k in range(TOP_K):
            _row_copy(yg_hbm, slot_ref[base + t * TOP_K + k], buf.at[k], t, sem).start()

    _for_each_row(tm, start)
    for k in range(TOP_K):
        pltpu.make_async_copy(yg_hbm.at[pl.ds(0, tm), :], buf.at[k], sem).wait()
    g = gates_ref[...]
    o_ref[...] = g[:, 0:1] * buf[0] + g[:, 1:2] * buf[1]


def _moe_combine(yg, pair_slot, gates, tm=COMBINE_TM):
    n_tok = gates.shape[0]
    d = yg.shape[1]
    return pl.pallas_call(
        functools.partial(_combine_kernel, tm=tm),
        grid_spec=pltpu.PrefetchScalarGridSpec(
            num_scalar_prefetch=1,
            grid=(n_tok // tm,),
            in_specs=[pl.BlockSpec(memory_space=pl.ANY),
                      pl.BlockSpec((tm, TOP_K), lambda b, sl: (b, 0))],
            out_specs=pl.BlockSpec((tm, d), lambda b, sl: (b, 0)),
            scratch_shapes=[pltpu.VMEM((TOP_K, tm, d), _f32), pltpu.SemaphoreType.DMA]),
        out_shape=jax.ShapeDtypeStruct((n_tok, d), _f32),
        compiler_params=_cparams(1), name="moe_combine",
    )(pair_slot, yg, gates)


def _dense_ffn(x, w1, w3, w2, layer, tm=FFN_TM, tf=FFN_TF):
    m, d = x.shape
    f = w1.shape[2]
    return pl.pallas_call(
        functools.partial(_ffn_kernel, tm=tm),
        grid=(m // tm, f // tf),
        in_specs=[pl.BlockSpec((tm, d), lambda i, j: (i, 0)),
                  pl.BlockSpec((None, d, tf), lambda i, j: (layer, 0, j)),
                  pl.BlockSpec((None, d, tf), lambda i, j: (layer, 0, j)),
                  pl.BlockSpec((None, tf, d), lambda i, j: (layer, j, 0))],
        out_specs=pl.BlockSpec((tm, d), lambda i, j: (i, 0)),
        out_shape=jax.ShapeDtypeStruct((m, d), _f32),
        compiler_params=pltpu.CompilerParams(dimension_semantics=("arbitrary", "arbitrary"),
                                             vmem_limit_bytes=FFN_VMEM_LIMIT),
        name="dense_ffn",
    )(x, w1, w3, w2)


PREP_CHUNKS = 4
PREP_ROWS = PREP_CHUNKS * CHUNK


def _token_shift_block(x, prev, nxt, lane0, is_lat, first, last):
    rows, n = x.shape
    row = lax.broadcasted_iota(jnp.int32, x.shape, 0)
    ch = lax.broadcasted_iota(jnp.int32, x.shape, 1) + lane0
    down = pltpu.roll(x, 1, 0)
    up = pltpu.roll(x, rows - 1, 0)

    def context():
        zero_row = jnp.zeros((1, n), _f32)
        row_prev = jnp.where(first, zero_row, prev[CHUNK - 1:CHUNK, :])
        row_next = jnp.where(last, zero_row, nxt[0:1, :])
        return jnp.where(ch < A_IN // 2,
                         jnp.where(row == 0, row_prev, down),
                         jnp.where(row == rows - 1, row_next, up))

    def latent():
        q = A_IN // 4
        col = row % GRID_W
        above = jnp.concatenate([jnp.where(first, 0.0, prev), x[:rows - CHUNK]], axis=0)
        below = jnp.concatenate([x[CHUNK:], jnp.where(last, 0.0, nxt)], axis=0)
        return jnp.where(ch < q, jnp.where(col == 0, 0.0, down),
                         jnp.where(ch < 2 * q, jnp.where(col == GRID_W - 1, 0.0, up),
                                   jnp.where(ch < 3 * q, above, below)))

    return lax.cond(is_lat, latent, context)


def _rwkv_prep_kernel(xm_ref, pm_ref, nm_ref, xl_ref, pl_ref, nl_ref, mum_ref, mul_ref,
                      ww_ref, wa_ref, wg_ref, par_ref, seg_ref, segt_ref,
                      rt_ref, kt_ref, at_ref, bt_ref, kh_ref, bh_ref, v_ref, plast_ref, bonus_ref, g_ref,
                      *, st):
    s = pl.program_id(0) * PREP_CHUNKS
    is_lat, nc, c, _ = _chunk_pos(s, st)
    first, last = c == 0, c + PREP_CHUNKS == nc
    xm, xl = xm_ref[...], xl_ref[...]
    sm = _token_shift_block(xm, pm_ref[...], nm_ref[...], 0, is_lat, first, last)
    sl = _token_shift_block(xl, pl_ref[...], nl_ref[...], EV_MAIN, is_lat, first, last)
    pm = xm + mum_ref[...] * (sm - xm)
    lora = xl + mul_ref[...] * (sl - xl)
    r, k, v = pm[:, :D_A], pm[:, D_A:2 * D_A], pm[:, 2 * D_A:]
    par = par_ref[...]
    k_k, k_a, r_k = par[4:5], par[5:6], par[6:7]
    seg, seg_t = seg_ref[...], segt_ref[...]
    kk = k * k_k
    kk = kk / jnp.maximum(jnp.sqrt(_head_sum(kk * kk, seg, seg_t)), 1e-12)
    g_ref[...] = _dot(jax.nn.sigmoid(lora[:, 2 * LORA_W + 2 * LORA_A:GLA_LORA_OFF]), wg_ref[...])
    v_ref[...] = v.astype(_bf16)
    tanh_w = jnp.tanh(lora[:, :2 * LORA_W])
    lora_a = lora[:, 2 * LORA_W:2 * LORA_W + 2 * LORA_A]
    kdir_sum = jnp.zeros_like(k)
    t = lax.broadcasted_iota(jnp.int32, (PREP_ROWS, PREP_ROWS), 0)
    i = lax.broadcasted_iota(jnp.int32, (PREP_ROWS, PREP_ROWS), 1)
    same_chunk = jnp.where((t // CHUNK) == (i // CHUNK), 1.0, 0.0)
    for d in range(2):
        w = -jax.nn.softplus(-(par[d:d + 1] + _dot3(tanh_w, ww_ref[d, 0], ww_ref[d, 1]))) - 0.5
        ld = -jnp.exp(w)
        a = jax.nn.sigmoid(par[2 + d:3 + d] + _dot3(lora_a, wa_ref[d, 0], wa_ref[d, 1]))
        kdir = k * (1.0 + (a - 1.0) * k_a)
        kdir_sum = kdir_sum + kdir
        incl, _ = _tri_masks(d, PREP_ROWS, CHUNK)
        cum = _dot_exact_lhs(jnp.where(incl, 1.0, 0.0), ld)
        tot = _dot_exact_lhs(same_chunk, ld)
        e_pos, e_neg, e_end = jnp.exp(cum), jnp.exp(-cum), jnp.exp(tot - cum)
        kka = kk * a
        rt_ref[d] = (r * e_pos).astype(_bf16)
        kt_ref[d] = (kdir * e_neg).astype(_bf16)
        at_ref[d] = (kk * jnp.exp(cum - ld)).astype(_bf16)
        bt_ref[d] = (kka * e_neg).astype(_bf16)
        kh_ref[d] = (kdir * e_end).astype(_bf16)
        bh_ref[d] = (kka * e_end).astype(_bf16)
        e_tot = jnp.exp(tot)
        for j in range(PREP_CHUNKS):
            plast_ref[d, j] = e_tot[j * CHUNK:j * CHUNK + 1, :]
    bonus_ref[...] = _head_sum(r * kdir_sum * r_k, seg, seg_t) * v


def _rwkv_prep(p, st, mu_main, mu_lora, ww, wa, wg, par, seg, seg_t):
    n = p.shape[0]
    nch = st.n_chunks
    assert st.ctx_cps % PREP_CHUNKS == 0 and st.lat_cps % PREP_CHUNKS == 0
    cur = lambda b: (b, 0)
    prv = lambda b: (jnp.maximum(b * PREP_CHUNKS - 1, 0), 0)
    nxt = lambda b: (jnp.minimum((b + 1) * PREP_CHUNKS, nch - 1), 0)
    lo = lambda f: (lambda b: (f(b)[0], EV_LORA_BLK))
    main = lambda rows, f: pl.BlockSpec((rows, EV_MAIN), f)
    lora = lambda rows, f: pl.BlockSpec((rows, EV_LORA_W), lo(f))
    full = lambda a: pl.BlockSpec(a.shape, lambda b: (0,) * a.ndim)
    seq2 = pl.BlockSpec((2, PREP_ROWS, D_A), lambda b: (0, b, 0))
    seq1 = pl.BlockSpec((PREP_ROWS, D_A), lambda b: (b, 0))
    bf2 = jax.ShapeDtypeStruct((2, n, D_A), _bf16)
    return pl.pallas_call(
        functools.partial(_rwkv_prep_kernel, st=st),
        grid=(nch // PREP_CHUNKS,),
        in_specs=[main(PREP_ROWS, cur), main(CHUNK, prv), main(CHUNK, nxt),
                  lora(PREP_ROWS, cur), lora(CHUNK, prv), lora(CHUNK, nxt),
                  full(mu_main), full(mu_lora), full(ww), full(wa), full(wg), full(par), full(seg), full(seg_t)],
        out_specs=[seq2] * 6 + [seq1, pl.BlockSpec((2, PREP_CHUNKS, 1, D_A), lambda b: (0, b, 0, 0)), seq1, seq1],
        out_shape=[bf2] * 6 + [jax.ShapeDtypeStruct((n, D_A), _bf16),
                               jax.ShapeDtypeStruct((2, nch, 1, D_A), _f32),
                               jax.ShapeDtypeStruct((n, D_A), _f32),
                               jax.ShapeDtypeStruct((n, D_A), _f32)],
        compiler_params=_cparams(1), name="rwkv_prep",
    )(p, p, p, p, p, p, mu_main, mu_lora, ww, wa, wg, par, seg, seg_t)


N_PAIRS = H_A // 2
PAIR = 2 * CHUNK


def _stack_masked(x, keep_first):
    return jnp.concatenate([x * keep_first, x * (1 - keep_first)], axis=0)


def _rwkv_scan_kernel(*refs, st):
    (rt_f, kt_f, at_f, bt_f, kh_f, bh_f, v_f, pl_f,
     rt_b, kt_b, at_b, bt_b, kh_b, bh_b, v_b, pl_b, s0_ref, _acc, yf_ref, yb_ref, sout_ref, s_scr) = refs
    s = pl.program_id(0)
    is_lat, nc, c, _ = _chunk_pos(s, st)

    @pl.when(c == 0)
    def _():
        keep = jnp.where(is_lat, 1.0, 0.0)
        zero = jnp.zeros((HD_A, HD_A), _f32)
        for d in range(2):
            for p in range(N_PAIRS):
                top = jnp.concatenate([s0_ref[d, 2 * p], zero], axis=1)
                bot = jnp.concatenate([zero, s0_ref[d, 2 * p + 1]], axis=1)
                s_scr[d, p] = jnp.concatenate([top, bot], axis=0) * keep

    dirs = ((rt_f, kt_f, at_f, bt_f, kh_f, bh_f, v_f, pl_f, yf_ref),
            (rt_b, kt_b, at_b, bt_b, kh_b, bh_b, v_b, pl_b, yb_ref))
    chains = [(d, p) for d in range(2) for p in range(N_PAIRS)]
    masks = [_tri_masks(d, PAIR, CHUNK) for d in range(2)]
    eye = jnp.where(masks[0][0] & masks[1][0], 1.0, 0.0)

    lane = lax.broadcasted_iota(jnp.int32, (CHUNK, LANES), 1)
    keep_first = jnp.where(lane < HD_A, 1.0, 0.0).astype(_bf16)

    def load(ref, p):
        return _stack_masked(ref[:, p * LANES:(p + 1) * LANES], keep_first)

    ops = {}
    for d, p in chains:
        rt, kt, at, bt, kh, bh, v = (load(dirs[d][j], p) for j in range(7))
        incl, strict = masks[d]
        big = _dot_nt(jnp.concatenate([at, rt], axis=0), jnp.concatenate([kt, bt], axis=0))
        n = jnp.where(strict, big[:PAIR, PAIR:], 0.0)
        ops[d, p] = dict(rt=rt, at=at, bh=bh, kh=kh, v=v, t=eye - n, n=n.astype(_bf16),
                         mak=jnp.where(strict, big[:PAIR, :PAIR], 0.0).astype(_bf16),
                         mrk=jnp.where(incl, big[PAIR:, :PAIR], 0.0).astype(_bf16),
                         mrb=jnp.where(incl, big[PAIR:, PAIR:], 0.0).astype(_bf16))
    for _ in range(5):
        for ch in chains:
            ops[ch]["n"] = _dot(ops[ch]["n"], ops[ch]["n"]).astype(_bf16)
        for ch in chains:
            ops[ch]["t"] = ops[ch]["t"] + _dot(ops[ch]["t"], ops[ch]["n"])
    for ch in chains:
        o = ops[ch]
        o["mv"] = _dot(jnp.concatenate([o.pop("mak"), o.pop("mrk")], axis=0), o["v"])
    for ch in chains:
        o = ops[ch]
        o["xz"] = _dot(o.pop("t"), jnp.concatenate([o["at"], o["mv"][:PAIR].astype(_bf16)], axis=1))
    for ch in chains:
        o = ops[ch]
        o["qy"] = _dot(o.pop("mrb"), o["xz"])
    for ch in chains:
        o = ops[ch]
        o["upd"] = _dot_tn(o["xz"], o["bh"])
        o["vtk"] = _dot_tn(o["v"], o["kh"])
    for d, p in chains:
        o = ops[d, p]
        s0 = s_scr[d, p]
        q_eff = o["rt"].astype(_f32) - o["qy"][:, :LANES]
        y = _dot_nt(q_eff, s0) + o["mv"][PAIR:] - o["qy"][:, LANES:]
        dirs[d][8][:, p * LANES:(p + 1) * LANES] = y[:CHUNK] + y[CHUNK:]
        p_last = dirs[d][7][:, p * LANES:(p + 1) * LANES]
        s_scr[d, p] = s0 * p_last - _dot(s0, o["upd"][:LANES]) + o["vtk"] - o["upd"][LANES:]

    @pl.when((c == nc - 1) & jnp.logical_not(is_lat))
    def _():
        for d in range(2):
            for p in range(N_PAIRS):
                sp = s_scr[d, p]
                sout_ref[d, 2 * p] = sp[:HD_A, :HD_A]
                sout_ref[d, 2 * p + 1] = sp[HD_A:, HD_A:]


def _lat_state_spec(shape, st):
    nd = len(shape)
    return pl.BlockSpec((None,) + tuple(shape[1:]),
                        lambda s: (jnp.maximum(_seq_of(s, st) - st.n_ctx_seq, 0),) + (0,) * (nd - 1))


def _ctx_state_spec(shape, st, layer):
    nd = len(shape)
    return pl.BlockSpec((None, None) + tuple(shape[2:]),
                        lambda s: (jnp.minimum(_seq_of(s, st), st.n_ctx_seq - 1), layer) + (0,) * (nd - 2))


_ANY = pl.BlockSpec(memory_space=pl.ANY)


def _rwkv_scan(prep, s0, st, acc, layer):
    rt, kt, at, bt, kh, bh, v, p_last = prep
    n = v.shape[0]
    fwd = lambda s: s
    bwd = lambda s: _bwd_chunk(s, st)
    seq2 = lambda d, f: pl.BlockSpec((None, CHUNK, D_A), lambda s: (d, f(s), 0))
    seq1 = lambda f: pl.BlockSpec((CHUNK, D_A), lambda s: (f(s), 0))
    pls = lambda d, f: pl.BlockSpec((None, None, 1, D_A), lambda s: (d, f(s), 0, 0))
    side = lambda d, f: [seq2(d, f)] * 6 + [seq1(f), pls(d, f)]
    y_shape = jax.ShapeDtypeStruct((n, D_A), _f32)
    args = (rt, kt, at, bt, kh, bh, v, p_last)
    return pl.pallas_call(
        functools.partial(_rwkv_scan_kernel, st=st),
        grid=(st.n_chunks,),
        in_specs=side(0, fwd) + side(1, bwd) + [_lat_state_spec(s0.shape, st), _ANY],
        out_specs=[seq1(fwd), seq1(bwd), _ctx_state_spec(acc.shape, st, layer)],
        out_shape=[y_shape, y_shape, jax.ShapeDtypeStruct(acc.shape, _f32)],
        input_output_aliases={2 * len(args) + 1: 2},
        scratch_shapes=[pltpu.VMEM((2, N_PAIRS, LANES, LANES), _f32)],
        compiler_params=_cparams(1), name="rwkv_scan",
    )(*args, *args, s0, acc)


def _gla_kernel(q_f, k_f, v_f, l_f, q_b, k_b, v_b, l_b, w_ref, bias_ref, s0_ref, _acc,
                of_ref, ob_ref, sout_ref, s_scr, *, st):
    s = pl.program_id(0)
    is_lat, nc, c, _ = _chunk_pos(s, st)

    @pl.when(c == 0)
    def _():
        keep = jnp.where(is_lat, 1.0, 0.0)
        for d in range(2):
            for h in range(H_B):
                s_scr[d, h] = s0_ref[d, h].T * keep

    dirs = ((q_f, k_f, v_f, l_f, of_ref), (q_b, k_b, v_b, l_b, ob_ref))
    work = []
    for d, (q_ref, k_ref, v_ref, l_ref, o_ref) in enumerate(dirs):
        incl, _ = _tri_masks(d)
        gate_lora = l_ref[:, GLA_LORA_OFF:GLA_LORA_OFF + LANES]
        la = jax.nn.log_sigmoid(_dot3(gate_lora, w_ref[d, 0], w_ref[d, 1]) + bias_ref[d:d + 1]) / GATE_NORM
        cum = _dot_exact_lhs(jnp.where(incl, 1.0, 0.0), la)
        tot = jnp.sum(la, axis=0, keepdims=True)
        k = k_ref[...]
        q_in = q_ref[...] * (DK_B ** -0.5) * jnp.exp(cum)
        k_in = k * jnp.exp(-cum)
        k_end = k * jnp.exp(tot - cum)
        e_tot = jnp.exp(tot)
        for h in range(H_B):
            ks = slice(h * DK_B, (h + 1) * DK_B)
            work.append((d, h, incl, q_in[:, ks], k_in[:, ks], k_end[:, ks], e_tot[:, ks]))
    att = [jnp.where(incl, _dot_nt(q_in, k_in), 0.0) for (_, _, incl, q_in, k_in, _, _) in work]
    for (d, h, _, q_in, _, k_end, e_tot), a in zip(work, att):
        v_ref, o_ref = dirs[d][2], dirs[d][4]
        vs = slice(h * DV_B, (h + 1) * DV_B)
        v = v_ref[:, vs]
        s_t = s_scr[d, h]
        o_ref[:, vs] = _dot(a, v) + _dot_nt(q_in, s_t)
        s_scr[d, h] = s_t * e_tot + _dot_tn(v, k_end)

    @pl.when((c == nc - 1) & jnp.logical_not(is_lat))
    def _():
        for d in range(2):
            for h in range(H_B):
                sout_ref[d, h] = s_scr[d, h].T


def _gla_scan(p, w, bias, s0, st, acc, layer):
    n = p.shape[0]
    fwd = lambda s: s
    bwd = lambda s: _bwd_chunk(s, st)
    blk = lambda width, col, f: pl.BlockSpec((CHUNK, width), lambda s: (f(s), col))
    side = lambda f: [blk(QK_B, EV_Q_BLK, f), blk(QK_B, EV_K_BLK, f), blk(V_B, EV_V_BLK, f),
                      blk(EV_LORA_W, EV_LORA_BLK, f)]
    full = lambda a: pl.BlockSpec(a.shape, lambda s: (0,) * a.ndim)
    o_spec = lambda f: pl.BlockSpec((CHUNK, V_B), lambda s: (f(s), 0))
    o_shape = jax.ShapeDtypeStruct((n, V_B), _f32)
    return pl.pallas_call(
        functools.partial(_gla_kernel, st=st),
        grid=(st.n_chunks,),
        in_specs=side(fwd) + side(bwd) + [full(w), full(bias), _lat_state_spec(s0.shape, st), _ANY],
        out_specs=[o_spec(fwd), o_spec(bwd), _ctx_state_spec(acc.shape, st, layer)],
        out_shape=[o_shape, o_shape, jax.ShapeDtypeStruct(acc.shape, _f32)],
        input_output_aliases={11: 2},
        scratch_shapes=[pltpu.VMEM((2, H_B, DV_B, DK_B), _f32)],
        compiler_params=_cparams(1), name="gla_scan",
    )(p, p, p, p, p, p, p, p, w, bias, s0, acc)


def _mlstm_kernel(q_f, k_f, v_f, g_f, q_b, k_b, v_b, g_b, bias_ref, c0_ref, n0_ref, m0_ref, _acc_c, _acc_n, _acc_m,
                  hf_ref, hb_ref, cout_ref, nout_ref, mout_ref, c_scr, n_scr, m_scr, *, st):
    s = pl.program_id(0)
    is_lat, nc, c, _ = _chunk_pos(s, st)

    @pl.when(c == 0)
    def _():
        keep = jnp.where(is_lat, 1.0, 0.0)
        for d in range(2):
            for h in range(H_C):
                c_scr[d, h] = c0_ref[d, h].T * keep
        n_scr[...] = n0_ref[...] * keep
        m_scr[...] = m0_ref[...] * keep

    L = CHUNK
    lane = lax.broadcasted_iota(jnp.int32, (L, LANES), 1)
    dirs = ((q_f, k_f, v_f, g_f, hf_ref), (q_b, k_b, v_b, g_b, hb_ref))
    work = []
    for d, (q_ref, k_ref, v_ref, g_ref, h_ref) in enumerate(dirs):
        gates = g_ref[...] + bias_ref[0:1]
        gates = jnp.where(lane < 2 * H_C, gates, jax.nn.log_sigmoid(gates))
        gates_t = gates.T
        for h in range(H_C):
            ji, jf = d * H_C + h, 2 * H_C + d * H_C + h
            ks = slice(h * DK_C, (h + 1) * DK_C)
            q = q_ref[:, ks] * (DK_C ** -0.5)
            k = k_ref[:, ks]
            work.append(dict(d=d, h=h, q=q, k=k, incl=_tri_masks(d)[0], incl_t=_tri_masks(1 - d)[0],
                             i_col=gates[:, ji:ji + 1], f_col=gates[:, jf:jf + 1],
                             i_row=gates_t[ji:ji + 1, :], f_row=gates_t[jf:jf + 1, :],
                             m_prev=m_scr[d, h], c_t=c_scr[d, h], n_row=n_scr[d, h]))
    for o in work:
        both = _dot_nt(o["q"], jnp.concatenate([o["c_t"], o["k"]], axis=0))
        o["qc"], o["s"] = both[:, :DV_C], both[:, DV_C:]
    for o in work:
        o["cum_col"] = jnp.sum(jnp.where(o["incl"], o["f_row"], 0.0), axis=1, keepdims=True)
        o["cum_row"] = jnp.sum(jnp.where(o["incl_t"], o["f_col"], 0.0), axis=0, keepdims=True)
        o["f_tot"] = jnp.sum(o["f_row"], axis=1, keepdims=True)
        o["qn"] = jnp.sum(o["q"] * o["n_row"], axis=1, keepdims=True)
    for o in work:
        o["log_d"] = jnp.where(o["incl"], o["cum_col"] - o["cum_row"] + o["i_row"], NEG_BIG)
        o["log_inter"] = o["cum_col"] + o["m_prev"]
        o["row_max"] = jnp.max(o["log_d"], axis=1, keepdims=True)
    for o in work:
        m_t = jnp.maximum(o["log_inter"], o["row_max"])
        o["m_t"] = m_t
        o["sm"] = o["s"] * jnp.exp(o.pop("log_d") - m_t)
        o["w_inter"] = jnp.exp(o["log_inter"] - m_t)
        o["m_new"] = m_t[L - 1:L, :] if o["d"] == 0 else m_t[0:1, :]
        o["kw"] = o["k"] * jnp.exp(o["f_tot"] - o["cum_col"] + o["i_col"] - o["m_new"])
        o["dec"] = jnp.exp(o["f_tot"] + o["m_prev"] - o["m_new"])
    for o in work:
        vs = slice(o["h"] * DV_C, (o["h"] + 1) * DV_C)
        o["v"] = dirs[o["d"]][2][:, vs]
        o["sv"] = _dot(o["sm"], o["v"])
        o["kv"] = _dot_tn(o["v"], o["kw"])
        o["s_sum"] = jnp.sum(o["sm"], axis=1, keepdims=True)
    for o in work:
        d, h = o["d"], o["h"]
        vs = slice(h * DV_C, (h + 1) * DV_C)
        num = o["w_inter"] * o["qc"] + o["sv"]
        den = o["w_inter"] * o["qn"] + o["s_sum"]
        dirs[d][4][:, vs] = num * (1.0 / jnp.maximum(jnp.abs(den), jnp.exp(-o["m_t"])))
        c_scr[d, h] = o["dec"] * o["c_t"] + o["kv"]
        n_scr[d, h] = o["dec"] * o["n_row"] + jnp.sum(o["kw"], axis=0, keepdims=True)
        m_scr[d, h] = o["m_new"]

    @pl.when((c == nc - 1) & jnp.logical_not(is_lat))
    def _():
        for d in range(2):
            for h in range(H_C):
                cout_ref[d, h] = c_scr[d, h].T
        nout_ref[...] = n_scr[...]
        mout_ref[...] = m_scr[...]


def _mlstm_scan(p, bias, c0, n0, m0, st, accs, layer):
    n = p.shape[0]
    fwd = lambda s: s
    bwd = lambda s: _bwd_chunk(s, st)
    blk = lambda width, col, f: pl.BlockSpec((CHUNK, width), lambda s: (f(s), col))
    side = lambda f: [blk(QK_C, 0, f), blk(QK_C, 1, f), blk(V_C, 1, f), blk(LANES, OD_GATE_BLK, f)]
    o_spec = lambda f: pl.BlockSpec((CHUNK, V_C), lambda s: (f(s), 0))
    o_shape = jax.ShapeDtypeStruct((n, V_C), _f32)
    states = (c0, n0, m0)
    return pl.pallas_call(
        functools.partial(_mlstm_kernel, st=st),
        grid=(st.n_chunks,),
        in_specs=side(fwd) + side(bwd) + [pl.BlockSpec(bias.shape, lambda s: (0, 0))]
        + [_lat_state_spec(a.shape, st) for a in states] + [_ANY] * 3,
        out_specs=[o_spec(fwd), o_spec(bwd)] + [_ctx_state_spec(a.shape, st, layer) for a in accs],
        out_shape=[o_shape, o_shape] + [jax.ShapeDtypeStruct(a.shape, _f32) for a in accs],
        input_output_aliases={12: 2, 13: 3, 14: 4},
        scratch_shapes=[pltpu.VMEM((2, H_C, DV_C, DK_C), _f32), pltpu.VMEM((2, H_C, 1, DK_C), _f32),
                        pltpu.VMEM((2, H_C, 1, 1), _f32)],
        compiler_params=_cparams(1), name="mlstm_scan",
    )(p, p, p, p, p, p, p, p, bias, c0, n0, m0, *accs)


def _head_rms_gate(o, g_row, gate, n_heads, width):
    outs = []
    for h in range(n_heads):
        hs = slice(h * width, (h + 1) * width)
        oh = o[:, hs]
        outs.append(oh * lax.rsqrt(jnp.mean(oh * oh, axis=-1, keepdims=True) + EPS) * g_row[:, hs] * gate[:, hs])
    return jnp.concatenate(outs, axis=1)


def _even_post_kernel(yf_ref, yb_ref, bonus_ref, g_ref, of_ref, ob_ref, og_ref, par_ref, gn_ref, seg_ref, segt_ref,
                      o_ref):
    y = yf_ref[...] + yb_ref[...]
    seg, seg_t = seg_ref[...], segt_ref[...]
    mean = _head_sum(y, seg, seg_t) * (1.0 / HD_A)
    yc = y - mean
    var = _head_sum(yc * yc, seg, seg_t) * (1.0 / HD_A)
    par = par_ref[...]
    ya = (yc * lax.rsqrt(var + EPS) * par[0:1] + par[1:2] + bonus_ref[...]) * g_ref[...]
    og = og_ref[...]
    yb = _head_rms_gate(of_ref[...] + ob_ref[...], gn_ref[...], og * jax.nn.sigmoid(og), H_B, DV_B)
    o_ref[:, :D_A] = ya.astype(_bf16)
    o_ref[:, D_A:] = yb.astype(_bf16)


def _even_post(yf, yb, bonus, g, of, ob, p, ln_par, gla_g, seg, seg_t, tm=256):
    n = yf.shape[0]
    row = lambda w, col=0: pl.BlockSpec((tm, w), lambda i: (i, col))
    full = lambda a: pl.BlockSpec(a.shape, lambda i: (0,) * a.ndim)
    return pl.pallas_call(
        _even_post_kernel,
        grid=(n // tm,),
        in_specs=[row(D_A)] * 4 + [row(V_B), row(V_B), row(V_B, EV_OG_BLK), full(ln_par), full(gla_g),
                  full(seg), full(seg_t)],
        out_specs=row(D_A + V_B),
        out_shape=jax.ShapeDtypeStruct((n, D_A + V_B), _bf16),
        compiler_params=_cparams(1), name="even_post",
    )(yf, yb, bonus, g, of, ob, p, ln_par, gla_g, seg, seg_t)


def _odd_post_kernel(hf_ref, hb_ref, o_ref_in, gn_ref, out_ref):
    out_ref[...] = _head_rms_gate(hf_ref[...] + hb_ref[...], gn_ref[...], jax.nn.sigmoid(o_ref_in[...]),
                                  H_C, DV_C).astype(_bf16)


def _odd_post(hf, hb, p, gn, tm=256):
    n = hf.shape[0]
    row = lambda col=0: pl.BlockSpec((tm, V_C), lambda i: (i, col))
    return pl.pallas_call(
        _odd_post_kernel,
        grid=(n // tm,),
        in_specs=[row(), row(), row(2), pl.BlockSpec(gn.shape, lambda i: (0, 0))],
        out_specs=row(),
        out_shape=jax.ShapeDtypeStruct((n, V_C), _bf16),
        compiler_params=_cparams(1), name="odd_post",
    )(hf, hb, p, gn)


def _even_mixer(p, st, layer, acc_rw, acc_gl, s_rwkv, s_gla, mu, w0, w2, a0, a2, g2, k_k, k_a, r_k, ln_g, ln_b,
                gla_w2, gla_b, gla_g):
    zpad = lambda x, rows_before, total: jnp.pad(x, ((rows_before, total - rows_before - x.shape[0]), (0, 0)))
    mu_main = mu[:EV_MAIN].reshape(1, EV_MAIN)
    mu_lora = jnp.pad(mu[EV_MAIN:], (0, EV_LORA_W - (A_IN - EV_MAIN))).reshape(1, EV_LORA_W)
    split = lambda w: jnp.stack(_split2(w), axis=0)
    ww = jnp.stack([split(zpad(w2[d], d * LORA_W, 2 * LORA_W)) for d in range(2)])
    wa = jnp.stack([split(zpad(a2[d], d * LORA_A, 2 * LORA_A)) for d in range(2)])
    wg = g2.astype(_bf16)
    par = jnp.stack([w0[0], w0[1], a0[0], a0[1], k_k, k_a, r_k, jnp.zeros_like(k_k)])
    head = np.arange(D_A) // HD_A
    seg_np = head[:, None] == np.arange(LANES)[None, :]
    seg, seg_t = jnp.asarray(seg_np, _bf16), jnp.asarray(seg_np.T, _bf16)
    *prep, bonus, g = _rwkv_prep(p, st, mu_main, mu_lora, ww, wa, wg, par, seg, seg_t)
    yf, yb, acc_rw = _rwkv_scan(prep, s_rwkv, st, acc_rw, layer)
    wl = jnp.stack([split(zpad(gla_w2[d], d * LORA_GK, LANES)) for d in range(2)])
    bias = jnp.pad(gla_b, ((0, MOD_ROWS - 2), (0, 0)))
    of, ob, acc_gl = _gla_scan(p, wl, bias, s_gla, st, acc_gl, layer)
    ln_par = jnp.pad(jnp.stack([ln_g, ln_b]), ((0, MOD_ROWS - 2), (0, 0)))
    y_in = _even_post(yf, yb, bonus, g, of, ob, p, ln_par, gla_g.reshape(1, V_B), seg, seg_t)
    return y_in, acc_rw, acc_gl


def _odd_mixer(p, st, layer, accs, s_c, s_n, s_m, b_i, b_f, g):
    bias = jnp.concatenate([b_i.reshape(-1), b_f.reshape(-1), jnp.zeros((LANES - 4 * H_C,), _f32)])
    bias = jnp.pad(bias[None, :], ((0, MOD_ROWS - 1), (0, 0)))
    hf, hb, *accs = _mlstm_scan(p, bias, s_c, s_n[..., None, :], s_m[..., None, None], st, accs, layer)
    return _odd_post(hf, hb, p, g.reshape(1, V_C)), tuple(accs)


def _moe(h, logits, w1, w3, w2, layer, tm=FFN_TM):
    n_tok = h.shape[0]
    n_pairs = n_tok * TOP_K
    n_tiles = n_pairs // tm + N_EXPERTS
    top_v, top_i = lax.top_k(logits[:, :N_EXPERTS], TOP_K)
    gates = jax.nn.softmax(top_v, axis=-1)
    pair_e = top_i.reshape(-1)
    order = jnp.argsort(pair_e, stable=True)
    one_hot = jax.nn.one_hot(pair_e, N_EXPERTS, dtype=jnp.int32)
    counts = jnp.sum(one_hot, axis=0)
    tiles_per_e = (counts + tm - 1) // tm
    tile_start = jnp.cumsum(tiles_per_e) - tiles_per_e
    group_start = jnp.cumsum(counts) - counts
    rank = jnp.sum((jnp.cumsum(one_hot, axis=0) - one_hot) * one_hot, axis=1)
    pair_slot = (jnp.sum(one_hot * tile_start[None, :], axis=1) * tm + rank).astype(jnp.int32)
    n_valid = jnp.sum(tiles_per_e)
    tile_ids = jnp.arange(n_tiles, dtype=jnp.int32)
    tile_e = jnp.sum((tile_ids[:, None] >= tile_start[None, :]).astype(jnp.int32), axis=1) - 1
    last_e = jnp.max(jnp.where(counts > 0, jnp.arange(N_EXPERTS), 0))
    tile_valid = tile_ids < n_valid
    tile_e = jnp.where(tile_valid, tile_e, last_e).astype(jnp.int32)
    tile_off = (tile_ids - tile_start[tile_e]) * tm
    tile_rows = jnp.where(tile_valid, jnp.clip(counts[tile_e] - tile_off, 0, tm), 0).astype(jnp.int32)
    tile_base = jnp.where(tile_valid, group_start[tile_e] + tile_off, 0).astype(jnp.int32)
    pair_token = (order // TOP_K).astype(jnp.int32)
    yg = _gathered_ffn(h, pair_token, w1, w3, w2, layer, tile_e, tile_rows, tile_base, tm=tm)
    return _moe_combine(yg, pair_slot, gates)


def kernel(x_prompt, x_sample, state_rwkv, state_gla, state_mlstm_c, state_mlstm_n, state_mlstm_m, c, c_ctx, mod_w, mod_b, norm_g, final_g, ev_w_in, ev_mu, rw_w0, rw_w2, rw_a0, rw_a2, rw_g2, rw_kk, rw_ka, rw_rk, rw_ln_g, rw_ln_b, gla_w2, gla_b, gla_norm_g, ev_w_out, ffn_w1, ffn_w3, ffn_w2, od_w_in, ml_b_i, ml_b_f, ml_norm_g, od_w_out, moe_router, moe_w1, moe_w3, moe_w2):
    D = D_MODEL
    Bp, Tp, _ = x_prompt.shape
    Bs, Ts, _ = x_sample.shape
    assert Ts // GRID_W * GRID_W == Ts and GRID_W == CHUNK and Tp % CHUNK == 0
    n_ctx = Bp * Tp
    st = _Streams(n_ctx_chunks=n_ctx // CHUNK, ctx_cps=Tp // CHUNK, lat_cps=Ts // CHUNK,
                  n_chunks=(n_ctx + Bs * Ts) // CHUNK, n_ctx_seq=Bp)
    x = (x_prompt.reshape(n_ctx, D), x_sample.reshape(Bs * Ts, D))

    cond = jnp.concatenate([c_ctx[None, :], c, jnp.zeros((MOD_ROWS - 1 - Bs, D), _f32)], axis=0)
    mod = _mod_table(jax.nn.silu(cond).astype(_bf16), mod_w, mod_b)

    n_even, n_odd = (DEPTH + 1) // 2, DEPTH // 2
    acc_rw = jnp.zeros((Bp, n_even, 2, H_A, HD_A, HD_A), _f32)
    acc_gl = jnp.zeros((Bp, n_even, 2, H_B, DK_B, DV_B), _f32)
    acc_ml = (jnp.zeros((Bp, n_odd, 2, H_C, DK_C, DV_C), _f32), jnp.zeros((Bp, n_odd, 2, H_C, 1, DK_C), _f32),
              jnp.zeros((Bp, n_odd, 2, H_C, 1, 1), _f32))
    pending = None

    b0 = A_IN
    ev_w_all = jnp.concatenate([
        ev_w_in[..., :EV_MAIN], ev_w_in[..., b0 + 2 * QK_B:b0 + 2 * QK_B + V_B],
        ev_w_in[..., b0 + 2 * QK_B + V_B + 2 * LORA_GK:], ev_w_in[..., b0:b0 + 2 * QK_B], ev_w_in[..., EV_MAIN:A_IN],
        ev_w_in[..., b0 + 2 * QK_B + V_B:b0 + 2 * QK_B + V_B + 2 * LORA_GK],
        jnp.zeros(ev_w_in.shape[:2] + (EV_LORA_W - (A_IN - EV_MAIN) - 2 * LORA_GK,), _f32)], axis=-1).astype(_bf16)
    od_w_all = jnp.pad(od_w_in, ((0, 0), (0, 0), (0, OD_N - od_w_in.shape[2]))).astype(_bf16)
    ev_w_out_b, od_w_out_b = ev_w_out.astype(_bf16), od_w_out.astype(_bf16)

    for l in range(DEPTH):
        i = l // 2
        sh1, sc1, gt1, sh2, sc2, gt2 = [mod[l, :, j * D:(j + 1) * D] for j in range(6)]
        if pending is None:
            h = _norm_mod(x, norm_g[l, 0], sc1, sh1, st)[0]
        else:
            x, h = _norm_mod(x, norm_g[l, 0], sc1, sh1, st, res=pending)
        if l % 2 == 0:
            p = _matmul(h, ev_w_all[i], tn=EV_N // 4)
            y_in, acc_rw, acc_gl = _even_mixer(
                p, st, i, acc_rw, acc_gl, state_rwkv[:, i], state_gla[:, i], ev_mu[i], rw_w0[i], rw_w2[i],
                rw_a0[i], rw_a2[i], rw_g2[i], rw_kk[i], rw_ka[i], rw_rk[i], rw_ln_g[i], rw_ln_b[i],
                gla_w2[i], gla_b[i], gla_norm_g[i])
            x, h = _matmul_residual_norm(y_in, ev_w_out_b[i], x, gt1, norm_g[l, 1], sc2, sh2, st)
            f = _dense_ffn(h, ffn_w1, ffn_w3, ffn_w2, i)
        else:
            p = _matmul(h, od_w_all[i], tn=OD_N // 7)
            y_in, acc_ml = _odd_mixer(
                p, st, i, acc_ml, state_mlstm_c[:, i], state_mlstm_n[:, i], state_mlstm_m[:, i],
                ml_b_i[i], ml_b_f[i], ml_norm_g[i])
            router = _split2(jnp.pad(moe_router[i], ((0, 0), (0, LANES - N_EXPERTS))))
            x, h, logits = _matmul_residual_norm(y_in, od_w_out_b[i], x, gt1, norm_g[l, 1], sc2, sh2, st,
                                                 router=router, h_dtype=_f32)
            f = _moe(h, logits, moe_w1, moe_w3, moe_w2, i)
        pending = (f, gt2)

    y_ctx, y_lat = _final_norm(x, pending[0], pending[1], final_g, st)
    return (y_ctx.reshape(Bp, Tp, D), y_lat.reshape(Bs, Ts, D), acc_rw, acc_gl,
            acc_ml[0], acc_ml[1][..., 0, :], acc_ml[2][..., 0, 0])
```

```python
import functools
from typing import NamedTuple

import jax
import jax.numpy as jnp
import numpy as np
from jax import lax
from jax.experimental import pallas as pl
from jax.experimental.pallas import tpu as pltpu

D_MODEL = 2048
DEPTH = 4
GRID_W = 64
EPS = 1e-6
CHUNK = 64
H_A, HD_A = 16, 64
D_A = H_A * HD_A
LORA_W, LORA_A, LORA_G = 64, 64, 128
A_IN = 3 * D_A + 2 * LORA_W + 2 * LORA_A + LORA_G
H_B, DK_B, DV_B = 4, 128, 256
QK_B, V_B = H_B * DK_B, H_B * DV_B
LORA_GK = 16
GATE_NORM = 16.0
H_C, DK_C, DV_C = 8, 128, 256
QK_C, V_C = H_C * DK_C, H_C * DV_C
D_FF = 5632
N_EXPERTS = 8
TOP_K = 2

LANES = 128
MOD_ROWS = 8
VMEM_LIMIT = 48 * 1024 * 1024
FFN_VMEM_LIMIT = 56 * 1024 * 1024
NEG_BIG = -1e30

EV_MAIN = 3 * D_A
EV_V_BLK = EV_MAIN // V_B
EV_OG_BLK = EV_V_BLK + 1
EV_Q_BLK = (EV_MAIN + 2 * V_B) // QK_B
EV_K_BLK = EV_Q_BLK + 1
EV_LORA_W = 512
EV_LORA_BLK = (EV_MAIN + 2 * V_B + 2 * QK_B) // EV_LORA_W
EV_N = EV_MAIN + 2 * V_B + 2 * QK_B + EV_LORA_W
GLA_LORA_OFF = 2 * LORA_W + 2 * LORA_A + LORA_G
OD_GATE_BLK = (2 * QK_C + 2 * V_C) // LANES
OD_N = 2 * QK_C + 2 * V_C + LANES

_f32 = jnp.float32
_bf16 = jnp.bfloat16


class _Streams(NamedTuple):
    n_ctx_chunks: int
    ctx_cps: int
    lat_cps: int
    n_chunks: int
    n_ctx_seq: int


def _chunk_pos(s, st):
    is_lat = s >= st.n_ctx_chunks
    sl = s - st.n_ctx_chunks
    nc = jnp.where(is_lat, st.lat_cps, st.ctx_cps)
    c = jnp.where(is_lat, sl % st.lat_cps, s % st.ctx_cps)
    seq = jnp.where(is_lat, st.n_ctx_seq + sl // st.lat_cps, s // st.ctx_cps)
    return is_lat, nc, c, seq


def _bwd_chunk(s, st):
    _, nc, c, _ = _chunk_pos(s, st)
    return s - c + (nc - 1 - c)


def _seq_of(s, st):
    return _chunk_pos(s, st)[3]


def _mod_row_of_chunk(s, st):
    is_lat, _, _, seq = _chunk_pos(s, st)
    return jnp.where(is_lat, 1 + seq - st.n_ctx_seq, 0)


def _cparams(n_axes):
    return pltpu.CompilerParams(dimension_semantics=("arbitrary",) * n_axes,
                                vmem_limit_bytes=VMEM_LIMIT)


def _dot(a, b):
    return jnp.dot(a.astype(_bf16), b.astype(_bf16), preferred_element_type=_f32)


def _dot_nt(a, b):
    return lax.dot_general(a.astype(_bf16), b.astype(_bf16), (((1,), (1,)), ((), ())),
                           preferred_element_type=_f32)


def _dot_tn(a, b):
    return lax.dot_general(a.astype(_bf16), b.astype(_bf16), (((0,), (0,)), ((), ())),
                           preferred_element_type=_f32)


def _split2(x):
    hi = x.astype(_bf16)
    return hi, (x - hi.astype(_f32)).astype(_bf16)


def _dot_exact_lhs(tri, x):
    hi, lo = _split2(x)
    t = tri.astype(_bf16)
    return jnp.dot(t, hi, preferred_element_type=_f32) + jnp.dot(t, lo, preferred_element_type=_f32)


def _dot_exact_rhs(x, ones):
    hi, lo = _split2(x)
    return jnp.dot(hi, ones, preferred_element_type=_f32) + jnp.dot(lo, ones, preferred_element_type=_f32)


def _head_sum(x, seg, seg_t):
    return _dot_exact_rhs(_dot_exact_rhs(x, seg), seg_t)


def _dot3(a, b_hi, b_lo):
    a_hi, a_lo = _split2(a)
    d = lambda u, v: jnp.dot(u, v, preferred_element_type=_f32)
    return d(a_hi, b_hi) + d(a_lo, b_hi) + d(a_hi, b_lo)


def _tri_masks(direction, n=CHUNK, period=CHUNK):
    t = lax.broadcasted_iota(jnp.int32, (n, n), 0)
    i = lax.broadcasted_iota(jnp.int32, (n, n), 1)
    same = (t // period) == (i // period) if n != period else True
    if direction == 0:
        return (i <= t) & same, (i < t) & same
    return (i >= t) & same, (i > t) & same


def _mod_kernel(a_ref, w_ref, b_ref, o_ref):
    o_ref[...] = _dot(a_ref[...], w_ref[...]) + b_ref[...]


def _mod_table(silu_rows, mod_w, mod_b, tn=1024):
    n_l, k, n = mod_w.shape
    return pl.pallas_call(
        _mod_kernel,
        grid=(n_l, n // tn),
        in_specs=[pl.BlockSpec((MOD_ROWS, k), lambda l, j: (0, 0)),
                  pl.BlockSpec((None, k, tn), lambda l, j: (l, 0, j)),
                  pl.BlockSpec((None, 1, tn), lambda l, j: (l, 0, j))],
        out_specs=pl.BlockSpec((None, MOD_ROWS, tn), lambda l, j: (l, 0, j)),
        out_shape=jax.ShapeDtypeStruct((n_l, MOD_ROWS, n), _f32),
        compiler_params=_cparams(2), name="mod_table",
    )(silu_rows, mod_w, mod_b.reshape(n_l, 1, n))


def _stream_specs(tm, d, st):
    n_ctx_blocks = st.n_ctx_chunks * CHUNK // tm
    return [pl.BlockSpec((tm, d), lambda i: (jnp.minimum(i, n_ctx_blocks - 1), 0)),
            pl.BlockSpec((tm, d), lambda i: (jnp.maximum(i - n_ctx_blocks, 0), 0))]


def _read_stream(i, tm, st, ctx_ref, lat_ref):
    return jnp.where(i * tm < st.n_ctx_chunks * CHUNK, ctx_ref[...], lat_ref[...])


def _norm_mod_kernel(*refs, tm, st, has_res, has_router, split_x):
    refs = list(refs)
    if split_x:
        x = _read_stream(pl.program_id(0), tm, st, refs.pop(0), refs.pop(0))
    else:
        x = refs.pop(0)[...]
    f_ref, gate_ref = (refs.pop(0), refs.pop(0)) if has_res else (None, None)
    g_ref, sc_ref, sh_ref = refs.pop(0), refs.pop(0), refs.pop(0)
    r_hi, r_lo = (refs.pop(0), refs.pop(0)) if has_router else (None, None)
    xo_ref = refs.pop(0) if has_res else None
    h_ref = refs.pop(0)
    row = _mod_row_of_chunk(pl.program_id(0) * (tm // CHUNK), st)
    if has_res:
        x = x + gate_ref[pl.ds(row, 1), :] * f_ref[...]
        xo_ref[...] = x
    inv = lax.rsqrt(jnp.mean(x * x, axis=-1, keepdims=True) + EPS)
    h = x * inv * g_ref[...] * (1.0 + sc_ref[pl.ds(row, 1), :]) + sh_ref[pl.ds(row, 1), :]
    h_ref[...] = h.astype(h_ref.dtype)
    if has_router:
        refs.pop(0)[...] = _dot3(h, r_hi[...], r_lo[...])


def _norm_mod(x, g, sc, sh, st, res=None, router=None, h_dtype=_bf16, tm=256):
    split_x = isinstance(x, tuple)
    d = x[0].shape[1] if split_x else x.shape[1]
    m = x[0].shape[0] + x[1].shape[0] if split_x else x.shape[0]
    row = pl.BlockSpec((tm, d), lambda i: (i, 0))
    tab = pl.BlockSpec((MOD_ROWS, d), lambda i: (0, 0))
    args, specs = (list(x), _stream_specs(tm, d, st)) if split_x else ([x], [row])
    outs, out_specs = [], []
    if res is not None:
        args += list(res)
        specs += [row, tab]
        outs.append(jax.ShapeDtypeStruct((m, d), _f32))
        out_specs.append(row)
    args += [g.reshape(1, d), sc, sh]
    specs += [pl.BlockSpec((1, d), lambda i: (0, 0)), tab, tab]
    outs.append(jax.ShapeDtypeStruct((m, d), h_dtype))
    out_specs.append(row)
    if router is not None:
        args += list(router)
        specs += [pl.BlockSpec((d, LANES), lambda i: (0, 0))] * 2
        outs.append(jax.ShapeDtypeStruct((m, LANES), _f32))
        out_specs.append(pl.BlockSpec((tm, LANES), lambda i: (i, 0)))
    return pl.pallas_call(
        functools.partial(_norm_mod_kernel, tm=tm, st=st, has_res=res is not None, has_router=router is not None,
                          split_x=split_x),
        grid=(m // tm,), in_specs=specs, out_specs=out_specs, out_shape=outs,
        compiler_params=_cparams(1), name="norm_mod",
    )(*args)


def _final_norm_kernel(x_ref, f_ref, gate_ref, g_ref, ctx_ref, lat_ref, *, tm, st):
    i = pl.program_id(0)
    row = _mod_row_of_chunk(i * (tm // CHUNK), st)
    x = x_ref[...] + gate_ref[pl.ds(row, 1), :] * f_ref[...]
    y = x * lax.rsqrt(jnp.mean(x * x, axis=-1, keepdims=True) + EPS) * g_ref[...]
    is_ctx = i * tm < st.n_ctx_chunks * CHUNK

    @pl.when(is_ctx)
    def _():
        ctx_ref[...] = y

    @pl.when(jnp.logical_not(is_ctx))
    def _():
        lat_ref[...] = y


def _final_norm(x, f, gate, g, st, tm=256):
    m, d = x.shape
    n_ctx = st.n_ctx_chunks * CHUNK
    row = pl.BlockSpec((tm, d), lambda i: (i, 0))
    return pl.pallas_call(
        functools.partial(_final_norm_kernel, tm=tm, st=st),
        grid=(m // tm,),
        in_specs=[row, row, pl.BlockSpec((MOD_ROWS, d), lambda i: (0, 0)), pl.BlockSpec((1, d), lambda i: (0, 0))],
        out_specs=_stream_specs(tm, d, st),
        out_shape=[jax.ShapeDtypeStruct((n_ctx, d), _f32), jax.ShapeDtypeStruct((m - n_ctx, d), _f32)],
        compiler_params=_cparams(1), name="final_norm",
    )(x, f, gate, g.reshape(1, d))


def _mm_kernel(a_ref, w_ref, o_ref):
    o_ref[...] = _dot(a_ref[...], w_ref[...])


def _mm_res_norm_kernel(*refs, tm, st, has_router, split_res):
    refs = list(refs)
    a_ref, w_ref = refs.pop(0), refs.pop(0)
    if split_res:
        res = _read_stream(pl.program_id(0), tm, st, refs.pop(0), refs.pop(0))
    else:
        res = refs.pop(0)[...]
    gate_ref, g_ref, sc_ref, sh_ref = refs.pop(0), refs.pop(0), refs.pop(0), refs.pop(0)
    r_hi, r_lo = (refs.pop(0), refs.pop(0)) if has_router else (None, None)
    outs = refs
    row = _mod_row_of_chunk(pl.program_id(0) * (tm // CHUNK), st)
    x = res + gate_ref[pl.ds(row, 1), :] * _dot(a_ref[...], w_ref[...])
    outs[0][...] = x
    inv = lax.rsqrt(jnp.mean(x * x, axis=-1, keepdims=True) + EPS)
    h = x * inv * g_ref[...] * (1.0 + sc_ref[pl.ds(row, 1), :]) + sh_ref[pl.ds(row, 1), :]
    outs[1][...] = h.astype(outs[1].dtype)
    if has_router:
        outs[2][...] = _dot3(h, r_hi[...], r_lo[...])


def _matmul(a, w, tm=1024, tn=512):
    m, k = a.shape
    n = w.shape[1]
    tm = min(tm, m)
    return pl.pallas_call(
        _mm_kernel,
        grid=(n // tn, m // tm),
        in_specs=[pl.BlockSpec((tm, k), lambda j, i: (i, 0)),
                  pl.BlockSpec((k, tn), lambda j, i: (0, j))],
        out_specs=pl.BlockSpec((tm, tn), lambda j, i: (i, j)),
        out_shape=jax.ShapeDtypeStruct((m, n), _f32),
        compiler_params=_cparams(2), name="matmul",
    )(a, w)


def _matmul_residual_norm(a, w, res, gate, g, sc, sh, st, router=None, h_dtype=_bf16, tm=256):
    m, k = a.shape
    n = w.shape[1]
    split_res = isinstance(res, tuple)
    row = lambda width: pl.BlockSpec((tm, width), lambda i: (i, 0))
    tab = pl.BlockSpec((MOD_ROWS, n), lambda i: (0, 0))
    args = [a, w] + (list(res) if split_res else [res]) + [gate, g.reshape(1, n), sc, sh]
    specs = ([row(k), pl.BlockSpec((k, n), lambda i: (0, 0))] + (_stream_specs(tm, n, st) if split_res else [row(n)])
             + [tab, pl.BlockSpec((1, n), lambda i: (0, 0)), tab, tab])
    outs = [jax.ShapeDtypeStruct((m, n), _f32), jax.ShapeDtypeStruct((m, n), h_dtype)]
    out_specs = [row(n), row(n)]
    if router is not None:
        args += list(router)
        specs += [pl.BlockSpec((n, LANES), lambda i: (0, 0))] * 2
        outs.append(jax.ShapeDtypeStruct((m, LANES), _f32))
        out_specs.append(row(LANES))
    return pl.pallas_call(
        functools.partial(_mm_res_norm_kernel, tm=tm, st=st, has_router=router is not None, split_res=split_res),
        grid=(m // tm,), in_specs=specs, out_specs=out_specs, out_shape=outs,
        compiler_params=_cparams(1), name="matmul_residual_norm",
    )(*args)


FFN_TM = 1024
FFN_TF = 256
FFN_ROW_STEP = 256


def _swiglu_accumulate(x_ref, w1_ref, w3_ref, w2_ref, o_ref, n_rows):
    x = x_ref[0:n_rows, :]
    a = _dot(x, w1_ref[...])
    b = _dot(x, w3_ref[...])
    o_ref[0:n_rows, :] += _dot(a * jax.nn.sigmoid(a) * b, w2_ref[...])


def _ffn_kernel(x_ref, w1_ref, w3_ref, w2_ref, o_ref, *, tm):
    @pl.when(pl.program_id(1) == 0)
    def _():
        o_ref[...] = jnp.zeros_like(o_ref)

    _swiglu_accumulate(x_ref, w1_ref, w3_ref, w2_ref, o_ref, tm)


def _row_copy(src_hbm, src_row, dst, dst_row, sem):
    return pltpu.make_async_copy(src_hbm.at[pl.ds(src_row, 1), :], dst.at[pl.ds(dst_row, 1), :], sem)


ROW_DMA_UNROLL = 8


def _for_each_row(n, body):
    groups = n // ROW_DMA_UNROLL

    def group(g, carry):
        r0 = pl.multiple_of(g * ROW_DMA_UNROLL, ROW_DMA_UNROLL)
        for u in range(ROW_DMA_UNROLL):
            body(r0 + u)
        return carry

    def single(r, carry):
        body(r)
        return carry

    lax.fori_loop(0, groups, group, 0)
    lax.fori_loop(groups * ROW_DMA_UNROLL, n, single, 0)


def _ffn_gather_kernel(te_ref, tr_ref, tb_ref, tok_ref, h_hbm, w1_ref, w3_ref, w2_ref, o_ref, xf_scr, xb_scr, sem,
                       *, tm, n_tiles):
    i, j = pl.program_id(0), pl.program_id(1)
    rows = tr_ref[i]

    def start_gather(tile):
        base = tb_ref[tile]
        _for_each_row(tr_ref[tile], lambda r: _row_copy(h_hbm, tok_ref[base + r], xf_scr, r, sem).start())

    def wait_gather(tile):
        @pl.when(tr_ref[tile] == tm)
        def _():
            pltpu.make_async_copy(h_hbm.at[pl.ds(0, tm), :], xf_scr, sem).wait()

        @pl.when(tr_ref[tile] < tm)
        def _():
            _for_each_row(tr_ref[tile], lambda r: _row_copy(h_hbm, 0, xf_scr, r, sem).wait())

    @pl.when(j == 0)
    def _():
        @pl.when(i == 0)
        def _():
            xf_scr[...] = jnp.zeros_like(xf_scr)
            start_gather(0)

        wait_gather(i)
        xb_scr[...] = xf_scr[...].astype(_bf16)
        o_ref[...] = jnp.zeros_like(o_ref)

        @pl.when(i + 1 < n_tiles)
        def _():
            start_gather(i + 1)

    for n_rows in range(FFN_ROW_STEP, tm + 1, FFN_ROW_STEP):
        @pl.when((rows > n_rows - FFN_ROW_STEP) & (rows <= n_rows))
        def _():
            _swiglu_accumulate(xb_scr, w1_ref, w3_ref, w2_ref, o_ref, n_rows)


def _gathered_ffn(h, pair_token, w1, w3, w2, layer, tile_expert, tile_rows, tile_base, tm=FFN_TM, tf=FFN_TF):
    d = h.shape[1]
    f = w1.shape[3]
    nj = f // tf
    n_tiles = tile_expert.shape[0]

    def jeff(i, j, tr):
        return jnp.where(tr[i] > 0, j, nj - 1)

    return pl.pallas_call(
        functools.partial(_ffn_gather_kernel, tm=tm, n_tiles=n_tiles),
        grid_spec=pltpu.PrefetchScalarGridSpec(
            num_scalar_prefetch=4,
            grid=(n_tiles, nj),
            in_specs=[pl.BlockSpec(memory_space=pl.ANY),
                      pl.BlockSpec((None, None, d, tf), lambda i, j, te, tr, *_: (layer, te[i], 0, jeff(i, j, tr))),
                      pl.BlockSpec((None, None, d, tf), lambda i, j, te, tr, *_: (layer, te[i], 0, jeff(i, j, tr))),
                      pl.BlockSpec((None, None, tf, d), lambda i, j, te, tr, *_: (layer, te[i], jeff(i, j, tr), 0))],
            out_specs=pl.BlockSpec((tm, d), lambda i, j, *_: (i, 0)),
            scratch_shapes=[pltpu.VMEM((tm, d), _f32), pltpu.VMEM((tm, d), _bf16), pltpu.SemaphoreType.DMA]),
        out_shape=jax.ShapeDtypeStruct((n_tiles * tm, d), _f32),
        compiler_params=pltpu.CompilerParams(dimension_semantics=("arbitrary", "arbitrary"),
                                             vmem_limit_bytes=FFN_VMEM_LIMIT),
        name="gathered_ffn",
    )(tile_expert, tile_rows, tile_base, pair_token, h, w1, w3, w2)


COMBINE_TM = 256


def _combine_kernel(slot_ref, yg_hbm, gates_ref, o_ref, buf, sem, *, tm):
    base = pl.program_id(0) * tm * TOP_K

    def start(t):
        for k in range(TOP_K):
            _row_copy(yg_hbm, slot_ref[base + t * TOP_K + k], buf.at[k], t, sem).start()

    _for_each_row(tm, start)
    for k in range(TOP_K):
        pltpu.make_async_copy(yg_hbm.at[pl.ds(0, tm), :], buf.at[k], sem).wait()
    g = gates_ref[...]
    o_ref[...] = g[:, 0:1] * buf[0] + g[:, 1:2] * buf[1]


def _moe_combine(yg, pair_slot, gates, tm=COMBINE_TM):
    n_tok = gates.shape[0]
    d = yg.shape[1]
    return pl.pallas_call(
        functools.partial(_combine_kernel, tm=tm),
        grid_spec=pltpu.PrefetchScalarGridSpec(
            num_scalar_prefetch=1,
            grid=(n_tok // tm,),
            in_specs=[pl.BlockSpec(memory_space=pl.ANY),
                      pl.BlockSpec((tm, TOP_K), lambda b, sl: (b, 0))],
            out_specs=pl.BlockSpec((tm, d), lambda b, sl: (b, 0)),
            scratch_shapes=[pltpu.VMEM((TOP_K, tm, d), _f32), pltpu.SemaphoreType.DMA]),
        out_shape=jax.ShapeDtypeStruct((n_tok, d), _f32),
        compiler_params=_cparams(1), name="moe_combine",
    )(pair_slot, yg, gates)


def _dense_ffn(x, w1, w3, w2, layer, tm=FFN_TM, tf=FFN_TF):
    m, d = x.shape
    f = w1.shape[2]
    return pl.pallas_call(
        functools.partial(_ffn_kernel, tm=tm),
        grid=(m // tm, f // tf),
        in_specs=[pl.BlockSpec((tm, d), lambda i, j: (i, 0)),
                  pl.BlockSpec((None, d, tf), lambda i, j: (layer, 0, j)),
                  pl.BlockSpec((None, d, tf), lambda i, j: (layer, 0, j)),
                  pl.BlockSpec((None, tf, d), lambda i, j: (layer, j, 0))],
        out_specs=pl.BlockSpec((tm, d), lambda i, j: (i, 0)),
        out_shape=jax.ShapeDtypeStruct((m, d), _f32),
        compiler_params=pltpu.CompilerParams(dimension_semantics=("arbitrary", "arbitrary"),
                                             vmem_limit_bytes=FFN_VMEM_LIMIT),
        name="dense_ffn",
    )(x, w1, w3, w2)


PREP_CHUNKS = 4
PREP_ROWS = PREP_CHUNKS * CHUNK


def _token_shift_block(x, prev, nxt, lane0, is_lat, first, last):
    rows, n = x.shape
    row = lax.broadcasted_iota(jnp.int32, x.shape, 0)
    ch = lax.broadcasted_iota(jnp.int32, x.shape, 1) + lane0
    down = pltpu.roll(x, 1, 0)
    up = pltpu.roll(x, rows - 1, 0)

    def context():
        zero_row = jnp.zeros((1, n), _f32)
        row_prev = jnp.where(first, zero_row, prev[CHUNK - 1:CHUNK, :])
        row_next = jnp.where(last, zero_row, nxt[0:1, :])
        return jnp.where(ch < A_IN // 2,
                         jnp.where(row == 0, row_prev, down),
                         jnp.where(row == rows - 1, row_next, up))

    def latent():
        q = A_IN // 4
        col = row % GRID_W
        above = jnp.concatenate([jnp.where(first, 0.0, prev), x[:rows - CHUNK]], axis=0)
        below = jnp.concatenate([x[CHUNK:], jnp.where(last, 0.0, nxt)], axis=0)
        return jnp.where(ch < q, jnp.where(col == 0, 0.0, down),
                         jnp.where(ch < 2 * q, jnp.where(col == GRID_W - 1, 0.0, up),
                                   jnp.where(ch < 3 * q, above, below)))

    return lax.cond(is_lat, latent, context)


def _rwkv_prep_kernel(xm_ref, pm_ref, nm_ref, xl_ref, pl_ref, nl_ref, mum_ref, mul_ref,
                      ww_ref, wa_ref, wg_ref, par_ref, seg_ref, segt_ref,
                      rt_ref, kt_ref, at_ref, bt_ref, kh_ref, bh_ref, v_ref, plast_ref, bonus_ref, g_ref,
                      *, st):
    s = pl.program_id(0) * PREP_CHUNKS
    is_lat, nc, c, _ = _chunk_pos(s, st)
    first, last = c == 0, c + PREP_CHUNKS == nc
    xm, xl = xm_ref[...], xl_ref[...]
    sm = _token_shift_block(xm, pm_ref[...], nm_ref[...], 0, is_lat, first, last)
    sl = _token_shift_block(xl, pl_ref[...], nl_ref[...], EV_MAIN, is_lat, first, last)
    pm = xm + mum_ref[...] * (sm - xm)
    lora = xl + mul_ref[...] * (sl - xl)
    r, k, v = pm[:, :D_A], pm[:, D_A:2 * D_A], pm[:, 2 * D_A:]
    par = par_ref[...]
    k_k, k_a, r_k = par[4:5], par[5:6], par[6:7]
    seg, seg_t = seg_ref[...], segt_ref[...]
    kk = k * k_k
    kk = kk / jnp.maximum(jnp.sqrt(_head_sum(kk * kk, seg, seg_t)), 1e-12)
    g_ref[...] = _dot(jax.nn.sigmoid(lora[:, 2 * LORA_W + 2 * LORA_A:GLA_LORA_OFF]), wg_ref[...])
    v_ref[...] = v.astype(_bf16)
    tanh_w = jnp.tanh(lora[:, :2 * LORA_W])
    lora_a = lora[:, 2 * LORA_W:2 * LORA_W + 2 * LORA_A]
    kdir_sum = jnp.zeros_like(k)
    t = lax.broadcasted_iota(jnp.int32, (PREP_ROWS, PREP_ROWS), 0)
    i = lax.broadcasted_iota(jnp.int32, (PREP_ROWS, PREP_ROWS), 1)
    same_chunk = jnp.where((t // CHUNK) == (i // CHUNK), 1.0, 0.0)
    for d in range(2):
        w = -jax.nn.softplus(-(par[d:d + 1] + _dot3(tanh_w, ww_ref[d, 0], ww_ref[d, 1]))) - 0.5
        ld = -jnp.exp(w)
        a = jax.nn.sigmoid(par[2 + d:3 + d] + _dot3(lora_a, wa_ref[d, 0], wa_ref[d, 1]))
        kdir = k * (1.0 + (a - 1.0) * k_a)
        kdir_sum = kdir_sum + kdir
        incl, _ = _tri_masks(d, PREP_ROWS, CHUNK)
        cum = _dot_exact_lhs(jnp.where(incl, 1.0, 0.0), ld)
        tot = _dot_exact_lhs(same_chunk, ld)
        e_pos, e_neg, e_end = jnp.exp(cum), jnp.exp(-cum), jnp.exp(tot - cum)
        kka = kk * a
        rt_ref[d] = (r * e_pos).astype(_bf16)
        kt_ref[d] = (kdir * e_neg).astype(_bf16)
        at_ref[d] = (kk * jnp.exp(cum - ld)).astype(_bf16)
        bt_ref[d] = (kka * e_neg).astype(_bf16)
        kh_ref[d] = (kdir * e_end).astype(_bf16)
        bh_ref[d] = (kka * e_end).astype(_bf16)
        e_tot = jnp.exp(tot)
        for j in range(PREP_CHUNKS):
            plast_ref[d, j] = e_tot[j * CHUNK:j * CHUNK + 1, :]
    bonus_ref[...] = _head_sum(r * kdir_sum * r_k, seg, seg_t) * v


def _rwkv_prep(p, st, mu_main, mu_lora, ww, wa, wg, par, seg, seg_t):
    n = p.shape[0]
    nch = st.n_chunks
    assert st.ctx_cps % PREP_CHUNKS == 0 and st.lat_cps % PREP_CHUNKS == 0
    cur = lambda b: (b, 0)
    prv = lambda b: (jnp.maximum(b * PREP_CHUNKS - 1, 0), 0)
    nxt = lambda b: (jnp.minimum((b + 1) * PREP_CHUNKS, nch - 1), 0)
    lo = lambda f: (lambda b: (f(b)[0], EV_LORA_BLK))
    main = lambda rows, f: pl.BlockSpec((rows, EV_MAIN), f)
    lora = lambda rows, f: pl.BlockSpec((rows, EV_LORA_W), lo(f))
    full = lambda a: pl.BlockSpec(a.shape, lambda b: (0,) * a.ndim)
    seq2 = pl.BlockSpec((2, PREP_ROWS, D_A), lambda b: (0, b, 0))
    seq1 = pl.BlockSpec((PREP_ROWS, D_A), lambda b: (b, 0))
    bf2 = jax.ShapeDtypeStruct((2, n, D_A), _bf16)
    return pl.pallas_call(
        functools.partial(_rwkv_prep_kernel, st=st),
        grid=(nch // PREP_CHUNKS,),
        in_specs=[main(PREP_ROWS, cur), main(CHUNK, prv), main(CHUNK, nxt),
                  lora(PREP_ROWS, cur), lora(CHUNK, prv), lora(CHUNK, nxt),
                  full(mu_main), full(mu_lora), full(ww), full(wa), full(wg), full(par), full(seg), full(seg_t)],
        out_specs=[seq2] * 6 + [seq1, pl.BlockSpec((2, PREP_CHUNKS, 1, D_A), lambda b: (0, b, 0, 0)), seq1, seq1],
        out_shape=[bf2] * 6 + [jax.ShapeDtypeStruct((n, D_A), _bf16),
                               jax.ShapeDtypeStruct((2, nch, 1, D_A), _f32),
                               jax.ShapeDtypeStruct((n, D_A), _f32),
                               jax.ShapeDtypeStruct((n, D_A), _f32)],
        compiler_params=_cparams(1), name="rwkv_prep",
    )(p, p, p, p, p, p, mu_main, mu_lora, ww, wa, wg, par, seg, seg_t)


N_PAIRS = H_A // 2
PAIR = 2 * CHUNK


def _stack_masked(x, keep_first):
    return jnp.concatenate([x * keep_first, x * (1 - keep_first)], axis=0)


def _rwkv_scan_kernel(*refs, st):
    (rt_f, kt_f, at_f, bt_f, kh_f, bh_f, v_f, pl_f,
     rt_b, kt_b, at_b, bt_b, kh_b, bh_b, v_b, pl_b, s0_ref, _acc, yf_ref, yb_ref, sout_ref, s_scr) = refs
    s = pl.program_id(0)
    is_lat, nc, c, _ = _chunk_pos(s, st)

    @pl.when(c == 0)
    def _():
        keep = jnp.where(is_lat, 1.0, 0.0)
        zero = jnp.zeros((HD_A, HD_A), _f32)
        for d in range(2):
            for p in range(N_PAIRS):
                top = jnp.concatenate([s0_ref[d, 2 * p], zero], axis=1)
                bot = jnp.concatenate([zero, s0_ref[d, 2 * p + 1]], axis=1)
                s_scr[d, p] = jnp.concatenate([top, bot], axis=0) * keep

    dirs = ((rt_f, kt_f, at_f, bt_f, kh_f, bh_f, v_f, pl_f, yf_ref),
            (rt_b, kt_b, at_b, bt_b, kh_b, bh_b, v_b, pl_b, yb_ref))
    chains = [(d, p) for d in range(2) for p in range(N_PAIRS)]
    masks = [_tri_masks(d, PAIR, CHUNK) for d in range(2)]
    eye = jnp.where(masks[0][0] & masks[1][0], 1.0, 0.0)

    lane = lax.broadcasted_iota(jnp.int32, (CHUNK, LANES), 1)
    keep_first = jnp.where(lane < HD_A, 1.0, 0.0).astype(_bf16)

    def load(ref, p):
        return _stack_masked(ref[:, p * LANES:(p + 1) * LANES], keep_first)

    ops = {}
    for d, p in chains:
        rt, kt, at, bt, kh, bh, v = (load(dirs[d][j], p) for j in range(7))
        incl, strict = masks[d]
        big = _dot_nt(jnp.concatenate([at, rt], axis=0), jnp.concatenate([kt, bt], axis=0))
        n = jnp.where(strict, big[:PAIR, PAIR:], 0.0)
        ops[d, p] = dict(rt=rt, at=at, bh=bh, kh=kh, v=v, t=eye - n, n=n.astype(_bf16),
                         mak=jnp.where(strict, big[:PAIR, :PAIR], 0.0).astype(_bf16),
                         mrk=jnp.where(incl, big[PAIR:, :PAIR], 0.0).astype(_bf16),
                         mrb=jnp.where(incl, big[PAIR:, PAIR:], 0.0).astype(_bf16))
    for _ in range(5):
        for ch in chains:
            ops[ch]["n"] = _dot(ops[ch]["n"], ops[ch]["n"]).astype(_bf16)
        for ch in chains:
            ops[ch]["t"] = ops[ch]["t"] + _dot(ops[ch]["t"], ops[ch]["n"])
    for ch in chains:
        o = ops[ch]
        o["mv"] = _dot(jnp.concatenate([o.pop("mak"), o.pop("mrk")], axis=0), o["v"])
    for ch in chains:
        o = ops[ch]
        o["xz"] = _dot(o.pop("t"), jnp.concatenate([o["at"], o["mv"][:PAIR].astype(_bf16)], axis=1))
    for ch in chains:
        o = ops[ch]
        o["qy"] = _dot(o.pop("mrb"), o["xz"])
    for ch in chains:
        o = ops[ch]
        o["upd"] = _dot_tn(o["xz"], o["bh"])
        o["vtk"] = _dot_tn(o["v"], o["kh"])
    for d, p in chains:
        o = ops[d, p]
        s0 = s_scr[d, p]
        q_eff = o["rt"].astype(_f32) - o["qy"][:, :LANES]
        y = _dot_nt(q_eff, s0) + o["mv"][PAIR:] - o["qy"][:, LANES:]
        dirs[d][8][:, p * LANES:(p + 1) * LANES] = y[:CHUNK] + y[CHUNK:]
        p_last = dirs[d][7][:, p * LANES:(p + 1) * LANES]
        s_scr[d, p] = s0 * p_last - _dot(s0, o["upd"][:LANES]) + o["vtk"] - o["upd"][LANES:]

    @pl.when((c == nc - 1) & jnp.logical_not(is_lat))
    def _():
        for d in range(2):
            for p in range(N_PAIRS):
                sp = s_scr[d, p]
                sout_ref[d, 2 * p] = sp[:HD_A, :HD_A]
                sout_ref[d, 2 * p + 1] = sp[HD_A:, HD_A:]


def _lat_state_spec(shape, st):
    nd = len(shape)
    return pl.BlockSpec((None,) + tuple(shape[1:]),
                        lambda s: (jnp.maximum(_seq_of(s, st) - st.n_ctx_seq, 0),) + (0,) * (nd - 1))


def _ctx_state_spec(shape, st, layer):
    nd = len(shape)
    return pl.BlockSpec((None, None) + tuple(shape[2:]),
                        lambda s: (jnp.minimum(_seq_of(s, st), st.n_ctx_seq - 1), layer) + (0,) * (nd - 2))


_ANY = pl.BlockSpec(memory_space=pl.ANY)


def _rwkv_scan(prep, s0, st, acc, layer):
    rt, kt, at, bt, kh, bh, v, p_last = prep
    n = v.shape[0]
    fwd = lambda s: s
    bwd = lambda s: _bwd_chunk(s, st)
    seq2 = lambda d, f: pl.BlockSpec((None, CHUNK, D_A), lambda s: (d, f(s), 0))
    seq1 = lambda f: pl.BlockSpec((CHUNK, D_A), lambda s: (f(s), 0))
    pls = lambda d, f: pl.BlockSpec((None, None, 1, D_A), lambda s: (d, f(s), 0, 0))
    side = lambda d, f: [seq2(d, f)] * 6 + [seq1(f), pls(d, f)]
    y_shape = jax.ShapeDtypeStruct((n, D_A), _f32)
    args = (rt, kt, at, bt, kh, bh, v, p_last)
    return pl.pallas_call(
        functools.partial(_rwkv_scan_kernel, st=st),
        grid=(st.n_chunks,),
        in_specs=side(0, fwd) + side(1, bwd) + [_lat_state_spec(s0.shape, st), _ANY],
        out_specs=[seq1(fwd), seq1(bwd), _ctx_state_spec(acc.shape, st, layer)],
        out_shape=[y_shape, y_shape, jax.ShapeDtypeStruct(acc.shape, _f32)],
        input_output_aliases={2 * len(args) + 1: 2},
        scratch_shapes=[pltpu.VMEM((2, N_PAIRS, LANES, LANES), _f32)],
        compiler_params=_cparams(1), name="rwkv_scan",
    )(*args, *args, s0, acc)


def _gla_kernel(q_f, k_f, v_f, l_f, q_b, k_b, v_b, l_b, w_ref, bias_ref, s0_ref, _acc,
                of_ref, ob_ref, sout_ref, s_scr, *, st):
    s = pl.program_id(0)
    is_lat, nc, c, _ = _chunk_pos(s, st)

    @pl.when(c == 0)
    def _():
        keep = jnp.where(is_lat, 1.0, 0.0)
        for d in range(2):
            for h in range(H_B):
                s_scr[d, h] = s0_ref[d, h].T * keep

    dirs = ((q_f, k_f, v_f, l_f, of_ref), (q_b, k_b, v_b, l_b, ob_ref))
    work = []
    for d, (q_ref, k_ref, v_ref, l_ref, o_ref) in enumerate(dirs):
        incl, _ = _tri_masks(d)
        gate_lora = l_ref[:, GLA_LORA_OFF:GLA_LORA_OFF + LANES]
        la = jax.nn.log_sigmoid(_dot3(gate_lora, w_ref[d, 0], w_ref[d, 1]) + bias_ref[d:d + 1]) / GATE_NORM
        cum = _dot_exact_lhs(jnp.where(incl, 1.0, 0.0), la)
        tot = jnp.sum(la, axis=0, keepdims=True)
        k = k_ref[...]
        q_in = q_ref[...] * (DK_B ** -0.5) * jnp.exp(cum)
        k_in = k * jnp.exp(-cum)
        k_end = k * jnp.exp(tot - cum)
        e_tot = jnp.exp(tot)
        for h in range(H_B):
            ks = slice(h * DK_B, (h + 1) * DK_B)
            work.append((d, h, incl, q_in[:, ks], k_in[:, ks], k_end[:, ks], e_tot[:, ks]))
    att = [jnp.where(incl, _dot_nt(q_in, k_in), 0.0) for (_, _, incl, q_in, k_in, _, _) in work]
    for (d, h, _, q_in, _, k_end, e_tot), a in zip(work, att):
        v_ref, o_ref = dirs[d][2], dirs[d][4]
        vs = slice(h * DV_B, (h + 1) * DV_B)
        v = v_ref[:, vs]
        s_t = s_scr[d, h]
        o_ref[:, vs] = _dot(a, v) + _dot_nt(q_in, s_t)
        s_scr[d, h] = s_t * e_tot + _dot_tn(v, k_end)

    @pl.when((c == nc - 1) & jnp.logical_not(is_lat))
    def _():
        for d in range(2):
            for h in range(H_B):
                sout_ref[d, h] = s_scr[d, h].T


def _gla_scan(p, w, bias, s0, st, acc, layer):
    n = p.shape[0]
    fwd = lambda s: s
    bwd = lambda s: _bwd_chunk(s, st)
    blk = lambda width, col, f: pl.BlockSpec((CHUNK, width), lambda s: (f(s), col))
    side = lambda f: [blk(QK_B, EV_Q_BLK, f), blk(QK_B, EV_K_BLK, f), blk(V_B, EV_V_BLK, f),
                      blk(EV_LORA_W, EV_LORA_BLK, f)]
    full = lambda a: pl.BlockSpec(a.shape, lambda s: (0,) * a.ndim)
    o_spec = lambda f: pl.BlockSpec((CHUNK, V_B), lambda s: (f(s), 0))
    o_shape = jax.ShapeDtypeStruct((n, V_B), _f32)
    return pl.pallas_call(
        functools.partial(_gla_kernel, st=st),
        grid=(st.n_chunks,),
        in_specs=side(fwd) + side(bwd) + [full(w), full(bias), _lat_state_spec(s0.shape, st), _ANY],
        out_specs=[o_spec(fwd), o_spec(bwd), _ctx_state_spec(acc.shape, st, layer)],
        out_shape=[o_shape, o_shape, jax.ShapeDtypeStruct(acc.shape, _f32)],
        input_output_aliases={11: 2},
        scratch_shapes=[pltpu.VMEM((2, H_B, DV_B, DK_B), _f32)],
        compiler_params=_cparams(1), name="gla_scan",
    )(p, p, p, p, p, p, p, p, w, bias, s0, acc)


def _mlstm_kernel(q_f, k_f, v_f, g_f, q_b, k_b, v_b, g_b, bias_ref, c0_ref, n0_ref, m0_ref, _acc_c, _acc_n, _acc_m,
                  hf_ref, hb_ref, cout_ref, nout_ref, mout_ref, c_scr, n_scr, m_scr, *, st):
    s = pl.program_id(0)
    is_lat, nc, c, _ = _chunk_pos(s, st)

    @pl.when(c == 0)
    def _():
        keep = jnp.where(is_lat, 1.0, 0.0)
        for d in range(2):
            for h in range(H_C):
                c_scr[d, h] = c0_ref[d, h].T * keep
        n_scr[...] = n0_ref[...] * keep
        m_scr[...] = m0_ref[...] * keep

    L = CHUNK
    lane = lax.broadcasted_iota(jnp.int32, (L, LANES), 1)
    dirs = ((q_f, k_f, v_f, g_f, hf_ref), (q_b, k_b, v_b, g_b, hb_ref))
    work = []
    for d, (q_ref, k_ref, v_ref, g_ref, h_ref) in enumerate(dirs):
        gates = g_ref[...] + bias_ref[0:1]
        gates = jnp.where(lane < 2 * H_C, gates, jax.nn.log_sigmoid(gates))
        gates_t = gates.T
        for h in range(H_C):
            ji, jf = d * H_C + h, 2 * H_C + d * H_C + h
            ks = slice(h * DK_C, (h + 1) * DK_C)
            q = q_ref[:, ks] * (DK_C ** -0.5)
            k = k_ref[:, ks]
            work.append(dict(d=d, h=h, q=q, k=k, incl=_tri_masks(d)[0], incl_t=_tri_masks(1 - d)[0],
                             i_col=gates[:, ji:ji + 1], f_col=gates[:, jf:jf + 1],
                             i_row=gates_t[ji:ji + 1, :], f_row=gates_t[jf:jf + 1, :],
                             m_prev=m_scr[d, h], c_t=c_scr[d, h], n_row=n_scr[d, h]))
    for o in work:
        o["s"] = _dot_nt(o["q"], o["k"])
        o["qc"] = _dot_nt(o["q"], o["c_t"])
    for o in work:
        o["cum_col"] = jnp.sum(jnp.where(o["incl"], o["f_row"], 0.0), axis=1, keepdims=True)
        o["cum_row"] = jnp.sum(jnp.where(o["incl_t"], o["f_col"], 0.0), axis=0, keepdims=True)
        o["f_tot"] = jnp.sum(o["f_row"], axis=1, keepdims=True)
        o["qn"] = jnp.sum(o["q"] * o["n_row"], axis=1, keepdims=True)
    for o in work:
        o["log_d"] = jnp.where(o["incl"], o["cum_col"] - o["cum_row"] + o["i_row"], NEG_BIG)
        o["log_inter"] = o["cum_col"] + o["m_prev"]
        o["row_max"] = jnp.max(o["log_d"], axis=1, keepdims=True)
    for o in work:
        m_t = jnp.maximum(o["log_inter"], o["row_max"])
        o["m_t"] = m_t
        o["sm"] = o["s"] * jnp.exp(o.pop("log_d") - m_t)
        o["w_inter"] = jnp.exp(o["log_inter"] - m_t)
        o["m_new"] = m_t[L - 1:L, :] if o["d"] == 0 else m_t[0:1, :]
        o["kw"] = o["k"] * jnp.exp(o["f_tot"] - o["cum_col"] + o["i_col"] - o["m_new"])
        o["dec"] = jnp.exp(o["f_tot"] + o["m_prev"] - o["m_new"])
    for o in work:
        vs = slice(o["h"] * DV_C, (o["h"] + 1) * DV_C)
        o["v"] = dirs[o["d"]][2][:, vs]
        o["sv"] = _dot(o["sm"], o["v"])
        o["kv"] = _dot_tn(o["v"], o["kw"])
        o["s_sum"] = jnp.sum(o["sm"], axis=1, keepdims=True)
    for o in work:
        d, h = o["d"], o["h"]
        vs = slice(h * DV_C, (h + 1) * DV_C)
        num = o["w_inter"] * o["qc"] + o["sv"]
        den = o["w_inter"] * o["qn"] + o["s_sum"]
        dirs[d][4][:, vs] = num * (1.0 / jnp.maximum(jnp.abs(den), jnp.exp(-o["m_t"])))
        c_scr[d, h] = o["dec"] * o["c_t"] + o["kv"]
        n_scr[d, h] = o["dec"] * o["n_row"] + jnp.sum(o["kw"], axis=0, keepdims=True)
        m_scr[d, h] = o["m_new"]

    @pl.when((c == nc - 1) & jnp.logical_not(is_lat))
    def _():
        for d in range(2):
            for h in range(H_C):
                cout_ref[d, h] = c_scr[d, h].T
        nout_ref[...] = n_scr[...]
        mout_ref[...] = m_scr[...]


def _mlstm_scan(p, bias, c0, n0, m0, st, accs, layer):
    n = p.shape[0]
    fwd = lambda s: s
    bwd = lambda s: _bwd_chunk(s, st)
    blk = lambda width, col, f: pl.BlockSpec((CHUNK, width), lambda s: (f(s), col))
    side = lambda f: [blk(QK_C, 0, f), blk(QK_C, 1, f), blk(V_C, 1, f), blk(LANES, OD_GATE_BLK, f)]
    o_spec = lambda f: pl.BlockSpec((CHUNK, V_C), lambda s: (f(s), 0))
    o_shape = jax.ShapeDtypeStruct((n, V_C), _f32)
    states = (c0, n0, m0)
    return pl.pallas_call(
        functools.partial(_mlstm_kernel, st=st),
        grid=(st.n_chunks,),
        in_specs=side(fwd) + side(bwd) + [pl.BlockSpec(bias.shape, lambda s: (0, 0))]
        + [_lat_state_spec(a.shape, st) for a in states] + [_ANY] * 3,
        out_specs=[o_spec(fwd), o_spec(bwd)] + [_ctx_state_spec(a.shape, st, layer) for a in accs],
        out_shape=[o_shape, o_shape] + [jax.ShapeDtypeStruct(a.shape, _f32) for a in accs],
        input_output_aliases={12: 2, 13: 3, 14: 4},
        scratch_shapes=[pltpu.VMEM((2, H_C, DV_C, DK_C), _f32), pltpu.VMEM((2, H_C, 1, DK_C), _f32),
                        pltpu.VMEM((2, H_C, 1, 1), _f32)],
        compiler_params=_cparams(1), name="mlstm_scan",
    )(p, p, p, p, p, p, p, p, bias, c0, n0, m0, *accs)


def _head_rms_gate(o, g_row, gate, n_heads, width):
    outs = []
    for h in range(n_heads):
        hs = slice(h * width, (h + 1) * width)
        oh = o[:, hs]
        outs.append(oh * lax.rsqrt(jnp.mean(oh * oh, axis=-1, keepdims=True) + EPS) * g_row[:, hs] * gate[:, hs])
    return jnp.concatenate(outs, axis=1)


def _even_post_kernel(yf_ref, yb_ref, bonus_ref, g_ref, of_ref, ob_ref, og_ref, par_ref, gn_ref, seg_ref, segt_ref,
                      o_ref):
    y = yf_ref[...] + yb_ref[...]
    seg, seg_t = seg_ref[...], segt_ref[...]
    mean = _head_sum(y, seg, seg_t) * (1.0 / HD_A)
    yc = y - mean
    var = _head_sum(yc * yc, seg, seg_t) * (1.0 / HD_A)
    par = par_ref[...]
    ya = (yc * lax.rsqrt(var + EPS) * par[0:1] + par[1:2] + bonus_ref[...]) * g_ref[...]
    og = og_ref[...]
    yb = _head_rms_gate(of_ref[...] + ob_ref[...], gn_ref[...], og * jax.nn.sigmoid(og), H_B, DV_B)
    o_ref[:, :D_A] = ya.astype(_bf16)
    o_ref[:, D_A:] = yb.astype(_bf16)


def _even_post(yf, yb, bonus, g, of, ob, p, ln_par, gla_g, seg, seg_t, tm=256):
    n = yf.shape[0]
    row = lambda w, col=0: pl.BlockSpec((tm, w), lambda i: (i, col))
    full = lambda a: pl.BlockSpec(a.shape, lambda i: (0,) * a.ndim)
    return pl.pallas_call(
        _even_post_kernel,
        grid=(n // tm,),
        in_specs=[row(D_A)] * 4 + [row(V_B), row(V_B), row(V_B, EV_OG_BLK), full(ln_par), full(gla_g),
                  full(seg), full(seg_t)],
        out_specs=row(D_A + V_B),
        out_shape=jax.ShapeDtypeStruct((n, D_A + V_B), _bf16),
        compiler_params=_cparams(1), name="even_post",
    )(yf, yb, bonus, g, of, ob, p, ln_par, gla_g, seg, seg_t)


def _odd_post_kernel(hf_ref, hb_ref, o_ref_in, gn_ref, out_ref):
    out_ref[...] = _head_rms_gate(hf_ref[...] + hb_ref[...], gn_ref[...], jax.nn.sigmoid(o_ref_in[...]),
                                  H_C, DV_C).astype(_bf16)


def _odd_post(hf, hb, p, gn, tm=256):
    n = hf.shape[0]
    row = lambda col=0: pl.BlockSpec((tm, V_C), lambda i: (i, col))
    return pl.pallas_call(
        _odd_post_kernel,
        grid=(n // tm,),
        in_specs=[row(), row(), row(2), pl.BlockSpec(gn.shape, lambda i: (0, 0))],
        out_specs=row(),
        out_shape=jax.ShapeDtypeStruct((n, V_C), _bf16),
        compiler_params=_cparams(1), name="odd_post",
    )(hf, hb, p, gn)


def _even_mixer(p, st, layer, acc_rw, acc_gl, s_rwkv, s_gla, mu, w0, w2, a0, a2, g2, k_k, k_a, r_k, ln_g, ln_b,
                gla_w2, gla_b, gla_g):
    zpad = lambda x, rows_before, total: jnp.pad(x, ((rows_before, total - rows_before - x.shape[0]), (0, 0)))
    mu_main = mu[:EV_MAIN].reshape(1, EV_MAIN)
    mu_lora = jnp.pad(mu[EV_MAIN:], (0, EV_LORA_W - (A_IN - EV_MAIN))).reshape(1, EV_LORA_W)
    split = lambda w: jnp.stack(_split2(w), axis=0)
    ww = jnp.stack([split(zpad(w2[d], d * LORA_W, 2 * LORA_W)) for d in range(2)])
    wa = jnp.stack([split(zpad(a2[d], d * LORA_A, 2 * LORA_A)) for d in range(2)])
    wg = g2.astype(_bf16)
    par = jnp.stack([w0[0], w0[1], a0[0], a0[1], k_k, k_a, r_k, jnp.zeros_like(k_k)])
    head = np.arange(D_A) // HD_A
    seg_np = head[:, None] == np.arange(LANES)[None, :]
    seg, seg_t = jnp.asarray(seg_np, _bf16), jnp.asarray(seg_np.T, _bf16)
    *prep, bonus, g = _rwkv_prep(p, st, mu_main, mu_lora, ww, wa, wg, par, seg, seg_t)
    yf, yb, acc_rw = _rwkv_scan(prep, s_rwkv, st, acc_rw, layer)
    wl = jnp.stack([split(zpad(gla_w2[d], d * LORA_GK, LANES)) for d in range(2)])
    bias = jnp.pad(gla_b, ((0, MOD_ROWS - 2), (0, 0)))
    of, ob, acc_gl = _gla_scan(p, wl, bias, s_gla, st, acc_gl, layer)
    ln_par = jnp.pad(jnp.stack([ln_g, ln_b]), ((0, MOD_ROWS - 2), (0, 0)))
    y_in = _even_post(yf, yb, bonus, g, of, ob, p, ln_par, gla_g.reshape(1, V_B), seg, seg_t)
    return y_in, acc_rw, acc_gl


def _odd_mixer(p, st, layer, accs, s_c, s_n, s_m, b_i, b_f, g):
    bias = jnp.concatenate([b_i.reshape(-1), b_f.reshape(-1), jnp.zeros((LANES - 4 * H_C,), _f32)])
    bias = jnp.pad(bias[None, :], ((0, MOD_ROWS - 1), (0, 0)))
    hf, hb, *accs = _mlstm_scan(p, bias, s_c, s_n[..., None, :], s_m[..., None, None], st, accs, layer)
    return _odd_post(hf, hb, p, g.reshape(1, V_C)), tuple(accs)


def _moe(h, logits, w1, w3, w2, layer, tm=FFN_TM):
    n_tok = h.shape[0]
    n_pairs = n_tok * TOP_K
    n_tiles = n_pairs // tm + N_EXPERTS
    top_v, top_i = lax.top_k(logits[:, :N_EXPERTS], TOP_K)
    gates = jax.nn.softmax(top_v, axis=-1)
    pair_e = top_i.reshape(-1)
    order = jnp.argsort(pair_e, stable=True)
    one_hot = jax.nn.one_hot(pair_e, N_EXPERTS, dtype=jnp.int32)
    counts = jnp.sum(one_hot, axis=0)
    tiles_per_e = (counts + tm - 1) // tm
    tile_start = jnp.cumsum(tiles_per_e) - tiles_per_e
    group_start = jnp.cumsum(counts) - counts
    rank = jnp.sum((jnp.cumsum(one_hot, axis=0) - one_hot) * one_hot, axis=1)
    pair_slot = (jnp.sum(one_hot * tile_start[None, :], axis=1) * tm + rank).astype(jnp.int32)
    n_valid = jnp.sum(tiles_per_e)
    tile_ids = jnp.arange(n_tiles, dtype=jnp.int32)
    tile_e = jnp.sum((tile_ids[:, None] >= tile_start[None, :]).astype(jnp.int32), axis=1) - 1
    last_e = jnp.max(jnp.where(counts > 0, jnp.arange(N_EXPERTS), 0))
    tile_valid = tile_ids < n_valid
    tile_e = jnp.where(tile_valid, tile_e, last_e).astype(jnp.int32)
    tile_off = (tile_ids - tile_start[tile_e]) * tm
    tile_rows = jnp.where(tile_valid, jnp.clip(counts[tile_e] - tile_off, 0, tm), 0).astype(jnp.int32)
    tile_base = jnp.where(tile_valid, group_start[tile_e] + tile_off, 0).astype(jnp.int32)
    pair_token = (order // TOP_K).astype(jnp.int32)
    yg = _gathered_ffn(h, pair_token, w1, w3, w2, layer, tile_e, tile_rows, tile_base, tm=tm)
    return _moe_combine(yg, pair_slot, gates)


def kernel(x_prompt, x_sample, state_rwkv, state_gla, state_mlstm_c, state_mlstm_n, state_mlstm_m, c, c_ctx, mod_w, mod_b, norm_g, final_g, ev_w_in, ev_mu, rw_w0, rw_w2, rw_a0, rw_a2, rw_g2, rw_kk, rw_ka, rw_rk, rw_ln_g, rw_ln_b, gla_w2, gla_b, gla_norm_g, ev_w_out, ffn_w1, ffn_w3, ffn_w2, od_w_in, ml_b_i, ml_b_f, ml_norm_g, od_w_out, moe_router, moe_w1, moe_w3, moe_w2):
    D = D_MODEL
    Bp, Tp, _ = x_prompt.shape
    Bs, Ts, _ = x_sample.shape
    assert Ts // GRID_W * GRID_W == Ts and GRID_W == CHUNK and Tp % CHUNK == 0
    n_ctx = Bp * Tp
    st = _Streams(n_ctx_chunks=n_ctx // CHUNK, ctx_cps=Tp // CHUNK, lat_cps=Ts // CHUNK,
                  n_chunks=(n_ctx + Bs * Ts) // CHUNK, n_ctx_seq=Bp)
    x = (x_prompt.reshape(n_ctx, D), x_sample.reshape(Bs * Ts, D))

    cond = jnp.concatenate([c_ctx[None, :], c, jnp.zeros((MOD_ROWS - 1 - Bs, D), _f32)], axis=0)
    mod = _mod_table(jax.nn.silu(cond).astype(_bf16), mod_w, mod_b)

    n_even, n_odd = (DEPTH + 1) // 2, DEPTH // 2
    acc_rw = jnp.zeros((Bp, n_even, 2, H_A, HD_A, HD_A), _f32)
    acc_gl = jnp.zeros((Bp, n_even, 2, H_B, DK_B, DV_B), _f32)
    acc_ml = (jnp.zeros((Bp, n_odd, 2, H_C, DK_C, DV_C), _f32), jnp.zeros((Bp, n_odd, 2, H_C, 1, DK_C), _f32),
              jnp.zeros((Bp, n_odd, 2, H_C, 1, 1), _f32))
    pending = None

    b0 = A_IN
    ev_w_all = jnp.concatenate([
        ev_w_in[..., :EV_MAIN], ev_w_in[..., b0 + 2 * QK_B:b0 + 2 * QK_B + V_B],
        ev_w_in[..., b0 + 2 * QK_B + V_B + 2 * LORA_GK:], ev_w_in[..., b0:b0 + 2 * QK_B], ev_w_in[..., EV_MAIN:A_IN],
        ev_w_in[..., b0 + 2 * QK_B + V_B:b0 + 2 * QK_B + V_B + 2 * LORA_GK],
        jnp.zeros(ev_w_in.shape[:2] + (EV_LORA_W - (A_IN - EV_MAIN) - 2 * LORA_GK,), _f32)], axis=-1).astype(_bf16)
    od_w_all = jnp.pad(od_w_in, ((0, 0), (0, 0), (0, OD_N - od_w_in.shape[2]))).astype(_bf16)
    ev_w_out_b, od_w_out_b = ev_w_out.astype(_bf16), od_w_out.astype(_bf16)

    for l in range(DEPTH):
        i = l // 2
        sh1, sc1, gt1, sh2, sc2, gt2 = [mod[l, :, j * D:(j + 1) * D] for j in range(6)]
        if pending is None:
            h = _norm_mod(x, norm_g[l, 0], sc1, sh1, st)[0]
        else:
            x, h = _norm_mod(x, norm_g[l, 0], sc1, sh1, st, res=pending)
        if l % 2 == 0:
            p = _matmul(h, ev_w_all[i], tn=EV_N // 4)
            y_in, acc_rw, acc_gl = _even_mixer(
                p, st, i, acc_rw, acc_gl, state_rwkv[:, i], state_gla[:, i], ev_mu[i], rw_w0[i], rw_w2[i],
                rw_a0[i], rw_a2[i], rw_g2[i], rw_kk[i], rw_ka[i], rw_rk[i], rw_ln_g[i], rw_ln_b[i],
                gla_w2[i], gla_b[i], gla_norm_g[i])
            x, h = _matmul_residual_norm(y_in, ev_w_out_b[i], x, gt1, norm_g[l, 1], sc2, sh2, st)
            f = _dense_ffn(h, ffn_w1, ffn_w3, ffn_w2, i)
        else:
            p = _matmul(h, od_w_all[i], tn=OD_N // 7)
            y_in, acc_ml = _odd_mixer(
                p, st, i, acc_ml, state_mlstm_c[:, i], state_mlstm_n[:, i], state_mlstm_m[:, i],
                ml_b_i[i], ml_b_f[i], ml_norm_g[i])
            router = _split2(jnp.pad(moe_router[i], ((0, 0), (0, LANES - N_EXPERTS))))
            x, h, logits = _matmul_residual_norm(y_in, od_w_out_b[i], x, gt1, norm_g[l, 1], sc2, sh2, st,
                                                 router=router, h_dtype=_f32)
            f = _moe(h, logits, moe_w1, moe_w3, moe_w2, i)
        pending = (f, gt2)

    y_ctx, y_lat = _final_norm(x, pending[0], pending[1], final_g, st)
    return (y_ctx.reshape(Bp, Tp, D), y_lat.reshape(Bs, Ts, D), acc_rw, acc_gl,
            acc_ml[0], acc_ml[1][..., 0, :], acc_ml[2][..., 0, 0])
```
